```python
import math
import jax, jax.numpy as jnp
from jax import lax
import numpy as np

D_MODEL = 1024
BATCH = 8
SEQ = 2048
DEPTH = 1
DEC_BATCH = 4
DEC_SEQ = 4096
PAST_LEN = 128

MIX_WIDTH = D_MODEL
A_HEADS = 8
A_HEAD_DIM = (MIX_WIDTH // 2) // A_HEADS
A_WIDTH = A_HEADS * A_HEAD_DIM
A_PATTERNS = ((128, 1), (512, 4), (2048, 16))
B_HEADS = 8
B_NOPE = 64
B_ROPE = 32
B_V = (MIX_WIDTH // 2) // B_HEADS
Q_LORA = 256
KV_LORA = 128
ROPE_BASE = 10000.0
IN_COLS = 3 * A_WIDTH + Q_LORA + KV_LORA + B_ROPE
MIX_OUT = A_WIDTH + B_HEADS * B_V
N_EXPERTS = 256
TOP_K = 8
N_GROUPS = 8
TOPK_GROUPS = 4
EXPERT_HIDDEN = 256
SHARED_HIDDEN = 256
ROUTED_SCALE = 2.5
ROW_BLOCK = 128
Q_BLOCK = 128
LN_EPS = 1e-5
RMS_EPS = 1e-6
NEG_BIG = -1e30
ALPHA = (2.0 * DEPTH) ** 0.25
BETA = (8.0 * DEPTH) ** -0.25

kernel_name = "hymba_dilated_mla_moe_deepnorm_encoder"


def layer_norm(x, g, b):
    xf = x.astype(jnp.float32)
    mu = jnp.mean(xf, axis=-1, keepdims=True)
    var = jnp.mean(jnp.square(xf - mu), axis=-1, keepdims=True)
    y = (xf - mu) * lax.rsqrt(var + LN_EPS) * g.astype(jnp.float32) + b.astype(jnp.float32)
    return y.astype(x.dtype)


def rms_norm(x, g):
    xf = x.astype(jnp.float32)
    y = xf * lax.rsqrt(jnp.mean(jnp.square(xf), axis=-1, keepdims=True) + RMS_EPS) * g.astype(jnp.float32)
    return y.astype(x.dtype)


def alibi_slopes(n_heads):
    return jnp.exp2(-8.0 * (jnp.arange(n_heads, dtype=jnp.float32) + 1.0) / n_heads)


def rope_tables(seq_len, dim):
    inv_freq = ROPE_BASE ** (-jnp.arange(0, dim, 2, dtype=jnp.float32) / dim)
    ang = jnp.arange(seq_len, dtype=jnp.float32)[:, None] * inv_freq[None, :]
    return jnp.cos(ang), jnp.sin(ang)


def apply_rope(x, cos, sin):
    half = x.shape[-1] // 2
    x1, x2 = x[..., :half], x[..., half:]
    cos = cos.astype(x.dtype)
    sin = sin.astype(x.dtype)
    return jnp.concatenate([x1 * cos - x2 * sin, x1 * sin + x2 * cos], axis=-1)


def dilated_window_attention(q, k, v):
    B, S, H, Dh = q.shape
    slopes = alibi_slopes(H)
    scale = Dh ** -0.5
    n_blocks = S // Q_BLOCK

    def one_block(blk):
        t = blk * Q_BLOCK + jnp.arange(Q_BLOCK)
        qb = lax.dynamic_slice_in_dim(q, blk * Q_BLOCK, Q_BLOCK, axis=1)
        outs, lses = [], []
        for window, dil in A_PATTERNS:
            half = window // (2 * dil)
            offs = jnp.arange(-half, half + 1) * dil
            pos = t[:, None] + offs[None, :]
            valid = (pos >= 0) & (pos < S)
            idx = jnp.clip(pos, 0, S - 1)
            kg = k[:, idx]
            vg = v[:, idx]
            logits = jnp.einsum('bthd,btjhd->bhtj', qb, kg, preferred_element_type=jnp.float32) * scale
            logits = logits - slopes[:, None, None] * jnp.abs(offs).astype(jnp.float32)[None, None, :]
            logits = jnp.where(valid[None, None], logits, NEG_BIG)
            lse = jax.nn.logsumexp(logits, axis=-1)
            p = jnp.exp(logits - lse[..., None])
            outs.append(jnp.einsum('bhtj,btjhd->bthd', p.astype(v.dtype), vg))
            lses.append(lse)
        w = jax.nn.softmax(jnp.stack(lses, axis=0), axis=0)
        return jnp.einsum('gbht,gbthd->bthd', w.astype(v.dtype), jnp.stack(outs, axis=0))

    ob = lax.map(one_block, jnp.arange(n_blocks))
    return ob.transpose(1, 0, 2, 3, 4).reshape(B, S, H * Dh)


def latent_attention(c_q, c_kv, k_rope_in, q_norm_g, w_uq, kv_norm_g, w_uk, w_uv):
    B, S, _ = c_q.shape
    q = (rms_norm(c_q, q_norm_g) @ w_uq).reshape(B, S, B_HEADS, B_NOPE + B_ROPE)
    q_nope, q_rope = q[..., :B_NOPE], q[..., B_NOPE:]
    ckv = rms_norm(c_kv, kv_norm_g)
    k_nope = (ckv @ w_uk).reshape(B, S, B_HEADS, B_NOPE)
    v = (ckv @ w_uv).reshape(B, S, B_HEADS, B_V)
    cos, sin = rope_tables(S, B_ROPE)
    q_rope = apply_rope(q_rope, cos[None, :, None, :], sin[None, :, None, :])
    k_rope = apply_rope(k_rope_in, cos[None], sin[None])
    scale = (B_NOPE + B_ROPE) ** -0.5
    n_blocks = S // Q_BLOCK

    def one_block(blk):
        qn = lax.dynamic_slice_in_dim(q_nope, blk * Q_BLOCK, Q_BLOCK, axis=1)
        qr = lax.dynamic_slice_in_dim(q_rope, blk * Q_BLOCK, Q_BLOCK, axis=1)
        logits = (jnp.einsum('bthd,bshd->bhts', qn, k_nope, preferred_element_type=jnp.float32)
                  + jnp.einsum('bthr,bsr->bhts', qr, k_rope, preferred_element_type=jnp.float32)) * scale
        p = jax.nn.softmax(logits, axis=-1)
        return jnp.einsum('bhts,bshd->bthd', p.astype(v.dtype), v)

    ob = lax.map(one_block, jnp.arange(n_blocks))
    return ob.transpose(1, 0, 2, 3, 4).reshape(B, S, B_HEADS * B_V)


def moe_ffn(x, w_router, router_bias, w_gate, w_up, w_down, ws_gate, ws_up, ws_down):
    T, D = x.shape
    scores = jax.nn.sigmoid(jnp.matmul(x, w_router, preferred_element_type=jnp.float32))
    sel = scores + router_bias.astype(jnp.float32)
    grp = sel.reshape(T, N_GROUPS, N_EXPERTS // N_GROUPS)
    grp_score = lax.top_k(grp, 2)[0].sum(-1)
    _, top_groups = lax.top_k(grp_score, TOPK_GROUPS)
    gmask = jnp.any(top_groups[:, :, None] == jnp.arange(N_GROUPS)[None, None, :], axis=1)
    emask = jnp.repeat(gmask, N_EXPERTS // N_GROUPS, axis=1)
    _, top_e = lax.top_k(jnp.where(emask, sel, NEG_BIG), TOP_K)
    gate = jnp.take_along_axis(scores, top_e, axis=1)
    gate = gate / jnp.sum(gate, axis=-1, keepdims=True) * ROUTED_SCALE

    A = T * TOP_K
    flat_e = top_e.reshape(A)
    order = jnp.argsort(flat_e)
    sorted_e = flat_e[order]
    counts = jnp.bincount(flat_e, length=N_EXPERTS)
    start = jnp.cumsum(counts) - counts
    padded = ((counts + ROW_BLOCK - 1) // ROW_BLOCK) * ROW_BLOCK
    pend = jnp.cumsum(padded)
    pstart = pend - padded
    dest_sorted = pstart[sorted_e] + (jnp.arange(A) - start[sorted_e])
    n_blocks = (A + ROW_BLOCK - 1) // ROW_BLOCK + N_EXPERTS
    P = n_blocks * ROW_BLOCK
    buf = jnp.zeros((P, D), x.dtype).at[dest_sorted].set(x[order // TOP_K])
    blk_expert = jnp.searchsorted(pend, jnp.arange(n_blocks) * ROW_BLOCK, side='right')
    blk_expert = jnp.minimum(blk_expert, N_EXPERTS - 1)

    def expert_block(args):
        xb, e = args
        h = jax.nn.silu(xb @ w_gate[e]) * (xb @ w_up[e])
        return h @ w_down[e]

    out = lax.map(expert_block, (buf.reshape(n_blocks, ROW_BLOCK, D), blk_expert)).reshape(P, D)
    dest = jnp.zeros((A,), dtype=dest_sorted.dtype).at[order].set(dest_sorted)
    y_assign = out[dest].reshape(T, TOP_K, D)
    routed = jnp.einsum('tk,tkd->td', gate.astype(x.dtype), y_assign)
    shared = (jax.nn.silu(x @ ws_gate) * (x @ ws_up)) @ ws_down
    return routed + shared


def encoder_layer(x, w_in, w_out, ln1_g, ln1_b, q_norm_g, w_uq, kv_norm_g, w_uk, w_uv,
                  w_router, router_bias, w_gate, w_up, w_down, ws_gate, ws_up, ws_down, ln2_g, ln2_b):
    B, S, D = x.shape
    proj = x @ w_in
    cuts = [A_WIDTH, 2 * A_WIDTH, 3 * A_WIDTH, 3 * A_WIDTH + Q_LORA, 3 * A_WIDTH + Q_LORA + KV_LORA]
    qa, ka, va, cq, ckv, kr = jnp.split(proj, cuts, axis=-1)
    hs = (B, S, A_HEADS, A_HEAD_DIM)
    o_a = dilated_window_attention(qa.reshape(hs), ka.reshape(hs), va.reshape(hs))
    o_b = latent_attention(cq, ckv, kr, q_norm_g, w_uq, kv_norm_g, w_uk, w_uv)
    mix = jnp.concatenate([o_a, o_b], axis=-1) @ w_out
    h = layer_norm(ALPHA * x + mix, ln1_g, ln1_b)
    f = moe_ffn(h.reshape(B * S, D), w_router, router_bias, w_gate, w_up, w_down,
                ws_gate, ws_up, ws_down).reshape(B, S, D)
    return layer_norm(ALPHA * h + f, ln2_g, ln2_b)


def setup_inputs(seed: int = 0) -> dict:
    key = jax.random.key(seed)
    ks = jax.random.split(key, 24)
    nrm = lambda k, shape, s: jax.random.normal(k, shape, jnp.float32) * s
    L = DEPTH
    col_scale = jnp.concatenate([jnp.ones((2 * A_WIDTH,), jnp.float32),
                                 jnp.full((A_WIDTH,), BETA, jnp.float32),
                                 jnp.ones((Q_LORA + KV_LORA + B_ROPE,), jnp.float32)])
    return {
        "x_prompt": jax.random.normal(ks[0], (BATCH, SEQ, D_MODEL), jnp.float32),
        "x_sample": jax.random.normal(ks[1], (DEC_BATCH, DEC_SEQ, D_MODEL), jnp.float32),
        "w_in": nrm(ks[2], (L, D_MODEL, IN_COLS), D_MODEL ** -0.5) * col_scale,
        "w_out": nrm(ks[3], (L, MIX_OUT, D_MODEL), BETA * MIX_OUT ** -0.5),
        "ln1_g": 1.0 + nrm(ks[4], (L, D_MODEL), 0.02),
        "ln1_b": nrm(ks[5], (L, D_MODEL), 0.02),
        "q_norm_g": 1.0 + nrm(ks[6], (L, Q_LORA), 0.02),
        "w_uq": nrm(ks[7], (L, Q_LORA, B_HEADS * (B_NOPE + B_ROPE)), Q_LORA ** -0.5),
        "kv_norm_g": 1.0 + nrm(ks[8], (L, KV_LORA), 0.02),
        "w_uk": nrm(ks[9], (L, KV_LORA, B_HEADS * B_NOPE), KV_LORA ** -0.5),
        "w_uv": nrm(ks[10], (L, KV_LORA, B_HEADS * B_V), BETA * KV_LORA ** -0.5),
        "w_router": nrm(ks[11], (L, D_MODEL, N_EXPERTS), D_MODEL ** -0.5),
        "router_bias": nrm(ks[12], (L, N_EXPERTS), 0.01),
        "w_gate": nrm(ks[13], (L, N_EXPERTS, D_MODEL, EXPERT_HIDDEN), D_MODEL ** -0.5),
        "w_up": nrm(ks[14], (L, N_EXPERTS, D_MODEL, EXPERT_HIDDEN), D_MODEL ** -0.5),
        "w_down": nrm(ks[15], (L, N_EXPERTS, EXPERT_HIDDEN, D_MODEL), BETA * EXPERT_HIDDEN ** -0.5),
        "ws_gate": nrm(ks[16], (L, D_MODEL, SHARED_HIDDEN), D_MODEL ** -0.5),
        "ws_up": nrm(ks[17], (L, D_MODEL, SHARED_HIDDEN), D_MODEL ** -0.5),
        "ws_down": nrm(ks[18], (L, SHARED_HIDDEN, D_MODEL), BETA * SHARED_HIDDEN ** -0.5),
        "ln2_g": 1.0 + nrm(ks[19], (L, D_MODEL), 0.02),
        "ln2_b": nrm(ks[20], (L, D_MODEL), 0.02),
    }


def reference(x_prompt, x_sample, w_in, w_out, ln1_g, ln1_b, q_norm_g, w_uq, kv_norm_g, w_uk, w_uv,
              w_router, router_bias, w_gate, w_up, w_down, ws_gate, ws_up, ws_down, ln2_g, ln2_b):
    def trunk(x):
        for l in range(DEPTH):
            x = encoder_layer(x, w_in[l], w_out[l], ln1_g[l], ln1_b[l], q_norm_g[l], w_uq[l],
                              kv_norm_g[l], w_uk[l], w_uv[l], w_router[l], router_bias[l],
                              w_gate[l], w_up[l], w_down[l], ws_gate[l], ws_up[l], ws_down[l],
                              ln2_g[l], ln2_b[l])
        return x
    y_prompt = trunk(x_prompt)
    y_sample = trunk(x_sample)
    return (y_prompt, y_sample)
```

```python
import functools
import math

import numpy as np
import jax
import jax.numpy as jnp
from jax import lax
from jax.experimental import pallas as pl
from jax.experimental.pallas import tpu as pltpu

F32 = jnp.float32
BF16 = jnp.bfloat16
I32 = jnp.int32

D_MODEL = 1024
A_HEADS = 8
A_HEAD_DIM = 64
A_WIDTH = A_HEADS * A_HEAD_DIM
A_PATTERNS = ((128, 1), (512, 4), (2048, 16))
A_HALF = 64
B_HEADS = 8
B_NOPE = 64
B_ROPE = 32
B_V = 64
Q_LORA = 256
KV_LORA = 128
ROPE_BASE = 10000.0
N_EXPERTS = 256
TOP_K = 8
N_GROUPS = 8
GROUP_SIZE = N_EXPERTS // N_GROUPS
TOPK_GROUPS = 4
EXPERT_HIDDEN = 256
ROUTED_SCALE = 2.5
LN_EPS = 1e-5
RMS_EPS = 1e-6
NEG_BIG = -1e30
ALPHA = 2.0 ** 0.25
LOG2E = 1.4426950408889634

LANES = 128
HEAD_PAD = 128
SLAB = D_MODEL // LANES
W1_COLS = 3 * A_WIDTH + Q_LORA + KV_LORA + 2 * LANES

TM_PROJ = 256
TQ_A = 128
TQ_B = 256
TM_MIX = 256
TM_DISP = 256
ROW_BLOCK = 256
TM_COMB = 128
VMEM_LIMIT = 48 * 1024 * 1024


def _slab_load(ref, n_tok, tok0=0):
    return jnp.concatenate([ref[pl.ds(tok0 * SLAB + c, n_tok, stride=SLAB), :] for c in range(SLAB)],
                           axis=1)


def _slab_store(ref, val):
    for c in range(SLAB):
        ref[pl.ds(c, val.shape[0], stride=SLAB), :] = val[:, c * LANES:(c + 1) * LANES]


def _nt_dot(a, b):
    return lax.dot_general(a, b, (((1,), (1,)), ((), ())), preferred_element_type=F32)


def _layer_norm(x, g, b):
    mu = jnp.mean(x, axis=-1, keepdims=True)
    xc = x - mu
    var = jnp.mean(xc * xc, axis=-1, keepdims=True)
    return xc * lax.rsqrt(var + LN_EPS) * g + b


def _rms_norm(x, g):
    return x * lax.rsqrt(jnp.mean(x * x, axis=-1, keepdims=True) + RMS_EPS) * g


def _proj_kernel(xp_ref, xs_ref, w1_ref, qg_ref, kvg_ref, wq_ref, wqr_ref, wuk_ref, wuv_ref,
                 cos_ref, sin_ref, qa_ref, ka_ref, va_ref, qb_ref, kb_ref, vb_ref, *, n_prompt_tiles):
    i = pl.program_id(0)
    x = jnp.where(i < n_prompt_tiles, xp_ref[...], xs_ref[...]).astype(BF16)
    p = jnp.dot(x, w1_ref[...], preferred_element_type=F32)
    qa_ref[...] = (p[:, 0:A_WIDTH] * (A_HEAD_DIM ** -0.5 * LOG2E)).astype(BF16)
    ka_ref[...] = p[:, A_WIDTH:2 * A_WIDTH].astype(BF16)
    va_ref[...] = p[:, 2 * A_WIDTH:3 * A_WIDTH].astype(BF16)
    c0 = 3 * A_WIDTH
    cq = p[:, c0:c0 + Q_LORA]
    ckv = p[:, c0 + Q_LORA:c0 + Q_LORA + KV_LORA]
    kr = p[:, c0 + Q_LORA + KV_LORA:c0 + Q_LORA + KV_LORA + LANES]
    krr = p[:, c0 + Q_LORA + KV_LORA + LANES:c0 + Q_LORA + KV_LORA + 2 * LANES]
    cos = cos_ref[...]
    sin = sin_ref[...]
    cos8 = jnp.concatenate([cos] * B_HEADS, axis=1)
    sin8 = jnp.concatenate([sin] * B_HEADS, axis=1)
    cqn = _rms_norm(cq, qg_ref[...]).astype(BF16)
    q = jnp.dot(cqn, wq_ref[...], preferred_element_type=F32)
    qr = jnp.dot(cqn, wqr_ref[...], preferred_element_type=F32)
    qscale = (B_NOPE + B_ROPE) ** -0.5 * LOG2E
    qb_ref[...] = ((q * cos8 + qr * sin8) * qscale).astype(BF16)
    ckvn = _rms_norm(ckv, kvg_ref[...]).astype(BF16)
    kn = jnp.dot(ckvn, wuk_ref[...], preferred_element_type=F32)
    krope = kr * cos + krr * sin
    kb_ref[...] = (kn + jnp.concatenate([krope] * B_HEADS, axis=1)).astype(BF16)
    vb_ref[...] = jnp.dot(ckvn, wuv_ref[...], preferred_element_type=F32).astype(BF16)


def _proj(xp, xs, w1, qg, kvg, wq, wqr, wuk, wuv, cos_t, sin_t, s_prompt, s_sample):
    tm = TM_PROJ
    n1 = xp.shape[0] // tm
    n2 = xs.shape[0] // tm
    t_all = xp.shape[0] + xs.shape[0]
    pt, st = s_prompt // tm, s_sample // tm

    def tab_idx(i):
        return (jnp.where(i < n1, i % pt, (i - n1) % st), 0)

    full = lambda shape: pl.BlockSpec(shape, lambda i: (0, 0))
    row = lambda cols: pl.BlockSpec((tm, cols), lambda i: (i, 0))
    return pl.pallas_call(
        functools.partial(_proj_kernel, n_prompt_tiles=n1),
        grid=(n1 + n2,),
        in_specs=[
            pl.BlockSpec((tm, D_MODEL), lambda i: (jnp.minimum(i, n1 - 1), 0)),
            pl.BlockSpec((tm, D_MODEL), lambda i: (jnp.maximum(i - n1, 0), 0)),
            full(w1.shape), full(qg.shape), full(kvg.shape), full(wq.shape), full(wqr.shape),
            full(wuk.shape), full(wuv.shape),
            pl.BlockSpec((tm, LANES), tab_idx), pl.BlockSpec((tm, LANES), tab_idx),
        ],
        out_specs=[row(A_WIDTH), row(A_WIDTH), row(A_WIDTH), row(B_HEADS * HEAD_PAD),
                   row(B_HEADS * HEAD_PAD), row(B_HEADS * B_V)],
        out_shape=[jax.ShapeDtypeStruct((t_all, A_WIDTH), BF16)] * 3
        + [jax.ShapeDtypeStruct((t_all, B_HEADS * HEAD_PAD), BF16)] * 2
        + [jax.ShapeDtypeStruct((t_all, B_HEADS * B_V), BF16)],
        compiler_params=pltpu.CompilerParams(dimension_semantics=("arbitrary",),
                                             vmem_limit_bytes=VMEM_LIMIT),
        name="proj",
    )(xp, xs, w1, qg, kvg, wq, wqr, wuk, wuv, cos_t, sin_t)


def _attn_a_kernel(qrow_ref, prow_ref, nrow_ref, rcol_ref, tstart_ref, llen_ref,
                   q_ref, kp_ref, kc_ref, kn_ref, vp_ref, vc_ref, vn_ref, bias_ref, o_ref, lse_ref):
    n = pl.program_id(0)
    tq = TQ_A
    kw = jnp.concatenate([kp_ref[tq - A_HALF:, :], kc_ref[...], kn_ref[:A_HALF, :]], axis=0)
    vw = jnp.concatenate([vp_ref[tq - A_HALF:, :], vc_ref[...], vn_ref[:A_HALF, :]], axis=0)
    wk = tq + 2 * A_HALF
    col = lax.broadcasted_iota(I32, (1, wk), 1) + (tstart_ref[n] - A_HALF)
    colpen = jnp.where((col >= 0) & (col < llen_ref[n]), 0.0, NEG_BIG).astype(F32)
    lane = lax.broadcasted_iota(I32, (tq, LANES), 1)
    low = lane < A_HEAD_DIM
    q = q_ref[...]
    for j in range(A_HEADS // 2):
        sl = slice(j * LANES, (j + 1) * LANES)
        qp, kpair, vpair = q[:, sl], kw[:, sl], vw[:, sl]
        outs, lses = [], []
        for e in range(2):
            qm = jnp.where(low if e == 0 else ~low, qp, jnp.zeros_like(qp))
            s = _nt_dot(qm, kpair) + bias_ref[2 * j + e] + colpen
            m = jnp.max(s, axis=1, keepdims=True)
            p = jnp.exp2(s - m)
            l = jnp.sum(p, axis=1, keepdims=True)
            o = jnp.dot(p.astype(BF16), vpair, preferred_element_type=F32)
            outs.append(o / l)
            lses.append(m + jnp.log2(l))
        o_ref[:, sl] = jnp.where(low, outs[0], outs[1]).astype(BF16)
        lse_ref[:, sl] = jnp.where(low, lses[0], lses[1])


def _attn_a_tables(seqs, dil):
    tq = TQ_A
    qrow, prow, nrow, rcol, tstart, llen = [], [], [], [], [], []
    for off, s_len in seqs:
        cls = s_len // dil
        nt = cls // tq
        base = (off // dil) // tq
        for r in range(dil):
            for i in range(nt):
                qrow.append(base + i)
                prow.append(base + max(i - 1, 0))
                nrow.append(base + min(i + 1, nt - 1))
                rcol.append(r)
                tstart.append(i * tq)
                llen.append(cls)
    return [jnp.asarray(np.asarray(a, np.int32)) for a in (qrow, prow, nrow, rcol, tstart, llen)]


def _attn_a_bias(dil):
    tq, wk = TQ_A, TQ_A + 2 * A_HALF
    delta = np.abs(np.arange(wk)[None, :] - A_HALF - np.arange(tq)[:, None]).astype(np.float64)
    slopes = 2.0 ** (-8.0 * (np.arange(A_HEADS) + 1.0) / A_HEADS)
    bias = -slopes[:, None, None] * (delta * dil)[None] * LOG2E
    bias = np.where((delta <= A_HALF)[None], bias, NEG_BIG)
    return jnp.asarray(bias.astype(np.float32))


def _attn_a(qa, ka, va, seqs, dil):
    t_all = qa.shape[0]
    tq = TQ_A
    view = lambda a: a.reshape(t_all // dil, dil * A_WIDTH)
    tabs = _attn_a_tables(seqs, dil)
    n_steps = int(tabs[0].shape[0])
    bias = _attn_a_bias(dil)
    cur = lambda n, qr, pr, nr, rc, ts, ll: (qr[n], rc[n])
    prev = lambda n, qr, pr, nr, rc, ts, ll: (pr[n], rc[n])
    nxt = lambda n, qr, pr, nr, rc, ts, ll: (nr[n], rc[n])
    blk = lambda im: pl.BlockSpec((tq, A_WIDTH), im)
    o, lse = pl.pallas_call(
        _attn_a_kernel,
        grid_spec=pltpu.PrefetchScalarGridSpec(
            num_scalar_prefetch=6, grid=(n_steps,),
            in_specs=[blk(cur), blk(prev), blk(cur), blk(nxt), blk(prev), blk(cur), blk(nxt),
                      pl.BlockSpec(bias.shape, lambda n, *_: (0, 0, 0))],
            out_specs=[blk(cur), blk(cur)]),
        out_shape=[jax.ShapeDtypeStruct((t_all // dil, dil * A_WIDTH), BF16),
                   jax.ShapeDtypeStruct((t_all // dil, dil * A_WIDTH), F32)],
        compiler_params=pltpu.CompilerParams(dimension_semantics=("arbitrary",),
                                             vmem_limit_bytes=VMEM_LIMIT),
        name=f"attn_a_d{dil}",
    )(*tabs, view(qa), view(ka), view(ka), view(ka), view(va), view(va), view(va), bias)
    return o.reshape(t_all, A_WIDTH), lse.reshape(t_all, A_WIDTH)


def _attn_b_kernel(q_ref, k_ref, v_ref, o_ref):
    v = v_ref[...]
    lane = lax.broadcasted_iota(I32, (q_ref.shape[0], LANES), 1)
    outs = []
    for e in range(2):
        sl = slice(e * HEAD_PAD, (e + 1) * HEAD_PAD)
        s = _nt_dot(q_ref[:, sl], k_ref[:, sl])
        m = jnp.max(s, axis=1, keepdims=True)
        p = jnp.exp2(s - m)
        l = jnp.sum(p, axis=1, keepdims=True)
        outs.append(jnp.dot(p.astype(BF16), v, preferred_element_type=F32) / l)
    o_ref[...] = jnp.where(lane < B_V, outs[0], outs[1]).astype(BF16)


def _attn_b(qb, kb, vb, off, n_batch, s_len):
    tq = TQ_B
    nq = s_len // tq
    qbase, kbase = off // tq, off // s_len
    return pl.pallas_call(
        _attn_b_kernel,
        grid=(n_batch, B_HEADS // 2, nq),
        in_specs=[
            pl.BlockSpec((tq, 2 * HEAD_PAD), lambda b, hp, qi: (qbase + b * nq + qi, hp)),
            pl.BlockSpec((s_len, 2 * HEAD_PAD), lambda b, hp, qi: (kbase + b, hp)),
            pl.BlockSpec((s_len, 2 * B_V), lambda b, hp, qi: (kbase + b, hp)),
        ],
        out_specs=pl.BlockSpec((tq, 2 * B_V), lambda b, hp, qi: (b * nq + qi, hp)),
        out_shape=jax.ShapeDtypeStruct((n_batch * s_len, B_HEADS * B_V), BF16),
        compiler_params=pltpu.CompilerParams(dimension_semantics=("arbitrary",) * 3,
                                             vmem_limit_bytes=VMEM_LIMIT),
        name=f"attn_b_s{s_len}",
    )(qb, kb, vb)


def _mix_kernel(xp_ref, xs_ref, o0_ref, o1_ref, o2_ref, l0_ref, l1_ref, l2_ref, obp_ref, obs_ref, wout_ref,
                g_ref, b_ref, wrh_ref, wrl_ref, rb_ref,
                h_ref, tope_ref, pos_ref, gate_ref, cnt_out_ref, cnt_ref, *, n_prompt_tiles):
    i = pl.program_id(0)
    tm = xp_ref.shape[0]

    @pl.when(i == 0)
    def _():
        cnt_ref[...] = jnp.zeros_like(cnt_ref)

    l0, l1, l2 = l0_ref[...], l1_ref[...], l2_ref[...]
    lmax = jnp.maximum(jnp.maximum(l0, l1), l2)
    e0, e1, e2 = jnp.exp2(l0 - lmax), jnp.exp2(l1 - lmax), jnp.exp2(l2 - lmax)
    oa = (e0 * o0_ref[...].astype(F32) + e1 * o1_ref[...].astype(F32)
          + e2 * o2_ref[...].astype(F32)) / (e0 + e1 + e2)
    is_prompt = i < n_prompt_tiles
    ob = jnp.where(is_prompt, obp_ref[...], obs_ref[...])
    mix = (jnp.dot(oa.astype(BF16), wout_ref[0:A_WIDTH, :], preferred_element_type=F32)
           + jnp.dot(ob, wout_ref[A_WIDTH:, :], preferred_element_type=F32))
    x = jnp.where(is_prompt, xp_ref[...], xs_ref[...])
    h = _layer_norm(ALPHA * x + mix, g_ref[...], b_ref[...])
    _slab_store(h_ref, h)

    h_hi = h.astype(BF16)
    h_lo = (h - h_hi.astype(F32)).astype(BF16)
    wrh = wrh_ref[...]
    logits = _nt_dot(wrh, h_hi) + _nt_dot(wrh, h_lo) + _nt_dot(wrl_ref[...], h_hi)
    scores = jax.nn.sigmoid(logits)
    sel = scores + rb_ref[...]

    sub = lax.broadcasted_iota(I32, (GROUP_SIZE, tm), 0).astype(F32)
    gscore = []
    for g in range(N_GROUPS):
        sg = sel[g * GROUP_SIZE:(g + 1) * GROUP_SIZE, :]
        m1 = jnp.max(sg, axis=0, keepdims=True)
        first = jnp.min(jnp.where(sg == m1, sub, float(GROUP_SIZE)), axis=0, keepdims=True)
        m2 = jnp.max(jnp.where(sub == first, -jnp.inf, sg), axis=0, keepdims=True)
        gscore.append(m1 + m2)
    cands = []
    for g in range(N_GROUPS):
        beaten = jnp.zeros((1, tm), F32)
        for g2 in range(N_GROUPS):
            if g2 == g:
                continue
            wins = (gscore[g2] > gscore[g]) | ((gscore[g2] == gscore[g]) & (g2 < g))
            beaten = beaten + wins.astype(F32)
        keep = beaten < float(TOPK_GROUPS)
        sg = sel[g * GROUP_SIZE:(g + 1) * GROUP_SIZE, :]
        cands.append(jnp.where(keep, sg, NEG_BIG))
    cand = jnp.concatenate(cands, axis=0)

    eidx = lax.broadcasted_iota(I32, (N_EXPERTS, tm), 0).astype(F32)
    picked_idx, picked_gate = [], []
    onehot = jnp.zeros((N_EXPERTS, tm), F32)
    for _ in range(TOP_K):
        mx = jnp.max(cand, axis=0, keepdims=True)
        fi = jnp.min(jnp.where(cand == mx, eidx, float(N_EXPERTS)), axis=0, keepdims=True)
        pick = eidx == fi
        picked_idx.append(fi)
        picked_gate.append(jnp.sum(jnp.where(pick, scores, 0.0), axis=0, keepdims=True))
        onehot = onehot + pick.astype(F32)
        cand = jnp.where(pick, -jnp.inf, cand)
    gsum = picked_gate[0]
    for k in range(1, TOP_K):
        gsum = gsum + picked_gate[k]

    tri = (lax.broadcasted_iota(I32, (tm, tm), 0) < lax.broadcasted_iota(I32, (tm, tm), 1))
    before = jnp.dot(onehot.astype(BF16), tri.astype(BF16), preferred_element_type=F32)
    rank = before + cnt_ref[:, 0:1]
    for k in range(TOP_K):
        pick = eidx == picked_idx[k]
        tope_ref[k:k + 1, :] = picked_idx[k].astype(I32)
        pos_ref[k:k + 1, :] = jnp.sum(jnp.where(pick, rank, 0.0), axis=0, keepdims=True).astype(I32)
        gate_ref[k:k + 1, :] = picked_gate[k] / gsum * ROUTED_SCALE
    cnt_ref[...] = cnt_ref[...] + jnp.sum(onehot, axis=1, keepdims=True)
    cnt_out_ref[...] = cnt_ref[...]


def _mix(xp, xs, outs, lses, obp, obs, wout, g, b, wrh, wrl, rb):
    tm = TM_MIX
    n1 = xp.shape[0] // tm
    n2 = xs.shape[0] // tm
    t_all = xp.shape[0] + xs.shape[0]
    full = lambda a: pl.BlockSpec(a.shape, lambda i: (0,) * a.ndim)
    row = lambda cols: pl.BlockSpec((tm, cols), lambda i: (i, 0))
    prow = lambda cols: pl.BlockSpec((tm, cols), lambda i: (jnp.minimum(i, n1 - 1), 0))
    srow = lambda cols: pl.BlockSpec((tm, cols), lambda i: (jnp.maximum(i - n1, 0), 0))
    col = pl.BlockSpec((TOP_K, tm), lambda i: (0, i))
    return pl.pallas_call(
        functools.partial(_mix_kernel, n_prompt_tiles=n1),
        grid=(n1 + n2,),
        in_specs=[
            prow(D_MODEL), srow(D_MODEL),
            row(A_WIDTH), row(A_WIDTH), row(A_WIDTH), row(A_WIDTH), row(A_WIDTH), row(A_WIDTH),
            prow(B_HEADS * B_V), srow(B_HEADS * B_V),
            full(wout), full(g), full(b), full(wrh), full(wrl), full(rb),
        ],
        out_specs=[pl.BlockSpec((SLAB * tm, LANES), lambda i: (i, 0)), col, col, col,
                   pl.BlockSpec((N_EXPERTS, LANES), lambda i: (0, 0))],
        out_shape=[jax.ShapeDtypeStruct((SLAB * t_all, LANES), F32),
                   jax.ShapeDtypeStruct((TOP_K, t_all), I32),
                   jax.ShapeDtypeStruct((TOP_K, t_all), I32),
                   jax.ShapeDtypeStruct((TOP_K, t_all), F32),
                   jax.ShapeDtypeStruct((N_EXPERTS, LANES), F32)],
        scratch_shapes=[pltpu.VMEM((N_EXPERTS, LANES), F32)],
        compiler_params=pltpu.CompilerParams(dimension_semantics=("arbitrary",),
                                             vmem_limit_bytes=VMEM_LIMIT),
        name="mix_router",
    )(xp, xs, *outs, *lses, obp, obs, wout, g, b, wrh, wrl, rb)


def _dispatch_kernel(zstart_ref, nused_ref, h_ref, tope_ref, pos_ref, pstart_ref, buf_ref, dest_ref,
                     dsm_ref, zero_ref, sem_ref, zsem_ref):
    i = pl.program_id(0)
    tm = tope_ref.shape[1]
    tile_rows = SLAB * ROW_BLOCK

    @pl.when(i == 0)
    def _():
        zero_ref[...] = jnp.zeros_like(zero_ref)

        def zfill(e, carry):
            start = pl.multiple_of(zstart_ref[e], SLAB)
            pltpu.make_async_copy(zero_ref, buf_ref.at[pl.ds(start, tile_rows), :], zsem_ref).start()
            return carry

        lax.fori_loop(0, N_EXPERTS, zfill, 0)
        span = buf_ref.at[pl.ds(0, N_EXPERTS * tile_rows), :]
        pltpu.make_async_copy(span, span, zsem_ref).wait()

        def ztail(j, carry):
            start = pl.multiple_of(j * tile_rows, tile_rows)
            pltpu.make_async_copy(zero_ref, buf_ref.at[pl.ds(start, tile_rows), :], zsem_ref).start()
            return carry

        def zwait(j, carry):
            pltpu.make_async_copy(zero_ref, buf_ref.at[pl.ds(0, tile_rows), :], zsem_ref).wait()
            return carry

        n_tiles = buf_ref.shape[0] // tile_rows
        lax.fori_loop(nused_ref[0], n_tiles, ztail, 0)
        lax.fori_loop(nused_ref[0], n_tiles, zwait, 0)

    eidx = lax.broadcasted_iota(I32, (N_EXPERTS, tm), 0)
    pstart = pstart_ref[...]
    for k in range(TOP_K):
        hit = eidx == tope_ref[k:k + 1, :]
        base = jnp.sum(jnp.where(hit, pstart, 0.0), axis=0, keepdims=True)
        dest_ref[k:k + 1, :] = (pos_ref[k:k + 1, :] + base.astype(I32)) * SLAB
    pltpu.sync_copy(dest_ref, dsm_ref)

    def scatter(t, carry):
        src = h_ref.at[pl.ds(pl.multiple_of(t * SLAB, SLAB), SLAB), :]
        for k in range(TOP_K):
            dst = buf_ref.at[pl.ds(pl.multiple_of(dsm_ref[k, t], SLAB), SLAB), :]
            pltpu.make_async_copy(src, dst, sem_ref).start()
        return carry

    lax.fori_loop(0, tm, scatter, 0, unroll=8)
    span = buf_ref.at[pl.ds(0, TOP_K * tm * SLAB), :]
    pltpu.make_async_copy(span, span, sem_ref).wait()


def _dispatch(h, tope, pos, pstart, zstart, nused, n_rows):
    tm = TM_DISP
    t_all = tope.shape[1]
    col = pl.BlockSpec((TOP_K, tm), lambda i, z, nu: (0, i))
    return pl.pallas_call(
        _dispatch_kernel,
        grid_spec=pltpu.PrefetchScalarGridSpec(
            num_scalar_prefetch=2, grid=(t_all // tm,),
            in_specs=[pl.BlockSpec((SLAB * tm, LANES), lambda i, z, nu: (i, 0)), col, col,
                      pl.BlockSpec((N_EXPERTS, 1), lambda i, z, nu: (0, 0))],
            out_specs=[pl.BlockSpec(memory_space=pl.ANY), col],
            scratch_shapes=[pltpu.SMEM((TOP_K, tm), I32),
                            pltpu.VMEM((SLAB * ROW_BLOCK, LANES), F32),
                            pltpu.SemaphoreType.DMA, pltpu.SemaphoreType.DMA]),
        out_shape=[jax.ShapeDtypeStruct((SLAB * n_rows, LANES), F32),
                   jax.ShapeDtypeStruct((TOP_K, t_all), I32)],
        compiler_params=pltpu.CompilerParams(dimension_semantics=("arbitrary",),
                                             vmem_limit_bytes=VMEM_LIMIT),
        name="dispatch",
    )(zstart, nused, h, tope, pos, pstart)


def _expert_kernel(texp_ref, nused_ref, x_ref, wg_ref, wu_ref, wd_ref, o_ref, wgb_ref, wub_ref, wdb_ref):
    i = pl.program_id(0)

    @pl.when(i < nused_ref[0])
    def _():
        e = texp_ref[i]

        @pl.when((i == 0) | (e != texp_ref[jnp.maximum(i - 1, 0)]))
        def _():
            wgb_ref[...] = wg_ref[...].astype(BF16)
            wub_ref[...] = wu_ref[...].astype(BF16)
            wdb_ref[...] = wd_ref[...].astype(BF16)

        x = _slab_load(x_ref, ROW_BLOCK).astype(BF16)
        g = jnp.dot(x, wgb_ref[...], preferred_element_type=F32)
        u = jnp.dot(x, wub_ref[...], preferred_element_type=F32)
        hmid = (g * jax.nn.sigmoid(g) * u).astype(BF16)
        _slab_store(o_ref, jnp.dot(hmid, wdb_ref[...], preferred_element_type=F32))


def _experts(buf, texp, nused, w_gate, w_up, w_down):
    rb = ROW_BLOCK
    n_tiles = buf.shape[0] // (SLAB * rb)
    xmap = lambda i, te, nu: (jnp.minimum(i, nu[0] - 1), 0)
    wmap = lambda i, te, nu: (te[i], 0, 0)
    return pl.pallas_call(
        _expert_kernel,
        grid_spec=pltpu.PrefetchScalarGridSpec(
            num_scalar_prefetch=2, grid=(n_tiles,),
            in_specs=[pl.BlockSpec((SLAB * rb, LANES), xmap),
                      pl.BlockSpec((None, D_MODEL, EXPERT_HIDDEN), wmap),
                      pl.BlockSpec((None, D_MODEL, EXPERT_HIDDEN), wmap),
                      pl.BlockSpec((None, EXPERT_HIDDEN, D_MODEL), wmap)],
            out_specs=pl.BlockSpec((SLAB * rb, LANES), xmap),
            scratch_shapes=[pltpu.VMEM((D_MODEL, EXPERT_HIDDEN), BF16),
                            pltpu.VMEM((D_MODEL, EXPERT_HIDDEN), BF16),
                            pltpu.VMEM((EXPERT_HIDDEN, D_MODEL), BF16)]),
        out_shape=jax.ShapeDtypeStruct(buf.shape, F32),
        input_output_aliases={2: 0},
        compiler_params=pltpu.CompilerParams(dimension_semantics=("arbitrary",),
                                             vmem_limit_bytes=VMEM_LIMIT),
        name="experts",
    )(texp, nused, buf, w_gate, w_up, w_down)


def _combine_kernel(h_ref, gate_ref, dest_ref, eo_ref, wsg_ref, wsu_ref, wsd_ref, g_ref, b_ref,
                    yp_ref, ys_ref, dsm_ref, rows_ref, sem_ref, *, n_prompt_tiles):
    i = pl.program_id(0)
    tm = gate_ref.shape[0]
    pltpu.sync_copy(dest_ref, dsm_ref)

    def gather(t, carry):
        for k in range(TOP_K):
            src = eo_ref.at[pl.ds(pl.multiple_of(dsm_ref[k, t], SLAB), SLAB), :]
            dst = rows_ref.at[pl.ds(pl.multiple_of((k * tm + t) * SLAB, SLAB), SLAB), :]
            pltpu.make_async_copy(src, dst, sem_ref).start()
        return carry

    lax.fori_loop(0, tm, gather, 0, unroll=8)

    h = _slab_load(h_ref, tm)
    hb = h.astype(BF16)
    sg = jnp.dot(hb, wsg_ref[...], preferred_element_type=F32)
    su = jnp.dot(hb, wsu_ref[...], preferred_element_type=F32)
    shared = jnp.dot((sg * jax.nn.sigmoid(sg) * su).astype(BF16), wsd_ref[...],
                     preferred_element_type=F32)

    pltpu.make_async_copy(eo_ref.at[pl.ds(0, TOP_K * tm * SLAB), :], rows_ref, sem_ref).wait()
    gate = gate_ref[...]
    routed = gate[:, 0:1] * _slab_load(rows_ref, tm)
    for k in range(1, TOP_K):
        routed = routed + gate[:, k:k + 1] * _slab_load(rows_ref, tm, k * tm)
    y = _layer_norm(ALPHA * h + (routed + shared), g_ref[...], b_ref[...])

    @pl.when(i < n_prompt_tiles)
    def _():
        yp_ref[...] = y

    @pl.when(i >= n_prompt_tiles)
    def _():
        ys_ref[...] = y


def _combine(h, gate_t, dest, eo, wsg, wsu, wsd, g, b, t_prompt):
    tm = TM_COMB
    t_all = gate_t.shape[0]
    n1 = t_prompt // tm
    n2 = (t_all - t_prompt) // tm
    full = lambda a: pl.BlockSpec(a.shape, lambda i: (0,) * a.ndim)
    return pl.pallas_call(
        functools.partial(_combine_kernel, n_prompt_tiles=n1),
        grid=(n1 + n2,),
        in_specs=[pl.BlockSpec((SLAB * tm, LANES), lambda i: (i, 0)),
                  pl.BlockSpec((tm, TOP_K), lambda i: (i, 0)),
                  pl.BlockSpec((TOP_K, tm), lambda i: (0, i)),
                  pl.BlockSpec(memory_space=pl.ANY),
                  full(wsg), full(wsu), full(wsd), full(g), full(b)],
        out_specs=[pl.BlockSpec((tm, D_MODEL), lambda i: (jnp.minimum(i, n1 - 1), 0)),
                   pl.BlockSpec((tm, D_MODEL), lambda i: (jnp.maximum(i - n1, 0), 0))],
        out_shape=[jax.ShapeDtypeStruct((t_prompt, D_MODEL), F32),
                   jax.ShapeDtypeStruct((t_all - t_prompt, D_MODEL), F32)],
        scratch_shapes=[pltpu.SMEM((TOP_K, tm), I32),
                        pltpu.VMEM((TOP_K * tm * SLAB, LANES), F32),
                        pltpu.SemaphoreType.DMA],
        compiler_params=pltpu.CompilerParams(dimension_semantics=("arbitrary",),
                                             vmem_limit_bytes=VMEM_LIMIT),
        name="combine",
    )(h, gate_t, dest, eo, wsg, wsu, wsd, g, b)


def _rope_tables(s_max):
    inv_freq = ROPE_BASE ** (-jnp.arange(0, B_ROPE, 2, dtype=F32) / B_ROPE)
    ang = jnp.arange(s_max, dtype=F32)[:, None] * inv_freq[None, :]
    cos, sin = jnp.cos(ang), jnp.sin(ang)
    ones = jnp.ones((s_max, B_NOPE), F32)
    zeros_n = jnp.zeros((s_max, B_NOPE), F32)
    zeros_p = jnp.zeros((s_max, HEAD_PAD - B_NOPE - B_ROPE), F32)
    return (jnp.concatenate([ones, cos, cos, zeros_p], axis=1),
            jnp.concatenate([zeros_n, sin, sin, zeros_p], axis=1))


def _rot_cols(w):
    half = B_ROPE // 2
    return jnp.concatenate([-w[..., half:], w[..., :half]], axis=-1)


def _layout_weights(w_in, w_uq, w_uk):
    c_kr = 3 * A_WIDTH + Q_LORA + KV_LORA
    w_kr = w_in[:, c_kr:c_kr + B_ROPE]
    pad_l = jnp.zeros((D_MODEL, B_NOPE), F32)
    pad_r = jnp.zeros((D_MODEL, HEAD_PAD - B_NOPE - B_ROPE), F32)
    w1 = jnp.concatenate([w_in[:, :c_kr], pad_l, w_kr, pad_r, pad_l, _rot_cols(w_kr), pad_r], axis=1)
    wq3 = w_uq.reshape(Q_LORA, B_HEADS, B_NOPE + B_ROPE)
    nope, rope = wq3[..., :B_NOPE], wq3[..., B_NOPE:]
    zpad = jnp.zeros((Q_LORA, B_HEADS, HEAD_PAD - B_NOPE - B_ROPE), F32)
    wq = jnp.concatenate([nope, rope, zpad], axis=-1).reshape(Q_LORA, B_HEADS * HEAD_PAD)
    wqr = jnp.concatenate([jnp.zeros_like(nope), _rot_cols(rope), zpad], axis=-1)
    wqr = wqr.reshape(Q_LORA, B_HEADS * HEAD_PAD)
    wk3 = w_uk.reshape(KV_LORA, B_HEADS, B_NOPE)
    wuk = jnp.concatenate([wk3, jnp.zeros((KV_LORA, B_HEADS, HEAD_PAD - B_NOPE), F32)], axis=-1)
    wuk = wuk.reshape(KV_LORA, B_HEADS * HEAD_PAD)
    return w1.astype(BF16), wq.astype(BF16), wqr.astype(BF16), wuk.astype(BF16)


def _forward(x_prompt, x_sample, w_in, w_out, ln1_g, ln1_b, q_norm_g, w_uq, kv_norm_g, w_uk, w_uv,
             w_router, router_bias, w_gate, w_up, w_down, ws_gate, ws_up, ws_down, ln2_g, ln2_b):
    b1, s1, _ = x_prompt.shape
    b2, s2, _ = x_sample.shape
    t1, t2 = b1 * s1, b2 * s2
    t_all = t1 + t2
    xp = x_prompt.reshape(t1, D_MODEL)
    xs = x_sample.reshape(t2, D_MODEL)
    seqs = [(b * s1, s1) for b in range(b1)] + [(t1 + b * s2, s2) for b in range(b2)]
    assert t1 % s2 == 0 and s1 % (TQ_A * 16) == 0 and s2 % (TQ_A * 16) == 0

    w1, wq, wqr, wuk = _layout_weights(w_in, w_uq, w_uk)
    cos_t, sin_t = _rope_tables(max(s1, s2))
    qa, ka, va, qb, kb, vb = _proj(xp, xs, w1, q_norm_g.reshape(1, -1), kv_norm_g.reshape(1, -1),
                                   wq, wqr, wuk, w_uv.astype(BF16), cos_t, sin_t, s1, s2)

    outs, lses = [], []
    for _, dil in A_PATTERNS:
        o, lse = _attn_a(qa, ka, va, seqs, dil)
        outs.append(o)
        lses.append(lse)
    obp = _attn_b(qb, kb, vb, 0, b1, s1)
    obs = _attn_b(qb, kb, vb, t1, b2, s2)

    wr_t = w_router.T
    wr_hi = wr_t.astype(BF16)
    wr_lo = (wr_t - wr_hi.astype(F32)).astype(BF16)
    h, tope, pos, gate, counts = _mix(xp, xs, outs, lses, obp, obs, w_out.astype(BF16),
                                      ln1_g.reshape(1, -1), ln1_b.reshape(1, -1),
                                      wr_hi, wr_lo, router_bias.reshape(-1, 1))

    cnt = counts[:, 0].astype(I32)
    padded = ((cnt + ROW_BLOCK - 1) // ROW_BLOCK) * ROW_BLOCK
    pend = jnp.cumsum(padded)
    pstart = pend - padded
    n_tiles = (t_all * TOP_K) // ROW_BLOCK + N_EXPERTS + 1
    tile_start = jnp.arange(n_tiles, dtype=I32) * ROW_BLOCK
    texp = jnp.sum((pend[None, :] <= tile_start[:, None]).astype(I32), axis=1)
    texp = jnp.minimum(texp, N_EXPERTS - 1).astype(I32)
    nused = (pend[-1:] // ROW_BLOCK).astype(I32)

    buf, dest = _dispatch(h, tope, pos, pstart.astype(F32).reshape(-1, 1),
                          ((pstart + cnt) * SLAB).astype(I32), nused, n_tiles * ROW_BLOCK)
    eo = _experts(buf, texp, nused, w_gate, w_up, w_down)
    yp, ys = _combine(h, gate.T, dest, eo, ws_gate.astype(BF16), ws_up.astype(BF16),
                      ws_down.astype(BF16), ln2_g.reshape(1, -1), ln2_b.reshape(1, -1), t1)
    return yp.reshape(b1, s1, D_MODEL), ys.reshape(b2, s2, D_MODEL)


def kernel(x_prompt, x_sample, w_in, w_out, ln1_g, ln1_b, q_norm_g, w_uq, kv_norm_g, w_uk, w_uv,
           w_router, router_bias, w_gate, w_up, w_down, ws_gate, ws_up, ws_down, ln2_g, ln2_b):
    params = (w_in, w_out, ln1_g, ln1_b, q_norm_g, w_uq, kv_norm_g, w_uk, w_uv, w_router, router_bias,
              w_gate, w_up, w_down, ws_gate, ws_up, ws_down, ln2_g, ln2_b)
    assert all(p.shape[0] == 1 for p in params), "one encoder layer"
    return _forward(x_prompt, x_sample, *[p.reshape(p.shape[1:]) for p in params])
```

```python
import functools
import math

import numpy as np
import jax
import jax.numpy as jnp
from jax import lax
from jax.experimental import pallas as pl
from jax.experimental.pallas import tpu as pltpu

F32 = jnp.float32
BF16 = jnp.bfloat16
I32 = jnp.int32

D_MODEL = 1024
A_HEADS = 8
A_HEAD_DIM = 64
A_WIDTH = A_HEADS * A_HEAD_DIM
A_PATTERNS = ((128, 1), (512, 4), (2048, 16))
A_HALF = 64
B_HEADS = 8
B_NOPE = 64
B_ROPE = 32
B_V = 64
Q_LORA = 256
KV_LORA = 128
ROPE_BASE = 10000.0
N_EXPERTS = 256
TOP_K = 8
N_GROUPS = 8
GROUP_SIZE = N_EXPERTS // N_GROUPS
TOPK_GROUPS = 4
EXPERT_HIDDEN = 256
ROUTED_SCALE = 2.5
LN_EPS = 1e-5
RMS_EPS = 1e-6
NEG_BIG = -1e30
ALPHA = 2.0 ** 0.25
LOG2E = 1.4426950408889634

LANES = 128
HEAD_PAD = 128
SLAB = D_MODEL // LANES
W1_COLS = 3 * A_WIDTH + Q_LORA + KV_LORA + 2 * LANES

TM_PROJ = 256
TQ_A = 128
TQ_B = 256
TM_MIX = 256
TM_DISP = 256
ROW_BLOCK = 256
TM_COMB = 128
N_XBUF = 3
N_OBUF = 2
VMEM_LIMIT = 48 * 1024 * 1024


def _slab_load(ref, n_tok, tok0=0):
    return jnp.concatenate([ref[pl.ds(tok0 * SLAB + c, n_tok, stride=SLAB), :] for c in range(SLAB)],
                           axis=1)


def _slab_store(ref, val):
    for c in range(SLAB):
        ref[pl.ds(c, val.shape[0], stride=SLAB), :] = val[:, c * LANES:(c + 1) * LANES]


def _nt_dot(a, b):
    return lax.dot_general(a, b, (((1,), (1,)), ((), ())), preferred_element_type=F32)


def _layer_norm(x, g, b):
    mu = jnp.mean(x, axis=-1, keepdims=True)
    xc = x - mu
    var = jnp.mean(xc * xc, axis=-1, keepdims=True)
    return xc * lax.rsqrt(var + LN_EPS) * g + b


def _rms_norm(x, g):
    return x * lax.rsqrt(jnp.mean(x * x, axis=-1, keepdims=True) + RMS_EPS) * g


def _proj_kernel(xp_ref, xs_ref, w1_ref, qg_ref, kvg_ref, wq_ref, wqr_ref, wuk_ref, wuv_ref,
                 cos_ref, sin_ref, qa_ref, ka_ref, va_ref, qb_ref, kb_ref, vb_ref, *, n_prompt_tiles):
    i = pl.program_id(0)
    x = jnp.where(i < n_prompt_tiles, xp_ref[...], xs_ref[...]).astype(BF16)
    p = jnp.dot(x, w1_ref[...], preferred_element_type=F32)
    qa_ref[...] = (p[:, 0:A_WIDTH] * (A_HEAD_DIM ** -0.5 * LOG2E)).astype(BF16)
    ka_ref[...] = p[:, A_WIDTH:2 * A_WIDTH].astype(BF16)
    va_ref[...] = p[:, 2 * A_WIDTH:3 * A_WIDTH].astype(BF16)
    c0 = 3 * A_WIDTH
    cq = p[:, c0:c0 + Q_LORA]
    ckv = p[:, c0 + Q_LORA:c0 + Q_LORA + KV_LORA]
    kr = p[:, c0 + Q_LORA + KV_LORA:c0 + Q_LORA + KV_LORA + LANES]
    krr = p[:, c0 + Q_LORA + KV_LORA + LANES:c0 + Q_LORA + KV_LORA + 2 * LANES]
    cos = cos_ref[...]
    sin = sin_ref[...]
    cos8 = jnp.concatenate([cos] * B_HEADS, axis=1)
    sin8 = jnp.concatenate([sin] * B_HEADS, axis=1)
    cqn = _rms_norm(cq, qg_ref[...]).astype(BF16)
    q = jnp.dot(cqn, wq_ref[...], preferred_element_type=F32)
    qr = jnp.dot(cqn, wqr_ref[...], preferred_element_type=F32)
    qscale = (B_NOPE + B_ROPE) ** -0.5 * LOG2E
    qb_ref[...] = ((q * cos8 + qr * sin8) * qscale).astype(BF16)
    ckvn = _rms_norm(ckv, kvg_ref[...]).astype(BF16)
    kn = jnp.dot(ckvn, wuk_ref[...], preferred_element_type=F32)
    krope = kr * cos + krr * sin
    kb_ref[...] = (kn + jnp.concatenate([krope] * B_HEADS, axis=1)).astype(BF16)
    vb_ref[...] = jnp.dot(ckvn, wuv_ref[...], preferred_element_type=F32).astype(BF16)


def _proj(xp, xs, w1, qg, kvg, wq, wqr, wuk, wuv, cos_t, sin_t, s_prompt, s_sample):
    tm = TM_PROJ
    n1 = xp.shape[0] // tm
    n2 = xs.shape[0] // tm
    t_all = xp.shape[0] + xs.shape[0]
    pt, st = s_prompt // tm, s_sample // tm

    def tab_idx(i):
        return (jnp.where(i < n1, i % pt, (i - n1) % st), 0)

    full = lambda shape: pl.BlockSpec(shape, lambda i: (0, 0))
    row = lambda cols: pl.BlockSpec((tm, cols), lambda i: (i, 0))
    return pl.pallas_call(
        functools.partial(_proj_kernel, n_prompt_tiles=n1),
        grid=(n1 + n2,),
        in_specs=[
            pl.BlockSpec((tm, D_MODEL), lambda i: (jnp.minimum(i, n1 - 1), 0)),
            pl.BlockSpec((tm, D_MODEL), lambda i: (jnp.maximum(i - n1, 0), 0)),
            full(w1.shape), full(qg.shape), full(kvg.shape), full(wq.shape), full(wqr.shape),
            full(wuk.shape), full(wuv.shape),
            pl.BlockSpec((tm, LANES), tab_idx), pl.BlockSpec((tm, LANES), tab_idx),
        ],
        out_specs=[row(A_WIDTH), row(A_WIDTH), row(A_WIDTH), row(B_HEADS * HEAD_PAD),
                   row(B_HEADS * HEAD_PAD), row(B_HEADS * B_V)],
        out_shape=[jax.ShapeDtypeStruct((t_all, A_WIDTH), BF16)] * 3
        + [jax.ShapeDtypeStruct((t_all, B_HEADS * HEAD_PAD), BF16)] * 2
        + [jax.ShapeDtypeStruct((t_all, B_HEADS * B_V), BF16)],
        compiler_params=pltpu.CompilerParams(dimension_semantics=("arbitrary",),
                                             vmem_limit_bytes=VMEM_LIMIT),
        name="proj",
    )(xp, xs, w1, qg, kvg, wq, wqr, wuk, wuv, cos_t, sin_t)


def _attn_a_kernel(qrow_ref, prow_ref, nrow_ref, rcol_ref, tstart_ref, llen_ref,
                   q_ref, kp_ref, kc_ref, kn_ref, vp_ref, vc_ref, vn_ref, bias_ref, o_ref, lse_ref):
    n = pl.program_id(0)
    tq = TQ_A
    kw = jnp.concatenate([kp_ref[tq - A_HALF:, :], kc_ref[...], kn_ref[:A_HALF, :]], axis=0)
    vw = jnp.concatenate([vp_ref[tq - A_HALF:, :], vc_ref[...], vn_ref[:A_HALF, :]], axis=0)
    wk = tq + 2 * A_HALF
    col = lax.broadcasted_iota(I32, (1, wk), 1) + (tstart_ref[n] - A_HALF)
    colpen = jnp.where((col >= 0) & (col < llen_ref[n]), 0.0, NEG_BIG).astype(F32)
    lane = lax.broadcasted_iota(I32, (tq, LANES), 1)
    low = lane < A_HEAD_DIM
    q = q_ref[...]
    for j in range(A_HEADS // 2):
        sl = slice(j * LANES, (j + 1) * LANES)
        qp, kpair, vpair = q[:, sl], kw[:, sl], vw[:, sl]
        outs, lses = [], []
        for e in range(2):
            qm = jnp.where(low if e == 0 else ~low, qp, jnp.zeros_like(qp))
            s = _nt_dot(qm, kpair) + bias_ref[2 * j + e] + colpen
            m = jnp.max(s, axis=1, keepdims=True)
            p = jnp.exp2(s - m)
            l = jnp.sum(p, axis=1, keepdims=True)
            o = jnp.dot(p.astype(BF16), vpair, preferred_element_type=F32)
            outs.append(o / l)
            lses.append(m + jnp.log2(l))
        o_ref[:, sl] = jnp.where(low, outs[0], outs[1]).astype(BF16)
        lse_ref[:, sl] = jnp.where(low, lses[0], lses[1])


def _attn_a_tables(seqs, dil):
    tq = TQ_A
    qrow, prow, nrow, rcol, tstart, llen = [], [], [], [], [], []
    for off, s_len in seqs:
        cls = s_len // dil
        nt = cls // tq
        base = (off // dil) // tq
        for r in range(dil):
            for i in range(nt):
                qrow.append(base + i)
                prow.append(base + max(i - 1, 0))
                nrow.append(base + min(i + 1, nt - 1))
                rcol.append(r)
                tstart.append(i * tq)
                llen.append(cls)
    return [jnp.asarray(np.asarray(a, np.int32)) for a in (qrow, prow, nrow, rcol, tstart, llen)]


def _attn_a_bias(dil):
    tq, wk = TQ_A, TQ_A + 2 * A_HALF
    delta = np.abs(np.arange(wk)[None, :] - A_HALF - np.arange(tq)[:, None]).astype(np.float64)
    slopes = 2.0 ** (-8.0 * (np.arange(A_HEADS) + 1.0) / A_HEADS)
    bias = -slopes[:, None, None] * (delta * dil)[None] * LOG2E
    bias = np.where((delta <= A_HALF)[None], bias, NEG_BIG)
    return jnp.asarray(bias.astype(np.float32))


def _attn_a(qa, ka, va, seqs, dil):
    t_all = qa.shape[0]
    tq = TQ_A
    view = lambda a: a.reshape(t_all // dil, dil * A_WIDTH)
    tabs = _attn_a_tables(seqs, dil)
    n_steps = int(tabs[0].shape[0])
    bias = _attn_a_bias(dil)
    cur = lambda n, qr, pr, nr, rc, ts, ll: (qr[n], rc[n])
    prev = lambda n, qr, pr, nr, rc, ts, ll: (pr[n], rc[n])
    nxt = lambda n, qr, pr, nr, rc, ts, ll: (nr[n], rc[n])
    blk = lambda im: pl.BlockSpec((tq, A_WIDTH), im)
    o, lse = pl.pallas_call(
        _attn_a_kernel,
        grid_spec=pltpu.PrefetchScalarGridSpec(
            num_scalar_prefetch=6, grid=(n_steps,),
            in_specs=[blk(cur), blk(prev), blk(cur), blk(nxt), blk(prev), blk(cur), blk(nxt),
                      pl.BlockSpec(bias.shape, lambda n, *_: (0, 0, 0))],
            out_specs=[blk(cur), blk(cur)]),
        out_shape=[jax.ShapeDtypeStruct((t_all // dil, dil * A_WIDTH), BF16),
                   jax.ShapeDtypeStruct((t_all // dil, dil * A_WIDTH), F32)],
        compiler_params=pltpu.CompilerParams(dimension_semantics=("arbitrary",),
                                             vmem_limit_bytes=VMEM_LIMIT),
        name=f"attn_a_d{dil}",
    )(*tabs, view(qa), view(ka), view(ka), view(ka), view(va), view(va), view(va), bias)
    return o.reshape(t_all, A_WIDTH), lse.reshape(t_all, A_WIDTH)


def _attn_b_kernel(q_ref, k_ref, v_ref, o_ref):
    v = v_ref[...]
    lane = lax.broadcasted_iota(I32, (q_ref.shape[0], LANES), 1)
    outs = []
    for e in range(2):
        sl = slice(e * HEAD_PAD, (e + 1) * HEAD_PAD)
        s = _nt_dot(q_ref[:, sl], k_ref[:, sl])
        m = jnp.max(s, axis=1, keepdims=True)
        p = jnp.exp2(s - m)
        l = jnp.sum(p, axis=1, keepdims=True)
        outs.append(jnp.dot(p.astype(BF16), v, preferred_element_type=F32) / l)
    o_ref[...] = jnp.where(lane < B_V, outs[0], outs[1]).astype(BF16)


def _attn_b(qb, kb, vb, off, n_batch, s_len):
    tq = TQ_B
    nq = s_len // tq
    qbase, kbase = off // tq, off // s_len
    return pl.pallas_call(
        _attn_b_kernel,
        grid=(n_batch, B_HEADS // 2, nq),
        in_specs=[
            pl.BlockSpec((tq, 2 * HEAD_PAD), lambda b, hp, qi: (qbase + b * nq + qi, hp)),
            pl.BlockSpec((s_len, 2 * HEAD_PAD), lambda b, hp, qi: (kbase + b, hp)),
            pl.BlockSpec((s_len, 2 * B_V), lambda b, hp, qi: (kbase + b, hp)),
        ],
        out_specs=pl.BlockSpec((tq, 2 * B_V), lambda b, hp, qi: (b * nq + qi, hp)),
        out_shape=jax.ShapeDtypeStruct((n_batch * s_len, B_HEADS * B_V), BF16),
        compiler_params=pltpu.CompilerParams(dimension_semantics=("arbitrary",) * 3,
                                             vmem_limit_bytes=VMEM_LIMIT),
        name=f"attn_b_s{s_len}",
    )(qb, kb, vb)


def _mix_kernel(xp_ref, xs_ref, o0_ref, o1_ref, o2_ref, l0_ref, l1_ref, l2_ref, obp_ref, obs_ref, wout_ref,
                g_ref, b_ref, wrh_ref, wrl_ref, rb_ref,
                h_ref, tope_ref, pos_ref, gate_ref, cnt_out_ref, cnt_ref, *, n_prompt_tiles):
    i = pl.program_id(0)
    tm = xp_ref.shape[0]

    @pl.when(i == 0)
    def _():
        cnt_ref[...] = jnp.zeros_like(cnt_ref)

    l0, l1, l2 = l0_ref[...], l1_ref[...], l2_ref[...]
    lmax = jnp.maximum(jnp.maximum(l0, l1), l2)
    e0, e1, e2 = jnp.exp2(l0 - lmax), jnp.exp2(l1 - lmax), jnp.exp2(l2 - lmax)
    oa = (e0 * o0_ref[...].astype(F32) + e1 * o1_ref[...].astype(F32)
          + e2 * o2_ref[...].astype(F32)) / (e0 + e1 + e2)
    is_prompt = i < n_prompt_tiles
    ob = jnp.where(is_prompt, obp_ref[...], obs_ref[...])
    mix = (jnp.dot(oa.astype(BF16), wout_ref[0:A_WIDTH, :], preferred_element_type=F32)
           + jnp.dot(ob, wout_ref[A_WIDTH:, :], preferred_element_type=F32))
    x = jnp.where(is_prompt, xp_ref[...], xs_ref[...])
    h = _layer_norm(ALPHA * x + mix, g_ref[...], b_ref[...])
    _slab_store(h_ref, h)

    h_hi = h.astype(BF16)
    h_lo = (h - h_hi.astype(F32)).astype(BF16)
    wrh = wrh_ref[...]
    logits = _nt_dot(wrh, h_hi) + _nt_dot(wrh, h_lo) + _nt_dot(wrl_ref[...], h_hi)
    scores = jax.nn.sigmoid(logits)
    sel = scores + rb_ref[...]

    sub = lax.broadcasted_iota(I32, (GROUP_SIZE, tm), 0).astype(F32)
    gscore = []
    for g in range(N_GROUPS):
        sg = sel[g * GROUP_SIZE:(g + 1) * GROUP_SIZE, :]
        m1 = jnp.max(sg, axis=0, keepdims=True)
        first = jnp.min(jnp.where(sg == m1, sub, float(GROUP_SIZE)), axis=0, keepdims=True)
        m2 = jnp.max(jnp.where(sub == first, -jnp.inf, sg), axis=0, keepdims=True)
        gscore.append(m1 + m2)
    cands = []
    for g in range(N_GROUPS):
        beaten = jnp.zeros((1, tm), F32)
        for g2 in range(N_GROUPS):
            if g2 == g:
                continue
            wins = (gscore[g2] > gscore[g]) | ((gscore[g2] == gscore[g]) & (g2 < g))
            beaten = beaten + wins.astype(F32)
        keep = beaten < float(TOPK_GROUPS)
        sg = sel[g * GROUP_SIZE:(g + 1) * GROUP_SIZE, :]
        cands.append(jnp.where(keep, sg, NEG_BIG))
    cand = jnp.concatenate(cands, axis=0)

    eidx = lax.broadcasted_iota(I32, (N_EXPERTS, tm), 0).astype(F32)
    picked_idx, picked_gate = [], []
    onehot = jnp.zeros((N_EXPERTS, tm), F32)
    for _ in range(TOP_K):
        mx = jnp.max(cand, axis=0, keepdims=True)
        fi = jnp.min(jnp.where(cand == mx, eidx, float(N_EXPERTS)), axis=0, keepdims=True)
        pick = eidx == fi
        picked_idx.append(fi)
        picked_gate.append(jnp.sum(jnp.where(pick, scores, 0.0), axis=0, keepdims=True))
        onehot = onehot + pick.astype(F32)
        cand = jnp.where(pick, -jnp.inf, cand)
    gsum = picked_gate[0]
    for k in range(1, TOP_K):
        gsum = gsum + picked_gate[k]

    tri = (lax.broadcasted_iota(I32, (tm, tm), 0) < lax.broadcasted_iota(I32, (tm, tm), 1))
    before = jnp.dot(onehot.astype(BF16), tri.astype(BF16), preferred_element_type=F32)
    rank = before + cnt_ref[:, 0:1]
    for k in range(TOP_K):
        pick = eidx == picked_idx[k]
        tope_ref[k:k + 1, :] = picked_idx[k].astype(I32)
        pos_ref[k:k + 1, :] = jnp.sum(jnp.where(pick, rank, 0.0), axis=0, keepdims=True).astype(I32)
        gate_ref[k:k + 1, :] = picked_gate[k] / gsum * ROUTED_SCALE
    cnt_ref[...] = cnt_ref[...] + jnp.sum(onehot, axis=1, keepdims=True)
    cnt_out_ref[...] = cnt_ref[...]


def _mix(xp, xs, outs, lses, obp, obs, wout, g, b, wrh, wrl, rb):
    tm = TM_MIX
    n1 = xp.shape[0] // tm
    n2 = xs.shape[0] // tm
    t_all = xp.shape[0] + xs.shape[0]
    full = lambda a: pl.BlockSpec(a.shape, lambda i: (0,) * a.ndim)
    row = lambda cols: pl.BlockSpec((tm, cols), lambda i: (i, 0))
    prow = lambda cols: pl.BlockSpec((tm, cols), lambda i: (jnp.minimum(i, n1 - 1), 0))
    srow = lambda cols: pl.BlockSpec((tm, cols), lambda i: (jnp.maximum(i - n1, 0), 0))
    col = pl.BlockSpec((TOP_K, tm), lambda i: (0, i))
    return pl.pallas_call(
        functools.partial(_mix_kernel, n_prompt_tiles=n1),
        grid=(n1 + n2,),
        in_specs=[
            prow(D_MODEL), srow(D_MODEL),
            row(A_WIDTH), row(A_WIDTH), row(A_WIDTH), row(A_WIDTH), row(A_WIDTH), row(A_WIDTH),
            prow(B_HEADS * B_V), srow(B_HEADS * B_V),
            full(wout), full(g), full(b), full(wrh), full(wrl), full(rb),
        ],
        out_specs=[pl.BlockSpec((SLAB * tm, LANES), lambda i: (i, 0)), col, col, col,
                   pl.BlockSpec((N_EXPERTS, LANES), lambda i: (0, 0))],
        out_shape=[jax.ShapeDtypeStruct((SLAB * t_all, LANES), F32),
                   jax.ShapeDtypeStruct((TOP_K, t_all), I32),
                   jax.ShapeDtypeStruct((TOP_K, t_all), I32),
                   jax.ShapeDtypeStruct((TOP_K, t_all), F32),
                   jax.ShapeDtypeStruct((N_EXPERTS, LANES), F32)],
        scratch_shapes=[pltpu.VMEM((N_EXPERTS, LANES), F32)],
        compiler_params=pltpu.CompilerParams(dimension_semantics=("arbitrary",),
                                             vmem_limit_bytes=VMEM_LIMIT),
        name="mix_router",
    )(xp, xs, *outs, *lses, obp, obs, wout, g, b, wrh, wrl, rb)


def _dispatch_kernel(zstart_ref, nused_ref, h_ref, tope_ref, pos_ref, pstart_ref, buf_ref, dest_ref,
                     dsm_ref, zero_ref, sem_ref, zsem_ref):
    i = pl.program_id(0)
    tm = tope_ref.shape[1]
    tile_rows = SLAB * ROW_BLOCK

    @pl.when(i == 0)
    def _():
        zero_ref[...] = jnp.zeros_like(zero_ref)

        def zfill(e, carry):
            start = pl.multiple_of(zstart_ref[e], SLAB)
            pltpu.make_async_copy(zero_ref, buf_ref.at[pl.ds(start, tile_rows), :], zsem_ref).start()
            return carry

        lax.fori_loop(0, N_EXPERTS, zfill, 0)
        span = buf_ref.at[pl.ds(0, N_EXPERTS * tile_rows), :]
        pltpu.make_async_copy(span, span, zsem_ref).wait()

        def ztail(j, carry):
            start = pl.multiple_of(j * tile_rows, tile_rows)
            pltpu.make_async_copy(zero_ref, buf_ref.at[pl.ds(start, tile_rows), :], zsem_ref).start()
            return carry

        def zwait(j, carry):
            pltpu.make_async_copy(zero_ref, buf_ref.at[pl.ds(0, tile_rows), :], zsem_ref).wait()
            return carry

        n_tiles = buf_ref.shape[0] // tile_rows
        lax.fori_loop(nused_ref[0], n_tiles, ztail, 0)
        lax.fori_loop(nused_ref[0], n_tiles, zwait, 0)

    eidx = lax.broadcasted_iota(I32, (N_EXPERTS, tm), 0)
    pstart = pstart_ref[...]
    for k in range(TOP_K):
        hit = eidx == tope_ref[k:k + 1, :]
        base = jnp.sum(jnp.where(hit, pstart, 0.0), axis=0, keepdims=True)
        dest_ref[k:k + 1, :] = (pos_ref[k:k + 1, :] + base.astype(I32)) * SLAB
    pltpu.sync_copy(dest_ref, dsm_ref)

    def scatter(t, carry):
        src = h_ref.at[pl.ds(pl.multiple_of(t * SLAB, SLAB), SLAB), :]
        for k in range(TOP_K):
            dst = buf_ref.at[pl.ds(pl.multiple_of(dsm_ref[k, t], SLAB), SLAB), :]
            pltpu.make_async_copy(src, dst, sem_ref).start(priority=k % 2)
        return carry

    lax.fori_loop(0, tm, scatter, 0, unroll=8)
    span = buf_ref.at[pl.ds(0, TOP_K * tm * SLAB), :]
    pltpu.make_async_copy(span, span, sem_ref).wait()


def _dispatch(h, tope, pos, pstart, zstart, nused, n_rows):
    tm = TM_DISP
    t_all = tope.shape[1]
    col = pl.BlockSpec((TOP_K, tm), lambda i, z, nu: (0, i))
    return pl.pallas_call(
        _dispatch_kernel,
        grid_spec=pltpu.PrefetchScalarGridSpec(
            num_scalar_prefetch=2, grid=(t_all // tm,),
            in_specs=[pl.BlockSpec((SLAB * tm, LANES), lambda i, z, nu: (i, 0)), col, col,
                      pl.BlockSpec((N_EXPERTS, 1), lambda i, z, nu: (0, 0))],
            out_specs=[pl.BlockSpec(memory_space=pl.ANY), col],
            scratch_shapes=[pltpu.SMEM((TOP_K, tm), I32),
                            pltpu.VMEM((SLAB * ROW_BLOCK, LANES), F32),
                            pltpu.SemaphoreType.DMA, pltpu.SemaphoreType.DMA]),
        out_shape=[jax.ShapeDtypeStruct((SLAB * n_rows, LANES), F32),
                   jax.ShapeDtypeStruct((TOP_K, t_all), I32)],
        compiler_params=pltpu.CompilerParams(dimension_semantics=("arbitrary",),
                                             vmem_limit_bytes=VMEM_LIMIT),
        name="dispatch",
    )(zstart, nused, h, tope, pos, pstart)


def _expert_kernel(tfirst_ref, ntile_ref, nused_ref, x_hbm, wg_ref, wu_ref, wd_ref, o_hbm,
                   xbuf, obuf, wgb_ref, wub_ref, wdb_ref, xsem, osem):
    e = pl.program_id(0)
    tile_rows = SLAB * ROW_BLOCK
    nused = nused_ref[0]

    def x_copy(g, slot):
        start = pl.multiple_of(g * tile_rows, tile_rows)
        return pltpu.make_async_copy(x_hbm.at[pl.ds(start, tile_rows), :], xbuf.at[slot], xsem.at[slot])

    def o_copy(g, slot):
        start = pl.multiple_of(g * tile_rows, tile_rows)
        return pltpu.make_async_copy(obuf.at[slot], o_hbm.at[pl.ds(start, tile_rows), :], osem.at[slot])

    @pl.when(e == 0)
    def _():
        for j in range(N_XBUF - 1):
            @pl.when(j < nused)
            def _():
                x_copy(j, j).start()

    n_e = ntile_ref[e]

    @pl.when(n_e > 0)
    def _():
        wgb_ref[...] = wg_ref[...].astype(BF16)
        wub_ref[...] = wu_ref[...].astype(BF16)
        wdb_ref[...] = wd_ref[...].astype(BF16)

    def tile(j, carry):
        g = tfirst_ref[e] + j
        ahead = g + (N_XBUF - 1)

        @pl.when(ahead < nused)
        def _():
            x_copy(ahead, ahead % N_XBUF).start()

        slot = g % N_XBUF
        x_copy(g, slot).wait()
        x = _slab_load(xbuf.at[slot], ROW_BLOCK).astype(BF16)
        gt = jnp.dot(x, wgb_ref[...], preferred_element_type=F32)
        up = jnp.dot(x, wub_ref[...], preferred_element_type=F32)
        hmid = (gt * jax.nn.sigmoid(gt) * up).astype(BF16)
        out = jnp.dot(hmid, wdb_ref[...], preferred_element_type=F32)
        oslot = g % N_OBUF

        @pl.when(g >= N_OBUF)
        def _():
            o_copy(g - N_OBUF, oslot).wait()

        _slab_store(obuf.at[oslot], out)
        o_copy(g, oslot).start()
        return carry

    lax.fori_loop(0, n_e, tile, 0)

    @pl.when(e == N_EXPERTS - 1)
    def _():
        for j in range(N_OBUF):
            @pl.when(nused > j)
            def _():
                o_copy(nused - 1 - j, (nused - 1 - j) % N_OBUF).wait()


def _experts(buf, tfirst, ntile, nused, w_gate, w_up, w_down):
    tile_rows = SLAB * ROW_BLOCK
    wmap = lambda e, tf, nt, nu: (e, 0, 0)
    return pl.pallas_call(
        _expert_kernel,
        grid_spec=pltpu.PrefetchScalarGridSpec(
            num_scalar_prefetch=3, grid=(N_EXPERTS,),
            in_specs=[pl.BlockSpec(memory_space=pl.ANY),
                      pl.BlockSpec((None, D_MODEL, EXPERT_HIDDEN), wmap),
                      pl.BlockSpec((None, D_MODEL, EXPERT_HIDDEN), wmap),
                      pl.BlockSpec((None, EXPERT_HIDDEN, D_MODEL), wmap)],
            out_specs=pl.BlockSpec(memory_space=pl.ANY),
            scratch_shapes=[pltpu.VMEM((N_XBUF, tile_rows, LANES), F32),
                            pltpu.VMEM((N_OBUF, tile_rows, LANES), F32),
                            pltpu.VMEM((D_MODEL, EXPERT_HIDDEN), BF16),
                            pltpu.VMEM((D_MODEL, EXPERT_HIDDEN), BF16),
                            pltpu.VMEM((EXPERT_HIDDEN, D_MODEL), BF16),
                            pltpu.SemaphoreType.DMA((N_XBUF,)),
                            pltpu.SemaphoreType.DMA((N_OBUF,))]),
        out_shape=jax.ShapeDtypeStruct(buf.shape, F32),
        input_output_aliases={3: 0},
        compiler_params=pltpu.CompilerParams(dimension_semantics=("arbitrary",),
                                             vmem_limit_bytes=VMEM_LIMIT),
        name="experts",
    )(tfirst, ntile, nused, buf, w_gate, w_up, w_down)


def _combine_kernel(h_ref, gate_ref, dest_ref, eo_ref, wsg_ref, wsu_ref, wsd_ref, g_ref, b_ref,
                    yp_ref, ys_ref, dsm_ref, rows_ref, sem_ref, *, n_prompt_tiles):
    i = pl.program_id(0)
    tm = gate_ref.shape[0]
    pltpu.sync_copy(dest_ref, dsm_ref)

    def gather(t, carry):
        for k in range(TOP_K):
            src = eo_ref.at[pl.ds(pl.multiple_of(dsm_ref[k, t], SLAB), SLAB), :]
            dst = rows_ref.at[pl.ds(pl.multiple_of((k * tm + t) * SLAB, SLAB), SLAB), :]
            pltpu.make_async_copy(src, dst, sem_ref).start(priority=k % 2)
        return carry

    lax.fori_loop(0, tm, gather, 0, unroll=8)

    h = _slab_load(h_ref, tm)
    hb = h.astype(BF16)
    sg = jnp.dot(hb, wsg_ref[...], preferred_element_type=F32)
    su = jnp.dot(hb, wsu_ref[...], preferred_element_type=F32)
    shared = jnp.dot((sg * jax.nn.sigmoid(sg) * su).astype(BF16), wsd_ref[...],
                     preferred_element_type=F32)

    pltpu.make_async_copy(eo_ref.at[pl.ds(0, TOP_K * tm * SLAB), :], rows_ref, sem_ref).wait()
    gate = gate_ref[...]
    routed = gate[:, 0:1] * _slab_load(rows_ref, tm)
    for k in range(1, TOP_K):
        routed = routed + gate[:, k:k + 1] * _slab_load(rows_ref, tm, k * tm)
    y = _layer_norm(ALPHA * h + (routed + shared), g_ref[...], b_ref[...])

    @pl.when(i < n_prompt_tiles)
    def _():
        yp_ref[...] = y

    @pl.when(i >= n_prompt_tiles)
    def _():
        ys_ref[...] = y


def _combine(h, gate_t, dest, eo, wsg, wsu, wsd, g, b, t_prompt):
    tm = TM_COMB
    t_all = gate_t.shape[0]
    n1 = t_prompt // tm
    n2 = (t_all - t_prompt) // tm
    full = lambda a: pl.BlockSpec(a.shape, lambda i: (0,) * a.ndim)
    return pl.pallas_call(
        functools.partial(_combine_kernel, n_prompt_tiles=n1),
        grid=(n1 + n2,),
        in_specs=[pl.BlockSpec((SLAB * tm, LANES), lambda i: (i, 0)),
                  pl.BlockSpec((tm, TOP_K), lambda i: (i, 0)),
                  pl.BlockSpec((TOP_K, tm), lambda i: (0, i)),
                  pl.BlockSpec(memory_space=pl.ANY),
                  full(wsg), full(wsu), full(wsd), full(g), full(b)],
        out_specs=[pl.BlockSpec((tm, D_MODEL), lambda i: (jnp.minimum(i, n1 - 1), 0)),
                   pl.BlockSpec((tm, D_MODEL), lambda i: (jnp.maximum(i - n1, 0), 0))],
        out_shape=[jax.ShapeDtypeStruct((t_prompt, D_MODEL), F32),
                   jax.ShapeDtypeStruct((t_all - t_prompt, D_MODEL), F32)],
        scratch_shapes=[pltpu.SMEM((TOP_K, tm), I32),
                        pltpu.VMEM((TOP_K * tm * SLAB, LANES), F32),
                        pltpu.SemaphoreType.DMA],
        compiler_params=pltpu.CompilerParams(dimension_semantics=("arbitrary",),
                                             vmem_limit_bytes=VMEM_LIMIT),
        name="combine",
    )(h, gate_t, dest, eo, wsg, wsu, wsd, g, b)


def _rope_tables(s_max):
    inv_freq = ROPE_BASE ** (-jnp.arange(0, B_ROPE, 2, dtype=F32) / B_ROPE)
    ang = jnp.arange(s_max, dtype=F32)[:, None] * inv_freq[None, :]
    cos, sin = jnp.cos(ang), jnp.sin(ang)
    ones = jnp.ones((s_max, B_NOPE), F32)
    zeros_n = jnp.zeros((s_max, B_NOPE), F32)
    zeros_p = jnp.zeros((s_max, HEAD_PAD - B_NOPE - B_ROPE), F32)
    return (jnp.concatenate([ones, cos, cos, zeros_p], axis=1),
            jnp.concatenate([zeros_n, sin, sin, zeros_p], axis=1))


def _rot_cols(w):
    half = B_ROPE // 2
    return jnp.concatenate([-w[..., half:], w[..., :half]], axis=-1)


def _layout_weights(w_in, w_uq, w_uk):
    c_kr = 3 * A_WIDTH + Q_LORA + KV_LORA
    w_kr = w_in[:, c_kr:c_kr + B_ROPE]
    pad_l = jnp.zeros((D_MODEL, B_NOPE), F32)
    pad_r = jnp.zeros((D_MODEL, HEAD_PAD - B_NOPE - B_ROPE), F32)
    w1 = jnp.concatenate([w_in[:, :c_kr], pad_l, w_kr, pad_r, pad_l, _rot_cols(w_kr), pad_r], axis=1)
    wq3 = w_uq.reshape(Q_LORA, B_HEADS, B_NOPE + B_ROPE)
    nope, rope = wq3[..., :B_NOPE], wq3[..., B_NOPE:]
    zpad = jnp.zeros((Q_LORA, B_HEADS, HEAD_PAD - B_NOPE - B_ROPE), F32)
    wq = jnp.concatenate([nope, rope, zpad], axis=-1).reshape(Q_LORA, B_HEADS * HEAD_PAD)
    wqr = jnp.concatenate([jnp.zeros_like(nope), _rot_cols(rope), zpad], axis=-1)
    wqr = wqr.reshape(Q_LORA, B_HEADS * HEAD_PAD)
    wk3 = w_uk.reshape(KV_LORA, B_HEADS, B_NOPE)
    wuk = jnp.concatenate([wk3, jnp.zeros((KV_LORA, B_HEADS, HEAD_PAD - B_NOPE), F32)], axis=-1)
    wuk = wuk.reshape(KV_LORA, B_HEADS * HEAD_PAD)
    return w1.astype(BF16), wq.astype(BF16), wqr.astype(BF16), wuk.astype(BF16)


def _forward(x_prompt, x_sample, w_in, w_out, ln1_g, ln1_b, q_norm_g, w_uq, kv_norm_g, w_uk, w_uv,
             w_router, router_bias, w_gate, w_up, w_down, ws_gate, ws_up, ws_down, ln2_g, ln2_b):
    b1, s1, _ = x_prompt.shape
    b2, s2, _ = x_sample.shape
    t1, t2 = b1 * s1, b2 * s2
    t_all = t1 + t2
    xp = x_prompt.reshape(t1, D_MODEL)
    xs = x_sample.reshape(t2, D_MODEL)
    seqs = [(b * s1, s1) for b in range(b1)] + [(t1 + b * s2, s2) for b in range(b2)]
    assert t1 % s2 == 0 and s1 % (TQ_A * 16) == 0 and s2 % (TQ_A * 16) == 0

    w1, wq, wqr, wuk = _layout_weights(w_in, w_uq, w_uk)
    cos_t, sin_t = _rope_tables(max(s1, s2))
    qa, ka, va, qb, kb, vb = _proj(xp, xs, w1, q_norm_g.reshape(1, -1), kv_norm_g.reshape(1, -1),
                                   wq, wqr, wuk, w_uv.astype(BF16), cos_t, sin_t, s1, s2)

    outs, lses = [], []
    for _, dil in A_PATTERNS:
        o, lse = _attn_a(qa, ka, va, seqs, dil)
        outs.append(o)
        lses.append(lse)
    obp = _attn_b(qb, kb, vb, 0, b1, s1)
    obs = _attn_b(qb, kb, vb, t1, b2, s2)

    wr_t = w_router.T
    wr_hi = wr_t.astype(BF16)
    wr_lo = (wr_t - wr_hi.astype(F32)).astype(BF16)
    h, tope, pos, gate, counts = _mix(xp, xs, outs, lses, obp, obs, w_out.astype(BF16),
                                      ln1_g.reshape(1, -1), ln1_b.reshape(1, -1),
                                      wr_hi, wr_lo, router_bias.reshape(-1, 1))

    cnt = counts[:, 0].astype(I32)
    padded = ((cnt + ROW_BLOCK - 1) // ROW_BLOCK) * ROW_BLOCK
    pend = jnp.cumsum(padded)
    pstart = pend - padded
    n_tiles = (t_all * TOP_K) // ROW_BLOCK + N_EXPERTS + 1
    nused = (pend[-1:] // ROW_BLOCK).astype(I32)

    buf, dest = _dispatch(h, tope, pos, pstart.astype(F32).reshape(-1, 1),
                          ((pstart + cnt) * SLAB).astype(I32), nused, n_tiles * ROW_BLOCK)
    eo = _experts(buf, (pstart // ROW_BLOCK).astype(I32), (padded // ROW_BLOCK).astype(I32), nused,
                  w_gate, w_up, w_down)
    yp, ys = _combine(h, gate.T, dest, eo, ws_gate.astype(BF16), ws_up.astype(BF16),
                      ws_down.astype(BF16), ln2_g.reshape(1, -1), ln2_b.reshape(1, -1), t1)
    return yp.reshape(b1, s1, D_MODEL), ys.reshape(b2, s2, D_MODEL)


def kernel(x_prompt, x_sample, w_in, w_out, ln1_g, ln1_b, q_norm_g, w_uq, kv_norm_g, w_uk, w_uv,
           w_router, router_bias, w_gate, w_up, w_down, ws_gate, ws_up, ws_down, ln2_g, ln2_b):
    params = (w_in, w_out, ln1_g, ln1_b, q_norm_g, w_uq, kv_norm_g, w_uk, w_uv, w_router, router_bias,
              w_gate, w_up, w_down, ws_gate, ws_up, ws_down, ln2_g, ln2_b)
    assert all(p.shape[0] == 1 for p in params), "one encoder layer"
    return _forward(x_prompt, x_sample, *[p.reshape(p.shape[1:]) for p in params])
```

```python
import functools
import math

import numpy as np
import jax
import jax.numpy as jnp
from jax import lax
from jax.experimental import pallas as pl
from jax.experimental.pallas import tpu as pltpu

F32 = jnp.float32
BF16 = jnp.bfloat16
I32 = jnp.int32

D_MODEL = 1024
A_HEADS = 8
A_HEAD_DIM = 64
A_WIDTH = A_HEADS * A_HEAD_DIM
A_PATTERNS = ((128, 1), (512, 4), (2048, 16))
A_HALF = 64
B_HEADS = 8
B_NOPE = 64
B_ROPE = 32
B_V = 64
Q_LORA = 256
KV_LORA = 128
ROPE_BASE = 10000.0
N_EXPERTS = 256
TOP_K = 8
N_GROUPS = 8
GROUP_SIZE = N_EXPERTS // N_GROUPS
TOPK_GROUPS = 4
EXPERT_HIDDEN = 256
ROUTED_SCALE = 2.5
LN_EPS = 1e-5
RMS_EPS = 1e-6
NEG_BIG = -1e30
ALPHA = 2.0 ** 0.25
LOG2E = 1.4426950408889634

LANES = 128
HEAD_PAD = 128
SLAB = D_MODEL // LANES
W1_COLS = 3 * A_WIDTH + Q_LORA + KV_LORA + 2 * LANES

TM_PROJ = 256
TQ_A = 128
TQ_B = 512
TM_MIX = 256
TM_DISP = 256
ROW_BLOCK = 256
TM_COMB = 128
N_XBUF = 3
N_OBUF = 2
VMEM_LIMIT = 48 * 1024 * 1024


def _slab_load(ref, n_tok, tok0=0):
    return jnp.concatenate([ref[pl.ds(tok0 * SLAB + c, n_tok, stride=SLAB), :] for c in range(SLAB)],
                           axis=1)


def _slab_store(ref, val):
    for c in range(SLAB):
        ref[pl.ds(c, val.shape[0], stride=SLAB), :] = val[:, c * LANES:(c + 1) * LANES]


def _nt_dot(a, b):
    return lax.dot_general(a, b, (((1,), (1,)), ((), ())), preferred_element_type=F32)


def _layer_norm(x, g, b):
    mu = jnp.mean(x, axis=-1, keepdims=True)
    xc = x - mu
    var = jnp.mean(xc * xc, axis=-1, keepdims=True)
    return xc * lax.rsqrt(var + LN_EPS) * g + b


def _rms_norm(x, g):
    return x * lax.rsqrt(jnp.mean(x * x, axis=-1, keepdims=True) + RMS_EPS) * g


def _proj_kernel(xp_ref, xs_ref, w1_ref, qg_ref, kvg_ref, wq_ref, wqr_ref, wuk_ref, wuvt_ref,
                 cos_ref, sin_ref, perm4_ref, perm16_ref,
                 qa_ref, ka_ref, va_ref, q4_ref, k4_ref, v4_ref, q16_ref, k16_ref, v16_ref,
                 qb_ref, kb_ref, vbt_ref, *, n_prompt_tiles):
    i = pl.program_id(0)
    tm = xp_ref.shape[0]
    x = jnp.where(i < n_prompt_tiles, xp_ref[...], xs_ref[...]).astype(BF16)
    p = jnp.dot(x, w1_ref[...], preferred_element_type=F32)
    qa = (p[:, 0:A_WIDTH] * (A_HEAD_DIM ** -0.5 * LOG2E)).astype(BF16)
    ka = p[:, A_WIDTH:2 * A_WIDTH].astype(BF16)
    va = p[:, 2 * A_WIDTH:3 * A_WIDTH].astype(BF16)
    qa_ref[...] = qa
    ka_ref[...] = ka
    va_ref[...] = va
    qkv = jnp.concatenate([qa, ka, va], axis=1)
    for dil, perm_ref, outs in ((4, perm4_ref, (q4_ref, k4_ref, v4_ref)),
                                (16, perm16_ref, (q16_ref, k16_ref, v16_ref))):
        cm = jnp.dot(perm_ref[...], qkv, preferred_element_type=F32).astype(BF16)
        rows = tm // dil
        for r in range(dil):
            for j, o_ref in enumerate(outs):
                o_ref[:, r * A_WIDTH:(r + 1) * A_WIDTH] = cm[r * rows:(r + 1) * rows,
                                                             j * A_WIDTH:(j + 1) * A_WIDTH]
    c0 = 3 * A_WIDTH
    cq = p[:, c0:c0 + Q_LORA]
    ckv = p[:, c0 + Q_LORA:c0 + Q_LORA + KV_LORA]
    kr = p[:, c0 + Q_LORA + KV_LORA:c0 + Q_LORA + KV_LORA + LANES]
    krr = p[:, c0 + Q_LORA + KV_LORA + LANES:c0 + Q_LORA + KV_LORA + 2 * LANES]
    cos = cos_ref[...]
    sin = sin_ref[...]
    cos8 = jnp.concatenate([cos] * B_HEADS, axis=1)
    sin8 = jnp.concatenate([sin] * B_HEADS, axis=1)
    cqn = _rms_norm(cq, qg_ref[...]).astype(BF16)
    q = jnp.dot(cqn, wq_ref[...], preferred_element_type=F32)
    qr = jnp.dot(cqn, wqr_ref[...], preferred_element_type=F32)
    qscale = (B_NOPE + B_ROPE) ** -0.5 * LOG2E
    qb_ref[...] = ((q * cos8 + qr * sin8) * qscale).astype(BF16)
    ckvn = _rms_norm(ckv, kvg_ref[...]).astype(BF16)
    kn = jnp.dot(ckvn, wuk_ref[...], preferred_element_type=F32)
    krope = kr * cos + krr * sin
    kb_ref[...] = (kn + jnp.concatenate([krope] * B_HEADS, axis=1)).astype(BF16)
    vbt_ref[...] = _nt_dot(wuvt_ref[...], ckvn).astype(BF16)


def _class_perm(tm, dil):
    rows = tm // dil
    c = np.arange(tm)
    src = (c % rows) * dil + c // rows
    perm = np.zeros((tm, tm), np.float32)
    perm[c, src] = 1.0
    return jnp.asarray(perm, BF16)


def _proj(xp, xs, w1, qg, kvg, wq, wqr, wuk, wuvt, cos_t, sin_t, s_prompt, s_sample):
    tm = TM_PROJ
    n1 = xp.shape[0] // tm
    n2 = xs.shape[0] // tm
    t_all = xp.shape[0] + xs.shape[0]
    pt, st = s_prompt // tm, s_sample // tm
    perm4, perm16 = _class_perm(tm, 4), _class_perm(tm, 16)

    def tab_idx(i):
        return (jnp.where(i < n1, i % pt, (i - n1) % st), 0)

    full = lambda shape: pl.BlockSpec(shape, lambda i: (0, 0))
    row = lambda cols: pl.BlockSpec((tm, cols), lambda i: (i, 0))
    cls = lambda dil: pl.BlockSpec((tm // dil, dil * A_WIDTH), lambda i: (i, 0))
    cls_shape = lambda dil: jax.ShapeDtypeStruct((t_all // dil, dil * A_WIDTH), BF16)
    return pl.pallas_call(
        functools.partial(_proj_kernel, n_prompt_tiles=n1),
        grid=(n1 + n2,),
        in_specs=[
            pl.BlockSpec((tm, D_MODEL), lambda i: (jnp.minimum(i, n1 - 1), 0)),
            pl.BlockSpec((tm, D_MODEL), lambda i: (jnp.maximum(i - n1, 0), 0)),
            full(w1.shape), full(qg.shape), full(kvg.shape), full(wq.shape), full(wqr.shape),
            full(wuk.shape), full(wuvt.shape),
            pl.BlockSpec((tm, LANES), tab_idx), pl.BlockSpec((tm, LANES), tab_idx),
            full(perm4.shape), full(perm16.shape),
        ],
        out_specs=[row(A_WIDTH)] * 3 + [cls(4)] * 3 + [cls(16)] * 3
        + [row(B_HEADS * HEAD_PAD), row(B_HEADS * HEAD_PAD),
           pl.BlockSpec((B_HEADS * B_V, tm), lambda i: (0, i))],
        out_shape=[jax.ShapeDtypeStruct((t_all, A_WIDTH), BF16)] * 3
        + [cls_shape(4)] * 3 + [cls_shape(16)] * 3
        + [jax.ShapeDtypeStruct((t_all, B_HEADS * HEAD_PAD), BF16)] * 2
        + [jax.ShapeDtypeStruct((B_HEADS * B_V, t_all), BF16)],
        compiler_params=pltpu.CompilerParams(dimension_semantics=("arbitrary",),
                                             vmem_limit_bytes=VMEM_LIMIT),
        name="proj",
    )(xp, xs, w1, qg, kvg, wq, wqr, wuk, wuvt, cos_t, sin_t, perm4, perm16)


def _attn_a_kernel(qrow_ref, prow_ref, nrow_ref, rcol_ref, tstart_ref, llen_ref,
                   q_ref, kp_ref, kc_ref, kn_ref, vp_ref, vc_ref, vn_ref, bias_ref, o_ref, lse_ref):
    n = pl.program_id(0)
    tq = TQ_A
    kw = jnp.concatenate([kp_ref[tq - A_HALF:, :], kc_ref[...], kn_ref[:A_HALF, :]], axis=0)
    vw = jnp.concatenate([vp_ref[tq - A_HALF:, :], vc_ref[...], vn_ref[:A_HALF, :]], axis=0)
    wk = tq + 2 * A_HALF
    col = lax.broadcasted_iota(I32, (1, wk), 1) + (tstart_ref[n] - A_HALF)
    colpen = jnp.where((col >= 0) & (col < llen_ref[n]), 0.0, NEG_BIG).astype(F32)
    lane = lax.broadcasted_iota(I32, (tq, LANES), 1)
    low = lane < A_HEAD_DIM
    q = q_ref[...]
    for j in range(A_HEADS // 2):
        sl = slice(j * LANES, (j + 1) * LANES)
        qp, kpair, vpair = q[:, sl], kw[:, sl], vw[:, sl]
        outs, lses = [], []
        for e in range(2):
            qm = jnp.where(low if e == 0 else ~low, qp, jnp.zeros_like(qp))
            s = _nt_dot(qm, kpair) + bias_ref[2 * j + e] + colpen
            m = jnp.max(s, axis=1, keepdims=True)
            p = jnp.exp2(s - m)
            l = jnp.sum(p, axis=1, keepdims=True)
            o = jnp.dot(p.astype(BF16), vpair, preferred_element_type=F32)
            outs.append(o / l)
            lses.append(m + jnp.log2(l))
        o_ref[:, sl] = jnp.where(low, outs[0], outs[1]).astype(BF16)
        lse_ref[:, sl] = jnp.where(low, lses[0], lses[1])


def _attn_a_tables(seqs, dil):
    tq = TQ_A
    qrow, prow, nrow, rcol, tstart, llen = [], [], [], [], [], []
    for off, s_len in seqs:
        cls = s_len // dil
        nt = cls // tq
        base = (off // dil) // tq
        for r in range(dil):
            for i in range(nt):
                qrow.append(base + i)
                prow.append(base + max(i - 1, 0))
                nrow.append(base + min(i + 1, nt - 1))
                rcol.append(r)
                tstart.append(i * tq)
                llen.append(cls)
    return [jnp.asarray(np.asarray(a, np.int32)) for a in (qrow, prow, nrow, rcol, tstart, llen)]


def _attn_a_bias(dil):
    tq, wk = TQ_A, TQ_A + 2 * A_HALF
    delta = np.abs(np.arange(wk)[None, :] - A_HALF - np.arange(tq)[:, None]).astype(np.float64)
    slopes = 2.0 ** (-8.0 * (np.arange(A_HEADS) + 1.0) / A_HEADS)
    bias = -slopes[:, None, None] * (delta * dil)[None] * LOG2E
    bias = np.where((delta <= A_HALF)[None], bias, NEG_BIG)
    return jnp.asarray(bias.astype(np.float32))


def _attn_a(qc, kc, vc, seqs, dil):
    t_all = qc.shape[0] * dil
    tq = TQ_A
    tabs = _attn_a_tables(seqs, dil)
    n_steps = int(tabs[0].shape[0])
    bias = _attn_a_bias(dil)
    cur = lambda n, qr, pr, nr, rc, ts, ll: (qr[n], rc[n])
    prev = lambda n, qr, pr, nr, rc, ts, ll: (pr[n], rc[n])
    nxt = lambda n, qr, pr, nr, rc, ts, ll: (nr[n], rc[n])
    blk = lambda im: pl.BlockSpec((tq, A_WIDTH), im)
    o, lse = pl.pallas_call(
        _attn_a_kernel,
        grid_spec=pltpu.PrefetchScalarGridSpec(
            num_scalar_prefetch=6, grid=(n_steps,),
            in_specs=[blk(cur), blk(prev), blk(cur), blk(nxt), blk(prev), blk(cur), blk(nxt),
                      pl.BlockSpec(bias.shape, lambda n, *_: (0, 0, 0))],
            out_specs=[blk(cur), blk(cur)]),
        out_shape=[jax.ShapeDtypeStruct((t_all // dil, dil * A_WIDTH), BF16),
                   jax.ShapeDtypeStruct((t_all // dil, dil * A_WIDTH), F32)],
        compiler_params=pltpu.CompilerParams(dimension_semantics=("arbitrary",),
                                             vmem_limit_bytes=VMEM_LIMIT),
        name=f"attn_a_d{dil}",
    )(*tabs, qc, kc, kc, kc, vc, vc, vc, bias)
    return o.reshape(t_all, A_WIDTH), lse.reshape(t_all, A_WIDTH)


def _attn_b_kernel(q_ref, k_ref, vt_ref, o_ref):
    for e in range(2):
        sl = slice(e * HEAD_PAD, (e + 1) * HEAD_PAD)
        st = _nt_dot(k_ref[:, sl], q_ref[:, sl])
        m = jnp.max(st, axis=0, keepdims=True)
        p = jnp.exp2(st - m)
        l = jnp.sum(p, axis=0, keepdims=True)
        ot = jnp.dot(vt_ref[e * B_V:(e + 1) * B_V, :], p.astype(BF16),
                     preferred_element_type=F32)
        o_ref[e * B_V:(e + 1) * B_V, :] = (ot / l).astype(BF16)


def _attn_b(qb, kb, vbt, off, n_batch, s_len):
    tq = TQ_B
    nq = s_len // tq
    qbase, kbase = off // tq, off // s_len
    return pl.pallas_call(
        _attn_b_kernel,
        grid=(n_batch, B_HEADS // 2, nq),
        in_specs=[
            pl.BlockSpec((tq, 2 * HEAD_PAD), lambda b, hp, qi: (qbase + b * nq + qi, hp)),
            pl.BlockSpec((s_len, 2 * HEAD_PAD), lambda b, hp, qi: (kbase + b, hp)),
            pl.BlockSpec((2 * B_V, s_len), lambda b, hp, qi: (hp, kbase + b)),
        ],
        out_specs=pl.BlockSpec((2 * B_V, tq), lambda b, hp, qi: (hp, b * nq + qi)),
        out_shape=jax.ShapeDtypeStruct((B_HEADS * B_V, n_batch * s_len), BF16),
        compiler_params=pltpu.CompilerParams(dimension_semantics=("arbitrary",) * 3,
                                             vmem_limit_bytes=VMEM_LIMIT),
        name=f"attn_b_s{s_len}",
    )(qb, kb, vbt)


def _mix_kernel(xp_ref, xs_ref, o0_ref, o1_ref, o2_ref, l0_ref, l1_ref, l2_ref, obp_ref, obs_ref, wout_ref,
                g_ref, b_ref, wrh_ref, wrl_ref, rb_ref,
                h_ref, tope_ref, pos_ref, gate_ref, cnt_out_ref, cnt_ref, *, n_prompt_tiles):
    i = pl.program_id(0)
    tm = xp_ref.shape[0]

    @pl.when(i == 0)
    def _():
        cnt_ref[...] = jnp.zeros_like(cnt_ref)

    l0, l1, l2 = l0_ref[...], l1_ref[...], l2_ref[...]
    lmax = jnp.maximum(jnp.maximum(l0, l1), l2)
    e0, e1, e2 = jnp.exp2(l0 - lmax), jnp.exp2(l1 - lmax), jnp.exp2(l2 - lmax)
    oa = (e0 * o0_ref[...].astype(F32) + e1 * o1_ref[...].astype(F32)
          + e2 * o2_ref[...].astype(F32)) / (e0 + e1 + e2)
    is_prompt = i < n_prompt_tiles
    obt = jnp.where(is_prompt, obp_ref[...], obs_ref[...])
    mix = (jnp.dot(oa.astype(BF16), wout_ref[0:A_WIDTH, :], preferred_element_type=F32)
           + lax.dot_general(obt, wout_ref[A_WIDTH:, :], (((0,), (0,)), ((), ())),
                             preferred_element_type=F32))
    x = jnp.where(is_prompt, xp_ref[...], xs_ref[...])
    h = _layer_norm(ALPHA * x + mix, g_ref[...], b_ref[...])
    _slab_store(h_ref, h)

    h_hi = h.astype(BF16)
    h_lo = (h - h_hi.astype(F32)).astype(BF16)
    wrh = wrh_ref[...]
    logits = _nt_dot(wrh, h_hi) + _nt_dot(wrh, h_lo) + _nt_dot(wrl_ref[...], h_hi)
    scores = jax.nn.sigmoid(logits)
    sel = scores + rb_ref[...]

    sub = lax.broadcasted_iota(I32, (GROUP_SIZE, tm), 0).astype(F32)
    gscore = []
    for g in range(N_GROUPS):
        sg = sel[g * GROUP_SIZE:(g + 1) * GROUP_SIZE, :]
        m1 = jnp.max(sg, axis=0, keepdims=True)
        first = jnp.min(jnp.where(sg == m1, sub, float(GROUP_SIZE)), axis=0, keepdims=True)
        m2 = jnp.max(jnp.where(sub == first, -jnp.inf, sg), axis=0, keepdims=True)
        gscore.append(m1 + m2)
    cands = []
    for g in range(N_GROUPS):
        beaten = jnp.zeros((1, tm), F32)
        for g2 in range(N_GROUPS):
            if g2 == g:
                continue
            wins = (gscore[g2] > gscore[g]) | ((gscore[g2] == gscore[g]) & (g2 < g))
            beaten = beaten + wins.astype(F32)
        keep = beaten < float(TOPK_GROUPS)
        sg = sel[g * GROUP_SIZE:(g + 1) * GROUP_SIZE, :]
        cands.append(jnp.where(keep, sg, NEG_BIG))
    cand = jnp.concatenate(cands, axis=0)

    eidx = lax.broadcasted_iota(I32, (N_EXPERTS, tm), 0).astype(F32)
    picked_idx, picked_gate = [], []
    onehot = jnp.zeros((N_EXPERTS, tm), F32)
    for _ in range(TOP_K):
        mx = jnp.max(cand, axis=0, keepdims=True)
        fi = jnp.min(jnp.where(cand == mx, eidx, float(N_EXPERTS)), axis=0, keepdims=True)
        pick = eidx == fi
        picked_idx.append(fi)
        picked_gate.append(jnp.sum(jnp.where(pick, scores, 0.0), axis=0, keepdims=True))
        onehot = onehot + pick.astype(F32)
        cand = jnp.where(pick, -jnp.inf, cand)
    gsum = picked_gate[0]
    for k in range(1, TOP_K):
        gsum = gsum + picked_gate[k]

    tri = (lax.broadcasted_iota(I32, (tm, tm), 0) < lax.broadcasted_iota(I32, (tm, tm), 1))
    before = jnp.dot(onehot.astype(BF16), tri.astype(BF16), preferred_element_type=F32)
    rank = before + cnt_ref[:, 0:1]
    for k in range(TOP_K):
        pick = eidx == picked_idx[k]
        tope_ref[k:k + 1, :] = picked_idx[k].astype(I32)
        pos_ref[k:k + 1, :] = jnp.sum(jnp.where(pick, rank, 0.0), axis=0, keepdims=True).astype(I32)
        gate_ref[k:k + 1, :] = picked_gate[k] / gsum * ROUTED_SCALE
    cnt_ref[...] = cnt_ref[...] + jnp.sum(onehot, axis=1, keepdims=True)
    cnt_out_ref[...] = cnt_ref[...]


def _mix(xp, xs, outs, lses, obp, obs, wout, g, b, wrh, wrl, rb):
    tm = TM_MIX
    n1 = xp.shape[0] // tm
    n2 = xs.shape[0] // tm
    t_all = xp.shape[0] + xs.shape[0]
    full = lambda a: pl.BlockSpec(a.shape, lambda i: (0,) * a.ndim)
    row = lambda cols: pl.BlockSpec((tm, cols), lambda i: (i, 0))
    prow = lambda cols: pl.BlockSpec((tm, cols), lambda i: (jnp.minimum(i, n1 - 1), 0))
    srow = lambda cols: pl.BlockSpec((tm, cols), lambda i: (jnp.maximum(i - n1, 0), 0))
    col = pl.BlockSpec((TOP_K, tm), lambda i: (0, i))
    return pl.pallas_call(
        functools.partial(_mix_kernel, n_prompt_tiles=n1),
        grid=(n1 + n2,),
        in_specs=[
            prow(D_MODEL), srow(D_MODEL),
            row(A_WIDTH), row(A_WIDTH), row(A_WIDTH), row(A_WIDTH), row(A_WIDTH), row(A_WIDTH),
            pl.BlockSpec((B_HEADS * B_V, tm), lambda i: (0, jnp.minimum(i, n1 - 1))),
            pl.BlockSpec((B_HEADS * B_V, tm), lambda i: (0, jnp.maximum(i - n1, 0))),
            full(wout), full(g), full(b), full(wrh), full(wrl), full(rb),
        ],
        out_specs=[pl.BlockSpec((SLAB * tm, LANES), lambda i: (i, 0)), col, col, col,
                   pl.BlockSpec((N_EXPERTS, LANES), lambda i: (0, 0))],
        out_shape=[jax.ShapeDtypeStruct((SLAB * t_all, LANES), F32),
                   jax.ShapeDtypeStruct((TOP_K, t_all), I32),
                   jax.ShapeDtypeStruct((TOP_K, t_all), I32),
                   jax.ShapeDtypeStruct((TOP_K, t_all), F32),
                   jax.ShapeDtypeStruct((N_EXPERTS, LANES), F32)],
        scratch_shapes=[pltpu.VMEM((N_EXPERTS, LANES), F32)],
        compiler_params=pltpu.CompilerParams(dimension_semantics=("arbitrary",),
                                             vmem_limit_bytes=VMEM_LIMIT),
        name="mix_router",
    )(xp, xs, *outs, *lses, obp, obs, wout, g, b, wrh, wrl, rb)


def _dispatch_kernel(zstart_ref, nused_ref, h_ref, tope_ref, pos_ref, pstart_ref, buf_ref, dest_ref,
                     dsm_ref, zero_ref, sem_ref, zsem_ref):
    i = pl.program_id(0)
    tm = tope_ref.shape[1]
    tile_rows = SLAB * ROW_BLOCK

    @pl.when(i == 0)
    def _():
        zero_ref[...] = jnp.zeros_like(zero_ref)

        def zfill(e, carry):
            start = pl.multiple_of(zstart_ref[e], SLAB)
            pltpu.make_async_copy(zero_ref, buf_ref.at[pl.ds(start, tile_rows), :], zsem_ref).start()
            return carry

        lax.fori_loop(0, N_EXPERTS, zfill, 0)
        span = buf_ref.at[pl.ds(0, N_EXPERTS * tile_rows), :]
        pltpu.make_async_copy(span, span, zsem_ref).wait()

        def ztail(j, carry):
            start = pl.multiple_of(j * tile_rows, tile_rows)
            pltpu.make_async_copy(zero_ref, buf_ref.at[pl.ds(start, tile_rows), :], zsem_ref).start()
            return carry

        def zwait(j, carry):
            pltpu.make_async_copy(zero_ref, buf_ref.at[pl.ds(0, tile_rows), :], zsem_ref).wait()
            return carry

        n_tiles = buf_ref.shape[0] // tile_rows
        lax.fori_loop(nused_ref[0], n_tiles, ztail, 0)
        lax.fori_loop(nused_ref[0], n_tiles, zwait, 0)

    eidx = lax.broadcasted_iota(I32, (N_EXPERTS, tm), 0)
    pstart = pstart_ref[...]
    for k in range(TOP_K):
        hit = eidx == tope_ref[k:k + 1, :]
        base = jnp.sum(jnp.where(hit, pstart, 0.0), axis=0, keepdims=True)
        dest_ref[k:k + 1, :] = (pos_ref[k:k + 1, :] + base.astype(I32)) * SLAB
    pltpu.sync_copy(dest_ref, dsm_ref)

    def scatter(t, carry):
        src = h_ref.at[pl.ds(pl.multiple_of(t * SLAB, SLAB), SLAB), :]
        for k in range(TOP_K):
            dst = buf_ref.at[pl.ds(pl.multiple_of(dsm_ref[k, t], SLAB), SLAB), :]
            pltpu.make_async_copy(src, dst, sem_ref).start(priority=k % 2)
        return carry

    lax.fori_loop(0, tm, scatter, 0, unroll=8)
    span = buf_ref.at[pl.ds(0, TOP_K * tm * SLAB), :]
    pltpu.make_async_copy(span, span, sem_ref).wait()


def _dispatch(h, tope, pos, pstart, zstart, nused, n_rows):
    tm = TM_DISP
    t_all = tope.shape[1]
    col = pl.BlockSpec((TOP_K, tm), lambda i, z, nu: (0, i))
    return pl.pallas_call(
        _dispatch_kernel,
        grid_spec=pltpu.PrefetchScalarGridSpec(
            num_scalar_prefetch=2, grid=(t_all // tm,),
            in_specs=[pl.BlockSpec((SLAB * tm, LANES), lambda i, z, nu: (i, 0)), col, col,
                      pl.BlockSpec((N_EXPERTS, 1), lambda i, z, nu: (0, 0))],
            out_specs=[pl.BlockSpec(memory_space=pl.ANY), col],
            scratch_shapes=[pltpu.SMEM((TOP_K, tm), I32),
                            pltpu.VMEM((SLAB * ROW_BLOCK, LANES), F32),
                            pltpu.SemaphoreType.DMA, pltpu.SemaphoreType.DMA]),
        out_shape=[jax.ShapeDtypeStruct((SLAB * n_rows, LANES), F32),
                   jax.ShapeDtypeStruct((TOP_K, t_all), I32)],
        compiler_params=pltpu.CompilerParams(dimension_semantics=("arbitrary",),
                                             vmem_limit_bytes=VMEM_LIMIT),
        name="dispatch",
    )(zstart, nused, h, tope, pos, pstart)


def _expert_kernel(tfirst_ref, ntile_ref, nused_ref, x_hbm, wg_ref, wu_ref, wd_ref, o_hbm,
                   xbuf, obuf, wgb_ref, wub_ref, wdb_ref, xsem, osem):
    e = pl.program_id(0)
    tile_rows = SLAB * ROW_BLOCK
    nused = nused_ref[0]

    def x_copy(g, slot):
        start = pl.multiple_of(g * tile_rows, tile_rows)
        return pltpu.make_async_copy(x_hbm.at[pl.ds(start, tile_rows), :], xbuf.at[slot], xsem.at[slot])

    def o_copy(g, slot):
        start = pl.multiple_of(g * tile_rows, tile_rows)
        return pltpu.make_async_copy(obuf.at[slot], o_hbm.at[pl.ds(start, tile_rows), :], osem.at[slot])

    @pl.when(e == 0)
    def _():
        for j in range(N_XBUF - 1):
            @pl.when(j < nused)
            def _():
                x_copy(j, j).start()

    n_e = ntile_ref[e]

    @pl.when(n_e > 0)
    def _():
        wgb_ref[...] = wg_ref[...].astype(BF16)
        wub_ref[...] = wu_ref[...].astype(BF16)
        wdb_ref[...] = wd_ref[...].astype(BF16)

    def tile(j, carry):
        g = tfirst_ref[e] + j
        ahead = g + (N_XBUF - 1)

        @pl.when(ahead < nused)
        def _():
            x_copy(ahead, ahead % N_XBUF).start()

        slot = g % N_XBUF
        x_copy(g, slot).wait()
        x = _slab_load(xbuf.at[slot], ROW_BLOCK).astype(BF16)
        gt = jnp.dot(x, wgb_ref[...], preferred_element_type=F32)
        up = jnp.dot(x, wub_ref[...], preferred_element_type=F32)
        hmid = (gt * jax.nn.sigmoid(gt) * up).astype(BF16)
        out = jnp.dot(hmid, wdb_ref[...], preferred_element_type=F32)
        oslot = g % N_OBUF

        @pl.when(g >= N_OBUF)
        def _():
            o_copy(g - N_OBUF, oslot).wait()

        _slab_store(obuf.at[oslot], out)
        o_copy(g, oslot).start()
        return carry

    lax.fori_loop(0, n_e, tile, 0)

    @pl.when(e == N_EXPERTS - 1)
    def _():
        for j in range(N_OBUF):
            @pl.when(nused > j)
            def _():
                o_copy(nused - 1 - j, (nused - 1 - j) % N_OBUF).wait()


def _experts(buf, tfirst, ntile, nused, w_gate, w_up, w_down):
    tile_rows = SLAB * ROW_BLOCK
    wmap = lambda e, tf, nt, nu: (e, 0, 0)
    return pl.pallas_call(
        _expert_kernel,
        grid_spec=pltpu.PrefetchScalarGridSpec(
            num_scalar_prefetch=3, grid=(N_EXPERTS,),
            in_specs=[pl.BlockSpec(memory_space=pl.ANY),
                      pl.BlockSpec((None, D_MODEL, EXPERT_HIDDEN), wmap),
                      pl.BlockSpec((None, D_MODEL, EXPERT_HIDDEN), wmap),
                      pl.BlockSpec((None, EXPERT_HIDDEN, D_MODEL), wmap)],
            out_specs=pl.BlockSpec(memory_space=pl.ANY),
            scratch_shapes=[pltpu.VMEM((N_XBUF, tile_rows, LANES), F32),
                            pltpu.VMEM((N_OBUF, tile_rows, LANES), F32),
                            pltpu.VMEM((D_MODEL, EXPERT_HIDDEN), BF16),
                            pltpu.VMEM((D_MODEL, EXPERT_HIDDEN), BF16),
                            pltpu.VMEM((EXPERT_HIDDEN, D_MODEL), BF16),
                            pltpu.SemaphoreType.DMA((N_XBUF,)),
                            pltpu.SemaphoreType.DMA((N_OBUF,))]),
        out_shape=jax.ShapeDtypeStruct(buf.shape, F32),
        input_output_aliases={3: 0},
        compiler_params=pltpu.CompilerParams(dimension_semantics=("arbitrary",),
                                             vmem_limit_bytes=VMEM_LIMIT),
        name="experts",
    )(tfirst, ntile, nused, buf, w_gate, w_up, w_down)


def _combine_kernel(h_ref, gate_ref, dest_ref, eo_ref, wsg_ref, wsu_ref, wsd_ref, g_ref, b_ref,
                    yp_ref, ys_ref, dsm_ref, rows_ref, sem_ref, *, n_prompt_tiles):
    i = pl.program_id(0)
    tm = gate_ref.shape[0]
    pltpu.sync_copy(dest_ref, dsm_ref)

    def gather(t, carry):
        for k in range(TOP_K):
            src = eo_ref.at[pl.ds(pl.multiple_of(dsm_ref[k, t], SLAB), SLAB), :]
            dst = rows_ref.at[pl.ds(pl.multiple_of((k * tm + t) * SLAB, SLAB), SLAB), :]
            pltpu.make_async_copy(src, dst, sem_ref).start(priority=k % 2)
        return carry

    lax.fori_loop(0, tm, gather, 0, unroll=8)

    h = _slab_load(h_ref, tm)
    hb = h.astype(BF16)
    sg = jnp.dot(hb, wsg_ref[...], preferred_element_type=F32)
    su = jnp.dot(hb, wsu_ref[...], preferred_element_type=F32)
    shared = jnp.dot((sg * jax.nn.sigmoid(sg) * su).astype(BF16), wsd_ref[...],
                     preferred_element_type=F32)

    pltpu.make_async_copy(eo_ref.at[pl.ds(0, TOP_K * tm * SLAB), :], rows_ref, sem_ref).wait()
    gate = gate_ref[...]
    routed = gate[:, 0:1] * _slab_load(rows_ref, tm)
    for k in range(1, TOP_K):
        routed = routed + gate[:, k:k + 1] * _slab_load(rows_ref, tm, k * tm)
    y = _layer_norm(ALPHA * h + (routed + shared), g_ref[...], b_ref[...])

    @pl.when(i < n_prompt_tiles)
    def _():
        yp_ref[...] = y

    @pl.when(i >= n_prompt_tiles)
    def _():
        ys_ref[...] = y


def _combine(h, gate_t, dest, eo, wsg, wsu, wsd, g, b, t_prompt):
    tm = TM_COMB
    t_all = gate_t.shape[0]
    n1 = t_prompt // tm
    n2 = (t_all - t_prompt) // tm
    full = lambda a: pl.BlockSpec(a.shape, lambda i: (0,) * a.ndim)
    return pl.pallas_call(
        functools.partial(_combine_kernel, n_prompt_tiles=n1),
        grid=(n1 + n2,),
        in_specs=[pl.BlockSpec((SLAB * tm, LANES), lambda i: (i, 0)),
                  pl.BlockSpec((tm, TOP_K), lambda i: (i, 0)),
                  pl.BlockSpec((TOP_K, tm), lambda i: (0, i)),
                  pl.BlockSpec(memory_space=pl.ANY),
                  full(wsg), full(wsu), full(wsd), full(g), full(b)],
        out_specs=[pl.BlockSpec((tm, D_MODEL), lambda i: (jnp.minimum(i, n1 - 1), 0)),
                   pl.BlockSpec((tm, D_MODEL), lambda i: (jnp.maximum(i - n1, 0), 0))],
        out_shape=[jax.ShapeDtypeStruct((t_prompt, D_MODEL), F32),
                   jax.ShapeDtypeStruct((t_all - t_prompt, D_MODEL), F32)],
        scratch_shapes=[pltpu.SMEM((TOP_K, tm), I32),
                        pltpu.VMEM((TOP_K * tm * SLAB, LANES), F32),
                        pltpu.SemaphoreType.DMA],
        compiler_params=pltpu.CompilerParams(dimension_semantics=("arbitrary",),
                                             vmem_limit_bytes=VMEM_LIMIT),
        name="combine",
    )(h, gate_t, dest, eo, wsg, wsu, wsd, g, b)


def _rope_tables(s_max):
    inv_freq = ROPE_BASE ** (-jnp.arange(0, B_ROPE, 2, dtype=F32) / B_ROPE)
    ang = jnp.arange(s_max, dtype=F32)[:, None] * inv_freq[None, :]
    cos, sin = jnp.cos(ang), jnp.sin(ang)
    ones = jnp.ones((s_max, B_NOPE), F32)
    zeros_n = jnp.zeros((s_max, B_NOPE), F32)
    zeros_p = jnp.zeros((s_max, HEAD_PAD - B_NOPE - B_ROPE), F32)
    return (jnp.concatenate([ones, cos, cos, zeros_p], axis=1),
            jnp.concatenate([zeros_n, sin, sin, zeros_p], axis=1))


def _rot_cols(w):
    half = B_ROPE // 2
    return jnp.concatenate([-w[..., half:], w[..., :half]], axis=-1)


def _layout_weights(w_in, w_uq, w_uk):
    c_kr = 3 * A_WIDTH + Q_LORA + KV_LORA
    w_kr = w_in[:, c_kr:c_kr + B_ROPE]
    pad_l = jnp.zeros((D_MODEL, B_NOPE), F32)
    pad_r = jnp.zeros((D_MODEL, HEAD_PAD - B_NOPE - B_ROPE), F32)
    w1 = jnp.concatenate([w_in[:, :c_kr], pad_l, w_kr, pad_r, pad_l, _rot_cols(w_kr), pad_r], axis=1)
    wq3 = w_uq.reshape(Q_LORA, B_HEADS, B_NOPE + B_ROPE)
    nope, rope = wq3[..., :B_NOPE], wq3[..., B_NOPE:]
    zpad = jnp.zeros((Q_LORA, B_HEADS, HEAD_PAD - B_NOPE - B_ROPE), F32)
    wq = jnp.concatenate([nope, rope, zpad], axis=-1).reshape(Q_LORA, B_HEADS * HEAD_PAD)
    wqr = jnp.concatenate([jnp.zeros_like(nope), _rot_cols(rope), zpad], axis=-1)
    wqr = wqr.reshape(Q_LORA, B_HEADS * HEAD_PAD)
    wk3 = w_uk.reshape(KV_LORA, B_HEADS, B_NOPE)
    wuk = jnp.concatenate([wk3, jnp.zeros((KV_LORA, B_HEADS, HEAD_PAD - B_NOPE), F32)], axis=-1)
    wuk = wuk.reshape(KV_LORA, B_HEADS * HEAD_PAD)
    return w1.astype(BF16), wq.astype(BF16), wqr.astype(BF16), wuk.astype(BF16)


def _forward(x_prompt, x_sample, w_in, w_out, ln1_g, ln1_b, q_norm_g, w_uq, kv_norm_g, w_uk, w_uv,
             w_router, router_bias, w_gate, w_up, w_down, ws_gate, ws_up, ws_down, ln2_g, ln2_b):
    b1, s1, _ = x_prompt.shape
    b2, s2, _ = x_sample.shape
    t1, t2 = b1 * s1, b2 * s2
    t_all = t1 + t2
    xp = x_prompt.reshape(t1, D_MODEL)
    xs = x_sample.reshape(t2, D_MODEL)
    seqs = [(b * s1, s1) for b in range(b1)] + [(t1 + b * s2, s2) for b in range(b2)]
    assert t1 % s2 == 0 and s1 % (TQ_A * 16) == 0 and s2 % (TQ_A * 16) == 0

    w1, wq, wqr, wuk = _layout_weights(w_in, w_uq, w_uk)
    cos_t, sin_t = _rope_tables(max(s1, s2))
    (qa, ka, va, q4, k4, v4, q16, k16, v16, qb, kb, vbt) = _proj(
        xp, xs, w1, q_norm_g.reshape(1, -1), kv_norm_g.reshape(1, -1), wq, wqr, wuk,
        w_uv.T.astype(BF16), cos_t, sin_t, s1, s2)

    outs, lses = [], []
    for (_, dil), qkv in zip(A_PATTERNS, ((qa, ka, va), (q4, k4, v4), (q16, k16, v16))):
        o, lse = _attn_a(*qkv, seqs, dil)
        outs.append(o)
        lses.append(lse)
    obp = _attn_b(qb, kb, vbt, 0, b1, s1)
    obs = _attn_b(qb, kb, vbt, t1, b2, s2)

    wr_t = w_router.T
    wr_hi = wr_t.astype(BF16)
    wr_lo = (wr_t - wr_hi.astype(F32)).astype(BF16)
    h, tope, pos, gate, counts = _mix(xp, xs, outs, lses, obp, obs, w_out.astype(BF16),
                                      ln1_g.reshape(1, -1), ln1_b.reshape(1, -1),
                                      wr_hi, wr_lo, router_bias.reshape(-1, 1))

    cnt = counts[:, 0].astype(I32)
    padded = ((cnt + ROW_BLOCK - 1) // ROW_BLOCK) * ROW_BLOCK
    pend = jnp.cumsum(padded)
    pstart = pend - padded
    n_tiles = (t_all * TOP_K) // ROW_BLOCK + N_EXPERTS + 1
    nused = (pend[-1:] // ROW_BLOCK).astype(I32)

    buf, dest = _dispatch(h, tope, pos, pstart.astype(F32).reshape(-1, 1),
                          ((pstart + cnt) * SLAB).astype(I32), nused, n_tiles * ROW_BLOCK)
    eo = _experts(buf, (pstart // ROW_BLOCK).astype(I32), (padded // ROW_BLOCK).astype(I32), nused,
                  w_gate, w_up, w_down)
    yp, ys = _combine(h, gate.T, dest, eo, ws_gate.astype(BF16), ws_up.astype(BF16),
                      ws_down.astype(BF16), ln2_g.reshape(1, -1), ln2_b.reshape(1, -1), t1)
    return yp.reshape(b1, s1, D_MODEL), ys.reshape(b2, s2, D_MODEL)


def kernel(x_prompt, x_sample, w_in, w_out, ln1_g, ln1_b, q_norm_g, w_uq, kv_norm_g, w_uk, w_uv,
           w_router, router_bias, w_gate, w_up, w_down, ws_gate, ws_up, ws_down, ln2_g, ln2_b):
    params = (w_in, w_out, ln1_g, ln1_b, q_norm_g, w_uq, kv_norm_g, w_uk, w_uv, w_router, router_bias,
              w_gate, w_up, w_down, ws_gate, ws_up, ws_down, ln2_g, ln2_b)
    assert all(p.shape[0] == 1 for p in params), "one encoder layer"
    return _forward(x_prompt, x_sample, *[p.reshape(p.shape[1:]) for p in params])
```

```python
import functools
import math

import numpy as np
import jax
import jax.numpy as jnp
from jax import lax
from jax.experimental import pallas as pl
from jax.experimental.pallas import tpu as pltpu

F32 = jnp.float32
BF16 = jnp.bfloat16
I32 = jnp.int32

D_MODEL = 1024
A_HEADS = 8
A_HEAD_DIM = 64
A_WIDTH = A_HEADS * A_HEAD_DIM
A_PATTERNS = ((128, 1), (512, 4), (2048, 16))
A_HALF = 64
B_HEADS = 8
B_NOPE = 64
B_ROPE = 32
B_V = 64
Q_LORA = 256
KV_LORA = 128
ROPE_BASE = 10000.0
N_EXPERTS = 256
TOP_K = 8
N_GROUPS = 8
GROUP_SIZE = N_EXPERTS // N_GROUPS
TOPK_GROUPS = 4
EXPERT_HIDDEN = 256
ROUTED_SCALE = 2.5
LN_EPS = 1e-5
RMS_EPS = 1e-6
NEG_BIG = -1e30
ALPHA = 2.0 ** 0.25
LOG2E = 1.4426950408889634

LANES = 128
HEAD_PAD = 128
SLAB = D_MODEL // LANES
W1_COLS = 3 * A_WIDTH + Q_LORA + KV_LORA + 2 * LANES

TM_PROJ = 256
TQ_A = 128
A_POS_TILES = {1: 4, 4: 2, 16: 1}
PERM_GROUP = 256
TQ_B = 512
TM_MIX = 256
TM_DISP = 256
ROW_BLOCK = 256
TM_COMB = 128
N_XBUF = 3
N_OBUF = 2
VMEM_LIMIT = 48 * 1024 * 1024


def _slab_load(ref, n_tok, tok0=0):
    return jnp.concatenate([ref[pl.ds(tok0 * SLAB + c, n_tok, stride=SLAB), :] for c in range(SLAB)],
                           axis=1)


def _slab_store(ref, val):
    for c in range(SLAB):
        ref[pl.ds(c, val.shape[0], stride=SLAB), :] = val[:, c * LANES:(c + 1) * LANES]


def _nt_dot(a, b):
    return lax.dot_general(a, b, (((1,), (1,)), ((), ())), preferred_element_type=F32)


def _layer_norm(x, g, b):
    mu = jnp.mean(x, axis=-1, keepdims=True)
    xc = x - mu
    var = jnp.mean(xc * xc, axis=-1, keepdims=True)
    return xc * lax.rsqrt(var + LN_EPS) * g + b


def _rms_norm(x, g):
    return x * lax.rsqrt(jnp.mean(x * x, axis=-1, keepdims=True) + RMS_EPS) * g


def _proj_kernel(xp_ref, xs_ref, w1_ref, qg_ref, kvg_ref, wq_ref, wqr_ref, wuk_ref, wuvt_ref,
                 cos_ref, sin_ref, perm4_ref, perm16_ref,
                 qa_ref, ka_ref, va_ref, q4_ref, k4_ref, v4_ref, q16_ref, k16_ref, v16_ref,
                 qb_ref, kb_ref, vbt_ref, *, n_prompt_tiles):
    i = pl.program_id(0)
    tm = xp_ref.shape[0]
    x = jnp.where(i < n_prompt_tiles, xp_ref[...], xs_ref[...]).astype(BF16)
    p = jnp.dot(x, w1_ref[...], preferred_element_type=F32)
    qa = (p[:, 0:A_WIDTH] * (A_HEAD_DIM ** -0.5 * LOG2E)).astype(BF16)
    ka = p[:, A_WIDTH:2 * A_WIDTH].astype(BF16)
    va = p[:, 2 * A_WIDTH:3 * A_WIDTH].astype(BF16)
    qa_ref[...] = qa
    ka_ref[...] = ka
    va_ref[...] = va
    qkv = jnp.concatenate([qa, ka, va], axis=1)
    for dil, perm_ref, outs in ((4, perm4_ref, (q4_ref, k4_ref, v4_ref)),
                                (16, perm16_ref, (q16_ref, k16_ref, v16_ref))):
        cm = jnp.dot(perm_ref[...], qkv, preferred_element_type=F32).astype(BF16)
        rows = tm // dil
        for r in range(dil):
            for j, o_ref in enumerate(outs):
                o_ref[:, r * A_WIDTH:(r + 1) * A_WIDTH] = cm[r * rows:(r + 1) * rows,
                                                             j * A_WIDTH:(j + 1) * A_WIDTH]
    c0 = 3 * A_WIDTH
    cq = p[:, c0:c0 + Q_LORA]
    ckv = p[:, c0 + Q_LORA:c0 + Q_LORA + KV_LORA]
    kr = p[:, c0 + Q_LORA + KV_LORA:c0 + Q_LORA + KV_LORA + LANES]
    krr = p[:, c0 + Q_LORA + KV_LORA + LANES:c0 + Q_LORA + KV_LORA + 2 * LANES]
    cos = cos_ref[...]
    sin = sin_ref[...]
    cos8 = jnp.concatenate([cos] * B_HEADS, axis=1)
    sin8 = jnp.concatenate([sin] * B_HEADS, axis=1)
    cqn = _rms_norm(cq, qg_ref[...]).astype(BF16)
    q = jnp.dot(cqn, wq_ref[...], preferred_element_type=F32)
    qr = jnp.dot(cqn, wqr_ref[...], preferred_element_type=F32)
    qscale = (B_NOPE + B_ROPE) ** -0.5 * LOG2E
    qb_ref[...] = ((q * cos8 + qr * sin8) * qscale).astype(BF16)
    ckvn = _rms_norm(ckv, kvg_ref[...]).astype(BF16)
    kn = jnp.dot(ckvn, wuk_ref[...], preferred_element_type=F32)
    krope = kr * cos + krr * sin
    kb_ref[...] = (kn + jnp.concatenate([krope] * B_HEADS, axis=1)).astype(BF16)
    vbt_ref[...] = _nt_dot(wuvt_ref[...], ckvn).astype(BF16)


def _class_perm(tm, dil):
    rows = tm // dil
    c = np.arange(tm)
    src = (c % rows) * dil + c // rows
    perm = np.zeros((tm, tm), np.float32)
    perm[c, src] = 1.0
    return jnp.asarray(perm, BF16)


def _proj(xp, xs, w1, qg, kvg, wq, wqr, wuk, wuvt, cos_t, sin_t, s_prompt, s_sample):
    tm = TM_PROJ
    n1 = xp.shape[0] // tm
    n2 = xs.shape[0] // tm
    t_all = xp.shape[0] + xs.shape[0]
    pt, st = s_prompt // tm, s_sample // tm
    assert tm == PERM_GROUP
    perm4, perm16 = _class_perm(tm, 4), _class_perm(tm, 16)

    def tab_idx(i):
        return (jnp.where(i < n1, i % pt, (i - n1) % st), 0)

    full = lambda shape: pl.BlockSpec(shape, lambda i: (0, 0))
    row = lambda cols: pl.BlockSpec((tm, cols), lambda i: (i, 0))
    cls = lambda dil: pl.BlockSpec((tm // dil, dil * A_WIDTH), lambda i: (i, 0))
    cls_shape = lambda dil: jax.ShapeDtypeStruct((t_all // dil, dil * A_WIDTH), BF16)
    return pl.pallas_call(
        functools.partial(_proj_kernel, n_prompt_tiles=n1),
        grid=(n1 + n2,),
        in_specs=[
            pl.BlockSpec((tm, D_MODEL), lambda i: (jnp.minimum(i, n1 - 1), 0)),
            pl.BlockSpec((tm, D_MODEL), lambda i: (jnp.maximum(i - n1, 0), 0)),
            full(w1.shape), full(qg.shape), full(kvg.shape), full(wq.shape), full(wqr.shape),
            full(wuk.shape), full(wuvt.shape),
            pl.BlockSpec((tm, LANES), tab_idx), pl.BlockSpec((tm, LANES), tab_idx),
            full(perm4.shape), full(perm16.shape),
        ],
        out_specs=[row(A_WIDTH)] * 3 + [cls(4)] * 3 + [cls(16)] * 3
        + [row(B_HEADS * HEAD_PAD), row(B_HEADS * HEAD_PAD),
           pl.BlockSpec((B_HEADS * B_V, tm), lambda i: (0, i))],
        out_shape=[jax.ShapeDtypeStruct((t_all, A_WIDTH), BF16)] * 3
        + [cls_shape(4)] * 3 + [cls_shape(16)] * 3
        + [jax.ShapeDtypeStruct((t_all, B_HEADS * HEAD_PAD), BF16)] * 2
        + [jax.ShapeDtypeStruct((B_HEADS * B_V, t_all), BF16)],
        compiler_params=pltpu.CompilerParams(dimension_semantics=("arbitrary",),
                                             vmem_limit_bytes=VMEM_LIMIT),
        name="proj",
    )(xp, xs, w1, qg, kvg, wq, wqr, wuk, wuvt, cos_t, sin_t, perm4, perm16)


def _attn_a_kernel(qblk_ref, pblk_ref, nblk_ref, tstart_ref, llen_ref,
                   q_ref, kp_ref, kc_ref, kn_ref, vp_ref, vc_ref, vn_ref, bias_ref, perm_ref,
                   o_ref, lse_ref, cm_ref, *, n_pos, n_cls):
    n = pl.program_id(0)
    tq, wk = TQ_A, TQ_A + 2 * A_HALF
    lane = lax.broadcasted_iota(I32, (tq, LANES), 1)
    low = lane < A_HEAD_DIM
    colbase = lax.broadcasted_iota(I32, (1, wk), 1) + (tstart_ref[n] - A_HALF)
    cls_len = llen_ref[n]

    def one_class(r):
        cols = slice(0, A_WIDTH) if n_cls == 1 else pl.ds(pl.multiple_of(r * A_WIDTH, A_WIDTH), A_WIDTH)
        kcat = jnp.concatenate([kp_ref[:, cols], kc_ref[:, cols], kn_ref[:, cols]], axis=0)
        vcat = jnp.concatenate([vp_ref[:, cols], vc_ref[:, cols], vn_ref[:, cols]], axis=0)
        for j in range(n_pos):
            rows = slice(j * tq, (j + 1) * tq)
            q = q_ref[rows, cols]
            kw, vw = kcat[j * tq:j * tq + wk, :], vcat[j * tq:j * tq + wk, :]
            col = colbase + j * tq
            colpen = jnp.where((col >= 0) & (col < cls_len), 0.0, NEG_BIG).astype(F32)
            pairs = []
            lse_c = jnp.zeros((tq, LANES), F32)
            for jp in range(A_HEADS // 2):
                sl = slice(jp * LANES, (jp + 1) * LANES)
                qp, kpair, vpair = q[:, sl], kw[:, sl], vw[:, sl]
                outs = []
                for e in range(2):
                    qm = jnp.where(low if e == 0 else ~low, qp, jnp.zeros_like(qp))
                    s = _nt_dot(qm, kpair) + bias_ref[2 * jp + e] + colpen
                    m = jnp.max(s, axis=1, keepdims=True)
                    p = jnp.exp2(s - m)
                    l = jnp.sum(p, axis=1, keepdims=True)
                    outs.append(jnp.dot(p.astype(BF16), vpair, preferred_element_type=F32) / l)
                    lse_c = lse_c + jnp.where(lane == 2 * jp + e, m + jnp.log2(l), 0.0)
                pairs.append(jnp.where(low, outs[0], outs[1]))
            o_full = jnp.concatenate(pairs, axis=1).astype(BF16)
            if n_cls == 1:
                o_ref[rows, :] = o_full
                lse_ref[rows, :] = lse_c
            else:
                hi = lse_c.astype(BF16)
                rest = lse_c - hi.astype(F32)
                mid = rest.astype(BF16)
                lo = (rest - mid.astype(F32)).astype(BF16)
                cm_ref[r, rows, :] = jnp.concatenate([o_full, hi, mid, lo], axis=1)

    if n_cls == 1:
        one_class(0)
        return

    def body(r, carry):
        one_class(r)
        return carry

    lax.fori_loop(0, n_cls, body, 0)
    per = PERM_GROUP // n_cls
    for a in range(n_pos * tq * n_cls // PERM_GROUP):
        stack = jnp.concatenate([cm_ref[r, a * per:(a + 1) * per, :] for r in range(n_cls)], axis=0)
        nat = jnp.dot(perm_ref[...], stack, preferred_element_type=F32)
        rows = slice(a * PERM_GROUP, (a + 1) * PERM_GROUP)
        o_ref[rows, :] = nat[:, :A_WIDTH].astype(BF16)
        lse_ref[rows, :] = (nat[:, A_WIDTH:A_WIDTH + LANES] + nat[:, A_WIDTH + LANES:A_WIDTH + 2 * LANES]
                            + nat[:, A_WIDTH + 2 * LANES:])


def _attn_a_tables(seqs, dil, n_pos):
    rows = n_pos * TQ_A
    per_halo = rows // A_HALF
    qblk, pblk, nblk, tstart, llen = [], [], [], [], []
    for off, s_len in seqs:
        cls = s_len // dil
        steps = cls // rows
        base = (off // dil) // rows
        for i in range(steps):
            qblk.append(base + i)
            pblk.append((base + i) * per_halo - (1 if i > 0 else 0))
            nblk.append((base + i + 1) * per_halo - (0 if i < steps - 1 else 1))
            tstart.append(i * rows)
            llen.append(cls)
    return [jnp.asarray(np.asarray(a, np.int32)) for a in (qblk, pblk, nblk, tstart, llen)]


def _attn_a_bias(dil):
    tq, wk = TQ_A, TQ_A + 2 * A_HALF
    delta = np.abs(np.arange(wk)[None, :] - A_HALF - np.arange(tq)[:, None]).astype(np.float64)
    slopes = 2.0 ** (-8.0 * (np.arange(A_HEADS) + 1.0) / A_HEADS)
    bias = -slopes[:, None, None] * (delta * dil)[None] * LOG2E
    bias = np.where((delta <= A_HALF)[None], bias, NEG_BIG)
    return jnp.asarray(bias.astype(np.float32))


def _attn_a(qc, kc, vc, seqs, dil):
    t_all = qc.shape[0] * dil
    n_pos = A_POS_TILES[dil]
    rows, width = n_pos * TQ_A, dil * A_WIDTH
    tabs = _attn_a_tables(seqs, dil, n_pos)
    n_steps = int(tabs[0].shape[0])
    bias = _attn_a_bias(dil)
    perm = _class_perm(PERM_GROUP, dil).T
    cur = pl.BlockSpec((rows, width), lambda n, qb, pb, nb, ts, ll: (qb[n], 0))
    prev = pl.BlockSpec((A_HALF, width), lambda n, qb, pb, nb, ts, ll: (pb[n], 0))
    nxt = pl.BlockSpec((A_HALF, width), lambda n, qb, pb, nb, ts, ll: (nb[n], 0))
    tok = lambda cols: pl.BlockSpec((rows * dil, cols), lambda n, *_: (n, 0))
    return pl.pallas_call(
        functools.partial(_attn_a_kernel, n_pos=n_pos, n_cls=dil),
        grid_spec=pltpu.PrefetchScalarGridSpec(
            num_scalar_prefetch=5, grid=(n_steps,),
            in_specs=[cur, prev, cur, nxt, prev, cur, nxt,
                      pl.BlockSpec(bias.shape, lambda n, *_: (0, 0, 0)),
                      pl.BlockSpec(perm.shape, lambda n, *_: (0, 0))],
            out_specs=[tok(A_WIDTH), tok(LANES)],
            scratch_shapes=[pltpu.VMEM((dil, rows, A_WIDTH + 3 * LANES), BF16)]),
        out_shape=[jax.ShapeDtypeStruct((t_all, A_WIDTH), BF16),
                   jax.ShapeDtypeStruct((t_all, LANES), F32)],
        compiler_params=pltpu.CompilerParams(dimension_semantics=("arbitrary",),
                                             vmem_limit_bytes=VMEM_LIMIT),
        name=f"attn_a_d{dil}",
    )(*tabs, qc, kc, kc, kc, vc, vc, vc, bias, perm)


def _attn_b_kernel(q_ref, k_ref, vt_ref, o_ref):
    for e in range(2):
        sl = slice(e * HEAD_PAD, (e + 1) * HEAD_PAD)
        st = _nt_dot(k_ref[:, sl], q_ref[:, sl])
        m = jnp.max(st, axis=0, keepdims=True)
        p = jnp.exp2(st - m)
        l = jnp.sum(p, axis=0, keepdims=True)
        ot = jnp.dot(vt_ref[e * B_V:(e + 1) * B_V, :], p.astype(BF16),
                     preferred_element_type=F32)
        o_ref[e * B_V:(e + 1) * B_V, :] = (ot / l).astype(BF16)


def _attn_b(qb, kb, vbt, off, n_batch, s_len):
    tq = TQ_B
    nq = s_len // tq
    qbase, kbase = off // tq, off // s_len
    return pl.pallas_call(
        _attn_b_kernel,
        grid=(n_batch, B_HEADS // 2, nq),
        in_specs=[
            pl.BlockSpec((tq, 2 * HEAD_PAD), lambda b, hp, qi: (qbase + b * nq + qi, hp)),
            pl.BlockSpec((s_len, 2 * HEAD_PAD), lambda b, hp, qi: (kbase + b, hp)),
            pl.BlockSpec((2 * B_V, s_len), lambda b, hp, qi: (hp, kbase + b)),
        ],
        out_specs=pl.BlockSpec((2 * B_V, tq), lambda b, hp, qi: (hp, b * nq + qi)),
        out_shape=jax.ShapeDtypeStruct((B_HEADS * B_V, n_batch * s_len), BF16),
        compiler_params=pltpu.CompilerParams(dimension_semantics=("arbitrary",) * 3,
                                             vmem_limit_bytes=VMEM_LIMIT),
        name=f"attn_b_s{s_len}",
    )(qb, kb, vbt)


def _mix_kernel(xp_ref, xs_ref, o0_ref, o1_ref, o2_ref, l0_ref, l1_ref, l2_ref, spread_ref,
                obp_ref, obs_ref, wout_ref,
                g_ref, b_ref, wrh_ref, wrl_ref, rb_ref,
                h_ref, hp_ref, tope_ref, pos_ref, gate_ref, cnt_out_ref, cnt_ref, *, n_prompt_tiles):
    i = pl.program_id(0)
    tm = xp_ref.shape[0]

    @pl.when(i == 0)
    def _():
        cnt_ref[...] = jnp.zeros_like(cnt_ref)

    l0, l1, l2 = l0_ref[...], l1_ref[...], l2_ref[...]
    lmax = jnp.maximum(jnp.maximum(l0, l1), l2)
    e0, e1, e2 = jnp.exp2(l0 - lmax), jnp.exp2(l1 - lmax), jnp.exp2(l2 - lmax)
    inv = 1.0 / (e0 + e1 + e2)
    spread = spread_ref[...]

    def per_lane(w):
        hi = w.astype(BF16)
        lo = (w - hi.astype(F32)).astype(BF16)
        return (jnp.dot(hi, spread, preferred_element_type=F32)
                + jnp.dot(lo, spread, preferred_element_type=F32))

    oa = (per_lane(e0 * inv) * o0_ref[...].astype(F32) + per_lane(e1 * inv) * o1_ref[...].astype(F32)
          + per_lane(e2 * inv) * o2_ref[...].astype(F32))
    is_prompt = i < n_prompt_tiles
    obt = jnp.where(is_prompt, obp_ref[...], obs_ref[...])
    mix = (jnp.dot(oa.astype(BF16), wout_ref[0:A_WIDTH, :], preferred_element_type=F32)
           + lax.dot_general(obt, wout_ref[A_WIDTH:, :], (((0,), (0,)), ((), ())),
                             preferred_element_type=F32))
    x = jnp.where(is_prompt, xp_ref[...], xs_ref[...])
    h = _layer_norm(ALPHA * x + mix, g_ref[...], b_ref[...])
    h_ref[...] = h
    _slab_store(hp_ref, h)

    h_hi = h.astype(BF16)
    h_lo = (h - h_hi.astype(F32)).astype(BF16)
    wrh = wrh_ref[...]
    logits = _nt_dot(wrh, h_hi) + _nt_dot(wrh, h_lo) + _nt_dot(wrl_ref[...], h_hi)
    scores = jax.nn.sigmoid(logits)
    sel = scores + rb_ref[...]

    sub = lax.broadcasted_iota(I32, (GROUP_SIZE, tm), 0).astype(F32)
    gscore = []
    for g in range(N_GROUPS):
        sg = sel[g * GROUP_SIZE:(g + 1) * GROUP_SIZE, :]
        m1 = jnp.max(sg, axis=0, keepdims=True)
        first = jnp.min(jnp.where(sg == m1, sub, float(GROUP_SIZE)), axis=0, keepdims=True)
        m2 = jnp.max(jnp.where(sub == first, -jnp.inf, sg), axis=0, keepdims=True)
        gscore.append(m1 + m2)
    cands = []
    for g in range(N_GROUPS):
        beaten = jnp.zeros((1, tm), F32)
        for g2 in range(N_GROUPS):
            if g2 == g:
                continue
            wins = (gscore[g2] > gscore[g]) | ((gscore[g2] == gscore[g]) & (g2 < g))
            beaten = beaten + wins.astype(F32)
        keep = beaten < float(TOPK_GROUPS)
        sg = sel[g * GROUP_SIZE:(g + 1) * GROUP_SIZE, :]
        cands.append(jnp.where(keep, sg, NEG_BIG))
    cand = jnp.concatenate(cands, axis=0)

    eidx = lax.broadcasted_iota(I32, (N_EXPERTS, tm), 0).astype(F32)
    picked_idx, picked_gate = [], []
    onehot = jnp.zeros((N_EXPERTS, tm), F32)
    for _ in range(TOP_K):
        mx = jnp.max(cand, axis=0, keepdims=True)
        fi = jnp.min(jnp.where(cand == mx, eidx, float(N_EXPERTS)), axis=0, keepdims=True)
        pick = eidx == fi
        picked_idx.append(fi)
        picked_gate.append(jnp.sum(jnp.where(pick, scores, 0.0), axis=0, keepdims=True))
        onehot = onehot + pick.astype(F32)
        cand = jnp.where(pick, -jnp.inf, cand)
    gsum = picked_gate[0]
    for k in range(1, TOP_K):
        gsum = gsum + picked_gate[k]

    tri = (lax.broadcasted_iota(I32, (tm, tm), 0) < lax.broadcasted_iota(I32, (tm, tm), 1))
    before = jnp.dot(onehot.astype(BF16), tri.astype(BF16), preferred_element_type=F32)
    rank = before + cnt_ref[:, 0:1]
    for k in range(TOP_K):
        pick = eidx == picked_idx[k]
        tope_ref[k:k + 1, :] = picked_idx[k].astype(I32)
        pos_ref[k:k + 1, :] = jnp.sum(jnp.where(pick, rank, 0.0), axis=0, keepdims=True).astype(I32)
        gate_ref[k:k + 1, :] = picked_gate[k] / gsum * ROUTED_SCALE
    cnt_ref[...] = cnt_ref[...] + jnp.sum(onehot, axis=1, keepdims=True)
    cnt_out_ref[...] = cnt_ref[...]


def _mix(xp, xs, outs, lses, obp, obs, wout, g, b, wrh, wrl, rb):
    tm = TM_MIX
    n1 = xp.shape[0] // tm
    n2 = xs.shape[0] // tm
    t_all = xp.shape[0] + xs.shape[0]
    full = lambda a: pl.BlockSpec(a.shape, lambda i: (0,) * a.ndim)
    row = lambda cols: pl.BlockSpec((tm, cols), lambda i: (i, 0))
    prow = lambda cols: pl.BlockSpec((tm, cols), lambda i: (jnp.minimum(i, n1 - 1), 0))
    srow = lambda cols: pl.BlockSpec((tm, cols), lambda i: (jnp.maximum(i - n1, 0), 0))
    col = pl.BlockSpec((TOP_K, tm), lambda i: (0, i))
    head_of_lane = np.arange(A_WIDTH) // A_HEAD_DIM
    spread = jnp.asarray(np.arange(LANES)[:, None] == head_of_lane[None, :], BF16)
    return pl.pallas_call(
        functools.partial(_mix_kernel, n_prompt_tiles=n1),
        grid=(n1 + n2,),
        in_specs=[
            prow(D_MODEL), srow(D_MODEL),
            row(A_WIDTH), row(A_WIDTH), row(A_WIDTH), row(LANES), row(LANES), row(LANES), full(spread),
            pl.BlockSpec((B_HEADS * B_V, tm), lambda i: (0, jnp.minimum(i, n1 - 1))),
            pl.BlockSpec((B_HEADS * B_V, tm), lambda i: (0, jnp.maximum(i - n1, 0))),
            full(wout), full(g), full(b), full(wrh), full(wrl), full(rb),
        ],
        out_specs=[row(D_MODEL), pl.BlockSpec((SLAB * tm, LANES), lambda i: (i, 0)), col, col, col,
                   pl.BlockSpec((N_EXPERTS, LANES), lambda i: (0, 0))],
        out_shape=[jax.ShapeDtypeStruct((t_all, D_MODEL), F32),
                   jax.ShapeDtypeStruct((SLAB * t_all, LANES), F32),
                   jax.ShapeDtypeStruct((TOP_K, t_all), I32),
                   jax.ShapeDtypeStruct((TOP_K, t_all), I32),
                   jax.ShapeDtypeStruct((TOP_K, t_all), F32),
                   jax.ShapeDtypeStruct((N_EXPERTS, LANES), F32)],
        scratch_shapes=[pltpu.VMEM((N_EXPERTS, LANES), F32)],
        compiler_params=pltpu.CompilerParams(dimension_semantics=("arbitrary",),
                                             vmem_limit_bytes=VMEM_LIMIT),
        name="mix_router",
    )(xp, xs, *outs, *lses, spread, obp, obs, wout, g, b, wrh, wrl, rb)


def _dispatch_kernel(zstart_ref, nused_ref, h_ref, tope_ref, pos_ref, pstart_ref, buf_ref, dest_ref,
                     dsm_ref, zero_ref, sem_ref, zsem_ref):
    i = pl.program_id(0)
    tm = tope_ref.shape[1]
    tile_rows = SLAB * ROW_BLOCK

    @pl.when(i == 0)
    def _():
        zero_ref[...] = jnp.zeros_like(zero_ref)

        def zfill(e, carry):
            start = pl.multiple_of(zstart_ref[e], SLAB)
            pltpu.make_async_copy(zero_ref, buf_ref.at[pl.ds(start, tile_rows), :], zsem_ref).start()
            return carry

        lax.fori_loop(0, N_EXPERTS, zfill, 0)
        span = buf_ref.at[pl.ds(0, N_EXPERTS * tile_rows), :]
        pltpu.make_async_copy(span, span, zsem_ref).wait()

        def ztail(j, carry):
            start = pl.multiple_of(j * tile_rows, tile_rows)
            pltpu.make_async_copy(zero_ref, buf_ref.at[pl.ds(start, tile_rows), :], zsem_ref).start()
            return carry

        def zwait(j, carry):
            pltpu.make_async_copy(zero_ref, buf_ref.at[pl.ds(0, tile_rows), :], zsem_ref).wait()
            return carry

        n_tiles = buf_ref.shape[0] // tile_rows
        lax.fori_loop(nused_ref[0], n_tiles, ztail, 0)
        lax.fori_loop(nused_ref[0], n_tiles, zwait, 0)

    eidx = lax.broadcasted_iota(I32, (N_EXPERTS, tm), 0)
    pstart = pstart_ref[...]
    for k in range(TOP_K):
        hit = eidx == tope_ref[k:k + 1, :]
        base = jnp.sum(jnp.where(hit, pstart, 0.0), axis=0, keepdims=True)
        dest_ref[k:k + 1, :] = (pos_ref[k:k + 1, :] + base.astype(I32)) * SLAB
    pltpu.sync_copy(dest_ref, dsm_ref)

    def scatter(t, carry):
        src = h_ref.at[pl.ds(pl.multiple_of(t * SLAB, SLAB), SLAB), :]
        for k in range(TOP_K):
            dst = buf_ref.at[pl.ds(pl.multiple_of(dsm_ref[k, t], SLAB), SLAB), :]
            pltpu.make_async_copy(src, dst, sem_ref).start(priority=k % 2)
        return carry

    lax.fori_loop(0, tm, scatter, 0, unroll=8)
    span = buf_ref.at[pl.ds(0, TOP_K * tm * SLAB), :]
    pltpu.make_async_copy(span, span, sem_ref).wait()


def _dispatch(h, tope, pos, pstart, zstart, nused, n_rows):
    tm = TM_DISP
    t_all = tope.shape[1]
    col = pl.BlockSpec((TOP_K, tm), lambda i, z, nu: (0, i))
    return pl.pallas_call(
        _dispatch_kernel,
        grid_spec=pltpu.PrefetchScalarGridSpec(
            num_scalar_prefetch=2, grid=(t_all // tm,),
            in_specs=[pl.BlockSpec((SLAB * tm, LANES), lambda i, z, nu: (i, 0)), col, col,
                      pl.BlockSpec((N_EXPERTS, 1), lambda i, z, nu: (0, 0))],
            out_specs=[pl.BlockSpec(memory_space=pl.ANY), col],
            scratch_shapes=[pltpu.SMEM((TOP_K, tm), I32),
                            pltpu.VMEM((SLAB * ROW_BLOCK, LANES), F32),
                            pltpu.SemaphoreType.DMA, pltpu.SemaphoreType.DMA]),
        out_shape=[jax.ShapeDtypeStruct((SLAB * n_rows, LANES), F32),
                   jax.ShapeDtypeStruct((TOP_K, t_all), I32)],
        compiler_params=pltpu.CompilerParams(dimension_semantics=("arbitrary",),
                                             vmem_limit_bytes=VMEM_LIMIT),
        name="dispatch",
    )(zstart, nused, h, tope, pos, pstart)


def _expert_kernel(tfirst_ref, ntile_ref, nused_ref, x_hbm, wg_ref, wu_ref, wd_ref, o_hbm,
                   xbuf, obuf, wgb_ref, wub_ref, wdb_ref, xsem, osem):
    e = pl.program_id(0)
    tile_rows = SLAB * ROW_BLOCK
    nused = nused_ref[0]

    def x_copy(g, slot):
        start = pl.multiple_of(g * tile_rows, tile_rows)
        return pltpu.make_async_copy(x_hbm.at[pl.ds(start, tile_rows), :], xbuf.at[slot], xsem.at[slot])

    def o_copy(g, slot):
        start = pl.multiple_of(g * tile_rows, tile_rows)
        return pltpu.make_async_copy(obuf.at[slot], o_hbm.at[pl.ds(start, tile_rows), :], osem.at[slot])

    @pl.when(e == 0)
    def _():
        for j in range(N_XBUF - 1):
            @pl.when(j < nused)
            def _():
                x_copy(j, j).start()

    n_e = ntile_ref[e]

    @pl.when(n_e > 0)
    def _():
        wgb_ref[...] = wg_ref[...].astype(BF16)
        wub_ref[...] = wu_ref[...].astype(BF16)
        wdb_ref[...] = wd_ref[...].astype(BF16)

    def tile(j, carry):
        g = tfirst_ref[e] + j
        ahead = g + (N_XBUF - 1)

        @pl.when(ahead < nused)
        def _():
            x_copy(ahead, ahead % N_XBUF).start()

        slot = g % N_XBUF
        x_copy(g, slot).wait()
        x = _slab_load(xbuf.at[slot], ROW_BLOCK).astype(BF16)
        gt = jnp.dot(x, wgb_ref[...], preferred_element_type=F32)
        up = jnp.dot(x, wub_ref[...], preferred_element_type=F32)
        hmid = (gt * jax.nn.sigmoid(gt) * up).astype(BF16)
        out = jnp.dot(hmid, wdb_ref[...], preferred_element_type=F32)
        oslot = g % N_OBUF

        @pl.when(g >= N_OBUF)
        def _():
            o_copy(g - N_OBUF, oslot).wait()

        _slab_store(obuf.at[oslot], out)
        o_copy(g, oslot).start()
        return carry

    lax.fori_loop(0, n_e, tile, 0)

    @pl.when(e == N_EXPERTS - 1)
    def _():
        for j in range(N_OBUF):
            @pl.when(nused > j)
            def _():
                o_copy(nused - 1 - j, (nused - 1 - j) % N_OBUF).wait()


def _experts(buf, tfirst, ntile, nused, w_gate, w_up, w_down):
    tile_rows = SLAB * ROW_BLOCK
    wmap = lambda e, tf, nt, nu: (e, 0, 0)
    return pl.pallas_call(
        _expert_kernel,
        grid_spec=pltpu.PrefetchScalarGridSpec(
            num_scalar_prefetch=3, grid=(N_EXPERTS,),
            in_specs=[pl.BlockSpec(memory_space=pl.ANY),
                      pl.BlockSpec((None, D_MODEL, EXPERT_HIDDEN), wmap),
                      pl.BlockSpec((None, D_MODEL, EXPERT_HIDDEN), wmap),
                      pl.BlockSpec((None, EXPERT_HIDDEN, D_MODEL), wmap)],
            out_specs=pl.BlockSpec(memory_space=pl.ANY),
            scratch_shapes=[pltpu.VMEM((N_XBUF, tile_rows, LANES), F32),
                            pltpu.VMEM((N_OBUF, tile_rows, LANES), F32),
                            pltpu.VMEM((D_MODEL, EXPERT_HIDDEN), BF16),
                            pltpu.VMEM((D_MODEL, EXPERT_HIDDEN), BF16),
                            pltpu.VMEM((EXPERT_HIDDEN, D_MODEL), BF16),
                            pltpu.SemaphoreType.DMA((N_XBUF,)),
                            pltpu.SemaphoreType.DMA((N_OBUF,))]),
        out_shape=jax.ShapeDtypeStruct(buf.shape, F32),
        input_output_aliases={3: 0},
        compiler_params=pltpu.CompilerParams(dimension_semantics=("arbitrary",),
                                             vmem_limit_bytes=VMEM_LIMIT),
        name="experts",
    )(tfirst, ntile, nused, buf, w_gate, w_up, w_down)


def _combine_kernel(h_ref, gate_ref, dest_ref, eo_ref, wsg_ref, wsu_ref, wsd_ref, g_ref, b_ref,
                    yp_ref, ys_ref, dsm_ref, rows_ref, sem_ref, *, n_prompt_tiles):
    i = pl.program_id(0)
    tm = gate_ref.shape[0]
    pltpu.sync_copy(dest_ref, dsm_ref)

    def gather(t, carry):
        for k in range(TOP_K):
            src = eo_ref.at[pl.ds(pl.multiple_of(dsm_ref[k, t], SLAB), SLAB), :]
            dst = rows_ref.at[pl.ds(pl.multiple_of((k * tm + t) * SLAB, SLAB), SLAB), :]
            pltpu.make_async_copy(src, dst, sem_ref).start(priority=k % 2)
        return carry

    lax.fori_loop(0, tm, gather, 0, unroll=8)

    h = h_ref[...]
    hb = h.astype(BF16)
    sg = jnp.dot(hb, wsg_ref[...], preferred_element_type=F32)
    su = jnp.dot(hb, wsu_ref[...], preferred_element_type=F32)
    shared = jnp.dot((sg * jax.nn.sigmoid(sg) * su).astype(BF16), wsd_ref[...],
                     preferred_element_type=F32)

    pltpu.make_async_copy(eo_ref.at[pl.ds(0, TOP_K * tm * SLAB), :], rows_ref, sem_ref).wait()
    gate = gate_ref[...]
    routed = gate[:, 0:1] * _slab_load(rows_ref, tm)
    for k in range(1, TOP_K):
        routed = routed + gate[:, k:k + 1] * _slab_load(rows_ref, tm, k * tm)
    y = _layer_norm(ALPHA * h + (routed + shared), g_ref[...], b_ref[...])

    @pl.when(i < n_prompt_tiles)
    def _():
        yp_ref[...] = y

    @pl.when(i >= n_prompt_tiles)
    def _():
        ys_ref[...] = y


def _combine(h, gate_t, dest, eo, wsg, wsu, wsd, g, b, t_prompt):
    tm = TM_COMB
    t_all = gate_t.shape[0]
    n1 = t_prompt // tm
    n2 = (t_all - t_prompt) // tm
    full = lambda a: pl.BlockSpec(a.shape, lambda i: (0,) * a.ndim)
    return pl.pallas_call(
        functools.partial(_combine_kernel, n_prompt_tiles=n1),
        grid=(n1 + n2,),
        in_specs=[pl.BlockSpec((tm, D_MODEL), lambda i: (i, 0)),
                  pl.BlockSpec((tm, TOP_K), lambda i: (i, 0)),
                  pl.BlockSpec((TOP_K, tm), lambda i: (0, i)),
                  pl.BlockSpec(memory_space=pl.ANY),
                  full(wsg), full(wsu), full(wsd), full(g), full(b)],
        out_specs=[pl.BlockSpec((tm, D_MODEL), lambda i: (jnp.minimum(i, n1 - 1), 0)),
                   pl.BlockSpec((tm, D_MODEL), lambda i: (jnp.maximum(i - n1, 0), 0))],
        out_shape=[jax.ShapeDtypeStruct((t_prompt, D_MODEL), F32),
                   jax.ShapeDtypeStruct((t_all - t_prompt, D_MODEL), F32)],
        scratch_shapes=[pltpu.SMEM((TOP_K, tm), I32),
                        pltpu.VMEM((TOP_K * tm * SLAB, LANES), F32),
                        pltpu.SemaphoreType.DMA],
        compiler_params=pltpu.CompilerParams(dimension_semantics=("arbitrary",),
                                             vmem_limit_bytes=VMEM_LIMIT),
        name="combine",
    )(h, gate_t, dest, eo, wsg, wsu, wsd, g, b)


def _rope_tables(s_max):
    inv_freq = ROPE_BASE ** (-jnp.arange(0, B_ROPE, 2, dtype=F32) / B_ROPE)
    ang = jnp.arange(s_max, dtype=F32)[:, None] * inv_freq[None, :]
    cos, sin = jnp.cos(ang), jnp.sin(ang)
    ones = jnp.ones((s_max, B_NOPE), F32)
    zeros_n = jnp.zeros((s_max, B_NOPE), F32)
    zeros_p = jnp.zeros((s_max, HEAD_PAD - B_NOPE - B_ROPE), F32)
    return (jnp.concatenate([ones, cos, cos, zeros_p], axis=1),
            jnp.concatenate([zeros_n, sin, sin, zeros_p], axis=1))


def _rot_cols(w):
    half = B_ROPE // 2
    return jnp.concatenate([-w[..., half:], w[..., :half]], axis=-1)


def _layout_weights(w_in, w_uq, w_uk):
    c_kr = 3 * A_WIDTH + Q_LORA + KV_LORA
    w_kr = w_in[:, c_kr:c_kr + B_ROPE]
    pad_l = jnp.zeros((D_MODEL, B_NOPE), F32)
    pad_r = jnp.zeros((D_MODEL, HEAD_PAD - B_NOPE - B_ROPE), F32)
    w1 = jnp.concatenate([w_in[:, :c_kr], pad_l, w_kr, pad_r, pad_l, _rot_cols(w_kr), pad_r], axis=1)
    wq3 = w_uq.reshape(Q_LORA, B_HEADS, B_NOPE + B_ROPE)
    nope, rope = wq3[..., :B_NOPE], wq3[..., B_NOPE:]
    zpad = jnp.zeros((Q_LORA, B_HEADS, HEAD_PAD - B_NOPE - B_ROPE), F32)
    wq = jnp.concatenate([nope, rope, zpad], axis=-1).reshape(Q_LORA, B_HEADS * HEAD_PAD)
    wqr = jnp.concatenate([jnp.zeros_like(nope), _rot_cols(rope), zpad], axis=-1)
    wqr = wqr.reshape(Q_LORA, B_HEADS * HEAD_PAD)
    wk3 = w_uk.reshape(KV_LORA, B_HEADS, B_NOPE)
    wuk = jnp.concatenate([wk3, jnp.zeros((KV_LORA, B_HEADS, HEAD_PAD - B_NOPE), F32)], axis=-1)
    wuk = wuk.reshape(KV_LORA, B_HEADS * HEAD_PAD)
    return w1.astype(BF16), wq.astype(BF16), wqr.astype(BF16), wuk.astype(BF16)


def _forward(x_prompt, x_sample, w_in, w_out, ln1_g, ln1_b, q_norm_g, w_uq, kv_norm_g, w_uk, w_uv,
             w_router, router_bias, w_gate, w_up, w_down, ws_gate, ws_up, ws_down, ln2_g, ln2_b):
    b1, s1, _ = x_prompt.shape
    b2, s2, _ = x_sample.shape
    t1, t2 = b1 * s1, b2 * s2
    t_all = t1 + t2
    xp = x_prompt.reshape(t1, D_MODEL)
    xs = x_sample.reshape(t2, D_MODEL)
    seqs = [(b * s1, s1) for b in range(b1)] + [(t1 + b * s2, s2) for b in range(b2)]
    assert t1 % s2 == 0 and s1 % (TQ_A * 16) == 0 and s2 % (TQ_A * 16) == 0

    w1, wq, wqr, wuk = _layout_weights(w_in, w_uq, w_uk)
    cos_t, sin_t = _rope_tables(max(s1, s2))
    (qa, ka, va, q4, k4, v4, q16, k16, v16, qb, kb, vbt) = _proj(
        xp, xs, w1, q_norm_g.reshape(1, -1), kv_norm_g.reshape(1, -1), wq, wqr, wuk,
        w_uv.T.astype(BF16), cos_t, sin_t, s1, s2)

    outs, lses = [], []
    for (_, dil), qkv in zip(A_PATTERNS, ((qa, ka, va), (q4, k4, v4), (q16, k16, v16))):
        o, lse = _attn_a(*qkv, seqs, dil)
        outs.append(o)
        lses.append(lse)
    obp = _attn_b(qb, kb, vbt, 0, b1, s1)
    obs = _attn_b(qb, kb, vbt, t1, b2, s2)

    wr_t = w_router.T
    wr_hi = wr_t.astype(BF16)
    wr_lo = (wr_t - wr_hi.astype(F32)).astype(BF16)
    h, h_slab, tope, pos, gate, counts = _mix(xp, xs, outs, lses, obp, obs, w_out.astype(BF16),
                                                ln1_g.reshape(1, -1), ln1_b.reshape(1, -1),
                                                wr_hi, wr_lo, router_bias.reshape(-1, 1))

    cnt = counts[:, 0].astype(I32)
    padded = ((cnt + ROW_BLOCK - 1) // ROW_BLOCK) * ROW_BLOCK
    pend = jnp.cumsum(padded)
    pstart = pend - padded
    n_tiles = (t_all * TOP_K) // ROW_BLOCK + N_EXPERTS + 1
    nused = (pend[-1:] // ROW_BLOCK).astype(I32)

    buf, dest = _dispatch(h_slab, tope, pos, pstart.astype(F32).reshape(-1, 1),
                          ((pstart + cnt) * SLAB).astype(I32), nused, n_tiles * ROW_BLOCK)
    eo = _experts(buf, (pstart // ROW_BLOCK).astype(I32), (padded // ROW_BLOCK).astype(I32), nused,
                  w_gate, w_up, w_down)
    yp, ys = _combine(h, gate.T, dest, eo, ws_gate.astype(BF16), ws_up.astype(BF16),
                      ws_down.astype(BF16), ln2_g.reshape(1, -1), ln2_b.reshape(1, -1), t1)
    return yp.reshape(b1, s1, D_MODEL), ys.reshape(b2, s2, D_MODEL)


def kernel(x_prompt, x_sample, w_in, w_out, ln1_g, ln1_b, q_norm_g, w_uq, kv_norm_g, w_uk, w_uv,
           w_router, router_bias, w_gate, w_up, w_down, ws_gate, ws_up, ws_down, ln2_g, ln2_b):
    params = (w_in, w_out, ln1_g, ln1_b, q_norm_g, w_uq, kv_norm_g, w_uk, w_uv, w_router, router_bias,
              w_gate, w_up, w_down, ws_gate, ws_up, ws_down, ln2_g, ln2_b)
    assert all(p.shape[0] == 1 for p in params), "one encoder layer"
    return _forward(x_prompt, x_sample, *[p.reshape(p.shape[1:]) for p in params])
```

```python
import functools
import math

import numpy as np
import jax
import jax.numpy as jnp
from jax import lax
from jax.experimental import pallas as pl
from jax.experimental.pallas import tpu as pltpu

F32 = jnp.float32
BF16 = jnp.bfloat16
I32 = jnp.int32

D_MODEL = 1024
A_HEADS = 8
A_HEAD_DIM = 64
A_WIDTH = A_HEADS * A_HEAD_DIM
A_PATTERNS = ((128, 1), (512, 4), (2048, 16))
A_HALF = 64
B_HEADS = 8
B_NOPE = 64
B_ROPE = 32
B_V = 64
Q_LORA = 256
KV_LORA = 128
ROPE_BASE = 10000.0
N_EXPERTS = 256
TOP_K = 8
N_GROUPS = 8
GROUP_SIZE = N_EXPERTS // N_GROUPS
TOPK_GROUPS = 4
EXPERT_HIDDEN = 256
ROUTED_SCALE = 2.5
LN_EPS = 1e-5
RMS_EPS = 1e-6
NEG_BIG = -1e30
ALPHA = 2.0 ** 0.25
LOG2E = 1.4426950408889634

LANES = 128
HEAD_PAD = 128
SLAB = D_MODEL // LANES
W1_COLS = 3 * A_WIDTH + Q_LORA + KV_LORA + 2 * LANES

TM_PROJ = 256
TQ_A = 128
A_POS_TILES = {1: 4, 4: 2, 16: 1}
PERM_GROUP = 256
TQ_B = 512
KEY_CHUNK_B = 512
TM_MIX = 256
TM_DISP = 256
ROW_BLOCK = 256
TM_COMB = 128
N_XBUF = 3
N_OBUF = 2
VMEM_LIMIT = 48 * 1024 * 1024


def _slab_load(ref, n_tok, tok0=0):
    return jnp.concatenate([ref[pl.ds(tok0 * SLAB + c, n_tok, stride=SLAB), :] for c in range(SLAB)],
                           axis=1)


def _slab_store(ref, val):
    for c in range(SLAB):
        ref[pl.ds(c, val.shape[0], stride=SLAB), :] = val[:, c * LANES:(c + 1) * LANES]


def _nt_dot(a, b):
    return lax.dot_general(a, b, (((1,), (1,)), ((), ())), preferred_element_type=F32)


def _layer_norm(x, g, b):
    mu = jnp.mean(x, axis=-1, keepdims=True)
    xc = x - mu
    var = jnp.mean(xc * xc, axis=-1, keepdims=True)
    return xc * lax.rsqrt(var + LN_EPS) * g + b


def _rms_norm(x, g):
    return x * lax.rsqrt(jnp.mean(x * x, axis=-1, keepdims=True) + RMS_EPS) * g


def _proj_kernel(xp_ref, xs_ref, w1_ref, qg_ref, kvg_ref, wq_ref, wqr_ref, wuk_ref, wuvt_ref,
                 cos_ref, sin_ref, perm4_ref, perm16_ref,
                 qa_ref, ka_ref, va_ref, q4_ref, k4_ref, v4_ref, q16_ref, k16_ref, v16_ref,
                 qb_ref, kb_ref, vbt_ref, *, n_prompt_tiles):
    i = pl.program_id(0)
    tm = xp_ref.shape[0]
    x = jnp.where(i < n_prompt_tiles, xp_ref[...], xs_ref[...]).astype(BF16)
    p = jnp.dot(x, w1_ref[...], preferred_element_type=F32)
    qa = (p[:, 0:A_WIDTH] * (A_HEAD_DIM ** -0.5 * LOG2E)).astype(BF16)
    ka = p[:, A_WIDTH:2 * A_WIDTH].astype(BF16)
    va = p[:, 2 * A_WIDTH:3 * A_WIDTH].astype(BF16)
    qa_ref[...] = qa
    ka_ref[...] = ka
    va_ref[...] = va
    qkv = jnp.concatenate([qa, ka, va], axis=1)
    for dil, perm_ref, outs in ((4, perm4_ref, (q4_ref, k4_ref, v4_ref)),
                                (16, perm16_ref, (q16_ref, k16_ref, v16_ref))):
        cm = jnp.dot(perm_ref[...], qkv, preferred_element_type=F32).astype(BF16)
        rows = tm // dil
        for r in range(dil):
            for j, o_ref in enumerate(outs):
                o_ref[:, r * A_WIDTH:(r + 1) * A_WIDTH] = cm[r * rows:(r + 1) * rows,
                                                             j * A_WIDTH:(j + 1) * A_WIDTH]
    c0 = 3 * A_WIDTH
    cq = p[:, c0:c0 + Q_LORA]
    ckv = p[:, c0 + Q_LORA:c0 + Q_LORA + KV_LORA]
    kr = p[:, c0 + Q_LORA + KV_LORA:c0 + Q_LORA + KV_LORA + LANES]
    krr = p[:, c0 + Q_LORA + KV_LORA + LANES:c0 + Q_LORA + KV_LORA + 2 * LANES]
    cos = cos_ref[...]
    sin = sin_ref[...]
    cos8 = jnp.concatenate([cos] * B_HEADS, axis=1)
    sin8 = jnp.concatenate([sin] * B_HEADS, axis=1)
    cqn = _rms_norm(cq, qg_ref[...]).astype(BF16)
    q = jnp.dot(cqn, wq_ref[...], preferred_element_type=F32)
    qr = jnp.dot(cqn, wqr_ref[...], preferred_element_type=F32)
    qscale = (B_NOPE + B_ROPE) ** -0.5 * LOG2E
    qb_ref[...] = ((q * cos8 + qr * sin8) * qscale).astype(BF16)
    ckvn = _rms_norm(ckv, kvg_ref[...]).astype(BF16)
    kn = jnp.dot(ckvn, wuk_ref[...], preferred_element_type=F32)
    krope = kr * cos + krr * sin
    kb_ref[...] = (kn + jnp.concatenate([krope] * B_HEADS, axis=1)).astype(BF16)
    vbt_ref[...] = _nt_dot(wuvt_ref[...], ckvn).astype(BF16)


def _class_perm(tm, dil):
    rows = tm // dil
    c = np.arange(tm)
    src = (c % rows) * dil + c // rows
    perm = np.zeros((tm, tm), np.float32)
    perm[c, src] = 1.0
    return jnp.asarray(perm, BF16)


def _proj(xp, xs, w1, qg, kvg, wq, wqr, wuk, wuvt, cos_t, sin_t, s_prompt, s_sample):
    tm = TM_PROJ
    n1 = xp.shape[0] // tm
    n2 = xs.shape[0] // tm
    t_all = xp.shape[0] + xs.shape[0]
    pt, st = s_prompt // tm, s_sample // tm
    assert tm == PERM_GROUP
    perm4, perm16 = _class_perm(tm, 4), _class_perm(tm, 16)

    def tab_idx(i):
        return (jnp.where(i < n1, i % pt, (i - n1) % st), 0)

    full = lambda shape: pl.BlockSpec(shape, lambda i: (0, 0))
    row = lambda cols: pl.BlockSpec((tm, cols), lambda i: (i, 0))
    cls = lambda dil: pl.BlockSpec((tm // dil, dil * A_WIDTH), lambda i: (i, 0))
    cls_shape = lambda dil: jax.ShapeDtypeStruct((t_all // dil, dil * A_WIDTH), BF16)
    return pl.pallas_call(
        functools.partial(_proj_kernel, n_prompt_tiles=n1),
        grid=(n1 + n2,),
        in_specs=[
            pl.BlockSpec((tm, D_MODEL), lambda i: (jnp.minimum(i, n1 - 1), 0)),
            pl.BlockSpec((tm, D_MODEL), lambda i: (jnp.maximum(i - n1, 0), 0)),
            full(w1.shape), full(qg.shape), full(kvg.shape), full(wq.shape), full(wqr.shape),
            full(wuk.shape), full(wuvt.shape),
            pl.BlockSpec((tm, LANES), tab_idx), pl.BlockSpec((tm, LANES), tab_idx),
            full(perm4.shape), full(perm16.shape),
        ],
        out_specs=[row(A_WIDTH)] * 3 + [cls(4)] * 3 + [cls(16)] * 3
        + [row(B_HEADS * HEAD_PAD), row(B_HEADS * HEAD_PAD),
           pl.BlockSpec((B_HEADS * B_V, tm), lambda i: (0, i))],
        out_shape=[jax.ShapeDtypeStruct((t_all, A_WIDTH), BF16)] * 3
        + [cls_shape(4)] * 3 + [cls_shape(16)] * 3
        + [jax.ShapeDtypeStruct((t_all, B_HEADS * HEAD_PAD), BF16)] * 2
        + [jax.ShapeDtypeStruct((B_HEADS * B_V, t_all), BF16)],
        compiler_params=pltpu.CompilerParams(dimension_semantics=("arbitrary",),
                                             vmem_limit_bytes=VMEM_LIMIT),
        name="proj",
    )(xp, xs, w1, qg, kvg, wq, wqr, wuk, wuvt, cos_t, sin_t, perm4, perm16)


def _attn_a_kernel(qblk_ref, pblk_ref, nblk_ref, tstart_ref, llen_ref,
                   q_ref, kp_ref, kc_ref, kn_ref, vp_ref, vc_ref, vn_ref, bias_ref, perm_ref,
                   o_ref, lse_ref, cm_ref, *, n_pos, n_cls):
    n = pl.program_id(0)
    tq, wk = TQ_A, TQ_A + 2 * A_HALF
    lane = lax.broadcasted_iota(I32, (tq, LANES), 1)
    low = lane < A_HEAD_DIM
    colbase = lax.broadcasted_iota(I32, (1, wk), 1) + (tstart_ref[n] - A_HALF)
    cls_len = llen_ref[n]

    def one_class(r):
        cols = slice(0, A_WIDTH) if n_cls == 1 else pl.ds(pl.multiple_of(r * A_WIDTH, A_WIDTH), A_WIDTH)
        kcat = jnp.concatenate([kp_ref[:, cols], kc_ref[:, cols], kn_ref[:, cols]], axis=0)
        vcat = jnp.concatenate([vp_ref[:, cols], vc_ref[:, cols], vn_ref[:, cols]], axis=0)
        for j in range(n_pos):
            rows = slice(j * tq, (j + 1) * tq)
            q = q_ref[rows, cols]
            kw, vw = kcat[j * tq:j * tq + wk, :], vcat[j * tq:j * tq + wk, :]
            col = colbase + j * tq
            colpen = jnp.where((col >= 0) & (col < cls_len), 0.0, NEG_BIG).astype(F32)
            pairs = []
            lse_c = jnp.zeros((tq, LANES), F32)
            for jp in range(A_HEADS // 2):
                sl = slice(jp * LANES, (jp + 1) * LANES)
                qp, kpair, vpair = q[:, sl], kw[:, sl], vw[:, sl]
                outs = []
                for e in range(2):
                    qm = jnp.where(low if e == 0 else ~low, qp, jnp.zeros_like(qp))
                    s = _nt_dot(qm, kpair) + bias_ref[2 * jp + e] + colpen
                    m = jnp.max(s, axis=1, keepdims=True)
                    p = jnp.exp2(s - m)
                    l = jnp.sum(p, axis=1, keepdims=True)
                    outs.append(jnp.dot(p.astype(BF16), vpair, preferred_element_type=F32) / l)
                    lse_c = lse_c + jnp.where(lane == 2 * jp + e, m + jnp.log2(l), 0.0)
                pairs.append(jnp.where(low, outs[0], outs[1]))
            o_full = jnp.concatenate(pairs, axis=1).astype(BF16)
            if n_cls == 1:
                o_ref[rows, :] = o_full
                lse_ref[rows, :] = lse_c
            else:
                hi = lse_c.astype(BF16)
                rest = lse_c - hi.astype(F32)
                mid = rest.astype(BF16)
                lo = (rest - mid.astype(F32)).astype(BF16)
                cm_ref[r, rows, :] = jnp.concatenate([o_full, hi, mid, lo], axis=1)

    if n_cls == 1:
        one_class(0)
        return

    def body(r, carry):
        one_class(r)
        return carry

    lax.fori_loop(0, n_cls, body, 0)
    per = PERM_GROUP // n_cls
    for a in range(n_pos * tq * n_cls // PERM_GROUP):
        stack = jnp.concatenate([cm_ref[r, a * per:(a + 1) * per, :] for r in range(n_cls)], axis=0)
        nat = jnp.dot(perm_ref[...], stack, preferred_element_type=F32)
        rows = slice(a * PERM_GROUP, (a + 1) * PERM_GROUP)
        o_ref[rows, :] = nat[:, :A_WIDTH].astype(BF16)
        lse_ref[rows, :] = (nat[:, A_WIDTH:A_WIDTH + LANES] + nat[:, A_WIDTH + LANES:A_WIDTH + 2 * LANES]
                            + nat[:, A_WIDTH + 2 * LANES:])


def _attn_a_tables(seqs, dil, n_pos):
    rows = n_pos * TQ_A
    per_halo = rows // A_HALF
    qblk, pblk, nblk, tstart, llen = [], [], [], [], []
    for off, s_len in seqs:
        cls = s_len // dil
        steps = cls // rows
        base = (off // dil) // rows
        for i in range(steps):
            qblk.append(base + i)
            pblk.append((base + i) * per_halo - (1 if i > 0 else 0))
            nblk.append((base + i + 1) * per_halo - (0 if i < steps - 1 else 1))
            tstart.append(i * rows)
            llen.append(cls)
    return [jnp.asarray(np.asarray(a, np.int32)) for a in (qblk, pblk, nblk, tstart, llen)]


def _attn_a_bias(dil):
    tq, wk = TQ_A, TQ_A + 2 * A_HALF
    delta = np.abs(np.arange(wk)[None, :] - A_HALF - np.arange(tq)[:, None]).astype(np.float64)
    slopes = 2.0 ** (-8.0 * (np.arange(A_HEADS) + 1.0) / A_HEADS)
    bias = -slopes[:, None, None] * (delta * dil)[None] * LOG2E
    bias = np.where((delta <= A_HALF)[None], bias, NEG_BIG)
    return jnp.asarray(bias.astype(np.float32))


def _attn_a(qc, kc, vc, seqs, dil):
    t_all = qc.shape[0] * dil
    n_pos = A_POS_TILES[dil]
    rows, width = n_pos * TQ_A, dil * A_WIDTH
    tabs = _attn_a_tables(seqs, dil, n_pos)
    n_steps = int(tabs[0].shape[0])
    bias = _attn_a_bias(dil)
    perm = _class_perm(PERM_GROUP, dil).T
    cur = pl.BlockSpec((rows, width), lambda n, qb, pb, nb, ts, ll: (qb[n], 0))
    prev = pl.BlockSpec((A_HALF, width), lambda n, qb, pb, nb, ts, ll: (pb[n], 0))
    nxt = pl.BlockSpec((A_HALF, width), lambda n, qb, pb, nb, ts, ll: (nb[n], 0))
    tok = lambda cols: pl.BlockSpec((rows * dil, cols), lambda n, *_: (n, 0))
    return pl.pallas_call(
        functools.partial(_attn_a_kernel, n_pos=n_pos, n_cls=dil),
        grid_spec=pltpu.PrefetchScalarGridSpec(
            num_scalar_prefetch=5, grid=(n_steps,),
            in_specs=[cur, prev, cur, nxt, prev, cur, nxt,
                      pl.BlockSpec(bias.shape, lambda n, *_: (0, 0, 0)),
                      pl.BlockSpec(perm.shape, lambda n, *_: (0, 0))],
            out_specs=[tok(A_WIDTH), tok(LANES)],
            scratch_shapes=[pltpu.VMEM((dil, rows, A_WIDTH + 3 * LANES), BF16)]),
        out_shape=[jax.ShapeDtypeStruct((t_all, A_WIDTH), BF16),
                   jax.ShapeDtypeStruct((t_all, LANES), F32)],
        compiler_params=pltpu.CompilerParams(dimension_semantics=("arbitrary",),
                                             vmem_limit_bytes=VMEM_LIMIT),
        name=f"attn_a_d{dil}",
    )(*tabs, qc, kc, kc, kc, vc, vc, vc, bias, perm)


def _attn_b_kernel(q_ref, k_ref, vt_ref, o_ref, sta_ref, stb_ref, ma_ref, mb_ref):
    s = pl.program_id(0)
    tq, s_len = q_ref.shape[0], k_ref.shape[0]

    @pl.when(s == 0)
    def _():
        stb_ref[...] = jnp.zeros_like(stb_ref)
        mb_ref[...] = jnp.zeros_like(mb_ref)

    def step(st_new_ref, m_new_ref, st_old_ref, m_old_ref):
        m_old = m_old_ref[...]
        q = q_ref[...]
        m_new = jnp.full((1, tq), -jnp.inf, F32)
        l = jnp.zeros((1, tq), F32)
        acc = jnp.zeros((B_V, tq), F32)
        for c in range(s_len // KEY_CHUNK_B):
            ks = slice(c * KEY_CHUNK_B, (c + 1) * KEY_CHUNK_B)
            st_c = _nt_dot(k_ref[ks, :], q)
            st_new_ref[ks, :] = st_c
            m_new = jnp.maximum(m_new, jnp.max(st_c, axis=0, keepdims=True))
            p = jnp.exp2(st_old_ref[ks, :] - m_old)
            l = l + jnp.sum(p, axis=0, keepdims=True)
            acc = acc + jnp.dot(vt_ref[:, ks], p.astype(BF16), preferred_element_type=F32)
        m_new_ref[...] = m_new
        o_ref[...] = (acc / l).astype(BF16)

    @pl.when(s % 2 == 0)
    def _():
        step(sta_ref, ma_ref, stb_ref, mb_ref)

    @pl.when(s % 2 == 1)
    def _():
        step(stb_ref, mb_ref, sta_ref, ma_ref)


def _attn_b(qb, kb, vbt, off, n_batch, s_len):
    tq = TQ_B
    nq = s_len // tq
    qbase, kbase = off // tq, off // s_len
    n_items = n_batch * B_HEADS * nq

    def split(item):
        return item // (B_HEADS * nq), (item // nq) % B_HEADS, item % nq

    def q_map(s):
        b, h, qi = split(jnp.minimum(s, n_items - 1))
        return (qbase + b * nq + qi, h)

    def k_map(s):
        b, h, _ = split(jnp.minimum(s, n_items - 1))
        return (kbase + b, h)

    def v_map(s):
        b, h, _ = split(jnp.maximum(s - 1, 0))
        return (h, kbase + b)

    def o_map(s):
        b, h, qi = split(jnp.maximum(s - 1, 0))
        return (h, b * nq + qi)

    return pl.pallas_call(
        _attn_b_kernel,
        grid=(n_items + 1,),
        in_specs=[pl.BlockSpec((tq, HEAD_PAD), q_map),
                  pl.BlockSpec((s_len, HEAD_PAD), k_map),
                  pl.BlockSpec((B_V, s_len), v_map)],
        out_specs=pl.BlockSpec((B_V, tq), o_map),
        out_shape=jax.ShapeDtypeStruct((B_HEADS * B_V, n_batch * s_len), BF16),
        scratch_shapes=[pltpu.VMEM((s_len, tq), F32), pltpu.VMEM((s_len, tq), F32),
                        pltpu.VMEM((1, tq), F32), pltpu.VMEM((1, tq), F32)],
        compiler_params=pltpu.CompilerParams(dimension_semantics=("arbitrary",),
                                             vmem_limit_bytes=VMEM_LIMIT),
        name=f"attn_b_s{s_len}",
    )(qb, kb, vbt)


def _mix_kernel(xp_ref, xs_ref, o0_ref, o1_ref, o2_ref, l0_ref, l1_ref, l2_ref, spread_ref,
                obp_ref, obs_ref, wout_ref,
                g_ref, b_ref, wrh_ref, wrl_ref, rb_ref,
                h_ref, hp_ref, tope_ref, pos_ref, gate_ref, cnt_out_ref, cnt_ref, *, n_prompt_tiles):
    i = pl.program_id(0)
    tm = xp_ref.shape[0]

    @pl.when(i == 0)
    def _():
        cnt_ref[...] = jnp.zeros_like(cnt_ref)

    l0, l1, l2 = l0_ref[...], l1_ref[...], l2_ref[...]
    lmax = jnp.maximum(jnp.maximum(l0, l1), l2)
    e0, e1, e2 = jnp.exp2(l0 - lmax), jnp.exp2(l1 - lmax), jnp.exp2(l2 - lmax)
    inv = 1.0 / (e0 + e1 + e2)
    spread = spread_ref[...]

    def per_lane(w):
        hi = w.astype(BF16)
        lo = (w - hi.astype(F32)).astype(BF16)
        return (jnp.dot(hi, spread, preferred_element_type=F32)
                + jnp.dot(lo, spread, preferred_element_type=F32))

    oa = (per_lane(e0 * inv) * o0_ref[...].astype(F32) + per_lane(e1 * inv) * o1_ref[...].astype(F32)
          + per_lane(e2 * inv) * o2_ref[...].astype(F32))
    is_prompt = i < n_prompt_tiles
    obt = jnp.where(is_prompt, obp_ref[...], obs_ref[...])
    mix = (jnp.dot(oa.astype(BF16), wout_ref[0:A_WIDTH, :], preferred_element_type=F32)
           + lax.dot_general(obt, wout_ref[A_WIDTH:, :], (((0,), (0,)), ((), ())),
                             preferred_element_type=F32))
    x = jnp.where(is_prompt, xp_ref[...], xs_ref[...])
    h = _layer_norm(ALPHA * x + mix, g_ref[...], b_ref[...])
    h_ref[...] = h
    _slab_store(hp_ref, h)

    h_hi = h.astype(BF16)
    h_lo = (h - h_hi.astype(F32)).astype(BF16)
    wrh = wrh_ref[...]
    logits = _nt_dot(wrh, h_hi) + _nt_dot(wrh, h_lo) + _nt_dot(wrl_ref[...], h_hi)
    scores = jax.nn.sigmoid(logits)
    sel = scores + rb_ref[...]

    sub = lax.broadcasted_iota(I32, (GROUP_SIZE, tm), 0).astype(F32)
    gscore = []
    for g in range(N_GROUPS):
        sg = sel[g * GROUP_SIZE:(g + 1) * GROUP_SIZE, :]
        m1 = jnp.max(sg, axis=0, keepdims=True)
        first = jnp.min(jnp.where(sg == m1, sub, float(GROUP_SIZE)), axis=0, keepdims=True)
        m2 = jnp.max(jnp.where(sub == first, -jnp.inf, sg), axis=0, keepdims=True)
        gscore.append(m1 + m2)
    cands = []
    for g in range(N_GROUPS):
        beaten = jnp.zeros((1, tm), F32)
        for g2 in range(N_GROUPS):
            if g2 == g:
                continue
            wins = (gscore[g2] > gscore[g]) | ((gscore[g2] == gscore[g]) & (g2 < g))
            beaten = beaten + wins.astype(F32)
        keep = beaten < float(TOPK_GROUPS)
        sg = sel[g * GROUP_SIZE:(g + 1) * GROUP_SIZE, :]
        cands.append(jnp.where(keep, sg, NEG_BIG))
    cand = jnp.concatenate(cands, axis=0)

    eidx = lax.broadcasted_iota(I32, (N_EXPERTS, tm), 0).astype(F32)
    picked_idx, picked_gate = [], []
    onehot = jnp.zeros((N_EXPERTS, tm), F32)
    for _ in range(TOP_K):
        mx = jnp.max(cand, axis=0, keepdims=True)
        fi = jnp.min(jnp.where(cand == mx, eidx, float(N_EXPERTS)), axis=0, keepdims=True)
        pick = eidx == fi
        picked_idx.append(fi)
        picked_gate.append(jnp.sum(jnp.where(pick, scores, 0.0), axis=0, keepdims=True))
        onehot = onehot + pick.astype(F32)
        cand = jnp.where(pick, -jnp.inf, cand)
    gsum = picked_gate[0]
    for k in range(1, TOP_K):
        gsum = gsum + picked_gate[k]

    tri = (lax.broadcasted_iota(I32, (tm, tm), 0) < lax.broadcasted_iota(I32, (tm, tm), 1))
    before = jnp.dot(onehot.astype(BF16), tri.astype(BF16), preferred_element_type=F32)
    rank = before + cnt_ref[:, 0:1]
    for k in range(TOP_K):
        pick = eidx == picked_idx[k]
        tope_ref[k:k + 1, :] = picked_idx[k].astype(I32)
        pos_ref[k:k + 1, :] = jnp.sum(jnp.where(pick, rank, 0.0), axis=0, keepdims=True).astype(I32)
        gate_ref[k:k + 1, :] = picked_gate[k] / gsum * ROUTED_SCALE
    cnt_ref[...] = cnt_ref[...] + jnp.sum(onehot, axis=1, keepdims=True)
    cnt_out_ref[...] = cnt_ref[...]


def _mix(xp, xs, outs, lses, obp, obs, wout, g, b, wrh, wrl, rb):
    tm = TM_MIX
    n1 = xp.shape[0] // tm
    n2 = xs.shape[0] // tm
    t_all = xp.shape[0] + xs.shape[0]
    full = lambda a: pl.BlockSpec(a.shape, lambda i: (0,) * a.ndim)
    row = lambda cols: pl.BlockSpec((tm, cols), lambda i: (i, 0))
    prow = lambda cols: pl.BlockSpec((tm, cols), lambda i: (jnp.minimum(i, n1 - 1), 0))
    srow = lambda cols: pl.BlockSpec((tm, cols), lambda i: (jnp.maximum(i - n1, 0), 0))
    col = pl.BlockSpec((TOP_K, tm), lambda i: (0, i))
    head_of_lane = np.arange(A_WIDTH) // A_HEAD_DIM
    spread = jnp.asarray(np.arange(LANES)[:, None] == head_of_lane[None, :], BF16)
    return pl.pallas_call(
        functools.partial(_mix_kernel, n_prompt_tiles=n1),
        grid=(n1 + n2,),
        in_specs=[
            prow(D_MODEL), srow(D_MODEL),
            row(A_WIDTH), row(A_WIDTH), row(A_WIDTH), row(LANES), row(LANES), row(LANES), full(spread),
            pl.BlockSpec((B_HEADS * B_V, tm), lambda i: (0, jnp.minimum(i, n1 - 1))),
            pl.BlockSpec((B_HEADS * B_V, tm), lambda i: (0, jnp.maximum(i - n1, 0))),
            full(wout), full(g), full(b), full(wrh), full(wrl), full(rb),
        ],
        out_specs=[row(D_MODEL), pl.BlockSpec((SLAB * tm, LANES), lambda i: (i, 0)), col, col, col,
                   pl.BlockSpec((N_EXPERTS, LANES), lambda i: (0, 0))],
        out_shape=[jax.ShapeDtypeStruct((t_all, D_MODEL), F32),
                   jax.ShapeDtypeStruct((SLAB * t_all, LANES), F32),
                   jax.ShapeDtypeStruct((TOP_K, t_all), I32),
                   jax.ShapeDtypeStruct((TOP_K, t_all), I32),
                   jax.ShapeDtypeStruct((TOP_K, t_all), F32),
                   jax.ShapeDtypeStruct((N_EXPERTS, LANES), F32)],
        scratch_shapes=[pltpu.VMEM((N_EXPERTS, LANES), F32)],
        compiler_params=pltpu.CompilerParams(dimension_semantics=("arbitrary",),
                                             vmem_limit_bytes=VMEM_LIMIT),
        name="mix_router",
    )(xp, xs, *outs, *lses, spread, obp, obs, wout, g, b, wrh, wrl, rb)


def _dispatch_kernel(zstart_ref, nused_ref, h_ref, tope_ref, pos_ref, pstart_ref, buf_ref, dest_ref,
                     dsm_ref, zero_ref, sem_ref, zsem_ref):
    i = pl.program_id(0)
    tm = tope_ref.shape[1]
    tile_rows = SLAB * ROW_BLOCK

    @pl.when(i == 0)
    def _():
        zero_ref[...] = jnp.zeros_like(zero_ref)

        def zfill(e, carry):
            start = pl.multiple_of(zstart_ref[e], SLAB)
            pltpu.make_async_copy(zero_ref, buf_ref.at[pl.ds(start, tile_rows), :], zsem_ref).start()
            return carry

        lax.fori_loop(0, N_EXPERTS, zfill, 0)
        span = buf_ref.at[pl.ds(0, N_EXPERTS * tile_rows), :]
        pltpu.make_async_copy(span, span, zsem_ref).wait()

        def ztail(j, carry):
            start = pl.multiple_of(j * tile_rows, tile_rows)
            pltpu.make_async_copy(zero_ref, buf_ref.at[pl.ds(start, tile_rows), :], zsem_ref).start()
            return carry

        def zwait(j, carry):
            pltpu.make_async_copy(zero_ref, buf_ref.at[pl.ds(0, tile_rows), :], zsem_ref).wait()
            return carry

        n_tiles = buf_ref.shape[0] // tile_rows
        lax.fori_loop(nused_ref[0], n_tiles, ztail, 0)
        lax.fori_loop(nused_ref[0], n_tiles, zwait, 0)

    eidx = lax.broadcasted_iota(I32, (N_EXPERTS, tm), 0)
    pstart = pstart_ref[...]
    for k in range(TOP_K):
        hit = eidx == tope_ref[k:k + 1, :]
        base = jnp.sum(jnp.where(hit, pstart, 0.0), axis=0, keepdims=True)
        dest_ref[k:k + 1, :] = (pos_ref[k:k + 1, :] + base.astype(I32)) * SLAB
    pltpu.sync_copy(dest_ref, dsm_ref)

    def scatter(t, carry):
        src = h_ref.at[pl.ds(pl.multiple_of(t * SLAB, SLAB), SLAB), :]
        for k in range(TOP_K):
            dst = buf_ref.at[pl.ds(pl.multiple_of(dsm_ref[k, t], SLAB), SLAB), :]
            pltpu.make_async_copy(src, dst, sem_ref).start(priority=k % 2)
        return carry

    lax.fori_loop(0, tm, scatter, 0, unroll=8)
    span = buf_ref.at[pl.ds(0, TOP_K * tm * SLAB), :]
    pltpu.make_async_copy(span, span, sem_ref).wait()


def _dispatch(h, tope, pos, pstart, zstart, nused, n_rows):
    tm = TM_DISP
    t_all = tope.shape[1]
    col = pl.BlockSpec((TOP_K, tm), lambda i, z, nu: (0, i))
    return pl.pallas_call(
        _dispatch_kernel,
        grid_spec=pltpu.PrefetchScalarGridSpec(
            num_scalar_prefetch=2, grid=(t_all // tm,),
            in_specs=[pl.BlockSpec((SLAB * tm, LANES), lambda i, z, nu: (i, 0)), col, col,
                      pl.BlockSpec((N_EXPERTS, 1), lambda i, z, nu: (0, 0))],
            out_specs=[pl.BlockSpec(memory_space=pl.ANY), col],
            scratch_shapes=[pltpu.SMEM((TOP_K, tm), I32),
                            pltpu.VMEM((SLAB * ROW_BLOCK, LANES), F32),
                            pltpu.SemaphoreType.DMA, pltpu.SemaphoreType.DMA]),
        out_shape=[jax.ShapeDtypeStruct((SLAB * n_rows, LANES), F32),
                   jax.ShapeDtypeStruct((TOP_K, t_all), I32)],
        compiler_params=pltpu.CompilerParams(dimension_semantics=("arbitrary",),
                                             vmem_limit_bytes=VMEM_LIMIT),
        name="dispatch",
    )(zstart, nused, h, tope, pos, pstart)


def _expert_kernel(tfirst_ref, ntile_ref, nused_ref, x_hbm, wg_ref, wu_ref, wd_ref, o_hbm,
                   xbuf, obuf, wgb_ref, wub_ref, wdb_ref, xsem, osem):
    e = pl.program_id(0)
    tile_rows = SLAB * ROW_BLOCK
    nused = nused_ref[0]

    def x_copy(g, slot):
        start = pl.multiple_of(g * tile_rows, tile_rows)
        return pltpu.make_async_copy(x_hbm.at[pl.ds(start, tile_rows), :], xbuf.at[slot], xsem.at[slot])

    def o_copy(g, slot):
        start = pl.multiple_of(g * tile_rows, tile_rows)
        return pltpu.make_async_copy(obuf.at[slot], o_hbm.at[pl.ds(start, tile_rows), :], osem.at[slot])

    @pl.when(e == 0)
    def _():
        for j in range(N_XBUF - 1):
            @pl.when(j < nused)
            def _():
                x_copy(j, j).start()

    n_e = ntile_ref[e]

    @pl.when(n_e > 0)
    def _():
        wgb_ref[...] = wg_ref[...].astype(BF16)
        wub_ref[...] = wu_ref[...].astype(BF16)
        wdb_ref[...] = wd_ref[...].astype(BF16)

    def tile(j, carry):
        g = tfirst_ref[e] + j
        ahead = g + (N_XBUF - 1)

        @pl.when(ahead < nused)
        def _():
            x_copy(ahead, ahead % N_XBUF).start()

        slot = g % N_XBUF
        x_copy(g, slot).wait()
        x = _slab_load(xbuf.at[slot], ROW_BLOCK).astype(BF16)
        gt = jnp.dot(x, wgb_ref[...], preferred_element_type=F32)
        up = jnp.dot(x, wub_ref[...], preferred_element_type=F32)
        hmid = (gt * jax.nn.sigmoid(gt) * up).astype(BF16)
        out = jnp.dot(hmid, wdb_ref[...], preferred_element_type=F32)
        oslot = g % N_OBUF

        @pl.when(g >= N_OBUF)
        def _():
            o_copy(g - N_OBUF, oslot).wait()

        _slab_store(obuf.at[oslot], out)
        o_copy(g, oslot).start()
        return carry

    lax.fori_loop(0, n_e, tile, 0)

    @pl.when(e == N_EXPERTS - 1)
    def _():
        for j in range(N_OBUF):
            @pl.when(nused > j)
            def _():
                o_copy(nused - 1 - j, (nused - 1 - j) % N_OBUF).wait()


def _experts(buf, tfirst, ntile, nused, w_gate, w_up, w_down):
    tile_rows = SLAB * ROW_BLOCK
    wmap = lambda e, tf, nt, nu: (e, 0, 0)
    return pl.pallas_call(
        _expert_kernel,
        grid_spec=pltpu.PrefetchScalarGridSpec(
            num_scalar_prefetch=3, grid=(N_EXPERTS,),
            in_specs=[pl.BlockSpec(memory_space=pl.ANY),
                      pl.BlockSpec((None, D_MODEL, EXPERT_HIDDEN), wmap),
                      pl.BlockSpec((None, D_MODEL, EXPERT_HIDDEN), wmap),
                      pl.BlockSpec((None, EXPERT_HIDDEN, D_MODEL), wmap)],
            out_specs=pl.BlockSpec(memory_space=pl.ANY),
            scratch_shapes=[pltpu.VMEM((N_XBUF, tile_rows, LANES), F32),
                            pltpu.VMEM((N_OBUF, tile_rows, LANES), F32),
                            pltpu.VMEM((D_MODEL, EXPERT_HIDDEN), BF16),
                            pltpu.VMEM((D_MODEL, EXPERT_HIDDEN), BF16),
                            pltpu.VMEM((EXPERT_HIDDEN, D_MODEL), BF16),
                            pltpu.SemaphoreType.DMA((N_XBUF,)),
                            pltpu.SemaphoreType.DMA((N_OBUF,))]),
        out_shape=jax.ShapeDtypeStruct(buf.shape, F32),
        input_output_aliases={3: 0},
        compiler_params=pltpu.CompilerParams(dimension_semantics=("arbitrary",),
                                             vmem_limit_bytes=VMEM_LIMIT),
        name="experts",
    )(tfirst, ntile, nused, buf, w_gate, w_up, w_down)


def _combine_kernel(h_ref, gate_ref, dest_ref, eo_ref, wsg_ref, wsu_ref, wsd_ref, g_ref, b_ref,
                    yp_ref, ys_ref, dsm_ref, rows_ref, sem_ref, *, n_prompt_tiles):
    i = pl.program_id(0)
    tm = gate_ref.shape[0]
    pltpu.sync_copy(dest_ref, dsm_ref)

    def gather(t, carry):
        for k in range(TOP_K):
            src = eo_ref.at[pl.ds(pl.multiple_of(dsm_ref[k, t], SLAB), SLAB), :]
            dst = rows_ref.at[pl.ds(pl.multiple_of((k * tm + t) * SLAB, SLAB), SLAB), :]
            pltpu.make_async_copy(src, dst, sem_ref).start(priority=k % 2)
        return carry

    lax.fori_loop(0, tm, gather, 0, unroll=8)

    h = h_ref[...]
    hb = h.astype(BF16)
    sg = jnp.dot(hb, wsg_ref[...], preferred_element_type=F32)
    su = jnp.dot(hb, wsu_ref[...], preferred_element_type=F32)
    shared = jnp.dot((sg * jax.nn.sigmoid(sg) * su).astype(BF16), wsd_ref[...],
                     preferred_element_type=F32)

    pltpu.make_async_copy(eo_ref.at[pl.ds(0, TOP_K * tm * SLAB), :], rows_ref, sem_ref).wait()
    gate = gate_ref[...]
    routed = gate[:, 0:1] * _slab_load(rows_ref, tm)
    for k in range(1, TOP_K):
        routed = routed + gate[:, k:k + 1] * _slab_load(rows_ref, tm, k * tm)
    y = _layer_norm(ALPHA * h + (routed + shared), g_ref[...], b_ref[...])

    @pl.when(i < n_prompt_tiles)
    def _():
        yp_ref[...] = y

    @pl.when(i >= n_prompt_tiles)
    def _():
        ys_ref[...] = y


def _combine(h, gate_t, dest, eo, wsg, wsu, wsd, g, b, t_prompt):
    tm = TM_COMB
    t_all = gate_t.shape[0]
    n1 = t_prompt // tm
    n2 = (t_all - t_prompt) // tm
    full = lambda a: pl.BlockSpec(a.shape, lambda i: (0,) * a.ndim)
    return pl.pallas_call(
        functools.partial(_combine_kernel, n_prompt_tiles=n1),
        grid=(n1 + n2,),
        in_specs=[pl.BlockSpec((tm, D_MODEL), lambda i: (i, 0)),
                  pl.BlockSpec((tm, TOP_K), lambda i: (i, 0)),
                  pl.BlockSpec((TOP_K, tm), lambda i: (0, i)),
                  pl.BlockSpec(memory_space=pl.ANY),
                  full(wsg), full(wsu), full(wsd), full(g), full(b)],
        out_specs=[pl.BlockSpec((tm, D_MODEL), lambda i: (jnp.minimum(i, n1 - 1), 0)),
                   pl.BlockSpec((tm, D_MODEL), lambda i: (jnp.maximum(i - n1, 0), 0))],
        out_shape=[jax.ShapeDtypeStruct((t_prompt, D_MODEL), F32),
                   jax.ShapeDtypeStruct((t_all - t_prompt, D_MODEL), F32)],
        scratch_shapes=[pltpu.SMEM((TOP_K, tm), I32),
                        pltpu.VMEM((TOP_K * tm * SLAB, LANES), F32),
                        pltpu.SemaphoreType.DMA],
        compiler_params=pltpu.CompilerParams(dimension_semantics=("arbitrary",),
                                             vmem_limit_bytes=VMEM_LIMIT),
        name="combine",
    )(h, gate_t, dest, eo, wsg, wsu, wsd, g, b)


def _rope_tables(s_max):
    inv_freq = ROPE_BASE ** (-jnp.arange(0, B_ROPE, 2, dtype=F32) / B_ROPE)
    ang = jnp.arange(s_max, dtype=F32)[:, None] * inv_freq[None, :]
    cos, sin = jnp.cos(ang), jnp.sin(ang)
    ones = jnp.ones((s_max, B_NOPE), F32)
    zeros_n = jnp.zeros((s_max, B_NOPE), F32)
    zeros_p = jnp.zeros((s_max, HEAD_PAD - B_NOPE - B_ROPE), F32)
    return (jnp.concatenate([ones, cos, cos, zeros_p], axis=1),
            jnp.concatenate([zeros_n, sin, sin, zeros_p], axis=1))


def _rot_cols(w):
    half = B_ROPE // 2
    return jnp.concatenate([-w[..., half:], w[..., :half]], axis=-1)


def _layout_weights(w_in, w_uq, w_uk):
    c_kr = 3 * A_WIDTH + Q_LORA + KV_LORA
    w_kr = w_in[:, c_kr:c_kr + B_ROPE]
    pad_l = jnp.zeros((D_MODEL, B_NOPE), F32)
    pad_r = jnp.zeros((D_MODEL, HEAD_PAD - B_NOPE - B_ROPE), F32)
    w1 = jnp.concatenate([w_in[:, :c_kr], pad_l, w_kr, pad_r, pad_l, _rot_cols(w_kr), pad_r], axis=1)
    wq3 = w_uq.reshape(Q_LORA, B_HEADS, B_NOPE + B_ROPE)
    nope, rope = wq3[..., :B_NOPE], wq3[..., B_NOPE:]
    zpad = jnp.zeros((Q_LORA, B_HEADS, HEAD_PAD - B_NOPE - B_ROPE), F32)
    wq = jnp.concatenate([nope, rope, zpad], axis=-1).reshape(Q_LORA, B_HEADS * HEAD_PAD)
    wqr = jnp.concatenate([jnp.zeros_like(nope), _rot_cols(rope), zpad], axis=-1)
    wqr = wqr.reshape(Q_LORA, B_HEADS * HEAD_PAD)
    wk3 = w_uk.reshape(KV_LORA, B_HEADS, B_NOPE)
    wuk = jnp.concatenate([wk3, jnp.zeros((KV_LORA, B_HEADS, HEAD_PAD - B_NOPE), F32)], axis=-1)
    wuk = wuk.reshape(KV_LORA, B_HEADS * HEAD_PAD)
    return w1.astype(BF16), wq.astype(BF16), wqr.astype(BF16), wuk.astype(BF16)


def _forward(x_prompt, x_sample, w_in, w_out, ln1_g, ln1_b, q_norm_g, w_uq, kv_norm_g, w_uk, w_uv,
             w_router, router_bias, w_gate, w_up, w_down, ws_gate, ws_up, ws_down, ln2_g, ln2_b):
    b1, s1, _ = x_prompt.shape
    b2, s2, _ = x_sample.shape
    t1, t2 = b1 * s1, b2 * s2
    t_all = t1 + t2
    xp = x_prompt.reshape(t1, D_MODEL)
    xs = x_sample.reshape(t2, D_MODEL)
    seqs = [(b * s1, s1) for b in range(b1)] + [(t1 + b * s2, s2) for b in range(b2)]
    assert t1 % s2 == 0 and s1 % (TQ_A * 16) == 0 and s2 % (TQ_A * 16) == 0

    w1, wq, wqr, wuk = _layout_weights(w_in, w_uq, w_uk)
    cos_t, sin_t = _rope_tables(max(s1, s2))
    (qa, ka, va, q4, k4, v4, q16, k16, v16, qb, kb, vbt) = _proj(
        xp, xs, w1, q_norm_g.reshape(1, -1), kv_norm_g.reshape(1, -1), wq, wqr, wuk,
        w_uv.T.astype(BF16), cos_t, sin_t, s1, s2)

    outs, lses = [], []
    for (_, dil), qkv in zip(A_PATTERNS, ((qa, ka, va), (q4, k4, v4), (q16, k16, v16))):
        o, lse = _attn_a(*qkv, seqs, dil)
        outs.append(o)
        lses.append(lse)
    obp = _attn_b(qb, kb, vbt, 0, b1, s1)
    obs = _attn_b(qb, kb, vbt, t1, b2, s2)

    wr_t = w_router.T
    wr_hi = wr_t.astype(BF16)
    wr_lo = (wr_t - wr_hi.astype(F32)).astype(BF16)
    h, h_slab, tope, pos, gate, counts = _mix(xp, xs, outs, lses, obp, obs, w_out.astype(BF16),
                                                ln1_g.reshape(1, -1), ln1_b.reshape(1, -1),
                                                wr_hi, wr_lo, router_bias.reshape(-1, 1))

    cnt = counts[:, 0].astype(I32)
    padded = ((cnt + ROW_BLOCK - 1) // ROW_BLOCK) * ROW_BLOCK
    pend = jnp.cumsum(padded)
    pstart = pend - padded
    n_tiles = (t_all * TOP_K) // ROW_BLOCK + N_EXPERTS + 1
    nused = (pend[-1:] // ROW_BLOCK).astype(I32)

    buf, dest = _dispatch(h_slab, tope, pos, pstart.astype(F32).reshape(-1, 1),
                          ((pstart + cnt) * SLAB).astype(I32), nused, n_tiles * ROW_BLOCK)
    eo = _experts(buf, (pstart // ROW_BLOCK).astype(I32), (padded // ROW_BLOCK).astype(I32), nused,
                  w_gate, w_up, w_down)
    yp, ys = _combine(h, gate.T, dest, eo, ws_gate.astype(BF16), ws_up.astype(BF16),
                      ws_down.astype(BF16), ln2_g.reshape(1, -1), ln2_b.reshape(1, -1), t1)
    return yp.reshape(b1, s1, D_MODEL), ys.reshape(b2, s2, D_MODEL)


def kernel(x_prompt, x_sample, w_in, w_out, ln1_g, ln1_b, q_norm_g, w_uq, kv_norm_g, w_uk, w_uv,
           w_router, router_bias, w_gate, w_up, w_down, ws_gate, ws_up, ws_down, ln2_g, ln2_b):
    params = (w_in, w_out, ln1_g, ln1_b, q_norm_g, w_uq, kv_norm_g, w_uk, w_uv, w_router, router_bias,
              w_gate, w_up, w_down, ws_gate, ws_up, ws_down, ln2_g, ln2_b)
    assert all(p.shape[0] == 1 for p in params), "one encoder layer"
    return _forward(x_prompt, x_sample, *[p.reshape(p.shape[1:]) for p in params])
```

```python
import functools
import math

import numpy as np
import jax
import jax.numpy as jnp
from jax import lax
from jax.experimental import pallas as pl
from jax.experimental.pallas import tpu as pltpu

F32 = jnp.float32
BF16 = jnp.bfloat16
I32 = jnp.int32

D_MODEL = 1024
A_HEADS = 8
A_HEAD_DIM = 64
A_WIDTH = A_HEADS * A_HEAD_DIM
A_PATTERNS = ((128, 1), (512, 4), (2048, 16))
A_HALF = 64
B_HEADS = 8
B_NOPE = 64
B_ROPE = 32
B_V = 64
Q_LORA = 256
KV_LORA = 128
ROPE_BASE = 10000.0
N_EXPERTS = 256
TOP_K = 8
N_GROUPS = 8
GROUP_SIZE = N_EXPERTS // N_GROUPS
TOPK_GROUPS = 4
EXPERT_HIDDEN = 256
ROUTED_SCALE = 2.5
LN_EPS = 1e-5
RMS_EPS = 1e-6
NEG_BIG = -1e30
ALPHA = 2.0 ** 0.25
LOG2E = 1.4426950408889634

LANES = 128
HEAD_PAD = 128
SLAB = D_MODEL // LANES
W1_COLS = 3 * A_WIDTH + Q_LORA + KV_LORA + 2 * LANES

TM_PROJ = 256
TQ_A = 128
A_POS_TILES = {1: 4, 4: 2, 16: 1}
PERM_GROUP = 256
TQ_B = 512
KEY_CHUNK_B = 512
TM_MIX = 256
TM_DISP = 256
ROW_BLOCK = 256
TM_COMB = 128
N_XBUF = 3
N_OBUF = 2
VMEM_LIMIT = 48 * 1024 * 1024


def _slab_load(ref, n_tok, tok0=0):
    return jnp.concatenate([ref[pl.ds(tok0 * SLAB + c, n_tok, stride=SLAB), :] for c in range(SLAB)],
                           axis=1)


def _slab_store(ref, val):
    for c in range(SLAB):
        ref[pl.ds(c, val.shape[0], stride=SLAB), :] = val[:, c * LANES:(c + 1) * LANES]


def _slab_load_bf16(ref, stage_ref, n_tok, tok0=0):
    stage_ref[...] = ref[tok0 * SLAB:(tok0 + n_tok) * SLAB, :].astype(F32)
    return _slab_load(stage_ref, n_tok)


def _slab_store_bf16(ref, stage_ref, val):
    _slab_store(stage_ref, val)
    ref[...] = stage_ref[...].astype(BF16)


def _nt_dot(a, b):
    return lax.dot_general(a, b, (((1,), (1,)), ((), ())), preferred_element_type=F32)


def _layer_norm(x, g, b):
    mu = jnp.mean(x, axis=-1, keepdims=True)
    xc = x - mu
    var = jnp.mean(xc * xc, axis=-1, keepdims=True)
    return xc * lax.rsqrt(var + LN_EPS) * g + b


def _rms_norm(x, g):
    return x * lax.rsqrt(jnp.mean(x * x, axis=-1, keepdims=True) + RMS_EPS) * g


def _proj_kernel(xp_ref, xs_ref, w1_ref, qg_ref, kvg_ref, wq_ref, wqr_ref, wuk_ref, wuvt_ref,
                 cos_ref, sin_ref, perm4_ref, perm16_ref,
                 qa_ref, ka_ref, va_ref, q4_ref, k4_ref, v4_ref, q16_ref, k16_ref, v16_ref,
                 qb_ref, kb_ref, vbt_ref, *, n_prompt_tiles):
    i = pl.program_id(0)
    tm = xp_ref.shape[0]
    x = jnp.where(i < n_prompt_tiles, xp_ref[...], xs_ref[...]).astype(BF16)
    p = jnp.dot(x, w1_ref[...], preferred_element_type=F32)
    qa = (p[:, 0:A_WIDTH] * (A_HEAD_DIM ** -0.5 * LOG2E)).astype(BF16)
    ka = p[:, A_WIDTH:2 * A_WIDTH].astype(BF16)
    va = p[:, 2 * A_WIDTH:3 * A_WIDTH].astype(BF16)
    qa_ref[...] = qa
    ka_ref[...] = ka
    va_ref[...] = va
    qkv = jnp.concatenate([qa, ka, va], axis=1)
    for dil, perm_ref, outs in ((4, perm4_ref, (q4_ref, k4_ref, v4_ref)),
                                (16, perm16_ref, (q16_ref, k16_ref, v16_ref))):
        cm = jnp.dot(perm_ref[...], qkv, preferred_element_type=F32).astype(BF16)
        rows = tm // dil
        for r in range(dil):
            for j, o_ref in enumerate(outs):
                o_ref[:, r * A_WIDTH:(r + 1) * A_WIDTH] = cm[r * rows:(r + 1) * rows,
                                                             j * A_WIDTH:(j + 1) * A_WIDTH]
    c0 = 3 * A_WIDTH
    cq = p[:, c0:c0 + Q_LORA]
    ckv = p[:, c0 + Q_LORA:c0 + Q_LORA + KV_LORA]
    kr = p[:, c0 + Q_LORA + KV_LORA:c0 + Q_LORA + KV_LORA + LANES]
    krr = p[:, c0 + Q_LORA + KV_LORA + LANES:c0 + Q_LORA + KV_LORA + 2 * LANES]
    cos = cos_ref[...]
    sin = sin_ref[...]
    cos8 = jnp.concatenate([cos] * B_HEADS, axis=1)
    sin8 = jnp.concatenate([sin] * B_HEADS, axis=1)
    cqn = _rms_norm(cq, qg_ref[...]).astype(BF16)
    q = jnp.dot(cqn, wq_ref[...], preferred_element_type=F32)
    qr = jnp.dot(cqn, wqr_ref[...], preferred_element_type=F32)
    qscale = (B_NOPE + B_ROPE) ** -0.5 * LOG2E
    qb_ref[...] = ((q * cos8 + qr * sin8) * qscale).astype(BF16)
    ckvn = _rms_norm(ckv, kvg_ref[...]).astype(BF16)
    kn = jnp.dot(ckvn, wuk_ref[...], preferred_element_type=F32)
    krope = kr * cos + krr * sin
    kb_ref[...] = (kn + jnp.concatenate([krope] * B_HEADS, axis=1)).astype(BF16)
    vbt_ref[...] = _nt_dot(wuvt_ref[...], ckvn).astype(BF16)


def _class_perm(tm, dil):
    rows = tm // dil
    c = np.arange(tm)
    src = (c % rows) * dil + c // rows
    perm = np.zeros((tm, tm), np.float32)
    perm[c, src] = 1.0
    return jnp.asarray(perm, BF16)


def _proj(xp, xs, w1, qg, kvg, wq, wqr, wuk, wuvt, cos_t, sin_t, s_prompt, s_sample):
    tm = TM_PROJ
    n1 = xp.shape[0] // tm
    n2 = xs.shape[0] // tm
    t_all = xp.shape[0] + xs.shape[0]
    pt, st = s_prompt // tm, s_sample // tm
    assert tm == PERM_GROUP
    perm4, perm16 = _class_perm(tm, 4), _class_perm(tm, 16)

    def tab_idx(i):
        return (jnp.where(i < n1, i % pt, (i - n1) % st), 0)

    full = lambda shape: pl.BlockSpec(shape, lambda i: (0, 0))
    row = lambda cols: pl.BlockSpec((tm, cols), lambda i: (i, 0))
    cls = lambda dil: pl.BlockSpec((tm // dil, dil * A_WIDTH), lambda i: (i, 0))
    cls_shape = lambda dil: jax.ShapeDtypeStruct((t_all // dil, dil * A_WIDTH), BF16)
    return pl.pallas_call(
        functools.partial(_proj_kernel, n_prompt_tiles=n1),
        grid=(n1 + n2,),
        in_specs=[
            pl.BlockSpec((tm, D_MODEL), lambda i: (jnp.minimum(i, n1 - 1), 0)),
            pl.BlockSpec((tm, D_MODEL), lambda i: (jnp.maximum(i - n1, 0), 0)),
            full(w1.shape), full(qg.shape), full(kvg.shape), full(wq.shape), full(wqr.shape),
            full(wuk.shape), full(wuvt.shape),
            pl.BlockSpec((tm, LANES), tab_idx), pl.BlockSpec((tm, LANES), tab_idx),
            full(perm4.shape), full(perm16.shape),
        ],
        out_specs=[row(A_WIDTH)] * 3 + [cls(4)] * 3 + [cls(16)] * 3
        + [row(B_HEADS * HEAD_PAD), row(B_HEADS * HEAD_PAD),
           pl.BlockSpec((B_HEADS * B_V, tm), lambda i: (0, i))],
        out_shape=[jax.ShapeDtypeStruct((t_all, A_WIDTH), BF16)] * 3
        + [cls_shape(4)] * 3 + [cls_shape(16)] * 3
        + [jax.ShapeDtypeStruct((t_all, B_HEADS * HEAD_PAD), BF16)] * 2
        + [jax.ShapeDtypeStruct((B_HEADS * B_V, t_all), BF16)],
        compiler_params=pltpu.CompilerParams(dimension_semantics=("arbitrary",),
                                             vmem_limit_bytes=VMEM_LIMIT),
        name="proj",
    )(xp, xs, w1, qg, kvg, wq, wqr, wuk, wuvt, cos_t, sin_t, perm4, perm16)


def _attn_a_kernel(qblk_ref, pblk_ref, nblk_ref, tstart_ref, llen_ref,
                   q_ref, kp_ref, kc_ref, kn_ref, vp_ref, vc_ref, vn_ref, bias_ref, perm_ref,
                   o_ref, lse_ref, cm_ref, *, n_pos, n_cls):
    n = pl.program_id(0)
    tq, wk = TQ_A, TQ_A + 2 * A_HALF
    lane = lax.broadcasted_iota(I32, (tq, LANES), 1)
    low = lane < A_HEAD_DIM
    colbase = lax.broadcasted_iota(I32, (1, wk), 1) + (tstart_ref[n] - A_HALF)
    cls_len = llen_ref[n]

    def one_class(r):
        cols = slice(0, A_WIDTH) if n_cls == 1 else pl.ds(pl.multiple_of(r * A_WIDTH, A_WIDTH), A_WIDTH)
        kcat = jnp.concatenate([kp_ref[:, cols], kc_ref[:, cols], kn_ref[:, cols]], axis=0)
        vcat = jnp.concatenate([vp_ref[:, cols], vc_ref[:, cols], vn_ref[:, cols]], axis=0)
        for j in range(n_pos):
            rows = slice(j * tq, (j + 1) * tq)
            q = q_ref[rows, cols]
            kw, vw = kcat[j * tq:j * tq + wk, :], vcat[j * tq:j * tq + wk, :]
            col = colbase + j * tq
            colpen = jnp.where((col >= 0) & (col < cls_len), 0.0, NEG_BIG).astype(F32)
            pairs = []
            lse_c = jnp.zeros((tq, LANES), F32)
            for jp in range(A_HEADS // 2):
                sl = slice(jp * LANES, (jp + 1) * LANES)
                qp, kpair, vpair = q[:, sl], kw[:, sl], vw[:, sl]
                outs = []
                for e in range(2):
                    qm = jnp.where(low if e == 0 else ~low, qp, jnp.zeros_like(qp))
                    s = _nt_dot(qm, kpair) + bias_ref[2 * jp + e] + colpen
                    m = jnp.max(s, axis=1, keepdims=True)
                    p = jnp.exp2(s - m)
                    l = jnp.sum(p, axis=1, keepdims=True)
                    outs.append(jnp.dot(p.astype(BF16), vpair, preferred_element_type=F32) / l)
                    lse_c = lse_c + jnp.where(lane == 2 * jp + e, m + jnp.log2(l), 0.0)
                pairs.append(jnp.where(low, outs[0], outs[1]))
            o_full = jnp.concatenate(pairs, axis=1).astype(BF16)
            if n_cls == 1:
                o_ref[rows, :] = o_full
                lse_ref[rows, :] = lse_c
            else:
                hi = lse_c.astype(BF16)
                rest = lse_c - hi.astype(F32)
                mid = rest.astype(BF16)
                lo = (rest - mid.astype(F32)).astype(BF16)
                cm_ref[r, rows, :] = jnp.concatenate([o_full, hi, mid, lo], axis=1)

    if n_cls == 1:
        one_class(0)
        return

    def body(r, carry):
        one_class(r)
        return carry

    lax.fori_loop(0, n_cls, body, 0)
    per = PERM_GROUP // n_cls
    for a in range(n_pos * tq * n_cls // PERM_GROUP):
        stack = jnp.concatenate([cm_ref[r, a * per:(a + 1) * per, :] for r in range(n_cls)], axis=0)
        nat = jnp.dot(perm_ref[...], stack, preferred_element_type=F32)
        rows = slice(a * PERM_GROUP, (a + 1) * PERM_GROUP)
        o_ref[rows, :] = nat[:, :A_WIDTH].astype(BF16)
        lse_ref[rows, :] = (nat[:, A_WIDTH:A_WIDTH + LANES] + nat[:, A_WIDTH + LANES:A_WIDTH + 2 * LANES]
                            + nat[:, A_WIDTH + 2 * LANES:])


def _attn_a_tables(seqs, dil, n_pos):
    rows = n_pos * TQ_A
    per_halo = rows // A_HALF
    qblk, pblk, nblk, tstart, llen = [], [], [], [], []
    for off, s_len in seqs:
        cls = s_len // dil
        steps = cls // rows
        base = (off // dil) // rows
        for i in range(steps):
            qblk.append(base + i)
            pblk.append((base + i) * per_halo - (1 if i > 0 else 0))
            nblk.append((base + i + 1) * per_halo - (0 if i < steps - 1 else 1))
            tstart.append(i * rows)
            llen.append(cls)
    return [jnp.asarray(np.asarray(a, np.int32)) for a in (qblk, pblk, nblk, tstart, llen)]


def _attn_a_bias(dil):
    tq, wk = TQ_A, TQ_A + 2 * A_HALF
    delta = np.abs(np.arange(wk)[None, :] - A_HALF - np.arange(tq)[:, None]).astype(np.float64)
    slopes = 2.0 ** (-8.0 * (np.arange(A_HEADS) + 1.0) / A_HEADS)
    bias = -slopes[:, None, None] * (delta * dil)[None] * LOG2E
    bias = np.where((delta <= A_HALF)[None], bias, NEG_BIG)
    return jnp.asarray(bias.astype(np.float32))


def _attn_a(qc, kc, vc, seqs, dil):
    t_all = qc.shape[0] * dil
    n_pos = A_POS_TILES[dil]
    rows, width = n_pos * TQ_A, dil * A_WIDTH
    tabs = _attn_a_tables(seqs, dil, n_pos)
    n_steps = int(tabs[0].shape[0])
    bias = _attn_a_bias(dil)
    perm = _class_perm(PERM_GROUP, dil).T
    cur = pl.BlockSpec((rows, width), lambda n, qb, pb, nb, ts, ll: (qb[n], 0))
    prev = pl.BlockSpec((A_HALF, width), lambda n, qb, pb, nb, ts, ll: (pb[n], 0))
    nxt = pl.BlockSpec((A_HALF, width), lambda n, qb, pb, nb, ts, ll: (nb[n], 0))
    tok = lambda cols: pl.BlockSpec((rows * dil, cols), lambda n, *_: (n, 0))
    return pl.pallas_call(
        functools.partial(_attn_a_kernel, n_pos=n_pos, n_cls=dil),
        grid_spec=pltpu.PrefetchScalarGridSpec(
            num_scalar_prefetch=5, grid=(n_steps,),
            in_specs=[cur, prev, cur, nxt, prev, cur, nxt,
                      pl.BlockSpec(bias.shape, lambda n, *_: (0, 0, 0)),
                      pl.BlockSpec(perm.shape, lambda n, *_: (0, 0))],
            out_specs=[tok(A_WIDTH), tok(LANES)],
            scratch_shapes=[pltpu.VMEM((dil, rows, A_WIDTH + 3 * LANES), BF16)]),
        out_shape=[jax.ShapeDtypeStruct((t_all, A_WIDTH), BF16),
                   jax.ShapeDtypeStruct((t_all, LANES), F32)],
        compiler_params=pltpu.CompilerParams(dimension_semantics=("arbitrary",),
                                             vmem_limit_bytes=VMEM_LIMIT),
        name=f"attn_a_d{dil}",
    )(*tabs, qc, kc, kc, kc, vc, vc, vc, bias, perm)


def _attn_b_kernel(q_ref, k_ref, vt_ref, o_ref, sta_ref, stb_ref, ma_ref, mb_ref):
    s = pl.program_id(0)
    tq, s_len = q_ref.shape[0], k_ref.shape[0]

    @pl.when(s == 0)
    def _():
        stb_ref[...] = jnp.zeros_like(stb_ref)
        mb_ref[...] = jnp.zeros_like(mb_ref)

    def step(st_new_ref, m_new_ref, st_old_ref, m_old_ref):
        m_old = m_old_ref[...]
        q = q_ref[...]
        m_new = jnp.full((1, tq), -jnp.inf, F32)
        l = jnp.zeros((1, tq), F32)
        acc = jnp.zeros((B_V, tq), F32)
        for c in range(s_len // KEY_CHUNK_B):
            ks = slice(c * KEY_CHUNK_B, (c + 1) * KEY_CHUNK_B)
            st_c = _nt_dot(k_ref[ks, :], q)
            st_new_ref[ks, :] = st_c
            m_new = jnp.maximum(m_new, jnp.max(st_c, axis=0, keepdims=True))
            p = jnp.exp2(st_old_ref[ks, :] - m_old)
            l = l + jnp.sum(p, axis=0, keepdims=True)
            acc = acc + jnp.dot(vt_ref[:, ks], p.astype(BF16), preferred_element_type=F32)
        m_new_ref[...] = m_new
        o_ref[...] = (acc / l).astype(BF16)

    @pl.when(s % 2 == 0)
    def _():
        step(sta_ref, ma_ref, stb_ref, mb_ref)

    @pl.when(s % 2 == 1)
    def _():
        step(stb_ref, mb_ref, sta_ref, ma_ref)


def _attn_b(qb, kb, vbt, off, n_batch, s_len):
    tq = TQ_B
    nq = s_len // tq
    qbase, kbase = off // tq, off // s_len
    n_items = n_batch * B_HEADS * nq

    def split(item):
        return item // (B_HEADS * nq), (item // nq) % B_HEADS, item % nq

    def q_map(s):
        b, h, qi = split(jnp.minimum(s, n_items - 1))
        return (qbase + b * nq + qi, h)

    def k_map(s):
        b, h, _ = split(jnp.minimum(s, n_items - 1))
        return (kbase + b, h)

    def v_map(s):
        b, h, _ = split(jnp.maximum(s - 1, 0))
        return (h, kbase + b)

    def o_map(s):
        b, h, qi = split(jnp.maximum(s - 1, 0))
        return (h, b * nq + qi)

    return pl.pallas_call(
        _attn_b_kernel,
        grid=(n_items + 1,),
        in_specs=[pl.BlockSpec((tq, HEAD_PAD), q_map),
                  pl.BlockSpec((s_len, HEAD_PAD), k_map),
                  pl.BlockSpec((B_V, s_len), v_map)],
        out_specs=pl.BlockSpec((B_V, tq), o_map),
        out_shape=jax.ShapeDtypeStruct((B_HEADS * B_V, n_batch * s_len), BF16),
        scratch_shapes=[pltpu.VMEM((s_len, tq), F32), pltpu.VMEM((s_len, tq), F32),
                        pltpu.VMEM((1, tq), F32), pltpu.VMEM((1, tq), F32)],
        compiler_params=pltpu.CompilerParams(dimension_semantics=("arbitrary",),
                                             vmem_limit_bytes=VMEM_LIMIT),
        name=f"attn_b_s{s_len}",
    )(qb, kb, vbt)


def _mix_kernel(xp_ref, xs_ref, o0_ref, o1_ref, o2_ref, l0_ref, l1_ref, l2_ref, spread_ref,
                obp_ref, obs_ref, wout_ref,
                g_ref, b_ref, wrh_ref, wrl_ref, rb_ref,
                h_ref, hp_ref, tope_ref, pos_ref, gate_ref, cnt_out_ref, cnt_ref, stage_ref,
                *, n_prompt_tiles):
    i = pl.program_id(0)
    tm = xp_ref.shape[0]

    @pl.when(i == 0)
    def _():
        cnt_ref[...] = jnp.zeros_like(cnt_ref)

    l0, l1, l2 = l0_ref[...], l1_ref[...], l2_ref[...]
    lmax = jnp.maximum(jnp.maximum(l0, l1), l2)
    e0, e1, e2 = jnp.exp2(l0 - lmax), jnp.exp2(l1 - lmax), jnp.exp2(l2 - lmax)
    inv = 1.0 / (e0 + e1 + e2)
    spread = spread_ref[...]

    def per_lane(w):
        hi = w.astype(BF16)
        lo = (w - hi.astype(F32)).astype(BF16)
        return (jnp.dot(hi, spread, preferred_element_type=F32)
                + jnp.dot(lo, spread, preferred_element_type=F32))

    oa = (per_lane(e0 * inv) * o0_ref[...].astype(F32) + per_lane(e1 * inv) * o1_ref[...].astype(F32)
          + per_lane(e2 * inv) * o2_ref[...].astype(F32))
    is_prompt = i < n_prompt_tiles
    obt = jnp.where(is_prompt, obp_ref[...], obs_ref[...])
    mix = (jnp.dot(oa.astype(BF16), wout_ref[0:A_WIDTH, :], preferred_element_type=F32)
           + lax.dot_general(obt, wout_ref[A_WIDTH:, :], (((0,), (0,)), ((), ())),
                             preferred_element_type=F32))
    x = jnp.where(is_prompt, xp_ref[...], xs_ref[...])
    h = _layer_norm(ALPHA * x + mix, g_ref[...], b_ref[...])
    h_ref[...] = h
    _slab_store_bf16(hp_ref, stage_ref, h)

    h_hi = h.astype(BF16)
    h_lo = (h - h_hi.astype(F32)).astype(BF16)
    wrh = wrh_ref[...]
    logits = _nt_dot(wrh, h_hi) + _nt_dot(wrh, h_lo) + _nt_dot(wrl_ref[...], h_hi)
    scores = jax.nn.sigmoid(logits)
    sel = scores + rb_ref[...]

    sub = lax.broadcasted_iota(I32, (GROUP_SIZE, tm), 0).astype(F32)
    gscore = []
    for g in range(N_GROUPS):
        sg = sel[g * GROUP_SIZE:(g + 1) * GROUP_SIZE, :]
        m1 = jnp.max(sg, axis=0, keepdims=True)
        first = jnp.min(jnp.where(sg == m1, sub, float(GROUP_SIZE)), axis=0, keepdims=True)
        m2 = jnp.max(jnp.where(sub == first, -jnp.inf, sg), axis=0, keepdims=True)
        gscore.append(m1 + m2)
    cands = []
    for g in range(N_GROUPS):
        beaten = jnp.zeros((1, tm), F32)
        for g2 in range(N_GROUPS):
            if g2 == g:
                continue
            wins = (gscore[g2] > gscore[g]) | ((gscore[g2] == gscore[g]) & (g2 < g))
            beaten = beaten + wins.astype(F32)
        keep = beaten < float(TOPK_GROUPS)
        sg = sel[g * GROUP_SIZE:(g + 1) * GROUP_SIZE, :]
        cands.append(jnp.where(keep, sg, NEG_BIG))
    cand = jnp.concatenate(cands, axis=0)

    eidx = lax.broadcasted_iota(I32, (N_EXPERTS, tm), 0).astype(F32)
    picked_idx, picked_gate = [], []
    onehot = jnp.zeros((N_EXPERTS, tm), F32)
    for _ in range(TOP_K):
        mx = jnp.max(cand, axis=0, keepdims=True)
        fi = jnp.min(jnp.where(cand == mx, eidx, float(N_EXPERTS)), axis=0, keepdims=True)
        pick = eidx == fi
        picked_idx.append(fi)
        picked_gate.append(jnp.sum(jnp.where(pick, scores, 0.0), axis=0, keepdims=True))
        onehot = onehot + pick.astype(F32)
        cand = jnp.where(pick, -jnp.inf, cand)
    gsum = picked_gate[0]
    for k in range(1, TOP_K):
        gsum = gsum + picked_gate[k]

    tri = (lax.broadcasted_iota(I32, (tm, tm), 0) < lax.broadcasted_iota(I32, (tm, tm), 1))
    before = jnp.dot(onehot.astype(BF16), tri.astype(BF16), preferred_element_type=F32)
    rank = before + cnt_ref[:, 0:1]
    for k in range(TOP_K):
        pick = eidx == picked_idx[k]
        tope_ref[k:k + 1, :] = picked_idx[k].astype(I32)
        pos_ref[k:k + 1, :] = jnp.sum(jnp.where(pick, rank, 0.0), axis=0, keepdims=True).astype(I32)
        gate_ref[k:k + 1, :] = picked_gate[k] / gsum * ROUTED_SCALE
    cnt_ref[...] = cnt_ref[...] + jnp.sum(onehot, axis=1, keepdims=True)
    cnt_out_ref[...] = cnt_ref[...]


def _mix(xp, xs, outs, lses, obp, obs, wout, g, b, wrh, wrl, rb):
    tm = TM_MIX
    n1 = xp.shape[0] // tm
    n2 = xs.shape[0] // tm
    t_all = xp.shape[0] + xs.shape[0]
    full = lambda a: pl.BlockSpec(a.shape, lambda i: (0,) * a.ndim)
    row = lambda cols: pl.BlockSpec((tm, cols), lambda i: (i, 0))
    prow = lambda cols: pl.BlockSpec((tm, cols), lambda i: (jnp.minimum(i, n1 - 1), 0))
    srow = lambda cols: pl.BlockSpec((tm, cols), lambda i: (jnp.maximum(i - n1, 0), 0))
    col = pl.BlockSpec((TOP_K, tm), lambda i: (0, i))
    head_of_lane = np.arange(A_WIDTH) // A_HEAD_DIM
    spread = jnp.asarray(np.arange(LANES)[:, None] == head_of_lane[None, :], BF16)
    return pl.pallas_call(
        functools.partial(_mix_kernel, n_prompt_tiles=n1),
        grid=(n1 + n2,),
        in_specs=[
            prow(D_MODEL), srow(D_MODEL),
            row(A_WIDTH), row(A_WIDTH), row(A_WIDTH), row(LANES), row(LANES), row(LANES), full(spread),
            pl.BlockSpec((B_HEADS * B_V, tm), lambda i: (0, jnp.minimum(i, n1 - 1))),
            pl.BlockSpec((B_HEADS * B_V, tm), lambda i: (0, jnp.maximum(i - n1, 0))),
            full(wout), full(g), full(b), full(wrh), full(wrl), full(rb),
        ],
        out_specs=[row(D_MODEL), pl.BlockSpec((SLAB * tm, LANES), lambda i: (i, 0)), col, col, col,
                   pl.BlockSpec((N_EXPERTS, LANES), lambda i: (0, 0))],
        out_shape=[jax.ShapeDtypeStruct((t_all, D_MODEL), F32),
                   jax.ShapeDtypeStruct((SLAB * t_all, LANES), BF16),
                   jax.ShapeDtypeStruct((TOP_K, t_all), I32),
                   jax.ShapeDtypeStruct((TOP_K, t_all), I32),
                   jax.ShapeDtypeStruct((TOP_K, t_all), F32),
                   jax.ShapeDtypeStruct((N_EXPERTS, LANES), F32)],
        scratch_shapes=[pltpu.VMEM((N_EXPERTS, LANES), F32), pltpu.VMEM((SLAB * tm, LANES), F32)],
        compiler_params=pltpu.CompilerParams(dimension_semantics=("arbitrary",),
                                             vmem_limit_bytes=VMEM_LIMIT),
        name="mix_router",
    )(xp, xs, *outs, *lses, spread, obp, obs, wout, g, b, wrh, wrl, rb)


def _dispatch_kernel(zstart_ref, nused_ref, h_ref, tope_ref, pos_ref, pstart_ref, buf_ref, dest_ref,
                     dsm_ref, zero_ref, sem_ref, zsem_ref):
    i = pl.program_id(0)
    tm = tope_ref.shape[1]
    tile_rows = SLAB * ROW_BLOCK

    @pl.when(i == 0)
    def _():
        zero_ref[...] = jnp.zeros_like(zero_ref)

        def zfill(e, carry):
            start = pl.multiple_of(zstart_ref[e], SLAB)
            pltpu.make_async_copy(zero_ref, buf_ref.at[pl.ds(start, tile_rows), :], zsem_ref).start()
            return carry

        lax.fori_loop(0, N_EXPERTS, zfill, 0)
        span = buf_ref.at[pl.ds(0, N_EXPERTS * tile_rows), :]
        pltpu.make_async_copy(span, span, zsem_ref).wait()

        def ztail(j, carry):
            start = pl.multiple_of(j * tile_rows, tile_rows)
            pltpu.make_async_copy(zero_ref, buf_ref.at[pl.ds(start, tile_rows), :], zsem_ref).start()
            return carry

        def zwait(j, carry):
            pltpu.make_async_copy(zero_ref, buf_ref.at[pl.ds(0, tile_rows), :], zsem_ref).wait()
            return carry

        n_tiles = buf_ref.shape[0] // tile_rows
        lax.fori_loop(nused_ref[0], n_tiles, ztail, 0)
        lax.fori_loop(nused_ref[0], n_tiles, zwait, 0)

    eidx = lax.broadcasted_iota(I32, (N_EXPERTS, tm), 0)
    pstart = pstart_ref[...]
    for k in range(TOP_K):
        hit = eidx == tope_ref[k:k + 1, :]
        base = jnp.sum(jnp.where(hit, pstart, 0.0), axis=0, keepdims=True)
        dest_ref[k:k + 1, :] = (pos_ref[k:k + 1, :] + base.astype(I32)) * SLAB
    pltpu.sync_copy(dest_ref, dsm_ref)

    def scatter(t, carry):
        src = h_ref.at[pl.ds(pl.multiple_of(t * SLAB, SLAB), SLAB), :]
        for k in range(TOP_K):
            dst = buf_ref.at[pl.ds(pl.multiple_of(dsm_ref[k, t], SLAB), SLAB), :]
            pltpu.make_async_copy(src, dst, sem_ref).start(priority=k % 2)
        return carry

    lax.fori_loop(0, tm, scatter, 0, unroll=8)
    span = buf_ref.at[pl.ds(0, TOP_K * tm * SLAB), :]
    pltpu.make_async_copy(span, span, sem_ref).wait()


def _dispatch(h, tope, pos, pstart, zstart, nused, n_rows):
    tm = TM_DISP
    t_all = tope.shape[1]
    col = pl.BlockSpec((TOP_K, tm), lambda i, z, nu: (0, i))
    return pl.pallas_call(
        _dispatch_kernel,
        grid_spec=pltpu.PrefetchScalarGridSpec(
            num_scalar_prefetch=2, grid=(t_all // tm,),
            in_specs=[pl.BlockSpec((SLAB * tm, LANES), lambda i, z, nu: (i, 0)), col, col,
                      pl.BlockSpec((N_EXPERTS, 1), lambda i, z, nu: (0, 0))],
            out_specs=[pl.BlockSpec(memory_space=pl.ANY), col],
            scratch_shapes=[pltpu.SMEM((TOP_K, tm), I32),
                            pltpu.VMEM((SLAB * ROW_BLOCK, LANES), BF16),
                            pltpu.SemaphoreType.DMA, pltpu.SemaphoreType.DMA]),
        out_shape=[jax.ShapeDtypeStruct((SLAB * n_rows, LANES), BF16),
                   jax.ShapeDtypeStruct((TOP_K, t_all), I32)],
        compiler_params=pltpu.CompilerParams(dimension_semantics=("arbitrary",),
                                             vmem_limit_bytes=VMEM_LIMIT),
        name="dispatch",
    )(zstart, nused, h, tope, pos, pstart)


def _expert_kernel(tfirst_ref, ntile_ref, nused_ref, x_hbm, wg_ref, wu_ref, wd_ref, o_hbm,
                   xbuf, obuf, stage_ref, wgb_ref, wub_ref, wdb_ref, xsem, osem):
    e = pl.program_id(0)
    tile_rows = SLAB * ROW_BLOCK
    nused = nused_ref[0]

    def x_copy(g, slot):
        start = pl.multiple_of(g * tile_rows, tile_rows)
        return pltpu.make_async_copy(x_hbm.at[pl.ds(start, tile_rows), :], xbuf.at[slot], xsem.at[slot])

    def o_copy(g, slot):
        start = pl.multiple_of(g * tile_rows, tile_rows)
        return pltpu.make_async_copy(obuf.at[slot], o_hbm.at[pl.ds(start, tile_rows), :], osem.at[slot])

    @pl.when(e == 0)
    def _():
        for j in range(N_XBUF - 1):
            @pl.when(j < nused)
            def _():
                x_copy(j, j).start()

    n_e = ntile_ref[e]

    @pl.when(n_e > 0)
    def _():
        wgb_ref[...] = wg_ref[...].astype(BF16)
        wub_ref[...] = wu_ref[...].astype(BF16)
        wdb_ref[...] = wd_ref[...].astype(BF16)

    def tile(j, carry):
        g = tfirst_ref[e] + j
        ahead = g + (N_XBUF - 1)

        @pl.when(ahead < nused)
        def _():
            x_copy(ahead, ahead % N_XBUF).start()

        slot = g % N_XBUF
        x_copy(g, slot).wait()
        x = _slab_load_bf16(xbuf.at[slot], stage_ref, ROW_BLOCK).astype(BF16)
        gt = jnp.dot(x, wgb_ref[...], preferred_element_type=F32)
        up = jnp.dot(x, wub_ref[...], preferred_element_type=F32)
        hmid = (gt * jax.nn.sigmoid(gt) * up).astype(BF16)
        out = jnp.dot(hmid, wdb_ref[...], preferred_element_type=F32)
        oslot = g % N_OBUF

        @pl.when(g >= N_OBUF)
        def _():
            o_copy(g - N_OBUF, oslot).wait()

        _slab_store(obuf.at[oslot], out)
        o_copy(g, oslot).start()
        return carry

    lax.fori_loop(0, n_e, tile, 0)

    @pl.when(e == N_EXPERTS - 1)
    def _():
        for j in range(N_OBUF):
            @pl.when(nused > j)
            def _():
                o_copy(nused - 1 - j, (nused - 1 - j) % N_OBUF).wait()

        stage_ref[...] = jnp.zeros_like(stage_ref)
        n_tiles = o_hbm.shape[0] // tile_rows

        def tail_copy(g):
            start = pl.multiple_of(g * tile_rows, tile_rows)
            return pltpu.make_async_copy(stage_ref, o_hbm.at[pl.ds(start, tile_rows), :], osem.at[0])

        def ztail(g, carry):
            tail_copy(g).start()
            return carry

        def zwait(g, carry):
            tail_copy(g).wait()
            return carry

        lax.fori_loop(nused, n_tiles, ztail, 0)
        lax.fori_loop(nused, n_tiles, zwait, 0)


def _experts(buf, tfirst, ntile, nused, w_gate, w_up, w_down):
    tile_rows = SLAB * ROW_BLOCK
    wmap = lambda e, tf, nt, nu: (e, 0, 0)
    return pl.pallas_call(
        _expert_kernel,
        grid_spec=pltpu.PrefetchScalarGridSpec(
            num_scalar_prefetch=3, grid=(N_EXPERTS,),
            in_specs=[pl.BlockSpec(memory_space=pl.ANY),
                      pl.BlockSpec((None, D_MODEL, EXPERT_HIDDEN), wmap),
                      pl.BlockSpec((None, D_MODEL, EXPERT_HIDDEN), wmap),
                      pl.BlockSpec((None, EXPERT_HIDDEN, D_MODEL), wmap)],
            out_specs=pl.BlockSpec(memory_space=pl.ANY),
            scratch_shapes=[pltpu.VMEM((N_XBUF, tile_rows, LANES), BF16),
                            pltpu.VMEM((N_OBUF, tile_rows, LANES), F32),
                            pltpu.VMEM((tile_rows, LANES), F32),
                            pltpu.VMEM((D_MODEL, EXPERT_HIDDEN), BF16),
                            pltpu.VMEM((D_MODEL, EXPERT_HIDDEN), BF16),
                            pltpu.VMEM((EXPERT_HIDDEN, D_MODEL), BF16),
                            pltpu.SemaphoreType.DMA((N_XBUF,)),
                            pltpu.SemaphoreType.DMA((N_OBUF,))]),
        out_shape=jax.ShapeDtypeStruct(buf.shape, F32),
        compiler_params=pltpu.CompilerParams(dimension_semantics=("arbitrary",),
                                             vmem_limit_bytes=VMEM_LIMIT),
        name="experts",
    )(tfirst, ntile, nused, buf, w_gate, w_up, w_down)


def _combine_kernel(h_ref, gate_ref, dest_ref, dest_next_ref, eo_ref, wsg_ref, wsu_ref, wsd_ref, g_ref, b_ref,
                    yp_ref, ys_ref, dsm_ref, rows_ref, sem_ref, *, n_prompt_tiles, n_tiles):
    i = pl.program_id(0)
    tm = gate_ref.shape[0]
    slot = i % 2

    def start_gather(table_ref, to_slot):
        pltpu.sync_copy(table_ref, dsm_ref)

        def gather(t, carry):
            for k in range(TOP_K):
                src = eo_ref.at[pl.ds(pl.multiple_of(dsm_ref[k, t], SLAB), SLAB), :]
                dst = rows_ref.at[to_slot, pl.ds(pl.multiple_of((k * tm + t) * SLAB, SLAB), SLAB), :]
                pltpu.make_async_copy(src, dst, sem_ref.at[to_slot]).start(priority=k % 2)
            return carry

        lax.fori_loop(0, tm, gather, 0, unroll=8)

    @pl.when(i == 0)
    def _():
        start_gather(dest_ref, 0)

    for nxt in range(2):
        @pl.when((i + 1 < n_tiles) & (slot == 1 - nxt))
        def _():
            start_gather(dest_next_ref, nxt)

    h = h_ref[...]
    hb = h.astype(BF16)
    sg = jnp.dot(hb, wsg_ref[...], preferred_element_type=F32)
    su = jnp.dot(hb, wsu_ref[...], preferred_element_type=F32)
    shared = jnp.dot((sg * jax.nn.sigmoid(sg) * su).astype(BF16), wsd_ref[...],
                     preferred_element_type=F32)

    rows = rows_ref.at[slot]
    pltpu.make_async_copy(eo_ref.at[pl.ds(0, TOP_K * tm * SLAB), :], rows, sem_ref.at[slot]).wait()
    gate = gate_ref[...]
    routed = gate[:, 0:1] * _slab_load(rows, tm)
    for k in range(1, TOP_K):
        routed = routed + gate[:, k:k + 1] * _slab_load(rows, tm, k * tm)
    y = _layer_norm(ALPHA * h + (routed + shared), g_ref[...], b_ref[...])

    @pl.when(i < n_prompt_tiles)
    def _():
        yp_ref[...] = y

    @pl.when(i >= n_prompt_tiles)
    def _():
        ys_ref[...] = y


def _combine(h, gate_t, dest, eo, wsg, wsu, wsd, g, b, t_prompt):
    tm = TM_COMB
    t_all = gate_t.shape[0]
    n1 = t_prompt // tm
    n2 = (t_all - t_prompt) // tm
    full = lambda a: pl.BlockSpec(a.shape, lambda i: (0,) * a.ndim)
    return pl.pallas_call(
        functools.partial(_combine_kernel, n_prompt_tiles=n1, n_tiles=n1 + n2),
        grid=(n1 + n2,),
        in_specs=[pl.BlockSpec((tm, D_MODEL), lambda i: (i, 0)),
                  pl.BlockSpec((tm, TOP_K), lambda i: (i, 0)),
                  pl.BlockSpec((TOP_K, tm), lambda i: (0, i)),
                  pl.BlockSpec((TOP_K, tm), lambda i: (0, jnp.minimum(i + 1, n1 + n2 - 1))),
                  pl.BlockSpec(memory_space=pl.ANY),
                  full(wsg), full(wsu), full(wsd), full(g), full(b)],
        out_specs=[pl.BlockSpec((tm, D_MODEL), lambda i: (jnp.minimum(i, n1 - 1), 0)),
                   pl.BlockSpec((tm, D_MODEL), lambda i: (jnp.maximum(i - n1, 0), 0))],
        out_shape=[jax.ShapeDtypeStruct((t_prompt, D_MODEL), F32),
                   jax.ShapeDtypeStruct((t_all - t_prompt, D_MODEL), F32)],
        scratch_shapes=[pltpu.SMEM((TOP_K, tm), I32),
                        pltpu.VMEM((2, TOP_K * tm * SLAB, LANES), F32),
                        pltpu.SemaphoreType.DMA((2,))],
        compiler_params=pltpu.CompilerParams(dimension_semantics=("arbitrary",),
                                             vmem_limit_bytes=VMEM_LIMIT),
        name="combine",
    )(h, gate_t, dest, dest, eo, wsg, wsu, wsd, g, b)


def _rope_tables(s_max):
    inv_freq = ROPE_BASE ** (-jnp.arange(0, B_ROPE, 2, dtype=F32) / B_ROPE)
    ang = jnp.arange(s_max, dtype=F32)[:, None] * inv_freq[None, :]
    cos, sin = jnp.cos(ang), jnp.sin(ang)
    ones = jnp.ones((s_max, B_NOPE), F32)
    zeros_n = jnp.zeros((s_max, B_NOPE), F32)
    zeros_p = jnp.zeros((s_max, HEAD_PAD - B_NOPE - B_ROPE), F32)
    return (jnp.concatenate([ones, cos, cos, zeros_p], axis=1),
            jnp.concatenate([zeros_n, sin, sin, zeros_p], axis=1))


def _rot_cols(w):
    half = B_ROPE // 2
    return jnp.concatenate([-w[..., half:], w[..., :half]], axis=-1)


def _layout_weights(w_in, w_uq, w_uk):
    c_kr = 3 * A_WIDTH + Q_LORA + KV_LORA
    w_kr = w_in[:, c_kr:c_kr + B_ROPE]
    pad_l = jnp.zeros((D_MODEL, B_NOPE), F32)
    pad_r = jnp.zeros((D_MODEL, HEAD_PAD - B_NOPE - B_ROPE), F32)
    w1 = jnp.concatenate([w_in[:, :c_kr], pad_l, w_kr, pad_r, pad_l, _rot_cols(w_kr), pad_r], axis=1)
    wq3 = w_uq.reshape(Q_LORA, B_HEADS, B_NOPE + B_ROPE)
    nope, rope = wq3[..., :B_NOPE], wq3[..., B_NOPE:]
    zpad = jnp.zeros((Q_LORA, B_HEADS, HEAD_PAD - B_NOPE - B_ROPE), F32)
    wq = jnp.concatenate([nope, rope, zpad], axis=-1).reshape(Q_LORA, B_HEADS * HEAD_PAD)
    wqr = jnp.concatenate([jnp.zeros_like(nope), _rot_cols(rope), zpad], axis=-1)
    wqr = wqr.reshape(Q_LORA, B_HEADS * HEAD_PAD)
    wk3 = w_uk.reshape(KV_LORA, B_HEADS, B_NOPE)
    wuk = jnp.concatenate([wk3, jnp.zeros((KV_LORA, B_HEADS, HEAD_PAD - B_NOPE), F32)], axis=-1)
    wuk = wuk.reshape(KV_LORA, B_HEADS * HEAD_PAD)
    return w1.astype(BF16), wq.astype(BF16), wqr.astype(BF16), wuk.astype(BF16)


def _forward(x_prompt, x_sample, w_in, w_out, ln1_g, ln1_b, q_norm_g, w_uq, kv_norm_g, w_uk, w_uv,
             w_router, router_bias, w_gate, w_up, w_down, ws_gate, ws_up, ws_down, ln2_g, ln2_b):
    b1, s1, _ = x_prompt.shape
    b2, s2, _ = x_sample.shape
    t1, t2 = b1 * s1, b2 * s2
    t_all = t1 + t2
    xp = x_prompt.reshape(t1, D_MODEL)
    xs = x_sample.reshape(t2, D_MODEL)
    seqs = [(b * s1, s1) for b in range(b1)] + [(t1 + b * s2, s2) for b in range(b2)]
    assert t1 % s2 == 0 and s1 % (TQ_A * 16) == 0 and s2 % (TQ_A * 16) == 0

    w1, wq, wqr, wuk = _layout_weights(w_in, w_uq, w_uk)
    cos_t, sin_t = _rope_tables(max(s1, s2))
    (qa, ka, va, q4, k4, v4, q16, k16, v16, qb, kb, vbt) = _proj(
        xp, xs, w1, q_norm_g.reshape(1, -1), kv_norm_g.reshape(1, -1), wq, wqr, wuk,
        w_uv.T.astype(BF16), cos_t, sin_t, s1, s2)

    outs, lses = [], []
    for (_, dil), qkv in zip(A_PATTERNS, ((qa, ka, va), (q4, k4, v4), (q16, k16, v16))):
        o, lse = _attn_a(*qkv, seqs, dil)
        outs.append(o)
        lses.append(lse)
    obp = _attn_b(qb, kb, vbt, 0, b1, s1)
    obs = _attn_b(qb, kb, vbt, t1, b2, s2)

    wr_t = w_router.T
    wr_hi = wr_t.astype(BF16)
    wr_lo = (wr_t - wr_hi.astype(F32)).astype(BF16)
    h, h_slab, tope, pos, gate, counts = _mix(xp, xs, outs, lses, obp, obs, w_out.astype(BF16),
                                                ln1_g.reshape(1, -1), ln1_b.reshape(1, -1),
                                                wr_hi, wr_lo, router_bias.reshape(-1, 1))

    cnt = counts[:, 0].astype(I32)
    padded = ((cnt + ROW_BLOCK - 1) // ROW_BLOCK) * ROW_BLOCK
    pend = jnp.cumsum(padded)
    pstart = pend - padded
    n_tiles = (t_all * TOP_K) // ROW_BLOCK + N_EXPERTS + 1
    nused = (pend[-1:] // ROW_BLOCK).astype(I32)

    buf, dest = _dispatch(h_slab, tope, pos, pstart.astype(F32).reshape(-1, 1),
                          ((pstart + cnt) * SLAB).astype(I32), nused, n_tiles * ROW_BLOCK)
    eo = _experts(buf, (pstart // ROW_BLOCK).astype(I32), (padded // ROW_BLOCK).astype(I32), nused,
                  w_gate, w_up, w_down)
    yp, ys = _combine(h, gate.T, dest, eo, ws_gate.astype(BF16), ws_up.astype(BF16),
                      ws_down.astype(BF16), ln2_g.reshape(1, -1), ln2_b.reshape(1, -1), t1)
    return yp.reshape(b1, s1, D_MODEL), ys.reshape(b2, s2, D_MODEL)


def kernel(x_prompt, x_sample, w_in, w_out, ln1_g, ln1_b, q_norm_g, w_uq, kv_norm_g, w_uk, w_uv,
           w_router, router_bias, w_gate, w_up, w_down, ws_gate, ws_up, ws_down, ln2_g, ln2_b):
    params = (w_in, w_out, ln1_g, ln1_b, q_norm_g, w_uq, kv_norm_g, w_uk, w_uv, w_router, router_bias,
              w_gate, w_up, w_down, ws_gate, ws_up, ws_down, ln2_g, ln2_b)
    assert all(p.shape[0] == 1 for p in params), "one encoder layer"
    return _forward(x_prompt, x_sample, *[p.reshape(p.shape[1:]) for p in params])
```

```python
import functools
import math

import numpy as np
import jax
import jax.numpy as jnp
from jax import lax
from jax.experimental import pallas as pl
from jax.experimental.pallas import tpu as pltpu

F32 = jnp.float32
BF16 = jnp.bfloat16
I32 = jnp.int32

D_MODEL = 1024
A_HEADS = 8
A_HEAD_DIM = 64
A_WIDTH = A_HEADS * A_HEAD_DIM
A_PATTERNS = ((128, 1), (512, 4), (2048, 16))
A_HALF = 64
B_HEADS = 8
B_NOPE = 64
B_ROPE = 32
B_V = 64
Q_LORA = 256
KV_LORA = 128
ROPE_BASE = 10000.0
N_EXPERTS = 256
TOP_K = 8
N_GROUPS = 8
GROUP_SIZE = N_EXPERTS // N_GROUPS
TOPK_GROUPS = 4
EXPERT_HIDDEN = 256
ROUTED_SCALE = 2.5
LN_EPS = 1e-5
RMS_EPS = 1e-6
NEG_BIG = -1e30
ALPHA = 2.0 ** 0.25
LOG2E = 1.4426950408889634

LANES = 128
HEAD_PAD = 128
SUBLANES = 8
W1_COLS = 3 * A_WIDTH + Q_LORA + KV_LORA + 2 * LANES

TM_PROJ = 256
TQ_A = 128
A_POS_TILES = {1: 4, 4: 2, 16: 1}
PERM_GROUP = 256
TQ_B = 512
KEY_CHUNK_B = 512
TM_MIX = 256
TM_DISP = 128
ROW_BLOCK = 256
TM_COMB = 128
N_XBUF = 3
N_OBUF = 2
VMEM_LIMIT = 48 * 1024 * 1024


def _rows_shape(n_rows):
    return (n_rows // SUBLANES, D_MODEL // LANES, SUBLANES, LANES)


def _row_of(ref, group, sub):
    return ref.at[group, :, pl.ds(sub, 1), :]


def _rows_load(ref, n_rows, row0=0):
    g0 = row0 // SUBLANES
    return jnp.concatenate(
        [jnp.concatenate([ref[g0 + g, c] for g in range(n_rows // SUBLANES)], axis=0)
         for c in range(D_MODEL // LANES)], axis=1)


def _rows_store(ref, val):
    for g in range(val.shape[0] // SUBLANES):
        for c in range(D_MODEL // LANES):
            ref[g, c] = val[g * SUBLANES:(g + 1) * SUBLANES, c * LANES:(c + 1) * LANES]


def _nt_dot(a, b):
    return lax.dot_general(a, b, (((1,), (1,)), ((), ())), preferred_element_type=F32)


def _layer_norm(x, g, b):
    mu = jnp.mean(x, axis=-1, keepdims=True)
    xc = x - mu
    var = jnp.mean(xc * xc, axis=-1, keepdims=True)
    return xc * lax.rsqrt(var + LN_EPS) * g + b


def _rms_norm(x, g):
    return x * lax.rsqrt(jnp.mean(x * x, axis=-1, keepdims=True) + RMS_EPS) * g


def _proj_kernel(xp_ref, xs_ref, w1_ref, qg_ref, kvg_ref, wq_ref, wqr_ref, wuk_ref, wuvt_ref,
                 cos_ref, sin_ref, perm4_ref, perm16_ref,
                 qa_ref, ka_ref, va_ref, q4_ref, k4_ref, v4_ref, q16_ref, k16_ref, v16_ref,
                 qb_ref, kb_ref, vbt_ref, *, n_prompt_tiles):
    i = pl.program_id(0)
    tm = xp_ref.shape[0]
    x = jnp.where(i < n_prompt_tiles, xp_ref[...], xs_ref[...]).astype(BF16)
    p = jnp.dot(x, w1_ref[...], preferred_element_type=F32)
    qa = (p[:, 0:A_WIDTH] * (A_HEAD_DIM ** -0.5 * LOG2E)).astype(BF16)
    ka = p[:, A_WIDTH:2 * A_WIDTH].astype(BF16)
    va = p[:, 2 * A_WIDTH:3 * A_WIDTH].astype(BF16)
    qa_ref[...] = qa
    ka_ref[...] = ka
    va_ref[...] = va
    qkv = jnp.concatenate([qa, ka, va], axis=1)
    for dil, perm_ref, outs in ((4, perm4_ref, (q4_ref, k4_ref, v4_ref)),
                                (16, perm16_ref, (q16_ref, k16_ref, v16_ref))):
        cm = jnp.dot(perm_ref[...], qkv, preferred_element_type=F32).astype(BF16)
        rows = tm // dil
        for r in range(dil):
            for j, o_ref in enumerate(outs):
                o_ref[:, r * A_WIDTH:(r + 1) * A_WIDTH] = cm[r * rows:(r + 1) * rows,
                                                             j * A_WIDTH:(j + 1) * A_WIDTH]
    c0 = 3 * A_WIDTH
    cq = p[:, c0:c0 + Q_LORA]
    ckv = p[:, c0 + Q_LORA:c0 + Q_LORA + KV_LORA]
    kr = p[:, c0 + Q_LORA + KV_LORA:c0 + Q_LORA + KV_LORA + LANES]
    krr = p[:, c0 + Q_LORA + KV_LORA + LANES:c0 + Q_LORA + KV_LORA + 2 * LANES]
    cos = cos_ref[...]
    sin = sin_ref[...]
    cos8 = jnp.concatenate([cos] * B_HEADS, axis=1)
    sin8 = jnp.concatenate([sin] * B_HEADS, axis=1)
    cqn = _rms_norm(cq, qg_ref[...]).astype(BF16)
    q = jnp.dot(cqn, wq_ref[...], preferred_element_type=F32)
    qr = jnp.dot(cqn, wqr_ref[...], preferred_element_type=F32)
    qscale = (B_NOPE + B_ROPE) ** -0.5 * LOG2E
    qb_ref[...] = ((q * cos8 + qr * sin8) * qscale).astype(BF16)
    ckvn = _rms_norm(ckv, kvg_ref[...]).astype(BF16)
    kn = jnp.dot(ckvn, wuk_ref[...], preferred_element_type=F32)
    krope = kr * cos + krr * sin
    kb_ref[...] = (kn + jnp.concatenate([krope] * B_HEADS, axis=1)).astype(BF16)
    vbt_ref[...] = _nt_dot(wuvt_ref[...], ckvn).astype(BF16)


def _class_perm(tm, dil):
    rows = tm // dil
    c = np.arange(tm)
    src = (c % rows) * dil + c // rows
    perm = np.zeros((tm, tm), np.float32)
    perm[c, src] = 1.0
    return jnp.asarray(perm, BF16)


def _proj(xp, xs, w1, qg, kvg, wq, wqr, wuk, wuvt, cos_t, sin_t, s_prompt, s_sample):
    tm = TM_PROJ
    n1 = xp.shape[0] // tm
    n2 = xs.shape[0] // tm
    t_all = xp.shape[0] + xs.shape[0]
    pt, st = s_prompt // tm, s_sample // tm
    assert tm == PERM_GROUP
    perm4, perm16 = _class_perm(tm, 4), _class_perm(tm, 16)

    def tab_idx(i):
        return (jnp.where(i < n1, i % pt, (i - n1) % st), 0)

    full = lambda shape: pl.BlockSpec(shape, lambda i: (0, 0))
    row = lambda cols: pl.BlockSpec((tm, cols), lambda i: (i, 0))
    cls = lambda dil: pl.BlockSpec((tm // dil, dil * A_WIDTH), lambda i: (i, 0))
    cls_shape = lambda dil: jax.ShapeDtypeStruct((t_all // dil, dil * A_WIDTH), BF16)
    return pl.pallas_call(
        functools.partial(_proj_kernel, n_prompt_tiles=n1),
        grid=(n1 + n2,),
        in_specs=[
            pl.BlockSpec((tm, D_MODEL), lambda i: (jnp.minimum(i, n1 - 1), 0)),
            pl.BlockSpec((tm, D_MODEL), lambda i: (jnp.maximum(i - n1, 0), 0)),
            full(w1.shape), full(qg.shape), full(kvg.shape), full(wq.shape), full(wqr.shape),
            full(wuk.shape), full(wuvt.shape),
            pl.BlockSpec((tm, LANES), tab_idx), pl.BlockSpec((tm, LANES), tab_idx),
            full(perm4.shape), full(perm16.shape),
        ],
        out_specs=[row(A_WIDTH)] * 3 + [cls(4)] * 3 + [cls(16)] * 3
        + [row(B_HEADS * HEAD_PAD), row(B_HEADS * HEAD_PAD),
           pl.BlockSpec((B_HEADS * B_V, tm), lambda i: (0, i))],
        out_shape=[jax.ShapeDtypeStruct((t_all, A_WIDTH), BF16)] * 3
        + [cls_shape(4)] * 3 + [cls_shape(16)] * 3
        + [jax.ShapeDtypeStruct((t_all, B_HEADS * HEAD_PAD), BF16)] * 2
        + [jax.ShapeDtypeStruct((B_HEADS * B_V, t_all), BF16)],
        compiler_params=pltpu.CompilerParams(dimension_semantics=("arbitrary",),
                                             vmem_limit_bytes=VMEM_LIMIT),
        name="proj",
    )(xp, xs, w1, qg, kvg, wq, wqr, wuk, wuvt, cos_t, sin_t, perm4, perm16)


def _attn_a_kernel(qblk_ref, pblk_ref, nblk_ref, tstart_ref, llen_ref,
                   q_ref, kp_ref, kc_ref, kn_ref, vp_ref, vc_ref, vn_ref, bias_ref, perm_ref,
                   o_ref, lse_ref, cm_ref, *, n_pos, n_cls):
    n = pl.program_id(0)
    tq, wk = TQ_A, TQ_A + 2 * A_HALF
    lane = lax.broadcasted_iota(I32, (tq, LANES), 1)
    low = lane < A_HEAD_DIM
    colbase = lax.broadcasted_iota(I32, (1, wk), 1) + (tstart_ref[n] - A_HALF)
    cls_len = llen_ref[n]

    def one_class(r):
        cols = slice(0, A_WIDTH) if n_cls == 1 else pl.ds(pl.multiple_of(r * A_WIDTH, A_WIDTH), A_WIDTH)
        kcat = jnp.concatenate([kp_ref[:, cols], kc_ref[:, cols], kn_ref[:, cols]], axis=0)
        vcat = jnp.concatenate([vp_ref[:, cols], vc_ref[:, cols], vn_ref[:, cols]], axis=0)
        for j in range(n_pos):
            rows = slice(j * tq, (j + 1) * tq)
            q = q_ref[rows, cols]
            kw, vw = kcat[j * tq:j * tq + wk, :], vcat[j * tq:j * tq + wk, :]
            col = colbase + j * tq
            colpen = jnp.where((col >= 0) & (col < cls_len), 0.0, NEG_BIG).astype(F32)
            pairs = []
            lse_c = jnp.zeros((tq, LANES), F32)
            for jp in range(A_HEADS // 2):
                sl = slice(jp * LANES, (jp + 1) * LANES)
                qp, kpair, vpair = q[:, sl], kw[:, sl], vw[:, sl]
                outs = []
                for e in range(2):
                    qm = jnp.where(low if e == 0 else ~low, qp, jnp.zeros_like(qp))
                    s = _nt_dot(qm, kpair) + bias_ref[2 * jp + e] + colpen
                    m = jnp.max(s, axis=1, keepdims=True)
                    p = jnp.exp2(s - m)
                    l = jnp.sum(p, axis=1, keepdims=True)
                    outs.append(jnp.dot(p.astype(BF16), vpair, preferred_element_type=F32) / l)
                    lse_c = lse_c + jnp.where(lane == 2 * jp + e, m + jnp.log2(l), 0.0)
                pairs.append(jnp.where(low, outs[0], outs[1]))
            o_full = jnp.concatenate(pairs, axis=1).astype(BF16)
            if n_cls == 1:
                o_ref[rows, :] = o_full
                lse_ref[rows, :] = lse_c
            else:
                hi = lse_c.astype(BF16)
                rest = lse_c - hi.astype(F32)
                mid = rest.astype(BF16)
                lo = (rest - mid.astype(F32)).astype(BF16)
                cm_ref[r, rows, :] = jnp.concatenate([o_full, hi, mid, lo], axis=1)

    if n_cls == 1:
        one_class(0)
        return

    def body(r, carry):
        one_class(r)
        return carry

    lax.fori_loop(0, n_cls, body, 0)
    per = PERM_GROUP // n_cls
    for a in range(n_pos * tq * n_cls // PERM_GROUP):
        stack = jnp.concatenate([cm_ref[r, a * per:(a + 1) * per, :] for r in range(n_cls)], axis=0)
        nat = jnp.dot(perm_ref[...], stack, preferred_element_type=F32)
        rows = slice(a * PERM_GROUP, (a + 1) * PERM_GROUP)
        o_ref[rows, :] = nat[:, :A_WIDTH].astype(BF16)
        lse_ref[rows, :] = (nat[:, A_WIDTH:A_WIDTH + LANES] + nat[:, A_WIDTH + LANES:A_WIDTH + 2 * LANES]
                            + nat[:, A_WIDTH + 2 * LANES:])


def _attn_a_tables(seqs, dil, n_pos):
    rows = n_pos * TQ_A
    per_halo = rows // A_HALF
    qblk, pblk, nblk, tstart, llen = [], [], [], [], []
    for off, s_len in seqs:
        cls = s_len // dil
        steps = cls // rows
        base = (off // dil) // rows
        for i in range(steps):
            qblk.append(base + i)
            pblk.append((base + i) * per_halo - (1 if i > 0 else 0))
            nblk.append((base + i + 1) * per_halo - (0 if i < steps - 1 else 1))
            tstart.append(i * rows)
            llen.append(cls)
    return [jnp.asarray(np.asarray(a, np.int32)) for a in (qblk, pblk, nblk, tstart, llen)]


def _attn_a_bias(dil):
    tq, wk = TQ_A, TQ_A + 2 * A_HALF
    delta = np.abs(np.arange(wk)[None, :] - A_HALF - np.arange(tq)[:, None]).astype(np.float64)
    slopes = 2.0 ** (-8.0 * (np.arange(A_HEADS) + 1.0) / A_HEADS)
    bias = -slopes[:, None, None] * (delta * dil)[None] * LOG2E
    bias = np.where((delta <= A_HALF)[None], bias, NEG_BIG)
    return jnp.asarray(bias.astype(np.float32))


def _attn_a(qc, kc, vc, seqs, dil):
    t_all = qc.shape[0] * dil
    n_pos = A_POS_TILES[dil]
    rows, width = n_pos * TQ_A, dil * A_WIDTH
    tabs = _attn_a_tables(seqs, dil, n_pos)
    n_steps = int(tabs[0].shape[0])
    bias = _attn_a_bias(dil)
    perm = _class_perm(PERM_GROUP, dil).T
    cur = pl.BlockSpec((rows, width), lambda n, qb, pb, nb, ts, ll: (qb[n], 0))
    prev = pl.BlockSpec((A_HALF, width), lambda n, qb, pb, nb, ts, ll: (pb[n], 0))
    nxt = pl.BlockSpec((A_HALF, width), lambda n, qb, pb, nb, ts, ll: (nb[n], 0))
    tok = lambda cols: pl.BlockSpec((rows * dil, cols), lambda n, *_: (n, 0))
    return pl.pallas_call(
        functools.partial(_attn_a_kernel, n_pos=n_pos, n_cls=dil),
        grid_spec=pltpu.PrefetchScalarGridSpec(
            num_scalar_prefetch=5, grid=(n_steps,),
            in_specs=[cur, prev, cur, nxt, prev, cur, nxt,
                      pl.BlockSpec(bias.shape, lambda n, *_: (0, 0, 0)),
                      pl.BlockSpec(perm.shape, lambda n, *_: (0, 0))],
            out_specs=[tok(A_WIDTH), tok(LANES)],
            scratch_shapes=[pltpu.VMEM((dil, rows, A_WIDTH + 3 * LANES), BF16)]),
        out_shape=[jax.ShapeDtypeStruct((t_all, A_WIDTH), BF16),
                   jax.ShapeDtypeStruct((t_all, LANES), F32)],
        compiler_params=pltpu.CompilerParams(dimension_semantics=("arbitrary",),
                                             vmem_limit_bytes=VMEM_LIMIT),
        name=f"attn_a_d{dil}",
    )(*tabs, qc, kc, kc, kc, vc, vc, vc, bias, perm)


def _attn_b_kernel(q_ref, k_ref, vt_ref, o_ref, sta_ref, stb_ref, ma_ref, mb_ref):
    s = pl.program_id(0)
    tq, s_len = q_ref.shape[0], k_ref.shape[0]

    @pl.when(s == 0)
    def _():
        stb_ref[...] = jnp.zeros_like(stb_ref)
        mb_ref[...] = jnp.zeros_like(mb_ref)

    def step(st_new_ref, m_new_ref, st_old_ref, m_old_ref):
        m_old = m_old_ref[...]
        q = q_ref[...]
        m_new = jnp.full((1, tq), -jnp.inf, F32)
        l = jnp.zeros((1, tq), F32)
        acc = jnp.zeros((B_V, tq), F32)
        for c in range(s_len // KEY_CHUNK_B):
            ks = slice(c * KEY_CHUNK_B, (c + 1) * KEY_CHUNK_B)
            st_c = _nt_dot(k_ref[ks, :], q)
            st_new_ref[ks, :] = st_c
            m_new = jnp.maximum(m_new, jnp.max(st_c, axis=0, keepdims=True))
            p = jnp.exp2(st_old_ref[ks, :] - m_old)
            l = l + jnp.sum(p, axis=0, keepdims=True)
            acc = acc + jnp.dot(vt_ref[:, ks], p.astype(BF16), preferred_element_type=F32)
        m_new_ref[...] = m_new
        o_ref[...] = (acc / l).astype(BF16)

    @pl.when(s % 2 == 0)
    def _():
        step(sta_ref, ma_ref, stb_ref, mb_ref)

    @pl.when(s % 2 == 1)
    def _():
        step(stb_ref, mb_ref, sta_ref, ma_ref)


def _attn_b(qb, kb, vbt, off, n_batch, s_len):
    tq = TQ_B
    nq = s_len // tq
    qbase, kbase = off // tq, off // s_len
    n_items = n_batch * B_HEADS * nq

    def split(item):
        return item // (B_HEADS * nq), (item // nq) % B_HEADS, item % nq

    def q_map(s):
        b, h, qi = split(jnp.minimum(s, n_items - 1))
        return (qbase + b * nq + qi, h)

    def k_map(s):
        b, h, _ = split(jnp.minimum(s, n_items - 1))
        return (kbase + b, h)

    def v_map(s):
        b, h, _ = split(jnp.maximum(s - 1, 0))
        return (h, kbase + b)

    def o_map(s):
        b, h, qi = split(jnp.maximum(s - 1, 0))
        return (h, b * nq + qi)

    return pl.pallas_call(
        _attn_b_kernel,
        grid=(n_items + 1,),
        in_specs=[pl.BlockSpec((tq, HEAD_PAD), q_map),
                  pl.BlockSpec((s_len, HEAD_PAD), k_map),
                  pl.BlockSpec((B_V, s_len), v_map)],
        out_specs=pl.BlockSpec((B_V, tq), o_map),
        out_shape=jax.ShapeDtypeStruct((B_HEADS * B_V, n_batch * s_len), BF16),
        scratch_shapes=[pltpu.VMEM((s_len, tq), F32), pltpu.VMEM((s_len, tq), F32),
                        pltpu.VMEM((1, tq), F32), pltpu.VMEM((1, tq), F32)],
        compiler_params=pltpu.CompilerParams(dimension_semantics=("arbitrary",),
                                             vmem_limit_bytes=VMEM_LIMIT),
        name=f"attn_b_s{s_len}",
    )(qb, kb, vbt)


def _mix_kernel(xp_ref, xs_ref, o0_ref, o1_ref, o2_ref, l0_ref, l1_ref, l2_ref, spread_ref,
                obp_ref, obs_ref, wout_ref,
                g_ref, b_ref, wrh_ref, wrl_ref, rb_ref,
                h_ref, hr_ref, tope_ref, pos_ref, gate_ref, cnt_out_ref, cnt_ref, *, n_prompt_tiles):
    i = pl.program_id(0)
    tm = xp_ref.shape[0]

    @pl.when(i == 0)
    def _():
        cnt_ref[...] = jnp.zeros_like(cnt_ref)

    l0, l1, l2 = l0_ref[...], l1_ref[...], l2_ref[...]
    lmax = jnp.maximum(jnp.maximum(l0, l1), l2)
    e0, e1, e2 = jnp.exp2(l0 - lmax), jnp.exp2(l1 - lmax), jnp.exp2(l2 - lmax)
    inv = 1.0 / (e0 + e1 + e2)
    spread = spread_ref[...]

    def per_lane(w):
        hi = w.astype(BF16)
        lo = (w - hi.astype(F32)).astype(BF16)
        return (jnp.dot(hi, spread, preferred_element_type=F32)
                + jnp.dot(lo, spread, preferred_element_type=F32))

    oa = (per_lane(e0 * inv) * o0_ref[...].astype(F32) + per_lane(e1 * inv) * o1_ref[...].astype(F32)
          + per_lane(e2 * inv) * o2_ref[...].astype(F32))
    is_prompt = i < n_prompt_tiles
    obt = jnp.where(is_prompt, obp_ref[...], obs_ref[...])
    mix = (jnp.dot(oa.astype(BF16), wout_ref[0:A_WIDTH, :], preferred_element_type=F32)
           + lax.dot_general(obt, wout_ref[A_WIDTH:, :], (((0,), (0,)), ((), ())),
                             preferred_element_type=F32))
    x = jnp.where(is_prompt, xp_ref[...], xs_ref[...])
    h = _layer_norm(ALPHA * x + mix, g_ref[...], b_ref[...])
    h_ref[...] = h
    _rows_store(hr_ref, h)

    h_hi = h.astype(BF16)
    h_lo = (h - h_hi.astype(F32)).astype(BF16)
    wrh = wrh_ref[...]
    logits = _nt_dot(wrh, h_hi) + _nt_dot(wrh, h_lo) + _nt_dot(wrl_ref[...], h_hi)
    scores = jax.nn.sigmoid(logits)
    sel = scores + rb_ref[...]

    sub = lax.broadcasted_iota(I32, (GROUP_SIZE, tm), 0).astype(F32)
    gscore = []
    for g in range(N_GROUPS):
        sg = sel[g * GROUP_SIZE:(g + 1) * GROUP_SIZE, :]
        m1 = jnp.max(sg, axis=0, keepdims=True)
        first = jnp.min(jnp.where(sg == m1, sub, float(GROUP_SIZE)), axis=0, keepdims=True)
        m2 = jnp.max(jnp.where(sub == first, -jnp.inf, sg), axis=0, keepdims=True)
        gscore.append(m1 + m2)
    cands = []
    for g in range(N_GROUPS):
        beaten = jnp.zeros((1, tm), F32)
        for g2 in range(N_GROUPS):
            if g2 == g:
                continue
            wins = (gscore[g2] > gscore[g]) | ((gscore[g2] == gscore[g]) & (g2 < g))
            beaten = beaten + wins.astype(F32)
        keep = beaten < float(TOPK_GROUPS)
        sg = sel[g * GROUP_SIZE:(g + 1) * GROUP_SIZE, :]
        cands.append(jnp.where(keep, sg, NEG_BIG))
    cand = jnp.concatenate(cands, axis=0)

    eidx = lax.broadcasted_iota(I32, (N_EXPERTS, tm), 0).astype(F32)
    picked_idx, picked_gate = [], []
    onehot = jnp.zeros((N_EXPERTS, tm), F32)
    for _ in range(TOP_K):
        mx = jnp.max(cand, axis=0, keepdims=True)
        fi = jnp.min(jnp.where(cand == mx, eidx, float(N_EXPERTS)), axis=0, keepdims=True)
        pick = eidx == fi
        picked_idx.append(fi)
        picked_gate.append(jnp.sum(jnp.where(pick, scores, 0.0), axis=0, keepdims=True))
        onehot = onehot + pick.astype(F32)
        cand = jnp.where(pick, -jnp.inf, cand)
    gsum = picked_gate[0]
    for k in range(1, TOP_K):
        gsum = gsum + picked_gate[k]

    tri = (lax.broadcasted_iota(I32, (tm, tm), 0) < lax.broadcasted_iota(I32, (tm, tm), 1))
    before = jnp.dot(onehot.astype(BF16), tri.astype(BF16), preferred_element_type=F32)
    rank = before + cnt_ref[:, 0:1]
    for k in range(TOP_K):
        pick = eidx == picked_idx[k]
        tope_ref[k:k + 1, :] = picked_idx[k].astype(I32)
        pos_ref[k:k + 1, :] = jnp.sum(jnp.where(pick, rank, 0.0), axis=0, keepdims=True).astype(I32)
        gate_ref[k:k + 1, :] = picked_gate[k] / gsum * ROUTED_SCALE
    cnt_ref[...] = cnt_ref[...] + jnp.sum(onehot, axis=1, keepdims=True)
    cnt_out_ref[...] = cnt_ref[...]


def _mix(xp, xs, outs, lses, obp, obs, wout, g, b, wrh, wrl, rb):
    tm = TM_MIX
    n1 = xp.shape[0] // tm
    n2 = xs.shape[0] // tm
    t_all = xp.shape[0] + xs.shape[0]
    full = lambda a: pl.BlockSpec(a.shape, lambda i: (0,) * a.ndim)
    row = lambda cols: pl.BlockSpec((tm, cols), lambda i: (i, 0))
    prow = lambda cols: pl.BlockSpec((tm, cols), lambda i: (jnp.minimum(i, n1 - 1), 0))
    srow = lambda cols: pl.BlockSpec((tm, cols), lambda i: (jnp.maximum(i - n1, 0), 0))
    col = pl.BlockSpec((TOP_K, tm), lambda i: (0, i))
    head_of_lane = np.arange(A_WIDTH) // A_HEAD_DIM
    spread = jnp.asarray(np.arange(LANES)[:, None] == head_of_lane[None, :], BF16)
    return pl.pallas_call(
        functools.partial(_mix_kernel, n_prompt_tiles=n1),
        grid=(n1 + n2,),
        in_specs=[
            prow(D_MODEL), srow(D_MODEL),
            row(A_WIDTH), row(A_WIDTH), row(A_WIDTH), row(LANES), row(LANES), row(LANES), full(spread),
            pl.BlockSpec((B_HEADS * B_V, tm), lambda i: (0, jnp.minimum(i, n1 - 1))),
            pl.BlockSpec((B_HEADS * B_V, tm), lambda i: (0, jnp.maximum(i - n1, 0))),
            full(wout), full(g), full(b), full(wrh), full(wrl), full(rb),
        ],
        out_specs=[row(D_MODEL), pl.BlockSpec(_rows_shape(tm), lambda i: (i, 0, 0, 0)), col, col, col,
                   pl.BlockSpec((N_EXPERTS, LANES), lambda i: (0, 0))],
        out_shape=[jax.ShapeDtypeStruct((t_all, D_MODEL), F32),
                   jax.ShapeDtypeStruct(_rows_shape(t_all), F32),
                   jax.ShapeDtypeStruct((TOP_K, t_all), I32),
                   jax.ShapeDtypeStruct((TOP_K, t_all), I32),
                   jax.ShapeDtypeStruct((TOP_K, t_all), F32),
                   jax.ShapeDtypeStruct((N_EXPERTS, LANES), F32)],
        scratch_shapes=[pltpu.VMEM((N_EXPERTS, LANES), F32)],
        compiler_params=pltpu.CompilerParams(dimension_semantics=("arbitrary",),
                                             vmem_limit_bytes=VMEM_LIMIT),
        name="mix_router",
    )(xp, xs, *outs, *lses, spread, obp, obs, wout, g, b, wrh, wrl, rb)


def _dispatch_kernel(zstart_ref, nused_ref, h_ref, tope_ref, pos_ref, pstart_ref, buf_ref, dest_ref,
                     dg_ref, ds_ref, zero_ref, sem_ref, zsem_ref):
    i = pl.program_id(0)
    tm = tope_ref.shape[1]
    tile_groups = ROW_BLOCK // SUBLANES
    fill_groups = zero_ref.shape[0]

    @pl.when(i == 0)
    def _():
        zero_ref[...] = jnp.zeros_like(zero_ref)

        def zfill(e, carry):
            start = lax.shift_right_logical(zstart_ref[e], 3)
            pltpu.make_async_copy(zero_ref, buf_ref.at[pl.ds(start, fill_groups)], zsem_ref).start()
            return carry

        lax.fori_loop(0, N_EXPERTS, zfill, 0)
        span = buf_ref.at[pl.ds(0, N_EXPERTS * fill_groups)]
        pltpu.make_async_copy(span, span, zsem_ref).wait()

        def tail_copy(j):
            start = pl.multiple_of(j * tile_groups, tile_groups)
            return pltpu.make_async_copy(zero_ref.at[pl.ds(0, tile_groups)],
                                         buf_ref.at[pl.ds(start, tile_groups)], zsem_ref)

        def ztail(j, carry):
            tail_copy(j).start()
            return carry

        def zwait(j, carry):
            tail_copy(j).wait()
            return carry

        n_tiles = buf_ref.shape[0] // tile_groups
        lax.fori_loop(nused_ref[0], n_tiles, ztail, 0)
        lax.fori_loop(nused_ref[0], n_tiles, zwait, 0)

    eidx = lax.broadcasted_iota(I32, (N_EXPERTS, tm), 0)
    pstart = pstart_ref[...]
    for k in range(TOP_K):
        hit = eidx == tope_ref[k:k + 1, :]
        base = jnp.sum(jnp.where(hit, pstart, 0.0), axis=0, keepdims=True)
        row = pos_ref[k:k + 1, :] + base.astype(I32)
        dest_ref[0, 0:1, k * tm:(k + 1) * tm] = lax.shift_right_logical(row, 3)
        dest_ref[0, 1:2, k * tm:(k + 1) * tm] = row & (SUBLANES - 1)
    pltpu.sync_copy(dest_ref.at[0, 0], dg_ref)
    pltpu.sync_copy(dest_ref.at[0, 1], ds_ref)

    def scatter(g, carry):
        for j in range(SUBLANES):
            src = _row_of(h_ref, g, j)
            for k in range(TOP_K):
                slot = k * tm + g * SUBLANES + j
                pltpu.make_async_copy(src, _row_of(buf_ref, dg_ref[slot], ds_ref[slot]), sem_ref).start(
                    priority=k % 2)
        return carry

    lax.fori_loop(0, tm // SUBLANES, scatter, 0)
    span = buf_ref.at[pl.ds(0, TOP_K * tm // SUBLANES)]
    pltpu.make_async_copy(span, span, sem_ref).wait()


def _dispatch(h_rows, tope, pos, pstart, zstart, nused, n_rows):
    tm = TM_DISP
    t_all = tope.shape[1]
    col = pl.BlockSpec((TOP_K, tm), lambda i, z, nu: (0, i))
    table = pl.BlockSpec((1, 2, TOP_K * tm), lambda i, z, nu: (i, 0, 0))
    return pl.pallas_call(
        _dispatch_kernel,
        grid_spec=pltpu.PrefetchScalarGridSpec(
            num_scalar_prefetch=2, grid=(t_all // tm,),
            in_specs=[pl.BlockSpec(_rows_shape(tm), lambda i, z, nu: (i, 0, 0, 0)), col, col,
                      pl.BlockSpec((N_EXPERTS, 1), lambda i, z, nu: (0, 0))],
            out_specs=[pl.BlockSpec(memory_space=pl.ANY), table],
            scratch_shapes=[pltpu.SMEM((TOP_K * tm,), I32), pltpu.SMEM((TOP_K * tm,), I32),
                            pltpu.VMEM(_rows_shape(ROW_BLOCK + SUBLANES), F32),
                            pltpu.SemaphoreType.DMA, pltpu.SemaphoreType.DMA]),
        out_shape=[jax.ShapeDtypeStruct(_rows_shape(n_rows), F32),
                   jax.ShapeDtypeStruct((t_all // tm, 2, TOP_K * tm), I32)],
        compiler_params=pltpu.CompilerParams(dimension_semantics=("arbitrary",),
                                             vmem_limit_bytes=VMEM_LIMIT),
        name="dispatch",
    )(zstart, nused, h_rows, tope, pos, pstart)


def _expert_kernel(tfirst_ref, ntile_ref, nused_ref, x_hbm, wg_ref, wu_ref, wd_ref, o_hbm,
                   xbuf, obuf, wgb_ref, wub_ref, wdb_ref, xsem, osem):
    e = pl.program_id(0)
    tile_groups = ROW_BLOCK // SUBLANES
    nused = nused_ref[0]

    def x_copy(g, slot):
        start = pl.multiple_of(g * tile_groups, tile_groups)
        return pltpu.make_async_copy(x_hbm.at[pl.ds(start, tile_groups)], xbuf.at[slot], xsem.at[slot])

    def o_copy(g, slot):
        start = pl.multiple_of(g * tile_groups, tile_groups)
        return pltpu.make_async_copy(obuf.at[slot], o_hbm.at[pl.ds(start, tile_groups)], osem.at[slot])

    @pl.when(e == 0)
    def _():
        for j in range(N_XBUF - 1):
            @pl.when(j < nused)
            def _():
                x_copy(j, j).start()

    n_e = ntile_ref[e]

    @pl.when(n_e > 0)
    def _():
        wgb_ref[...] = wg_ref[...].astype(BF16)
        wub_ref[...] = wu_ref[...].astype(BF16)
        wdb_ref[...] = wd_ref[...].astype(BF16)

    def tile(j, carry):
        g = tfirst_ref[e] + j
        ahead = g + (N_XBUF - 1)

        @pl.when(ahead < nused)
        def _():
            x_copy(ahead, ahead % N_XBUF).start()

        slot = g % N_XBUF
        x_copy(g, slot).wait()
        x = _rows_load(xbuf.at[slot], ROW_BLOCK).astype(BF16)
        gt = jnp.dot(x, wgb_ref[...], preferred_element_type=F32)
        up = jnp.dot(x, wub_ref[...], preferred_element_type=F32)
        hmid = (gt * jax.nn.sigmoid(gt) * up).astype(BF16)
        out = jnp.dot(hmid, wdb_ref[...], preferred_element_type=F32)
        oslot = g % N_OBUF

        @pl.when(g >= N_OBUF)
        def _():
            o_copy(g - N_OBUF, oslot).wait()

        _rows_store(obuf.at[oslot], out)
        o_copy(g, oslot).start()
        return carry

    lax.fori_loop(0, n_e, tile, 0)

    @pl.when(e == N_EXPERTS - 1)
    def _():
        for j in range(N_OBUF):
            @pl.when(nused > j)
            def _():
                o_copy(nused - 1 - j, (nused - 1 - j) % N_OBUF).wait()


def _experts(buf, tfirst, ntile, nused, w_gate, w_up, w_down):
    tile = _rows_shape(ROW_BLOCK)
    wmap = lambda e, tf, nt, nu: (e, 0, 0)
    return pl.pallas_call(
        _expert_kernel,
        grid_spec=pltpu.PrefetchScalarGridSpec(
            num_scalar_prefetch=3, grid=(N_EXPERTS,),
            in_specs=[pl.BlockSpec(memory_space=pl.ANY),
                      pl.BlockSpec((None, D_MODEL, EXPERT_HIDDEN), wmap),
                      pl.BlockSpec((None, D_MODEL, EXPERT_HIDDEN), wmap),
                      pl.BlockSpec((None, EXPERT_HIDDEN, D_MODEL), wmap)],
            out_specs=pl.BlockSpec(memory_space=pl.ANY),
            scratch_shapes=[pltpu.VMEM((N_XBUF,) + tile, F32),
                            pltpu.VMEM((N_OBUF,) + tile, F32),
                            pltpu.VMEM((D_MODEL, EXPERT_HIDDEN), BF16),
                            pltpu.VMEM((D_MODEL, EXPERT_HIDDEN), BF16),
                            pltpu.VMEM((EXPERT_HIDDEN, D_MODEL), BF16),
                            pltpu.SemaphoreType.DMA((N_XBUF,)),
                            pltpu.SemaphoreType.DMA((N_OBUF,))]),
        out_shape=jax.ShapeDtypeStruct(buf.shape, F32),
        input_output_aliases={3: 0},
        compiler_params=pltpu.CompilerParams(dimension_semantics=("arbitrary",),
                                             vmem_limit_bytes=VMEM_LIMIT),
        name="experts",
    )(tfirst, ntile, nused, buf, w_gate, w_up, w_down)


def _combine_kernel(h_ref, gate_ref, dest_ref, dest_next_ref, eo_ref, wsg_ref, wsu_ref, wsd_ref, g_ref, b_ref,
                    yp_ref, ys_ref, dg_ref, ds_ref, rows_ref, sem_ref, *, n_prompt_tiles, n_tiles):
    i = pl.program_id(0)
    tm = gate_ref.shape[0]
    slot = i % 2

    def start_gather(table_ref, to_slot):
        pltpu.sync_copy(table_ref.at[0, 0], dg_ref)
        pltpu.sync_copy(table_ref.at[0, 1], ds_ref)
        rows = rows_ref.at[to_slot]

        def gather(g, carry):
            for j in range(SUBLANES):
                for k in range(TOP_K):
                    slot = k * tm + g * SUBLANES + j
                    pltpu.make_async_copy(_row_of(eo_ref, dg_ref[slot], ds_ref[slot]),
                                          _row_of(rows, k * (tm // SUBLANES) + g, j),
                                          sem_ref.at[to_slot]).start(priority=k % 2)
            return carry

        lax.fori_loop(0, tm // SUBLANES, gather, 0)

    @pl.when(i == 0)
    def _():
        start_gather(dest_ref, 0)

    for nxt in range(2):
        @pl.when((i + 1 < n_tiles) & (slot == 1 - nxt))
        def _():
            start_gather(dest_next_ref, nxt)

    h = h_ref[...]
    hb = h.astype(BF16)
    sg = jnp.dot(hb, wsg_ref[...], preferred_element_type=F32)
    su = jnp.dot(hb, wsu_ref[...], preferred_element_type=F32)
    shared = jnp.dot((sg * jax.nn.sigmoid(sg) * su).astype(BF16), wsd_ref[...],
                     preferred_element_type=F32)

    rows = rows_ref.at[slot]
    pltpu.make_async_copy(eo_ref.at[pl.ds(0, TOP_K * tm // SUBLANES)], rows, sem_ref.at[slot]).wait()
    gate = gate_ref[...]
    routed = gate[:, 0:1] * _rows_load(rows, tm)
    for k in range(1, TOP_K):
        routed = routed + gate[:, k:k + 1] * _rows_load(rows, tm, k * tm)
    y = _layer_norm(ALPHA * h + (routed + shared), g_ref[...], b_ref[...])

    @pl.when(i < n_prompt_tiles)
    def _():
        yp_ref[...] = y

    @pl.when(i >= n_prompt_tiles)
    def _():
        ys_ref[...] = y


def _combine(h, gate_t, dest, eo, wsg, wsu, wsd, g, b, t_prompt):
    tm = TM_COMB
    assert tm == TM_DISP
    t_all = gate_t.shape[0]
    n1 = t_prompt // tm
    n2 = (t_all - t_prompt) // tm
    full = lambda a: pl.BlockSpec(a.shape, lambda i: (0,) * a.ndim)
    return pl.pallas_call(
        functools.partial(_combine_kernel, n_prompt_tiles=n1, n_tiles=n1 + n2),
        grid=(n1 + n2,),
        in_specs=[pl.BlockSpec((tm, D_MODEL), lambda i: (i, 0)),
                  pl.BlockSpec((tm, TOP_K), lambda i: (i, 0)),
                  pl.BlockSpec((1, 2, TOP_K * tm), lambda i: (i, 0, 0)),
                  pl.BlockSpec((1, 2, TOP_K * tm), lambda i: (jnp.minimum(i + 1, n1 + n2 - 1), 0, 0)),
                  pl.BlockSpec(memory_space=pl.ANY),
                  full(wsg), full(wsu), full(wsd), full(g), full(b)],
        out_specs=[pl.BlockSpec((tm, D_MODEL), lambda i: (jnp.minimum(i, n1 - 1), 0)),
                   pl.BlockSpec((tm, D_MODEL), lambda i: (jnp.maximum(i - n1, 0), 0))],
        out_shape=[jax.ShapeDtypeStruct((t_prompt, D_MODEL), F32),
                   jax.ShapeDtypeStruct((t_all - t_prompt, D_MODEL), F32)],
        scratch_shapes=[pltpu.SMEM((TOP_K * tm,), I32), pltpu.SMEM((TOP_K * tm,), I32),
                        pltpu.VMEM((2,) + _rows_shape(TOP_K * tm), F32),
                        pltpu.SemaphoreType.DMA((2,))],
        compiler_params=pltpu.CompilerParams(dimension_semantics=("arbitrary",),
                                             vmem_limit_bytes=VMEM_LIMIT),
        name="combine",
    )(h, gate_t, dest, dest, eo, wsg, wsu, wsd, g, b)


def _rope_tables(s_max):
    inv_freq = ROPE_BASE ** (-jnp.arange(0, B_ROPE, 2, dtype=F32) / B_ROPE)
    ang = jnp.arange(s_max, dtype=F32)[:, None] * inv_freq[None, :]
    cos, sin = jnp.cos(ang), jnp.sin(ang)
    ones = jnp.ones((s_max, B_NOPE), F32)
    zeros_n = jnp.zeros((s_max, B_NOPE), F32)
    zeros_p = jnp.zeros((s_max, HEAD_PAD - B_NOPE - B_ROPE), F32)
    return (jnp.concatenate([ones, cos, cos, zeros_p], axis=1),
            jnp.concatenate([zeros_n, sin, sin, zeros_p], axis=1))


def _rot_cols(w):
    half = B_ROPE // 2
    return jnp.concatenate([-w[..., half:], w[..., :half]], axis=-1)


def _layout_weights(w_in, w_uq, w_uk):
    c_kr = 3 * A_WIDTH + Q_LORA + KV_LORA
    w_kr = w_in[:, c_kr:c_kr + B_ROPE]
    pad_l = jnp.zeros((D_MODEL, B_NOPE), F32)
    pad_r = jnp.zeros((D_MODEL, HEAD_PAD - B_NOPE - B_ROPE), F32)
    w1 = jnp.concatenate([w_in[:, :c_kr], pad_l, w_kr, pad_r, pad_l, _rot_cols(w_kr), pad_r], axis=1)
    wq3 = w_uq.reshape(Q_LORA, B_HEADS, B_NOPE + B_ROPE)
    nope, rope = wq3[..., :B_NOPE], wq3[..., B_NOPE:]
    zpad = jnp.zeros((Q_LORA, B_HEADS, HEAD_PAD - B_NOPE - B_ROPE), F32)
    wq = jnp.concatenate([nope, rope, zpad], axis=-1).reshape(Q_LORA, B_HEADS * HEAD_PAD)
    wqr = jnp.concatenate([jnp.zeros_like(nope), _rot_cols(rope), zpad], axis=-1)
    wqr = wqr.reshape(Q_LORA, B_HEADS * HEAD_PAD)
    wk3 = w_uk.reshape(KV_LORA, B_HEADS, B_NOPE)
    wuk = jnp.concatenate([wk3, jnp.zeros((KV_LORA, B_HEADS, HEAD_PAD - B_NOPE), F32)], axis=-1)
    wuk = wuk.reshape(KV_LORA, B_HEADS * HEAD_PAD)
    return w1.astype(BF16), wq.astype(BF16), wqr.astype(BF16), wuk.astype(BF16)


def _forward(x_prompt, x_sample, w_in, w_out, ln1_g, ln1_b, q_norm_g, w_uq, kv_norm_g, w_uk, w_uv,
             w_router, router_bias, w_gate, w_up, w_down, ws_gate, ws_up, ws_down, ln2_g, ln2_b):
    b1, s1, _ = x_prompt.shape
    b2, s2, _ = x_sample.shape
    t1, t2 = b1 * s1, b2 * s2
    t_all = t1 + t2
    xp = x_prompt.reshape(t1, D_MODEL)
    xs = x_sample.reshape(t2, D_MODEL)
    seqs = [(b * s1, s1) for b in range(b1)] + [(t1 + b * s2, s2) for b in range(b2)]
    assert t1 % s2 == 0 and s1 % (TQ_A * 16) == 0 and s2 % (TQ_A * 16) == 0

    w1, wq, wqr, wuk = _layout_weights(w_in, w_uq, w_uk)
    cos_t, sin_t = _rope_tables(max(s1, s2))
    (qa, ka, va, q4, k4, v4, q16, k16, v16, qb, kb, vbt) = _proj(
        xp, xs, w1, q_norm_g.reshape(1, -1), kv_norm_g.reshape(1, -1), wq, wqr, wuk,
        w_uv.T.astype(BF16), cos_t, sin_t, s1, s2)

    outs, lses = [], []
    for (_, dil), qkv in zip(A_PATTERNS, ((qa, ka, va), (q4, k4, v4), (q16, k16, v16))):
        o, lse = _attn_a(*qkv, seqs, dil)
        outs.append(o)
        lses.append(lse)
    obp = _attn_b(qb, kb, vbt, 0, b1, s1)
    obs = _attn_b(qb, kb, vbt, t1, b2, s2)

    wr_t = w_router.T
    wr_hi = wr_t.astype(BF16)
    wr_lo = (wr_t - wr_hi.astype(F32)).astype(BF16)
    h, h_rows, tope, pos, gate, counts = _mix(xp, xs, outs, lses, obp, obs, w_out.astype(BF16),
                                                ln1_g.reshape(1, -1), ln1_b.reshape(1, -1),
                                                wr_hi, wr_lo, router_bias.reshape(-1, 1))

    cnt = counts[:, 0].astype(I32)
    padded = ((cnt + ROW_BLOCK - 1) // ROW_BLOCK) * ROW_BLOCK
    pend = jnp.cumsum(padded)
    pstart = pend - padded
    n_tiles = (t_all * TOP_K) // ROW_BLOCK + N_EXPERTS + 1
    nused = (pend[-1:] // ROW_BLOCK).astype(I32)

    buf, dest = _dispatch(h_rows, tope, pos, pstart.astype(F32).reshape(-1, 1),
                          (pstart + cnt).astype(I32), nused, n_tiles * ROW_BLOCK)
    eo = _experts(buf, (pstart // ROW_BLOCK).astype(I32), (padded // ROW_BLOCK).astype(I32), nused,
                  w_gate, w_up, w_down)
    yp, ys = _combine(h, gate.T, dest, eo, ws_gate.astype(BF16), ws_up.astype(BF16),
                      ws_down.astype(BF16), ln2_g.reshape(1, -1), ln2_b.reshape(1, -1), t1)
    return yp.reshape(b1, s1, D_MODEL), ys.reshape(b2, s2, D_MODEL)


def kernel(x_prompt, x_sample, w_in, w_out, ln1_g, ln1_b, q_norm_g, w_uq, kv_norm_g, w_uk, w_uv,
           w_router, router_bias, w_gate, w_up, w_down, ws_gate, ws_up, ws_down, ln2_g, ln2_b):
    params = (w_in, w_out, ln1_g, ln1_b, q_norm_g, w_uq, kv_norm_g, w_uk, w_uv, w_router, router_bias,
              w_gate, w_up, w_down, ws_gate, ws_up, ws_down, ln2_g, ln2_b)
    assert all(p.shape[0] == 1 for p in params), "one encoder layer"
    return _forward(x_prompt, x_sample, *[p.reshape(p.shape[1:]) for p in params])
```

```python
import functools
import math

import numpy as np
import jax
import jax.numpy as jnp
from jax import lax
from jax.experimental import pallas as pl
from jax.experimental.pallas import tpu as pltpu

F32 = jnp.float32
BF16 = jnp.bfloat16
I32 = jnp.int32

D_MODEL = 1024
A_HEADS = 8
A_HEAD_DIM = 64
A_WIDTH = A_HEADS * A_HEAD_DIM
A_PATTERNS = ((128, 1), (512, 4), (2048, 16))
A_HALF = 64
B_HEADS = 8
B_NOPE = 64
B_ROPE = 32
B_V = 64
Q_LORA = 256
KV_LORA = 128
ROPE_BASE = 10000.0
N_EXPERTS = 256
TOP_K = 8
N_GROUPS = 8
GROUP_SIZE = N_EXPERTS // N_GROUPS
TOPK_GROUPS = 4
EXPERT_HIDDEN = 256
ROUTED_SCALE = 2.5
LN_EPS = 1e-5
RMS_EPS = 1e-6
NEG_BIG = -1e30
ALPHA = 2.0 ** 0.25
LOG2E = 1.4426950408889634

LANES = 128
HEAD_PAD = 128
SLAB = D_MODEL // LANES
W1_COLS = 3 * A_WIDTH + Q_LORA + KV_LORA + 2 * LANES

TM_PROJ = 256
TQ_A = 128
A_POS_TILES = {1: 4, 4: 2, 16: 1}
PERM_GROUP = 256
LOGITS_ELEMS_B = 4096 * 512
KEY_CHUNK_B = 512
TM_MIX = 256
TM_DISP = 256
ROW_BLOCK = 256
TM_COMB = 128
N_XBUF = 5
N_OBUF = 3
VMEM_LIMIT = 48 * 1024 * 1024


def _slab_load(ref, n_tok, tok0=0):
    return jnp.concatenate([ref[pl.ds(tok0 * SLAB + c, n_tok, stride=SLAB), :] for c in range(SLAB)],
                           axis=1)


def _slab_store(ref, val):
    for c in range(SLAB):
        ref[pl.ds(c, val.shape[0], stride=SLAB), :] = val[:, c * LANES:(c + 1) * LANES]


def _slab_load_bf16(ref, stage_ref, n_tok, tok0=0):
    stage_ref[...] = ref[tok0 * SLAB:(tok0 + n_tok) * SLAB, :].astype(F32)
    return _slab_load(stage_ref, n_tok)


def _slab_store_bf16(ref, stage_ref, val):
    _slab_store(stage_ref, val)
    ref[...] = stage_ref[...].astype(BF16)


def _nt_dot(a, b):
    return lax.dot_general(a, b, (((1,), (1,)), ((), ())), preferred_element_type=F32)


def _layer_norm(x, g, b):
    mu = jnp.mean(x, axis=-1, keepdims=True)
    xc = x - mu
    var = jnp.mean(xc * xc, axis=-1, keepdims=True)
    return xc * lax.rsqrt(var + LN_EPS) * g + b


def _rms_norm(x, g):
    return x * lax.rsqrt(jnp.mean(x * x, axis=-1, keepdims=True) + RMS_EPS) * g


def _proj_kernel(xp_ref, xs_ref, w1_ref, qg_ref, kvg_ref, wq_ref, wqr_ref, wuk_ref, wuvt_ref,
                 cos_ref, sin_ref, perm4_ref, perm16_ref,
                 qa_ref, ka_ref, va_ref, q4_ref, k4_ref, v4_ref, q16_ref, k16_ref, v16_ref,
                 qb_ref, kb_ref, vbt_ref, *, n_prompt_tiles):
    i = pl.program_id(0)
    tm = xp_ref.shape[0]
    x = jnp.where(i < n_prompt_tiles, xp_ref[...], xs_ref[...]).astype(BF16)
    p = jnp.dot(x, w1_ref[...], preferred_element_type=F32)
    qa = (p[:, 0:A_WIDTH] * (A_HEAD_DIM ** -0.5 * LOG2E)).astype(BF16)
    ka = p[:, A_WIDTH:2 * A_WIDTH].astype(BF16)
    va = p[:, 2 * A_WIDTH:3 * A_WIDTH].astype(BF16)
    qa_ref[...] = qa
    ka_ref[...] = ka
    va_ref[...] = va
    qkv = jnp.concatenate([qa, ka, va], axis=1)
    for dil, perm_ref, outs in ((4, perm4_ref, (q4_ref, k4_ref, v4_ref)),
                                (16, perm16_ref, (q16_ref, k16_ref, v16_ref))):
        cm = jnp.dot(perm_ref[...], qkv, preferred_element_type=F32).astype(BF16)
        rows = tm // dil
        for r in range(dil):
            for j, o_ref in enumerate(outs):
                o_ref[:, r * A_WIDTH:(r + 1) * A_WIDTH] = cm[r * rows:(r + 1) * rows,
                                                             j * A_WIDTH:(j + 1) * A_WIDTH]
    c0 = 3 * A_WIDTH
    cq = p[:, c0:c0 + Q_LORA]
    ckv = p[:, c0 + Q_LORA:c0 + Q_LORA + KV_LORA]
    kr = p[:, c0 + Q_LORA + KV_LORA:c0 + Q_LORA + KV_LORA + LANES]
    krr = p[:, c0 + Q_LORA + KV_LORA + LANES:c0 + Q_LORA + KV_LORA + 2 * LANES]
    cos = cos_ref[...]
    sin = sin_ref[...]
    cos8 = jnp.concatenate([cos] * B_HEADS, axis=1)
    sin8 = jnp.concatenate([sin] * B_HEADS, axis=1)
    cqn = _rms_norm(cq, qg_ref[...]).astype(BF16)
    q = jnp.dot(cqn, wq_ref[...], preferred_element_type=F32)
    qr = jnp.dot(cqn, wqr_ref[...], preferred_element_type=F32)
    qscale = (B_NOPE + B_ROPE) ** -0.5 * LOG2E
    qb_ref[...] = ((q * cos8 + qr * sin8) * qscale).astype(BF16)
    ckvn = _rms_norm(ckv, kvg_ref[...]).astype(BF16)
    kn = jnp.dot(ckvn, wuk_ref[...], preferred_element_type=F32)
    krope = kr * cos + krr * sin
    kb_ref[...] = (kn + jnp.concatenate([krope] * B_HEADS, axis=1)).astype(BF16)
    vbt_ref[...] = _nt_dot(wuvt_ref[...], ckvn).astype(BF16)


def _class_perm(tm, dil):
    rows = tm // dil
    c = np.arange(tm)
    src = (c % rows) * dil + c // rows
    perm = np.zeros((tm, tm), np.float32)
    perm[c, src] = 1.0
    return jnp.asarray(perm, BF16)


def _proj(xp, xs, w1, qg, kvg, wq, wqr, wuk, wuvt, cos_t, sin_t, s_prompt, s_sample):
    tm = TM_PROJ
    n1 = xp.shape[0] // tm
    n2 = xs.shape[0] // tm
    t_all = xp.shape[0] + xs.shape[0]
    pt, st = s_prompt // tm, s_sample // tm
    assert tm == PERM_GROUP
    perm4, perm16 = _class_perm(tm, 4), _class_perm(tm, 16)

    def tab_idx(i):
        return (jnp.where(i < n1, i % pt, (i - n1) % st), 0)

    full = lambda shape: pl.BlockSpec(shape, lambda i: (0, 0))
    row = lambda cols: pl.BlockSpec((tm, cols), lambda i: (i, 0))
    cls = lambda dil: pl.BlockSpec((tm // dil, dil * A_WIDTH), lambda i: (i, 0))
    cls_shape = lambda dil: jax.ShapeDtypeStruct((t_all // dil, dil * A_WIDTH), BF16)
    return pl.pallas_call(
        functools.partial(_proj_kernel, n_prompt_tiles=n1),
        grid=(n1 + n2,),
        in_specs=[
            pl.BlockSpec((tm, D_MODEL), lambda i: (jnp.minimum(i, n1 - 1), 0)),
            pl.BlockSpec((tm, D_MODEL), lambda i: (jnp.maximum(i - n1, 0), 0)),
            full(w1.shape), full(qg.shape), full(kvg.shape), full(wq.shape), full(wqr.shape),
            full(wuk.shape), full(wuvt.shape),
            pl.BlockSpec((tm, LANES), tab_idx), pl.BlockSpec((tm, LANES), tab_idx),
            full(perm4.shape), full(perm16.shape),
        ],
        out_specs=[row(A_WIDTH)] * 3 + [cls(4)] * 3 + [cls(16)] * 3
        + [row(B_HEADS * HEAD_PAD), row(B_HEADS * HEAD_PAD),
           pl.BlockSpec((B_HEADS * B_V, tm), lambda i: (0, i))],
        out_shape=[jax.ShapeDtypeStruct((t_all, A_WIDTH), BF16)] * 3
        + [cls_shape(4)] * 3 + [cls_shape(16)] * 3
        + [jax.ShapeDtypeStruct((t_all, B_HEADS * HEAD_PAD), BF16)] * 2
        + [jax.ShapeDtypeStruct((B_HEADS * B_V, t_all), BF16)],
        compiler_params=pltpu.CompilerParams(dimension_semantics=("arbitrary",),
                                             vmem_limit_bytes=VMEM_LIMIT),
        name="proj",
    )(xp, xs, w1, qg, kvg, wq, wqr, wuk, wuvt, cos_t, sin_t, perm4, perm16)


def _attn_a_kernel(qblk_ref, pblk_ref, nblk_ref, tstart_ref, llen_ref,
                   q_ref, kp_ref, kc_ref, kn_ref, vp_ref, vc_ref, vn_ref, bias_ref, perm_ref,
                   o_ref, lse_ref, cm_ref, *, n_pos, n_cls):
    n = pl.program_id(0)
    tq, wk = TQ_A, TQ_A + 2 * A_HALF
    lane = lax.broadcasted_iota(I32, (tq, LANES), 1)
    low = lane < A_HEAD_DIM
    colbase = lax.broadcasted_iota(I32, (1, wk), 1) + (tstart_ref[n] - A_HALF)
    cls_len = llen_ref[n]

    def one_class(r):
        cols = slice(0, A_WIDTH) if n_cls == 1 else pl.ds(pl.multiple_of(r * A_WIDTH, A_WIDTH), A_WIDTH)
        kcat = jnp.concatenate([kp_ref[:, cols], kc_ref[:, cols], kn_ref[:, cols]], axis=0)
        vcat = jnp.concatenate([vp_ref[:, cols], vc_ref[:, cols], vn_ref[:, cols]], axis=0)
        for j in range(n_pos):
            rows = slice(j * tq, (j + 1) * tq)
            q = q_ref[rows, cols]
            kw, vw = kcat[j * tq:j * tq + wk, :], vcat[j * tq:j * tq + wk, :]
            col = colbase + j * tq
            colpen = jnp.where((col >= 0) & (col < cls_len), 0.0, NEG_BIG).astype(F32)
            pairs = []
            lse_c = jnp.zeros((tq, LANES), F32)
            for jp in range(A_HEADS // 2):
                sl = slice(jp * LANES, (jp + 1) * LANES)
                qp, kpair, vpair = q[:, sl], kw[:, sl], vw[:, sl]
                outs = []
                for e in range(2):
                    qm = jnp.where(low if e == 0 else ~low, qp, jnp.zeros_like(qp))
                    s = _nt_dot(qm, kpair) + bias_ref[2 * jp + e] + colpen
                    m = jnp.max(s, axis=1, keepdims=True)
                    p = jnp.exp2(s - m)
                    l = jnp.sum(p, axis=1, keepdims=True)
                    outs.append(jnp.dot(p.astype(BF16), vpair, preferred_element_type=F32) / l)
                    lse_c = lse_c + jnp.where(lane == 2 * jp + e, m + jnp.log2(l), 0.0)
                pairs.append(jnp.where(low, outs[0], outs[1]))
            o_full = jnp.concatenate(pairs, axis=1).astype(BF16)
            if n_cls == 1:
                o_ref[rows, :] = o_full
                lse_ref[rows, :] = lse_c
            else:
                hi = lse_c.astype(BF16)
                rest = lse_c - hi.astype(F32)
                mid = rest.astype(BF16)
                lo = (rest - mid.astype(F32)).astype(BF16)
                cm_ref[r, rows, :] = jnp.concatenate([o_full, hi, mid, lo], axis=1)

    if n_cls == 1:
        one_class(0)
        return

    def body(r, carry):
        one_class(r)
        return carry

    lax.fori_loop(0, n_cls, body, 0)
    per = PERM_GROUP // n_cls
    for a in range(n_pos * tq * n_cls // PERM_GROUP):
        stack = jnp.concatenate([cm_ref[r, a * per:(a + 1) * per, :] for r in range(n_cls)], axis=0)
        nat = jnp.dot(perm_ref[...], stack, preferred_element_type=F32)
        rows = slice(a * PERM_GROUP, (a + 1) * PERM_GROUP)
        o_ref[rows, :] = nat[:, :A_WIDTH].astype(BF16)
        lse_ref[rows, :] = (nat[:, A_WIDTH:A_WIDTH + LANES] + nat[:, A_WIDTH + LANES:A_WIDTH + 2 * LANES]
                            + nat[:, A_WIDTH + 2 * LANES:])


def _attn_a_tables(seqs, dil, n_pos):
    rows = n_pos * TQ_A
    per_halo = rows // A_HALF
    qblk, pblk, nblk, tstart, llen = [], [], [], [], []
    for off, s_len in seqs:
        cls = s_len // dil
        steps = cls // rows
        base = (off // dil) // rows
        for i in range(steps):
            qblk.append(base + i)
            pblk.append((base + i) * per_halo - (1 if i > 0 else 0))
            nblk.append((base + i + 1) * per_halo - (0 if i < steps - 1 else 1))
            tstart.append(i * rows)
            llen.append(cls)
    return [jnp.asarray(np.asarray(a, np.int32)) for a in (qblk, pblk, nblk, tstart, llen)]


def _attn_a_bias(dil):
    tq, wk = TQ_A, TQ_A + 2 * A_HALF
    delta = np.abs(np.arange(wk)[None, :] - A_HALF - np.arange(tq)[:, None]).astype(np.float64)
    slopes = 2.0 ** (-8.0 * (np.arange(A_HEADS) + 1.0) / A_HEADS)
    bias = -slopes[:, None, None] * (delta * dil)[None] * LOG2E
    bias = np.where((delta <= A_HALF)[None], bias, NEG_BIG)
    return jnp.asarray(bias.astype(np.float32))


def _attn_a(qc, kc, vc, seqs, dil):
    t_all = qc.shape[0] * dil
    n_pos = A_POS_TILES[dil]
    rows, width = n_pos * TQ_A, dil * A_WIDTH
    tabs = _attn_a_tables(seqs, dil, n_pos)
    n_steps = int(tabs[0].shape[0])
    bias = _attn_a_bias(dil)
    perm = _class_perm(PERM_GROUP, dil).T
    cur = pl.BlockSpec((rows, width), lambda n, qb, pb, nb, ts, ll: (qb[n], 0))
    prev = pl.BlockSpec((A_HALF, width), lambda n, qb, pb, nb, ts, ll: (pb[n], 0))
    nxt = pl.BlockSpec((A_HALF, width), lambda n, qb, pb, nb, ts, ll: (nb[n], 0))
    tok = lambda cols: pl.BlockSpec((rows * dil, cols), lambda n, *_: (n, 0))
    return pl.pallas_call(
        functools.partial(_attn_a_kernel, n_pos=n_pos, n_cls=dil),
        grid_spec=pltpu.PrefetchScalarGridSpec(
            num_scalar_prefetch=5, grid=(n_steps,),
            in_specs=[cur, prev, cur, nxt, prev, cur, nxt,
                      pl.BlockSpec(bias.shape, lambda n, *_: (0, 0, 0)),
                      pl.BlockSpec(perm.shape, lambda n, *_: (0, 0))],
            out_specs=[tok(A_WIDTH), tok(LANES)],
            scratch_shapes=[pltpu.VMEM((dil, rows, A_WIDTH + 3 * LANES), BF16)]),
        out_shape=[jax.ShapeDtypeStruct((t_all, A_WIDTH), BF16),
                   jax.ShapeDtypeStruct((t_all, LANES), F32)],
        compiler_params=pltpu.CompilerParams(dimension_semantics=("arbitrary",),
                                             vmem_limit_bytes=VMEM_LIMIT),
        name=f"attn_a_d{dil}",
    )(*tabs, qc, kc, kc, kc, vc, vc, vc, bias, perm)


def _attn_b_kernel(q_ref, k_ref, vt_ref, o_ref, sta_ref, stb_ref, ma_ref, mb_ref):
    s = pl.program_id(0)
    tq, s_len = q_ref.shape[0], k_ref.shape[0]

    @pl.when(s == 0)
    def _():
        stb_ref[...] = jnp.zeros_like(stb_ref)
        mb_ref[...] = jnp.zeros_like(mb_ref)

    def step(st_new_ref, m_new_ref, st_old_ref, m_old_ref):
        m_old = m_old_ref[...]
        q = q_ref[...]
        m_new = jnp.full((1, tq), -jnp.inf, F32)
        l = jnp.zeros((1, tq), F32)
        acc = jnp.zeros((B_V, tq), F32)
        for c in range(s_len // KEY_CHUNK_B):
            ks = slice(c * KEY_CHUNK_B, (c + 1) * KEY_CHUNK_B)
            st_c = _nt_dot(k_ref[ks, :], q)
            st_new_ref[ks, :] = st_c
            m_new = jnp.maximum(m_new, jnp.max(st_c, axis=0, keepdims=True))
            p = jnp.exp2(st_old_ref[ks, :] - m_old)
            l = l + jnp.sum(p, axis=0, keepdims=True)
            acc = acc + jnp.dot(vt_ref[:, ks], p.astype(BF16), preferred_element_type=F32)
        m_new_ref[...] = m_new
        o_ref[...] = (acc / l).astype(BF16)

    @pl.when(s % 2 == 0)
    def _():
        step(sta_ref, ma_ref, stb_ref, mb_ref)

    @pl.when(s % 2 == 1)
    def _():
        step(stb_ref, mb_ref, sta_ref, ma_ref)


def _attn_b(qb, kb, vbt, off, n_batch, s_len):
    tq = min(s_len, LOGITS_ELEMS_B // s_len)
    nq = s_len // tq
    qbase, kbase = off // tq, off // s_len
    n_items = n_batch * B_HEADS * nq

    def split(item):
        return item // (B_HEADS * nq), (item // nq) % B_HEADS, item % nq

    def q_map(s):
        b, h, qi = split(jnp.minimum(s, n_items - 1))
        return (qbase + b * nq + qi, h)

    def k_map(s):
        b, h, _ = split(jnp.minimum(s, n_items - 1))
        return (kbase + b, h)

    def v_map(s):
        b, h, _ = split(jnp.maximum(s - 1, 0))
        return (h, kbase + b)

    def o_map(s):
        b, h, qi = split(jnp.maximum(s - 1, 0))
        return (h, b * nq + qi)

    return pl.pallas_call(
        _attn_b_kernel,
        grid=(n_items + 1,),
        in_specs=[pl.BlockSpec((tq, HEAD_PAD), q_map),
                  pl.BlockSpec((s_len, HEAD_PAD), k_map),
                  pl.BlockSpec((B_V, s_len), v_map)],
        out_specs=pl.BlockSpec((B_V, tq), o_map),
        out_shape=jax.ShapeDtypeStruct((B_HEADS * B_V, n_batch * s_len), BF16),
        scratch_shapes=[pltpu.VMEM((s_len, tq), F32), pltpu.VMEM((s_len, tq), F32),
                        pltpu.VMEM((1, tq), F32), pltpu.VMEM((1, tq), F32)],
        compiler_params=pltpu.CompilerParams(dimension_semantics=("arbitrary",),
                                             vmem_limit_bytes=VMEM_LIMIT),
        name=f"attn_b_s{s_len}",
    )(qb, kb, vbt)


def _mix_kernel(xp_ref, xs_ref, o0_ref, o1_ref, o2_ref, l0_ref, l1_ref, l2_ref, spread_ref,
                obp_ref, obs_ref, wout_ref,
                g_ref, b_ref, wrh_ref, wrl_ref, rb_ref,
                h_ref, hp_ref, tope_ref, pos_ref, gate_ref, cnt_out_ref, cnt_ref, stage_ref,
                *, n_prompt_tiles):
    i = pl.program_id(0)
    tm = xp_ref.shape[0]

    @pl.when(i == 0)
    def _():
        cnt_ref[...] = jnp.zeros_like(cnt_ref)

    l0, l1, l2 = l0_ref[...], l1_ref[...], l2_ref[...]
    lmax = jnp.maximum(jnp.maximum(l0, l1), l2)
    e0, e1, e2 = jnp.exp2(l0 - lmax), jnp.exp2(l1 - lmax), jnp.exp2(l2 - lmax)
    inv = 1.0 / (e0 + e1 + e2)
    spread = spread_ref[...]

    def per_lane(w):
        hi = w.astype(BF16)
        lo = (w - hi.astype(F32)).astype(BF16)
        return (jnp.dot(hi, spread, preferred_element_type=F32)
                + jnp.dot(lo, spread, preferred_element_type=F32))

    oa = (per_lane(e0 * inv) * o0_ref[...].astype(F32) + per_lane(e1 * inv) * o1_ref[...].astype(F32)
          + per_lane(e2 * inv) * o2_ref[...].astype(F32))
    is_prompt = i < n_prompt_tiles
    obt = jnp.where(is_prompt, obp_ref[...], obs_ref[...])
    mix = (jnp.dot(oa.astype(BF16), wout_ref[0:A_WIDTH, :], preferred_element_type=F32)
           + lax.dot_general(obt, wout_ref[A_WIDTH:, :], (((0,), (0,)), ((), ())),
                             preferred_element_type=F32))
    x = jnp.where(is_prompt, xp_ref[...], xs_ref[...])
    h = _layer_norm(ALPHA * x + mix, g_ref[...], b_ref[...])
    h_ref[...] = h
    _slab_store_bf16(hp_ref, stage_ref, h)

    h_hi = h.astype(BF16)
    h_lo = (h - h_hi.astype(F32)).astype(BF16)
    wrh = wrh_ref[...]
    logits = _nt_dot(wrh, h_hi) + _nt_dot(wrh, h_lo) + _nt_dot(wrl_ref[...], h_hi)
    scores = jax.nn.sigmoid(logits)
    sel = scores + rb_ref[...]

    sub = lax.broadcasted_iota(I32, (GROUP_SIZE, tm), 0).astype(F32)
    gscore = []
    for g in range(N_GROUPS):
        sg = sel[g * GROUP_SIZE:(g + 1) * GROUP_SIZE, :]
        m1 = jnp.max(sg, axis=0, keepdims=True)
        first = jnp.min(jnp.where(sg == m1, sub, float(GROUP_SIZE)), axis=0, keepdims=True)
        m2 = jnp.max(jnp.where(sub == first, -jnp.inf, sg), axis=0, keepdims=True)
        gscore.append(m1 + m2)
    cands = []
    for g in range(N_GROUPS):
        beaten = jnp.zeros((1, tm), F32)
        for g2 in range(N_GROUPS):
            if g2 == g:
                continue
            wins = (gscore[g2] > gscore[g]) | ((gscore[g2] == gscore[g]) & (g2 < g))
            beaten = beaten + wins.astype(F32)
        keep = beaten < float(TOPK_GROUPS)
        sg = sel[g * GROUP_SIZE:(g + 1) * GROUP_SIZE, :]
        cands.append(jnp.where(keep, sg, NEG_BIG))
    cand = jnp.concatenate(cands, axis=0)

    eidx = lax.broadcasted_iota(I32, (N_EXPERTS, tm), 0).astype(F32)
    picked_idx, picked_gate = [], []
    onehot = jnp.zeros((N_EXPERTS, tm), F32)
    for _ in range(TOP_K):
        mx = jnp.max(cand, axis=0, keepdims=True)
        fi = jnp.min(jnp.where(cand == mx, eidx, float(N_EXPERTS)), axis=0, keepdims=True)
        pick = eidx == fi
        picked_idx.append(fi)
        picked_gate.append(jnp.sum(jnp.where(pick, scores, 0.0), axis=0, keepdims=True))
        onehot = onehot + pick.astype(F32)
        cand = jnp.where(pick, -jnp.inf, cand)
    gsum = picked_gate[0]
    for k in range(1, TOP_K):
        gsum = gsum + picked_gate[k]

    tri = (lax.broadcasted_iota(I32, (tm, tm), 0) < lax.broadcasted_iota(I32, (tm, tm), 1))
    before = jnp.dot(onehot.astype(BF16), tri.astype(BF16), preferred_element_type=F32)
    rank = before + cnt_ref[:, 0:1]
    for k in range(TOP_K):
        pick = eidx == picked_idx[k]
        tope_ref[k:k + 1, :] = picked_idx[k].astype(I32)
        pos_ref[k:k + 1, :] = jnp.sum(jnp.where(pick, rank, 0.0), axis=0, keepdims=True).astype(I32)
        gate_ref[k:k + 1, :] = picked_gate[k] / gsum * ROUTED_SCALE
    cnt_ref[...] = cnt_ref[...] + jnp.sum(onehot, axis=1, keepdims=True)
    cnt_out_ref[...] = cnt_ref[...]


def _mix(xp, xs, outs, lses, obp, obs, wout, g, b, wrh, wrl, rb):
    tm = TM_MIX
    n1 = xp.shape[0] // tm
    n2 = xs.shape[0] // tm
    t_all = xp.shape[0] + xs.shape[0]
    full = lambda a: pl.BlockSpec(a.shape, lambda i: (0,) * a.ndim)
    row = lambda cols: pl.BlockSpec((tm, cols), lambda i: (i, 0))
    prow = lambda cols: pl.BlockSpec((tm, cols), lambda i: (jnp.minimum(i, n1 - 1), 0))
    srow = lambda cols: pl.BlockSpec((tm, cols), lambda i: (jnp.maximum(i - n1, 0), 0))
    col = pl.BlockSpec((TOP_K, tm), lambda i: (0, i))
    head_of_lane = np.arange(A_WIDTH) // A_HEAD_DIM
    spread = jnp.asarray(np.arange(LANES)[:, None] == head_of_lane[None, :], BF16)
    return pl.pallas_call(
        functools.partial(_mix_kernel, n_prompt_tiles=n1),
        grid=(n1 + n2,),
        in_specs=[
            prow(D_MODEL), srow(D_MODEL),
            row(A_WIDTH), row(A_WIDTH), row(A_WIDTH), row(LANES), row(LANES), row(LANES), full(spread),
            pl.BlockSpec((B_HEADS * B_V, tm), lambda i: (0, jnp.minimum(i, n1 - 1))),
            pl.BlockSpec((B_HEADS * B_V, tm), lambda i: (0, jnp.maximum(i - n1, 0))),
            full(wout), full(g), full(b), full(wrh), full(wrl), full(rb),
        ],
        out_specs=[row(D_MODEL), pl.BlockSpec((SLAB * tm, LANES), lambda i: (i, 0)), col, col, col,
                   pl.BlockSpec((N_EXPERTS, LANES), lambda i: (0, 0))],
        out_shape=[jax.ShapeDtypeStruct((t_all, D_MODEL), F32),
                   jax.ShapeDtypeStruct((SLAB * t_all, LANES), BF16),
                   jax.ShapeDtypeStruct((TOP_K, t_all), I32),
                   jax.ShapeDtypeStruct((TOP_K, t_all), I32),
                   jax.ShapeDtypeStruct((TOP_K, t_all), F32),
                   jax.ShapeDtypeStruct((N_EXPERTS, LANES), F32)],
        scratch_shapes=[pltpu.VMEM((N_EXPERTS, LANES), F32), pltpu.VMEM((SLAB * tm, LANES), F32)],
        compiler_params=pltpu.CompilerParams(dimension_semantics=("arbitrary",),
                                             vmem_limit_bytes=VMEM_LIMIT),
        name="mix_router",
    )(xp, xs, *outs, *lses, spread, obp, obs, wout, g, b, wrh, wrl, rb)


def _dispatch_kernel(zstart_ref, nused_ref, h_ref, tope_ref, pos_ref, pstart_ref, buf_ref, dest_ref,
                     dsm_ref, zero_ref, sem_ref, zsem_ref):
    i = pl.program_id(0)
    tm = tope_ref.shape[1]
    tile_rows = SLAB * ROW_BLOCK

    @pl.when(i == 0)
    def _():
        zero_ref[...] = jnp.zeros_like(zero_ref)

        def zfill(e, carry):
            start = pl.multiple_of(zstart_ref[e], SLAB)
            pltpu.make_async_copy(zero_ref, buf_ref.at[pl.ds(start, tile_rows), :], zsem_ref).start()
            return carry

        lax.fori_loop(0, N_EXPERTS, zfill, 0)
        span = buf_ref.at[pl.ds(0, N_EXPERTS * tile_rows), :]
        pltpu.make_async_copy(span, span, zsem_ref).wait()

        def ztail(j, carry):
            start = pl.multiple_of(j * tile_rows, tile_rows)
            pltpu.make_async_copy(zero_ref, buf_ref.at[pl.ds(start, tile_rows), :], zsem_ref).start()
            return carry

        def zwait(j, carry):
            pltpu.make_async_copy(zero_ref, buf_ref.at[pl.ds(0, tile_rows), :], zsem_ref).wait()
            return carry

        n_tiles = buf_ref.shape[0] // tile_rows
        lax.fori_loop(nused_ref[0], n_tiles, ztail, 0)
        lax.fori_loop(nused_ref[0], n_tiles, zwait, 0)

    eidx = lax.broadcasted_iota(I32, (N_EXPERTS, tm), 0)
    pstart = pstart_ref[...]
    for k in range(TOP_K):
        hit = eidx == tope_ref[k:k + 1, :]
        base = jnp.sum(jnp.where(hit, pstart, 0.0), axis=0, keepdims=True)
        dest_ref[k:k + 1, :] = (pos_ref[k:k + 1, :] + base.astype(I32)) * SLAB
    pltpu.sync_copy(dest_ref, dsm_ref)

    def scatter(t, carry):
        src = h_ref.at[pl.ds(pl.multiple_of(t * SLAB, SLAB), SLAB), :]
        for k in range(TOP_K):
            dst = buf_ref.at[pl.ds(pl.multiple_of(dsm_ref[k, t], SLAB), SLAB), :]
            pltpu.make_async_copy(src, dst, sem_ref).start(priority=k % 2)
        return carry

    lax.fori_loop(0, tm, scatter, 0, unroll=8)
    span = buf_ref.at[pl.ds(0, TOP_K * tm * SLAB), :]
    pltpu.make_async_copy(span, span, sem_ref).wait()


def _dispatch(h, tope, pos, pstart, zstart, nused, n_rows):
    tm = TM_DISP
    t_all = tope.shape[1]
    col = pl.BlockSpec((TOP_K, tm), lambda i, z, nu: (0, i))
    return pl.pallas_call(
        _dispatch_kernel,
        grid_spec=pltpu.PrefetchScalarGridSpec(
            num_scalar_prefetch=2, grid=(t_all // tm,),
            in_specs=[pl.BlockSpec((SLAB * tm, LANES), lambda i, z, nu: (i, 0)), col, col,
                      pl.BlockSpec((N_EXPERTS, 1), lambda i, z, nu: (0, 0))],
            out_specs=[pl.BlockSpec(memory_space=pl.ANY), col],
            scratch_shapes=[pltpu.SMEM((TOP_K, tm), I32),
                            pltpu.VMEM((SLAB * ROW_BLOCK, LANES), BF16),
                            pltpu.SemaphoreType.DMA, pltpu.SemaphoreType.DMA]),
        out_shape=[jax.ShapeDtypeStruct((SLAB * n_rows, LANES), BF16),
                   jax.ShapeDtypeStruct((TOP_K, t_all), I32)],
        compiler_params=pltpu.CompilerParams(dimension_semantics=("arbitrary",),
                                             vmem_limit_bytes=VMEM_LIMIT),
        name="dispatch",
    )(zstart, nused, h, tope, pos, pstart)


def _expert_kernel(tfirst_ref, ntile_ref, nused_ref, x_hbm, wg_ref, wu_ref, wd_ref, o_hbm,
                   xbuf, obuf, stage_ref, wgb_ref, wub_ref, wdb_ref, xsem, osem):
    e = pl.program_id(0)
    tile_rows = SLAB * ROW_BLOCK
    nused = nused_ref[0]

    def x_copy(g, slot):
        start = pl.multiple_of(g * tile_rows, tile_rows)
        return pltpu.make_async_copy(x_hbm.at[pl.ds(start, tile_rows), :], xbuf.at[slot], xsem.at[slot])

    def o_copy(g, slot):
        start = pl.multiple_of(g * tile_rows, tile_rows)
        return pltpu.make_async_copy(obuf.at[slot], o_hbm.at[pl.ds(start, tile_rows), :], osem.at[slot])

    @pl.when(e == 0)
    def _():
        for j in range(N_XBUF - 1):
            @pl.when(j < nused)
            def _():
                x_copy(j, j).start()

    n_e = ntile_ref[e]

    @pl.when(n_e > 0)
    def _():
        wgb_ref[...] = wg_ref[...].astype(BF16)
        wub_ref[...] = wu_ref[...].astype(BF16)
        wdb_ref[...] = wd_ref[...].astype(BF16)

    def tile(j, carry):
        g = tfirst_ref[e] + j
        ahead = g + (N_XBUF - 1)

        @pl.when(ahead < nused)
        def _():
            x_copy(ahead, ahead % N_XBUF).start()

        slot = g % N_XBUF
        x_copy(g, slot).wait()
        x = _slab_load_bf16(xbuf.at[slot], stage_ref, ROW_BLOCK).astype(BF16)
        gt = jnp.dot(x, wgb_ref[...], preferred_element_type=F32)
        up = jnp.dot(x, wub_ref[...], preferred_element_type=F32)
        hmid = (gt * jax.nn.sigmoid(gt) * up).astype(BF16)
        out = jnp.dot(hmid, wdb_ref[...], preferred_element_type=F32)
        oslot = g % N_OBUF

        @pl.when(g >= N_OBUF)
        def _():
            o_copy(g - N_OBUF, oslot).wait()

        _slab_store(obuf.at[oslot], out)
        o_copy(g, oslot).start()
        return carry

    lax.fori_loop(0, n_e, tile, 0)

    @pl.when(e == N_EXPERTS - 1)
    def _():
        for j in range(N_OBUF):
            @pl.when(nused > j)
            def _():
                o_copy(nused - 1 - j, (nused - 1 - j) % N_OBUF).wait()

        stage_ref[...] = jnp.zeros_like(stage_ref)
        n_tiles = o_hbm.shape[0] // tile_rows

        def tail_copy(g):
            start = pl.multiple_of(g * tile_rows, tile_rows)
            return pltpu.make_async_copy(stage_ref, o_hbm.at[pl.ds(start, tile_rows), :], osem.at[0])

        def ztail(g, carry):
            tail_copy(g).start()
            return carry

        def zwait(g, carry):
            tail_copy(g).wait()
            return carry

        lax.fori_loop(nused, n_tiles, ztail, 0)
        lax.fori_loop(nused, n_tiles, zwait, 0)


def _experts(buf, tfirst, ntile, nused, w_gate, w_up, w_down):
    tile_rows = SLAB * ROW_BLOCK
    wmap = lambda e, tf, nt, nu: (e, 0, 0)
    return pl.pallas_call(
        _expert_kernel,
        grid_spec=pltpu.PrefetchScalarGridSpec(
            num_scalar_prefetch=3, grid=(N_EXPERTS,),
            in_specs=[pl.BlockSpec(memory_space=pl.ANY),
                      pl.BlockSpec((None, D_MODEL, EXPERT_HIDDEN), wmap),
                      pl.BlockSpec((None, D_MODEL, EXPERT_HIDDEN), wmap),
                      pl.BlockSpec((None, EXPERT_HIDDEN, D_MODEL), wmap)],
            out_specs=pl.BlockSpec(memory_space=pl.ANY),
            scratch_shapes=[pltpu.VMEM((N_XBUF, tile_rows, LANES), BF16),
                            pltpu.VMEM((N_OBUF, tile_rows, LANES), F32),
                            pltpu.VMEM((tile_rows, LANES), F32),
                            pltpu.VMEM((D_MODEL, EXPERT_HIDDEN), BF16),
                            pltpu.VMEM((D_MODEL, EXPERT_HIDDEN), BF16),
                            pltpu.VMEM((EXPERT_HIDDEN, D_MODEL), BF16),
                            pltpu.SemaphoreType.DMA((N_XBUF,)),
                            pltpu.SemaphoreType.DMA((N_OBUF,))]),
        out_shape=jax.ShapeDtypeStruct(buf.shape, F32),
        compiler_params=pltpu.CompilerParams(dimension_semantics=("arbitrary",),
                                             vmem_limit_bytes=VMEM_LIMIT),
        name="experts",
    )(tfirst, ntile, nused, buf, w_gate, w_up, w_down)


def _combine_kernel(h_ref, gate_ref, dest_ref, dest_next_ref, eo_ref, wsg_ref, wsu_ref, wsd_ref, g_ref, b_ref,
                    yp_ref, ys_ref, dsm_ref, rows_ref, sem_ref, *, n_prompt_tiles, n_tiles):
    i = pl.program_id(0)
    tm = gate_ref.shape[0]
    slot = i % 2

    def start_gather(table_ref, to_slot):
        pltpu.sync_copy(table_ref, dsm_ref)

        def gather(t, carry):
            for k in range(TOP_K):
                src = eo_ref.at[pl.ds(pl.multiple_of(dsm_ref[k, t], SLAB), SLAB), :]
                dst = rows_ref.at[to_slot, pl.ds(pl.multiple_of((k * tm + t) * SLAB, SLAB), SLAB), :]
                pltpu.make_async_copy(src, dst, sem_ref.at[to_slot]).start(priority=k % 2)
            return carry

        lax.fori_loop(0, tm, gather, 0, unroll=8)

    @pl.when(i == 0)
    def _():
        start_gather(dest_ref, 0)

    for nxt in range(2):
        @pl.when((i + 1 < n_tiles) & (slot == 1 - nxt))
        def _():
            start_gather(dest_next_ref, nxt)

    h = h_ref[...]
    hb = h.astype(BF16)
    sg = jnp.dot(hb, wsg_ref[...], preferred_element_type=F32)
    su = jnp.dot(hb, wsu_ref[...], preferred_element_type=F32)
    shared = jnp.dot((sg * jax.nn.sigmoid(sg) * su).astype(BF16), wsd_ref[...],
                     preferred_element_type=F32)

    rows = rows_ref.at[slot]
    pltpu.make_async_copy(eo_ref.at[pl.ds(0, TOP_K * tm * SLAB), :], rows, sem_ref.at[slot]).wait()
    gate = gate_ref[...]
    routed = gate[:, 0:1] * _slab_load(rows, tm)
    for k in range(1, TOP_K):
        routed = routed + gate[:, k:k + 1] * _slab_load(rows, tm, k * tm)
    y = _layer_norm(ALPHA * h + (routed + shared), g_ref[...], b_ref[...])

    @pl.when(i < n_prompt_tiles)
    def _():
        yp_ref[...] = y

    @pl.when(i >= n_prompt_tiles)
    def _():
        ys_ref[...] = y


def _combine(h, gate_t, dest, eo, wsg, wsu, wsd, g, b, t_prompt):
    tm = TM_COMB
    t_all = gate_t.shape[0]
    n1 = t_prompt // tm
    n2 = (t_all - t_prompt) // tm
    full = lambda a: pl.BlockSpec(a.shape, lambda i: (0,) * a.ndim)
    return pl.pallas_call(
        functools.partial(_combine_kernel, n_prompt_tiles=n1, n_tiles=n1 + n2),
        grid=(n1 + n2,),
        in_specs=[pl.BlockSpec((tm, D_MODEL), lambda i: (i, 0)),
                  pl.BlockSpec((tm, TOP_K), lambda i: (i, 0)),
                  pl.BlockSpec((TOP_K, tm), lambda i: (0, i)),
                  pl.BlockSpec((TOP_K, tm), lambda i: (0, jnp.minimum(i + 1, n1 + n2 - 1))),
                  pl.BlockSpec(memory_space=pl.ANY),
                  full(wsg), full(wsu), full(wsd), full(g), full(b)],
        out_specs=[pl.BlockSpec((tm, D_MODEL), lambda i: (jnp.minimum(i, n1 - 1), 0)),
                   pl.BlockSpec((tm, D_MODEL), lambda i: (jnp.maximum(i - n1, 0), 0))],
        out_shape=[jax.ShapeDtypeStruct((t_prompt, D_MODEL), F32),
                   jax.ShapeDtypeStruct((t_all - t_prompt, D_MODEL), F32)],
        scratch_shapes=[pltpu.SMEM((TOP_K, tm), I32),
                        pltpu.VMEM((2, TOP_K * tm * SLAB, LANES), F32),
                        pltpu.SemaphoreType.DMA((2,))],
        compiler_params=pltpu.CompilerParams(dimension_semantics=("arbitrary",),
                                             vmem_limit_bytes=VMEM_LIMIT),
        name="combine",
    )(h, gate_t, dest, dest, eo, wsg, wsu, wsd, g, b)


def _rope_tables(s_max):
    inv_freq = ROPE_BASE ** (-jnp.arange(0, B_ROPE, 2, dtype=F32) / B_ROPE)
    ang = jnp.arange(s_max, dtype=F32)[:, None] * inv_freq[None, :]
    cos, sin = jnp.cos(ang), jnp.sin(ang)
    ones = jnp.ones((s_max, B_NOPE), F32)
    zeros_n = jnp.zeros((s_max, B_NOPE), F32)
    zeros_p = jnp.zeros((s_max, HEAD_PAD - B_NOPE - B_ROPE), F32)
    return (jnp.concatenate([ones, cos, cos, zeros_p], axis=1),
            jnp.concatenate([zeros_n, sin, sin, zeros_p], axis=1))


def _rot_cols(w):
    half = B_ROPE // 2
    return jnp.concatenate([-w[..., half:], w[..., :half]], axis=-1)


def _layout_weights(w_in, w_uq, w_uk):
    c_kr = 3 * A_WIDTH + Q_LORA + KV_LORA
    w_kr = w_in[:, c_kr:c_kr + B_ROPE]
    pad_l = jnp.zeros((D_MODEL, B_NOPE), F32)
    pad_r = jnp.zeros((D_MODEL, HEAD_PAD - B_NOPE - B_ROPE), F32)
    w1 = jnp.concatenate([w_in[:, :c_kr], pad_l, w_kr, pad_r, pad_l, _rot_cols(w_kr), pad_r], axis=1)
    wq3 = w_uq.reshape(Q_LORA, B_HEADS, B_NOPE + B_ROPE)
    nope, rope = wq3[..., :B_NOPE], wq3[..., B_NOPE:]
    zpad = jnp.zeros((Q_LORA, B_HEADS, HEAD_PAD - B_NOPE - B_ROPE), F32)
    wq = jnp.concatenate([nope, rope, zpad], axis=-1).reshape(Q_LORA, B_HEADS * HEAD_PAD)
    wqr = jnp.concatenate([jnp.zeros_like(nope), _rot_cols(rope), zpad], axis=-1)
    wqr = wqr.reshape(Q_LORA, B_HEADS * HEAD_PAD)
    wk3 = w_uk.reshape(KV_LORA, B_HEADS, B_NOPE)
    wuk = jnp.concatenate([wk3, jnp.zeros((KV_LORA, B_HEADS, HEAD_PAD - B_NOPE), F32)], axis=-1)
    wuk = wuk.reshape(KV_LORA, B_HEADS * HEAD_PAD)
    return w1.astype(BF16), wq.astype(BF16), wqr.astype(BF16), wuk.astype(BF16)


def _forward(x_prompt, x_sample, w_in, w_out, ln1_g, ln1_b, q_norm_g, w_uq, kv_norm_g, w_uk, w_uv,
             w_router, router_bias, w_gate, w_up, w_down, ws_gate, ws_up, ws_down, ln2_g, ln2_b):
    b1, s1, _ = x_prompt.shape
    b2, s2, _ = x_sample.shape
    t1, t2 = b1 * s1, b2 * s2
    t_all = t1 + t2
    xp = x_prompt.reshape(t1, D_MODEL)
    xs = x_sample.reshape(t2, D_MODEL)
    seqs = [(b * s1, s1) for b in range(b1)] + [(t1 + b * s2, s2) for b in range(b2)]
    assert t1 % s2 == 0 and s1 % (TQ_A * 16) == 0 and s2 % (TQ_A * 16) == 0

    w1, wq, wqr, wuk = _layout_weights(w_in, w_uq, w_uk)
    cos_t, sin_t = _rope_tables(max(s1, s2))
    (qa, ka, va, q4, k4, v4, q16, k16, v16, qb, kb, vbt) = _proj(
        xp, xs, w1, q_norm_g.reshape(1, -1), kv_norm_g.reshape(1, -1), wq, wqr, wuk,
        w_uv.T.astype(BF16), cos_t, sin_t, s1, s2)

    outs, lses = [], []
    for (_, dil), qkv in zip(A_PATTERNS, ((qa, ka, va), (q4, k4, v4), (q16, k16, v16))):
        o, lse = _attn_a(*qkv, seqs, dil)
        outs.append(o)
        lses.append(lse)
    obp = _attn_b(qb, kb, vbt, 0, b1, s1)
    obs = _attn_b(qb, kb, vbt, t1, b2, s2)

    wr_t = w_router.T
    wr_hi = wr_t.astype(BF16)
    wr_lo = (wr_t - wr_hi.astype(F32)).astype(BF16)
    h, h_slab, tope, pos, gate, counts = _mix(xp, xs, outs, lses, obp, obs, w_out.astype(BF16),
                                                ln1_g.reshape(1, -1), ln1_b.reshape(1, -1),
                                                wr_hi, wr_lo, router_bias.reshape(-1, 1))

    cnt = counts[:, 0].astype(I32)
    padded = ((cnt + ROW_BLOCK - 1) // ROW_BLOCK) * ROW_BLOCK
    pend = jnp.cumsum(padded)
    pstart = pend - padded
    n_tiles = (t_all * TOP_K) // ROW_BLOCK + N_EXPERTS + 1
    nused = (pend[-1:] // ROW_BLOCK).astype(I32)

    buf, dest = _dispatch(h_slab, tope, pos, pstart.astype(F32).reshape(-1, 1),
                          ((pstart + cnt) * SLAB).astype(I32), nused, n_tiles * ROW_BLOCK)
    eo = _experts(buf, (pstart // ROW_BLOCK).astype(I32), (padded // ROW_BLOCK).astype(I32), nused,
                  w_gate, w_up, w_down)
    yp, ys = _combine(h, gate.T, dest, eo, ws_gate.astype(BF16), ws_up.astype(BF16),
                      ws_down.astype(BF16), ln2_g.reshape(1, -1), ln2_b.reshape(1, -1), t1)
    return yp.reshape(b1, s1, D_MODEL), ys.reshape(b2, s2, D_MODEL)


def kernel(x_prompt, x_sample, w_in, w_out, ln1_g, ln1_b, q_norm_g, w_uq, kv_norm_g, w_uk, w_uv,
           w_router, router_bias, w_gate, w_up, w_down, ws_gate, ws_up, ws_down, ln2_g, ln2_b):
    params = (w_in, w_out, ln1_g, ln1_b, q_norm_g, w_uq, kv_norm_g, w_uk, w_uv, w_router, router_bias,
              w_gate, w_up, w_down, ws_gate, ws_up, ws_down, ln2_g, ln2_b)
    assert all(p.shape[0] == 1 for p in params), "one encoder layer"
    return _forward(x_prompt, x_sample, *[p.reshape(p.shape[1:]) for p in params])
```

```python
import functools
import math

import numpy as np
import jax
import jax.numpy as jnp
from jax import lax
from jax.experimental import pallas as pl
from jax.experimental.pallas import tpu as pltpu

F32 = jnp.float32
BF16 = jnp.bfloat16
I32 = jnp.int32

D_MODEL = 1024
A_HEADS = 8
A_HEAD_DIM = 64
A_WIDTH = A_HEADS * A_HEAD_DIM
A_PATTERNS = ((128, 1), (512, 4), (2048, 16))
A_HALF = 64
B_HEADS = 8
B_NOPE = 64
B_ROPE = 32
B_V = 64
Q_LORA = 256
KV_LORA = 128
ROPE_BASE = 10000.0
N_EXPERTS = 256
TOP_K = 8
N_GROUPS = 8
GROUP_SIZE = N_EXPERTS // N_GROUPS
TOPK_GROUPS = 4
EXPERT_HIDDEN = 256
ROUTED_SCALE = 2.5
LN_EPS = 1e-5
RMS_EPS = 1e-6
NEG_BIG = -1e30
ALPHA = 2.0 ** 0.25
LOG2E = 1.4426950408889634

LANES = 128
HEAD_PAD = 128
SLAB = D_MODEL // LANES
W1_COLS = 3 * A_WIDTH + Q_LORA + KV_LORA + 2 * LANES

TM_PROJ = 256
TQ_A = 128
A_POS_TILES = {1: 4, 4: 2, 16: 1}
PERM_GROUP = 256
LOGITS_ELEMS_B = 4096 * 512
KEY_CHUNK_B = 512
TM_MIX = 256
TM_DISP = 256
ROW_BLOCK = 256
TM_COMB = 128
N_XBUF = 8
N_OBUF = 4
VMEM_LIMIT = 48 * 1024 * 1024


def _slab_load(ref, n_tok, tok0=0):
    return jnp.concatenate([ref[pl.ds(tok0 * SLAB + c, n_tok, stride=SLAB), :] for c in range(SLAB)],
                           axis=1)


def _slab_store(ref, val):
    for c in range(SLAB):
        ref[pl.ds(c, val.shape[0], stride=SLAB), :] = val[:, c * LANES:(c + 1) * LANES]


def _slab_load_bf16(ref, stage_ref, n_tok, tok0=0):
    stage_ref[...] = ref[tok0 * SLAB:(tok0 + n_tok) * SLAB, :].astype(F32)
    return _slab_load(stage_ref, n_tok)


def _slab_store_bf16(ref, stage_ref, val):
    _slab_store(stage_ref, val)
    ref[...] = stage_ref[...].astype(BF16)


def _nt_dot(a, b):
    return lax.dot_general(a, b, (((1,), (1,)), ((), ())), preferred_element_type=F32)


def _layer_norm(x, g, b):
    mu = jnp.mean(x, axis=-1, keepdims=True)
    xc = x - mu
    var = jnp.mean(xc * xc, axis=-1, keepdims=True)
    return xc * lax.rsqrt(var + LN_EPS) * g + b


def _rms_norm(x, g):
    return x * lax.rsqrt(jnp.mean(x * x, axis=-1, keepdims=True) + RMS_EPS) * g


def _proj_kernel(xp_ref, xs_ref, w1_ref, qg_ref, kvg_ref, wq_ref, wqr_ref, wuk_ref, wuvt_ref,
                 cos_ref, sin_ref, perm4_ref, perm16_ref,
                 qa_ref, ka_ref, va_ref, q4_ref, k4_ref, v4_ref, q16_ref, k16_ref, v16_ref,
                 qb_ref, kb_ref, vbt_ref, *, n_prompt_tiles):
    i = pl.program_id(0)
    tm = xp_ref.shape[0]
    x = jnp.where(i < n_prompt_tiles, xp_ref[...], xs_ref[...]).astype(BF16)
    p = jnp.dot(x, w1_ref[...], preferred_element_type=F32)
    qa = (p[:, 0:A_WIDTH] * (A_HEAD_DIM ** -0.5 * LOG2E)).astype(BF16)
    ka = p[:, A_WIDTH:2 * A_WIDTH].astype(BF16)
    va = p[:, 2 * A_WIDTH:3 * A_WIDTH].astype(BF16)
    qa_ref[...] = qa
    ka_ref[...] = ka
    va_ref[...] = va
    qkv = jnp.concatenate([qa, ka, va], axis=1)
    for dil, perm_ref, outs in ((4, perm4_ref, (q4_ref, k4_ref, v4_ref)),
                                (16, perm16_ref, (q16_ref, k16_ref, v16_ref))):
        cm = jnp.dot(perm_ref[...], qkv, preferred_element_type=F32).astype(BF16)
        rows = tm // dil
        for r in range(dil):
            for j, o_ref in enumerate(outs):
                o_ref[:, r * A_WIDTH:(r + 1) * A_WIDTH] = cm[r * rows:(r + 1) * rows,
                                                             j * A_WIDTH:(j + 1) * A_WIDTH]
    c0 = 3 * A_WIDTH
    cq = p[:, c0:c0 + Q_LORA]
    ckv = p[:, c0 + Q_LORA:c0 + Q_LORA + KV_LORA]
    kr = p[:, c0 + Q_LORA + KV_LORA:c0 + Q_LORA + KV_LORA + LANES]
    krr = p[:, c0 + Q_LORA + KV_LORA + LANES:c0 + Q_LORA + KV_LORA + 2 * LANES]
    cos = cos_ref[...]
    sin = sin_ref[...]
    cos8 = jnp.concatenate([cos] * B_HEADS, axis=1)
    sin8 = jnp.concatenate([sin] * B_HEADS, axis=1)
    cqn = _rms_norm(cq, qg_ref[...]).astype(BF16)
    q = jnp.dot(cqn, wq_ref[...], preferred_element_type=F32)
    qr = jnp.dot(cqn, wqr_ref[...], preferred_element_type=F32)
    qscale = (B_NOPE + B_ROPE) ** -0.5 * LOG2E
    qb_ref[...] = ((q * cos8 + qr * sin8) * qscale).astype(BF16)
    ckvn = _rms_norm(ckv, kvg_ref[...]).astype(BF16)
    kn = jnp.dot(ckvn, wuk_ref[...], preferred_element_type=F32)
    krope = kr * cos + krr * sin
    kb_ref[...] = (kn + jnp.concatenate([krope] * B_HEADS, axis=1)).astype(BF16)
    vbt_ref[...] = _nt_dot(wuvt_ref[...], ckvn).astype(BF16)


def _class_perm(tm, dil):
    rows = tm // dil
    c = np.arange(tm)
    src = (c % rows) * dil + c // rows
    perm = np.zeros((tm, tm), np.float32)
    perm[c, src] = 1.0
    return jnp.asarray(perm, BF16)


def _proj(xp, xs, w1, qg, kvg, wq, wqr, wuk, wuvt, cos_t, sin_t, s_prompt, s_sample):
    tm = TM_PROJ
    n1 = xp.shape[0] // tm
    n2 = xs.shape[0] // tm
    t_all = xp.shape[0] + xs.shape[0]
    pt, st = s_prompt // tm, s_sample // tm
    assert tm == PERM_GROUP
    perm4, perm16 = _class_perm(tm, 4), _class_perm(tm, 16)

    def tab_idx(i):
        return (jnp.where(i < n1, i % pt, (i - n1) % st), 0)

    full = lambda shape: pl.BlockSpec(shape, lambda i: (0, 0))
    row = lambda cols: pl.BlockSpec((tm, cols), lambda i: (i, 0))
    cls = lambda dil: pl.BlockSpec((tm // dil, dil * A_WIDTH), lambda i: (i, 0))
    cls_shape = lambda dil: jax.ShapeDtypeStruct((t_all // dil, dil * A_WIDTH), BF16)
    return pl.pallas_call(
        functools.partial(_proj_kernel, n_prompt_tiles=n1),
        grid=(n1 + n2,),
        in_specs=[
            pl.BlockSpec((tm, D_MODEL), lambda i: (jnp.minimum(i, n1 - 1), 0)),
            pl.BlockSpec((tm, D_MODEL), lambda i: (jnp.maximum(i - n1, 0), 0)),
            full(w1.shape), full(qg.shape), full(kvg.shape), full(wq.shape), full(wqr.shape),
            full(wuk.shape), full(wuvt.shape),
            pl.BlockSpec((tm, LANES), tab_idx), pl.BlockSpec((tm, LANES), tab_idx),
            full(perm4.shape), full(perm16.shape),
        ],
        out_specs=[row(A_WIDTH)] * 3 + [cls(4)] * 3 + [cls(16)] * 3
        + [row(B_HEADS * HEAD_PAD), row(B_HEADS * HEAD_PAD),
           pl.BlockSpec((B_HEADS * B_V, tm), lambda i: (0, i))],
        out_shape=[jax.ShapeDtypeStruct((t_all, A_WIDTH), BF16)] * 3
        + [cls_shape(4)] * 3 + [cls_shape(16)] * 3
        + [jax.ShapeDtypeStruct((t_all, B_HEADS * HEAD_PAD), BF16)] * 2
        + [jax.ShapeDtypeStruct((B_HEADS * B_V, t_all), BF16)],
        compiler_params=pltpu.CompilerParams(dimension_semantics=("arbitrary",),
                                             vmem_limit_bytes=VMEM_LIMIT),
        name="proj",
    )(xp, xs, w1, qg, kvg, wq, wqr, wuk, wuvt, cos_t, sin_t, perm4, perm16)


def _attn_a_kernel(qblk_ref, pblk_ref, nblk_ref, tstart_ref, llen_ref,
                   q_ref, kp_ref, kc_ref, kn_ref, vp_ref, vc_ref, vn_ref, bias_ref, perm_ref,
                   o_ref, lse_ref, cm_ref, *, n_pos, n_cls):
    n = pl.program_id(0)
    tq, wk = TQ_A, TQ_A + 2 * A_HALF
    lane = lax.broadcasted_iota(I32, (tq, LANES), 1)
    low = lane < A_HEAD_DIM
    colbase = lax.broadcasted_iota(I32, (1, wk), 1) + (tstart_ref[n] - A_HALF)
    cls_len = llen_ref[n]

    def one_class(r):
        cols = slice(0, A_WIDTH) if n_cls == 1 else pl.ds(pl.multiple_of(r * A_WIDTH, A_WIDTH), A_WIDTH)
        kcat = jnp.concatenate([kp_ref[:, cols], kc_ref[:, cols], kn_ref[:, cols]], axis=0)
        vcat = jnp.concatenate([vp_ref[:, cols], vc_ref[:, cols], vn_ref[:, cols]], axis=0)
        for j in range(n_pos):
            rows = slice(j * tq, (j + 1) * tq)
            q = q_ref[rows, cols]
            kw, vw = kcat[j * tq:j * tq + wk, :], vcat[j * tq:j * tq + wk, :]
            col = colbase + j * tq
            colpen = jnp.where((col >= 0) & (col < cls_len), 0.0, NEG_BIG).astype(F32)
            pairs = []
            lse_c = jnp.zeros((tq, LANES), F32)
            for jp in range(A_HEADS // 2):
                sl = slice(jp * LANES, (jp + 1) * LANES)
                qp, kpair, vpair = q[:, sl], kw[:, sl], vw[:, sl]
                outs = []
                for e in range(2):
                    qm = jnp.where(low if e == 0 else ~low, qp, jnp.zeros_like(qp))
                    s = _nt_dot(qm, kpair) + bias_ref[2 * jp + e] + colpen
                    m = jnp.max(s, axis=1, keepdims=True)
                    p = jnp.exp2(s - m)
                    l = jnp.sum(p, axis=1, keepdims=True)
                    outs.append(jnp.dot(p.astype(BF16), vpair, preferred_element_type=F32) / l)
                    lse_c = lse_c + jnp.where(lane == 2 * jp + e, m + jnp.log2(l), 0.0)
                pairs.append(jnp.where(low, outs[0], outs[1]))
            o_full = jnp.concatenate(pairs, axis=1).astype(BF16)
            if n_cls == 1:
                o_ref[rows, :] = o_full
                lse_ref[rows, :] = lse_c
            else:
                hi = lse_c.astype(BF16)
                rest = lse_c - hi.astype(F32)
                mid = rest.astype(BF16)
                lo = (rest - mid.astype(F32)).astype(BF16)
                cm_ref[r, rows, :] = jnp.concatenate([o_full, hi, mid, lo], axis=1)

    if n_cls == 1:
        one_class(0)
        return

    def body(r, carry):
        one_class(r)
        return carry

    lax.fori_loop(0, n_cls, body, 0)
    per = PERM_GROUP // n_cls
    for a in range(n_pos * tq * n_cls // PERM_GROUP):
        stack = jnp.concatenate([cm_ref[r, a * per:(a + 1) * per, :] for r in range(n_cls)], axis=0)
        nat = jnp.dot(perm_ref[...], stack, preferred_element_type=F32)
        rows = slice(a * PERM_GROUP, (a + 1) * PERM_GROUP)
        o_ref[rows, :] = nat[:, :A_WIDTH].astype(BF16)
        lse_ref[rows, :] = (nat[:, A_WIDTH:A_WIDTH + LANES] + nat[:, A_WIDTH + LANES:A_WIDTH + 2 * LANES]
                            + nat[:, A_WIDTH + 2 * LANES:])


def _attn_a_tables(seqs, dil, n_pos):
    rows = n_pos * TQ_A
    per_halo = rows // A_HALF
    qblk, pblk, nblk, tstart, llen = [], [], [], [], []
    for off, s_len in seqs:
        cls = s_len // dil
        steps = cls // rows
        base = (off // dil) // rows
        for i in range(steps):
            qblk.append(base + i)
            pblk.append((base + i) * per_halo - (1 if i > 0 else 0))
            nblk.append((base + i + 1) * per_halo - (0 if i < steps - 1 else 1))
            tstart.append(i * rows)
            llen.append(cls)
    return [jnp.asarray(np.asarray(a, np.int32)) for a in (qblk, pblk, nblk, tstart, llen)]


def _attn_a_bias(dil):
    tq, wk = TQ_A, TQ_A + 2 * A_HALF
    delta = np.abs(np.arange(wk)[None, :] - A_HALF - np.arange(tq)[:, None]).astype(np.float64)
    slopes = 2.0 ** (-8.0 * (np.arange(A_HEADS) + 1.0) / A_HEADS)
    bias = -slopes[:, None, None] * (delta * dil)[None] * LOG2E
    bias = np.where((delta <= A_HALF)[None], bias, NEG_BIG)
    return jnp.asarray(bias.astype(np.float32))


def _attn_a(qc, kc, vc, seqs, dil):
    t_all = qc.shape[0] * dil
    n_pos = A_POS_TILES[dil]
    rows, width = n_pos * TQ_A, dil * A_WIDTH
    tabs = _attn_a_tables(seqs, dil, n_pos)
    n_steps = int(tabs[0].shape[0])
    bias = _attn_a_bias(dil)
    perm = _class_perm(PERM_GROUP, dil).T
    cur = pl.BlockSpec((rows, width), lambda n, qb, pb, nb, ts, ll: (qb[n], 0))
    prev = pl.BlockSpec((A_HALF, width), lambda n, qb, pb, nb, ts, ll: (pb[n], 0))
    nxt = pl.BlockSpec((A_HALF, width), lambda n, qb, pb, nb, ts, ll: (nb[n], 0))
    tok = lambda cols: pl.BlockSpec((rows * dil, cols), lambda n, *_: (n, 0))
    return pl.pallas_call(
        functools.partial(_attn_a_kernel, n_pos=n_pos, n_cls=dil),
        grid_spec=pltpu.PrefetchScalarGridSpec(
            num_scalar_prefetch=5, grid=(n_steps,),
            in_specs=[cur, prev, cur, nxt, prev, cur, nxt,
                      pl.BlockSpec(bias.shape, lambda n, *_: (0, 0, 0)),
                      pl.BlockSpec(perm.shape, lambda n, *_: (0, 0))],
            out_specs=[tok(A_WIDTH), tok(LANES)],
            scratch_shapes=[pltpu.VMEM((dil, rows, A_WIDTH + 3 * LANES), BF16)]),
        out_shape=[jax.ShapeDtypeStruct((t_all, A_WIDTH), BF16),
                   jax.ShapeDtypeStruct((t_all, LANES), F32)],
        compiler_params=pltpu.CompilerParams(dimension_semantics=("arbitrary",),
                                             vmem_limit_bytes=VMEM_LIMIT),
        name=f"attn_a_d{dil}",
    )(*tabs, qc, kc, kc, kc, vc, vc, vc, bias, perm)


def _attn_b_kernel(q_ref, k_ref, vt_ref, o_ref, sta_ref, stb_ref, ma_ref, mb_ref):
    s = pl.program_id(0)
    tq, s_len = q_ref.shape[0], k_ref.shape[0]

    @pl.when(s == 0)
    def _():
        stb_ref[...] = jnp.zeros_like(stb_ref)
        mb_ref[...] = jnp.zeros_like(mb_ref)

    def step(st_new_ref, m_new_ref, st_old_ref, m_old_ref):
        m_old = m_old_ref[...]
        q = q_ref[...]
        m_new = jnp.full((1, tq), -jnp.inf, F32)
        l = jnp.zeros((1, tq), F32)
        acc = jnp.zeros((B_V, tq), F32)
        for c in range(s_len // KEY_CHUNK_B):
            ks = slice(c * KEY_CHUNK_B, (c + 1) * KEY_CHUNK_B)
            st_c = _nt_dot(k_ref[ks, :], q)
            st_new_ref[ks, :] = st_c
            m_new = jnp.maximum(m_new, jnp.max(st_c, axis=0, keepdims=True))
            p = jnp.exp2(st_old_ref[ks, :] - m_old)
            l = l + jnp.sum(p, axis=0, keepdims=True)
            acc = acc + jnp.dot(vt_ref[:, ks], p.astype(BF16), preferred_element_type=F32)
        m_new_ref[...] = m_new
        o_ref[...] = (acc / l).astype(BF16)

    @pl.when(s % 2 == 0)
    def _():
        step(sta_ref, ma_ref, stb_ref, mb_ref)

    @pl.when(s % 2 == 1)
    def _():
        step(stb_ref, mb_ref, sta_ref, ma_ref)


def _attn_b(qb, kb, vbt, off, n_batch, s_len):
    tq = min(s_len, LOGITS_ELEMS_B // s_len)
    nq = s_len // tq
    qbase, kbase = off // tq, off // s_len
    n_items = n_batch * B_HEADS * nq

    def split(item):
        return item // (B_HEADS * nq), (item // nq) % B_HEADS, item % nq

    def q_map(s):
        b, h, qi = split(jnp.minimum(s, n_items - 1))
        return (qbase + b * nq + qi, h)

    def k_map(s):
        b, h, _ = split(jnp.minimum(s, n_items - 1))
        return (kbase + b, h)

    def v_map(s):
        b, h, _ = split(jnp.maximum(s - 1, 0))
        return (h, kbase + b)

    def o_map(s):
        b, h, qi = split(jnp.maximum(s - 1, 0))
        return (h, b * nq + qi)

    return pl.pallas_call(
        _attn_b_kernel,
        grid=(n_items + 1,),
        in_specs=[pl.BlockSpec((tq, HEAD_PAD), q_map),
                  pl.BlockSpec((s_len, HEAD_PAD), k_map),
                  pl.BlockSpec((B_V, s_len), v_map)],
        out_specs=pl.BlockSpec((B_V, tq), o_map),
        out_shape=jax.ShapeDtypeStruct((B_HEADS * B_V, n_batch * s_len), BF16),
        scratch_shapes=[pltpu.VMEM((s_len, tq), F32), pltpu.VMEM((s_len, tq), F32),
                        pltpu.VMEM((1, tq), F32), pltpu.VMEM((1, tq), F32)],
        compiler_params=pltpu.CompilerParams(dimension_semantics=("arbitrary",),
                                             vmem_limit_bytes=VMEM_LIMIT),
        name=f"attn_b_s{s_len}",
    )(qb, kb, vbt)


def _mix_kernel(xp_ref, xs_ref, o0_ref, o1_ref, o2_ref, l0_ref, l1_ref, l2_ref, spread_ref,
                obp_ref, obs_ref, wout_ref,
                g_ref, b_ref, wrh_ref, wrl_ref, rb_ref,
                h_ref, hp_ref, tope_ref, pos_ref, gate_ref, cnt_out_ref, cnt_ref, stage_ref,
                *, n_prompt_tiles):
    i = pl.program_id(0)
    tm = xp_ref.shape[0]

    @pl.when(i == 0)
    def _():
        cnt_ref[...] = jnp.zeros_like(cnt_ref)

    l0, l1, l2 = l0_ref[...], l1_ref[...], l2_ref[...]
    lmax = jnp.maximum(jnp.maximum(l0, l1), l2)
    e0, e1, e2 = jnp.exp2(l0 - lmax), jnp.exp2(l1 - lmax), jnp.exp2(l2 - lmax)
    inv = 1.0 / (e0 + e1 + e2)
    spread = spread_ref[...]

    def per_lane(w):
        hi = w.astype(BF16)
        lo = (w - hi.astype(F32)).astype(BF16)
        return (jnp.dot(hi, spread, preferred_element_type=F32)
                + jnp.dot(lo, spread, preferred_element_type=F32))

    oa = (per_lane(e0 * inv) * o0_ref[...].astype(F32) + per_lane(e1 * inv) * o1_ref[...].astype(F32)
          + per_lane(e2 * inv) * o2_ref[...].astype(F32))
    is_prompt = i < n_prompt_tiles
    obt = jnp.where(is_prompt, obp_ref[...], obs_ref[...])
    mix = (jnp.dot(oa.astype(BF16), wout_ref[0:A_WIDTH, :], preferred_element_type=F32)
           + lax.dot_general(obt, wout_ref[A_WIDTH:, :], (((0,), (0,)), ((), ())),
                             preferred_element_type=F32))
    x = jnp.where(is_prompt, xp_ref[...], xs_ref[...])
    h = _layer_norm(ALPHA * x + mix, g_ref[...], b_ref[...])
    h_ref[...] = h
    _slab_store_bf16(hp_ref, stage_ref, h)

    h_hi = h.astype(BF16)
    h_lo = (h - h_hi.astype(F32)).astype(BF16)
    wrh = wrh_ref[...]
    logits = _nt_dot(wrh, h_hi) + _nt_dot(wrh, h_lo) + _nt_dot(wrl_ref[...], h_hi)
    scores = jax.nn.sigmoid(logits)
    sel = scores + rb_ref[...]

    sub = lax.broadcasted_iota(I32, (GROUP_SIZE, tm), 0).astype(F32)
    gscore = []
    for g in range(N_GROUPS):
        sg = sel[g * GROUP_SIZE:(g + 1) * GROUP_SIZE, :]
        m1 = jnp.max(sg, axis=0, keepdims=True)
        first = jnp.min(jnp.where(sg == m1, sub, float(GROUP_SIZE)), axis=0, keepdims=True)
        m2 = jnp.max(jnp.where(sub == first, -jnp.inf, sg), axis=0, keepdims=True)
        gscore.append(m1 + m2)
    cands = []
    for g in range(N_GROUPS):
        beaten = jnp.zeros((1, tm), F32)
        for g2 in range(N_GROUPS):
            if g2 == g:
                continue
            wins = (gscore[g2] > gscore[g]) | ((gscore[g2] == gscore[g]) & (g2 < g))
            beaten = beaten + wins.astype(F32)
        keep = beaten < float(TOPK_GROUPS)
        sg = sel[g * GROUP_SIZE:(g + 1) * GROUP_SIZE, :]
        cands.append(jnp.where(keep, sg, NEG_BIG))
    cand = jnp.concatenate(cands, axis=0)

    eidx = lax.broadcasted_iota(I32, (N_EXPERTS, tm), 0).astype(F32)
    picked_idx, picked_gate = [], []
    onehot = jnp.zeros((N_EXPERTS, tm), F32)
    for _ in range(TOP_K):
        mx = jnp.max(cand, axis=0, keepdims=True)
        fi = jnp.min(jnp.where(cand == mx, eidx, float(N_EXPERTS)), axis=0, keepdims=True)
        pick = eidx == fi
        picked_idx.append(fi)
        picked_gate.append(jnp.sum(jnp.where(pick, scores, 0.0), axis=0, keepdims=True))
        onehot = onehot + pick.astype(F32)
        cand = jnp.where(pick, -jnp.inf, cand)
    gsum = picked_gate[0]
    for k in range(1, TOP_K):
        gsum = gsum + picked_gate[k]

    tri = (lax.broadcasted_iota(I32, (tm, tm), 0) < lax.broadcasted_iota(I32, (tm, tm), 1))
    before = jnp.dot(onehot.astype(BF16), tri.astype(BF16), preferred_element_type=F32)
    rank = before + cnt_ref[:, 0:1]
    for k in range(TOP_K):
        pick = eidx == picked_idx[k]
        tope_ref[k:k + 1, :] = picked_idx[k].astype(I32)
        pos_ref[k:k + 1, :] = jnp.sum(jnp.where(pick, rank, 0.0), axis=0, keepdims=True).astype(I32)
        gate_ref[k:k + 1, :] = picked_gate[k] / gsum * ROUTED_SCALE
    cnt_ref[...] = cnt_ref[...] + jnp.sum(onehot, axis=1, keepdims=True)
    cnt_out_ref[...] = cnt_ref[...]


def _mix(xp, xs, outs, lses, obp, obs, wout, g, b, wrh, wrl, rb):
    tm = TM_MIX
    n1 = xp.shape[0] // tm
    n2 = xs.shape[0] // tm
    t_all = xp.shape[0] + xs.shape[0]
    full = lambda a: pl.BlockSpec(a.shape, lambda i: (0,) * a.ndim)
    row = lambda cols: pl.BlockSpec((tm, cols), lambda i: (i, 0))
    prow = lambda cols: pl.BlockSpec((tm, cols), lambda i: (jnp.minimum(i, n1 - 1), 0))
    srow = lambda cols: pl.BlockSpec((tm, cols), lambda i: (jnp.maximum(i - n1, 0), 0))
    col = pl.BlockSpec((TOP_K, tm), lambda i: (0, i))
    head_of_lane = np.arange(A_WIDTH) // A_HEAD_DIM
    spread = jnp.asarray(np.arange(LANES)[:, None] == head_of_lane[None, :], BF16)
    return pl.pallas_call(
        functools.partial(_mix_kernel, n_prompt_tiles=n1),
        grid=(n1 + n2,),
        in_specs=[
            prow(D_MODEL), srow(D_MODEL),
            row(A_WIDTH), row(A_WIDTH), row(A_WIDTH), row(LANES), row(LANES), row(LANES), full(spread),
            pl.BlockSpec((B_HEADS * B_V, tm), lambda i: (0, jnp.minimum(i, n1 - 1))),
            pl.BlockSpec((B_HEADS * B_V, tm), lambda i: (0, jnp.maximum(i - n1, 0))),
            full(wout), full(g), full(b), full(wrh), full(wrl), full(rb),
        ],
        out_specs=[row(D_MODEL), pl.BlockSpec((SLAB * tm, LANES), lambda i: (i, 0)), col, col, col,
                   pl.BlockSpec((N_EXPERTS, LANES), lambda i: (0, 0))],
        out_shape=[jax.ShapeDtypeStruct((t_all, D_MODEL), F32),
                   jax.ShapeDtypeStruct((SLAB * t_all, LANES), BF16),
                   jax.ShapeDtypeStruct((TOP_K, t_all), I32),
                   jax.ShapeDtypeStruct((TOP_K, t_all), I32),
                   jax.ShapeDtypeStruct((TOP_K, t_all), F32),
                   jax.ShapeDtypeStruct((N_EXPERTS, LANES), F32)],
        scratch_shapes=[pltpu.VMEM((N_EXPERTS, LANES), F32), pltpu.VMEM((SLAB * tm, LANES), F32)],
        compiler_params=pltpu.CompilerParams(dimension_semantics=("arbitrary",),
                                             vmem_limit_bytes=VMEM_LIMIT),
        name="mix_router",
    )(xp, xs, *outs, *lses, spread, obp, obs, wout, g, b, wrh, wrl, rb)


def _dispatch_kernel(zstart_ref, nused_ref, h_ref, tope_ref, pos_ref, pstart_ref, buf_ref, dest_ref,
                     dsm_ref, zero_ref, sem_ref, zsem_ref):
    i = pl.program_id(0)
    tm = tope_ref.shape[1]
    tile_rows = SLAB * ROW_BLOCK

    @pl.when(i == 0)
    def _():
        zero_ref[...] = jnp.zeros_like(zero_ref)

        def zfill(e, carry):
            start = pl.multiple_of(zstart_ref[e], SLAB)
            pltpu.make_async_copy(zero_ref, buf_ref.at[pl.ds(start, tile_rows), :], zsem_ref).start()
            return carry

        lax.fori_loop(0, N_EXPERTS, zfill, 0)
        span = buf_ref.at[pl.ds(0, N_EXPERTS * tile_rows), :]
        pltpu.make_async_copy(span, span, zsem_ref).wait()

        def ztail(j, carry):
            start = pl.multiple_of(j * tile_rows, tile_rows)
            pltpu.make_async_copy(zero_ref, buf_ref.at[pl.ds(start, tile_rows), :], zsem_ref).start()
            return carry

        def zwait(j, carry):
            pltpu.make_async_copy(zero_ref, buf_ref.at[pl.ds(0, tile_rows), :], zsem_ref).wait()
            return carry

        n_tiles = buf_ref.shape[0] // tile_rows
        lax.fori_loop(nused_ref[0], n_tiles, ztail, 0)
        lax.fori_loop(nused_ref[0], n_tiles, zwait, 0)

    eidx = lax.broadcasted_iota(I32, (N_EXPERTS, tm), 0)
    pstart = pstart_ref[...]
    for k in range(TOP_K):
        hit = eidx == tope_ref[k:k + 1, :]
        base = jnp.sum(jnp.where(hit, pstart, 0.0), axis=0, keepdims=True)
        dest_ref[k:k + 1, :] = (pos_ref[k:k + 1, :] + base.astype(I32)) * SLAB
    pltpu.sync_copy(dest_ref, dsm_ref)

    def scatter(t, carry):
        src = h_ref.at[pl.ds(pl.multiple_of(t * SLAB, SLAB), SLAB), :]
        for k in range(TOP_K):
            dst = buf_ref.at[pl.ds(pl.multiple_of(dsm_ref[k, t], SLAB), SLAB), :]
            pltpu.make_async_copy(src, dst, sem_ref).start(priority=k % 2)
        return carry

    lax.fori_loop(0, tm, scatter, 0, unroll=8)
    span = buf_ref.at[pl.ds(0, TOP_K * tm * SLAB), :]
    pltpu.make_async_copy(span, span, sem_ref).wait()


def _dispatch(h, tope, pos, pstart, zstart, nused, n_rows):
    tm = TM_DISP
    t_all = tope.shape[1]
    col = pl.BlockSpec((TOP_K, tm), lambda i, z, nu: (0, i))
    return pl.pallas_call(
        _dispatch_kernel,
        grid_spec=pltpu.PrefetchScalarGridSpec(
            num_scalar_prefetch=2, grid=(t_all // tm,),
            in_specs=[pl.BlockSpec((SLAB * tm, LANES), lambda i, z, nu: (i, 0)), col, col,
                      pl.BlockSpec((N_EXPERTS, 1), lambda i, z, nu: (0, 0))],
            out_specs=[pl.BlockSpec(memory_space=pl.ANY), col],
            scratch_shapes=[pltpu.SMEM((TOP_K, tm), I32),
                            pltpu.VMEM((SLAB * ROW_BLOCK, LANES), BF16),
                            pltpu.SemaphoreType.DMA, pltpu.SemaphoreType.DMA]),
        out_shape=[jax.ShapeDtypeStruct((SLAB * n_rows, LANES), BF16),
                   jax.ShapeDtypeStruct((TOP_K, t_all), I32)],
        compiler_params=pltpu.CompilerParams(dimension_semantics=("arbitrary",),
                                             vmem_limit_bytes=VMEM_LIMIT),
        name="dispatch",
    )(zstart, nused, h, tope, pos, pstart)


def _expert_kernel(tfirst_ref, ntile_ref, nused_ref, x_hbm, wg_ref, wu_ref, wd_ref, o_hbm,
                   xbuf, obuf, stage_ref, wgb_ref, wub_ref, wdb_ref, xsem, osem):
    e = pl.program_id(0)
    tile_rows = SLAB * ROW_BLOCK
    nused = nused_ref[0]

    def x_copy(g, slot):
        start = pl.multiple_of(g * tile_rows, tile_rows)
        return pltpu.make_async_copy(x_hbm.at[pl.ds(start, tile_rows), :], xbuf.at[slot], xsem.at[slot])

    def o_copy(g, slot):
        start = pl.multiple_of(g * tile_rows, tile_rows)
        return pltpu.make_async_copy(obuf.at[slot], o_hbm.at[pl.ds(start, tile_rows), :], osem.at[slot])

    @pl.when(e == 0)
    def _():
        for j in range(N_XBUF - 1):
            @pl.when(j < nused)
            def _():
                x_copy(j, j).start()

    n_e = ntile_ref[e]

    @pl.when(n_e > 0)
    def _():
        wgb_ref[...] = wg_ref[...].astype(BF16)
        wub_ref[...] = wu_ref[...].astype(BF16)
        wdb_ref[...] = wd_ref[...].astype(BF16)

    def tile(j, carry):
        g = tfirst_ref[e] + j
        ahead = g + (N_XBUF - 1)

        @pl.when(ahead < nused)
        def _():
            x_copy(ahead, ahead % N_XBUF).start()

        slot = g % N_XBUF
        x_copy(g, slot).wait()
        x = _slab_load_bf16(xbuf.at[slot], stage_ref, ROW_BLOCK).astype(BF16)
        gt = jnp.dot(x, wgb_ref[...], preferred_element_type=F32)
        up = jnp.dot(x, wub_ref[...], preferred_element_type=F32)
        hmid = (gt * jax.nn.sigmoid(gt) * up).astype(BF16)
        out = jnp.dot(hmid, wdb_ref[...], preferred_element_type=F32)
        oslot = g % N_OBUF

        @pl.when(g >= N_OBUF)
        def _():
            o_copy(g - N_OBUF, oslot).wait()

        _slab_store(obuf.at[oslot], out)
        o_copy(g, oslot).start()
        return carry

    lax.fori_loop(0, n_e, tile, 0)

    @pl.when(e == N_EXPERTS - 1)
    def _():
        for j in range(N_OBUF):
            @pl.when(nused > j)
            def _():
                o_copy(nused - 1 - j, (nused - 1 - j) % N_OBUF).wait()

        stage_ref[...] = jnp.zeros_like(stage_ref)
        n_tiles = o_hbm.shape[0] // tile_rows

        def tail_copy(g):
            start = pl.multiple_of(g * tile_rows, tile_rows)
            return pltpu.make_async_copy(stage_ref, o_hbm.at[pl.ds(start, tile_rows), :], osem.at[0])

        def ztail(g, carry):
            tail_copy(g).start()
            return carry

        def zwait(g, carry):
            tail_copy(g).wait()
            return carry

        lax.fori_loop(nused, n_tiles, ztail, 0)
        lax.fori_loop(nused, n_tiles, zwait, 0)


def _experts(buf, tfirst, ntile, nused, w_gate, w_up, w_down):
    tile_rows = SLAB * ROW_BLOCK
    wmap = lambda e, tf, nt, nu: (e, 0, 0)
    return pl.pallas_call(
        _expert_kernel,
        grid_spec=pltpu.PrefetchScalarGridSpec(
            num_scalar_prefetch=3, grid=(N_EXPERTS,),
            in_specs=[pl.BlockSpec(memory_space=pl.ANY),
                      pl.BlockSpec((None, D_MODEL, EXPERT_HIDDEN), wmap),
                      pl.BlockSpec((None, D_MODEL, EXPERT_HIDDEN), wmap),
                      pl.BlockSpec((None, EXPERT_HIDDEN, D_MODEL), wmap)],
            out_specs=pl.BlockSpec(memory_space=pl.ANY),
            scratch_shapes=[pltpu.VMEM((N_XBUF, tile_rows, LANES), BF16),
                            pltpu.VMEM((N_OBUF, tile_rows, LANES), F32),
                            pltpu.VMEM((tile_rows, LANES), F32),
                            pltpu.VMEM((D_MODEL, EXPERT_HIDDEN), BF16),
                            pltpu.VMEM((D_MODEL, EXPERT_HIDDEN), BF16),
                            pltpu.VMEM((EXPERT_HIDDEN, D_MODEL), BF16),
                            pltpu.SemaphoreType.DMA((N_XBUF,)),
                            pltpu.SemaphoreType.DMA((N_OBUF,))]),
        out_shape=jax.ShapeDtypeStruct(buf.shape, F32),
        compiler_params=pltpu.CompilerParams(dimension_semantics=("arbitrary",),
                                             vmem_limit_bytes=VMEM_LIMIT),
        name="experts",
    )(tfirst, ntile, nused, buf, w_gate, w_up, w_down)


def _combine_kernel(h_ref, gate_ref, dest_ref, dest1_ref, dest2_ref, eo_ref, wsg_ref, wsu_ref, wsd_ref,
                    g_ref, b_ref, yp_ref, ys_ref, dsm_ref, rows_ref, sem_ref, *, n_prompt_tiles, n_tiles):
    i = pl.program_id(0)
    tm = gate_ref.shape[0]
    slot = i % 2
    rows = rows_ref.at[slot]
    nxt_rows = rows_ref.at[1 - slot]
    nxt_sem = sem_ref.at[1 - slot]

    @pl.when(i == 0)
    def _():
        pltpu.sync_copy(dest_ref, dsm_ref)

        def gather(t, carry):
            for k in range(TOP_K):
                src = eo_ref.at[pl.ds(pl.multiple_of(dsm_ref[k, t], SLAB), SLAB), :]
                dst = rows_ref.at[0, pl.ds(pl.multiple_of((k * tm + t) * SLAB, SLAB), SLAB), :]
                pltpu.make_async_copy(src, dst, sem_ref.at[0]).start(priority=k % 2)
            return carry

        lax.fori_loop(0, tm, gather, 0, unroll=8)
        pltpu.sync_copy(dest1_ref, dsm_ref)

    def issue_next(k):
        for t in range(tm):
            src = eo_ref.at[pl.ds(pl.multiple_of(dsm_ref[k, t], SLAB), SLAB), :]
            dst = nxt_rows.at[pl.ds((k * tm + t) * SLAB, SLAB), :]
            pltpu.make_async_copy(src, dst, nxt_sem).start(priority=t % 2)

    h = h_ref[...]
    hb = h.astype(BF16)
    sg = jnp.dot(hb, wsg_ref[...], preferred_element_type=F32)
    su = jnp.dot(hb, wsu_ref[...], preferred_element_type=F32)
    shared = jnp.dot((sg * jax.nn.sigmoid(sg) * su).astype(BF16), wsd_ref[...],
                     preferred_element_type=F32)

    issue_next(0)
    whole = eo_ref.at[pl.ds(0, TOP_K * tm * SLAB), :]
    pltpu.make_async_copy(whole, rows, sem_ref.at[slot]).wait()
    gate = gate_ref[...]
    routed = gate[:, 0:1] * _slab_load(rows, tm)
    for k in range(1, TOP_K):
        issue_next(k)
        routed = routed + gate[:, k:k + 1] * _slab_load(rows, tm, k * tm)
    y = _layer_norm(ALPHA * h + (routed + shared), g_ref[...], b_ref[...])

    @pl.when(i < n_prompt_tiles)
    def _():
        yp_ref[...] = y

    @pl.when(i >= n_prompt_tiles)
    def _():
        ys_ref[...] = y

    pltpu.sync_copy(dest2_ref, dsm_ref)

    @pl.when(i == n_tiles - 1)
    def _():
        pltpu.make_async_copy(whole, nxt_rows, nxt_sem).wait()


def _combine(h, gate_t, dest, eo, wsg, wsu, wsd, g, b, t_prompt):
    tm = TM_COMB
    t_all = gate_t.shape[0]
    n1 = t_prompt // tm
    n2 = (t_all - t_prompt) // tm
    full = lambda a: pl.BlockSpec(a.shape, lambda i: (0,) * a.ndim)
    return pl.pallas_call(
        functools.partial(_combine_kernel, n_prompt_tiles=n1, n_tiles=n1 + n2),
        grid=(n1 + n2,),
        in_specs=[pl.BlockSpec((tm, D_MODEL), lambda i: (i, 0)),
                  pl.BlockSpec((tm, TOP_K), lambda i: (i, 0)),
                  pl.BlockSpec((TOP_K, tm), lambda i: (0, i)),
                  pl.BlockSpec((TOP_K, tm), lambda i: (0, jnp.minimum(i + 1, n1 + n2 - 1))),
                  pl.BlockSpec((TOP_K, tm), lambda i: (0, jnp.minimum(i + 2, n1 + n2 - 1))),
                  pl.BlockSpec(memory_space=pl.ANY),
                  full(wsg), full(wsu), full(wsd), full(g), full(b)],
        out_specs=[pl.BlockSpec((tm, D_MODEL), lambda i: (jnp.minimum(i, n1 - 1), 0)),
                   pl.BlockSpec((tm, D_MODEL), lambda i: (jnp.maximum(i - n1, 0), 0))],
        out_shape=[jax.ShapeDtypeStruct((t_prompt, D_MODEL), F32),
                   jax.ShapeDtypeStruct((t_all - t_prompt, D_MODEL), F32)],
        scratch_shapes=[pltpu.SMEM((TOP_K, tm), I32),
                        pltpu.VMEM((2, TOP_K * tm * SLAB, LANES), F32),
                        pltpu.SemaphoreType.DMA((2,))],
        compiler_params=pltpu.CompilerParams(dimension_semantics=("arbitrary",),
                                             vmem_limit_bytes=VMEM_LIMIT),
        name="combine",
    )(h, gate_t, dest, dest, dest, eo, wsg, wsu, wsd, g, b)


def _rope_tables(s_max):
    inv_freq = ROPE_BASE ** (-jnp.arange(0, B_ROPE, 2, dtype=F32) / B_ROPE)
    ang = jnp.arange(s_max, dtype=F32)[:, None] * inv_freq[None, :]
    cos, sin = jnp.cos(ang), jnp.sin(ang)
    ones = jnp.ones((s_max, B_NOPE), F32)
    zeros_n = jnp.zeros((s_max, B_NOPE), F32)
    zeros_p = jnp.zeros((s_max, HEAD_PAD - B_NOPE - B_ROPE), F32)
    return (jnp.concatenate([ones, cos, cos, zeros_p], axis=1),
            jnp.concatenate([zeros_n, sin, sin, zeros_p], axis=1))


def _rot_cols(w):
    half = B_ROPE // 2
    return jnp.concatenate([-w[..., half:], w[..., :half]], axis=-1)


def _layout_weights(w_in, w_uq, w_uk):
    c_kr = 3 * A_WIDTH + Q_LORA + KV_LORA
    w_kr = w_in[:, c_kr:c_kr + B_ROPE]
    pad_l = jnp.zeros((D_MODEL, B_NOPE), F32)
    pad_r = jnp.zeros((D_MODEL, HEAD_PAD - B_NOPE - B_ROPE), F32)
    w1 = jnp.concatenate([w_in[:, :c_kr], pad_l, w_kr, pad_r, pad_l, _rot_cols(w_kr), pad_r], axis=1)
    wq3 = w_uq.reshape(Q_LORA, B_HEADS, B_NOPE + B_ROPE)
    nope, rope = wq3[..., :B_NOPE], wq3[..., B_NOPE:]
    zpad = jnp.zeros((Q_LORA, B_HEADS, HEAD_PAD - B_NOPE - B_ROPE), F32)
    wq = jnp.concatenate([nope, rope, zpad], axis=-1).reshape(Q_LORA, B_HEADS * HEAD_PAD)
    wqr = jnp.concatenate([jnp.zeros_like(nope), _rot_cols(rope), zpad], axis=-1)
    wqr = wqr.reshape(Q_LORA, B_HEADS * HEAD_PAD)
    wk3 = w_uk.reshape(KV_LORA, B_HEADS, B_NOPE)
    wuk = jnp.concatenate([wk3, jnp.zeros((KV_LORA, B_HEADS, HEAD_PAD - B_NOPE), F32)], axis=-1)
    wuk = wuk.reshape(KV_LORA, B_HEADS * HEAD_PAD)
    return w1.astype(BF16), wq.astype(BF16), wqr.astype(BF16), wuk.astype(BF16)


def _forward(x_prompt, x_sample, w_in, w_out, ln1_g, ln1_b, q_norm_g, w_uq, kv_norm_g, w_uk, w_uv,
             w_router, router_bias, w_gate, w_up, w_down, ws_gate, ws_up, ws_down, ln2_g, ln2_b):
    b1, s1, _ = x_prompt.shape
    b2, s2, _ = x_sample.shape
    t1, t2 = b1 * s1, b2 * s2
    t_all = t1 + t2
    xp = x_prompt.reshape(t1, D_MODEL)
    xs = x_sample.reshape(t2, D_MODEL)
    seqs = [(b * s1, s1) for b in range(b1)] + [(t1 + b * s2, s2) for b in range(b2)]
    assert t1 % s2 == 0 and s1 % (TQ_A * 16) == 0 and s2 % (TQ_A * 16) == 0

    w1, wq, wqr, wuk = _layout_weights(w_in, w_uq, w_uk)
    cos_t, sin_t = _rope_tables(max(s1, s2))
    (qa, ka, va, q4, k4, v4, q16, k16, v16, qb, kb, vbt) = _proj(
        xp, xs, w1, q_norm_g.reshape(1, -1), kv_norm_g.reshape(1, -1), wq, wqr, wuk,
        w_uv.T.astype(BF16), cos_t, sin_t, s1, s2)

    outs, lses = [], []
    for (_, dil), qkv in zip(A_PATTERNS, ((qa, ka, va), (q4, k4, v4), (q16, k16, v16))):
        o, lse = _attn_a(*qkv, seqs, dil)
        outs.append(o)
        lses.append(lse)
    obp = _attn_b(qb, kb, vbt, 0, b1, s1)
    obs = _attn_b(qb, kb, vbt, t1, b2, s2)

    wr_t = w_router.T
    wr_hi = wr_t.astype(BF16)
    wr_lo = (wr_t - wr_hi.astype(F32)).astype(BF16)
    h, h_slab, tope, pos, gate, counts = _mix(xp, xs, outs, lses, obp, obs, w_out.astype(BF16),
                                                ln1_g.reshape(1, -1), ln1_b.reshape(1, -1),
                                                wr_hi, wr_lo, router_bias.reshape(-1, 1))

    cnt = counts[:, 0].astype(I32)
    padded = ((cnt + ROW_BLOCK - 1) // ROW_BLOCK) * ROW_BLOCK
    pend = jnp.cumsum(padded)
    pstart = pend - padded
    n_tiles = (t_all * TOP_K) // ROW_BLOCK + N_EXPERTS + 1
    nused = (pend[-1:] // ROW_BLOCK).astype(I32)

    buf, dest = _dispatch(h_slab, tope, pos, pstart.astype(F32).reshape(-1, 1),
                          ((pstart + cnt) * SLAB).astype(I32), nused, n_tiles * ROW_BLOCK)
    eo = _experts(buf, (pstart // ROW_BLOCK).astype(I32), (padded // ROW_BLOCK).astype(I32), nused,
                  w_gate, w_up, w_down)
    yp, ys = _combine(h, gate.T, dest, eo, ws_gate.astype(BF16), ws_up.astype(BF16),
                      ws_down.astype(BF16), ln2_g.reshape(1, -1), ln2_b.reshape(1, -1), t1)
    return yp.reshape(b1, s1, D_MODEL), ys.reshape(b2, s2, D_MODEL)


def kernel(x_prompt, x_sample, w_in, w_out, ln1_g, ln1_b, q_norm_g, w_uq, kv_norm_g, w_uk, w_uv,
           w_router, router_bias, w_gate, w_up, w_down, ws_gate, ws_up, ws_down, ln2_g, ln2_b):
    params = (w_in, w_out, ln1_g, ln1_b, q_norm_g, w_uq, kv_norm_g, w_uk, w_uv, w_router, router_bias,
              w_gate, w_up, w_down, ws_gate, ws_up, ws_down, ln2_g, ln2_b)
    assert all(p.shape[0] == 1 for p in params), "one encoder layer"
    return _forward(x_prompt, x_sample, *[p.reshape(p.shape[1:]) for p in params])
```

```python
import functools
import math

import numpy as np
import jax
import jax.numpy as jnp
from jax import lax
from jax.experimental import pallas as pl
from jax.experimental.pallas import tpu as pltpu

F32 = jnp.float32
BF16 = jnp.bfloat16
I32 = jnp.int32

D_MODEL = 1024
A_HEADS = 8
A_HEAD_DIM = 64
A_WIDTH = A_HEADS * A_HEAD_DIM
A_PATTERNS = ((128, 1), (512, 4), (2048, 16))
A_HALF = 64
B_HEADS = 8
B_NOPE = 64
B_ROPE = 32
B_V = 64
Q_LORA = 256
KV_LORA = 128
ROPE_BASE = 10000.0
N_EXPERTS = 256
TOP_K = 8
N_GROUPS = 8
GROUP_SIZE = N_EXPERTS // N_GROUPS
TOPK_GROUPS = 4
EXPERT_HIDDEN = 256
ROUTED_SCALE = 2.5
LN_EPS = 1e-5
RMS_EPS = 1e-6
NEG_BIG = -1e30
ALPHA = 2.0 ** 0.25
LOG2E = 1.4426950408889634

LANES = 128
HEAD_PAD = 128
SLAB = D_MODEL // LANES
W1_COLS = 3 * A_WIDTH + Q_LORA + KV_LORA + 2 * LANES

TM_PROJ = 256
TQ_A = 128
A_POS_TILES = {1: 4, 4: 2, 16: 1}
PERM_GROUP = 256
LOGITS_ELEMS_B = 4096 * 512
KEY_CHUNK_B = 512
TM_MIX = 256
TM_DISP = 256
ROW_BLOCK = 256
TM_COMB = 128
N_XBUF = 8
N_OBUF = 4
VMEM_LIMIT = 48 * 1024 * 1024


def _slab_load(ref, n_tok, tok0=0):
    return jnp.concatenate([ref[pl.ds(tok0 * SLAB + c, n_tok, stride=SLAB), :] for c in range(SLAB)],
                           axis=1)


def _slab_store(ref, val):
    for c in range(SLAB):
        ref[pl.ds(c, val.shape[0], stride=SLAB), :] = val[:, c * LANES:(c + 1) * LANES]


def _slab_load_bf16(ref, stage_ref, n_tok, tok0=0):
    stage_ref[...] = ref[tok0 * SLAB:(tok0 + n_tok) * SLAB, :].astype(F32)
    return _slab_load(stage_ref, n_tok)


def _slab_store_bf16(ref, stage_ref, val):
    _slab_store(stage_ref, val)
    ref[...] = stage_ref[...].astype(BF16)


def _nt_dot(a, b):
    return lax.dot_general(a, b, (((1,), (1,)), ((), ())), preferred_element_type=F32)


def _layer_norm(x, g, b):
    mu = jnp.mean(x, axis=-1, keepdims=True)
    xc = x - mu
    var = jnp.mean(xc * xc, axis=-1, keepdims=True)
    return xc * lax.rsqrt(var + LN_EPS) * g + b


def _rms_norm(x, g):
    return x * lax.rsqrt(jnp.mean(x * x, axis=-1, keepdims=True) + RMS_EPS) * g


def _proj_kernel(xp_ref, xs_ref, w1_ref, qg_ref, kvg_ref, wq_ref, wqr_ref, wuk_ref, wuvt_ref,
                 cos_ref, sin_ref, perm4_ref, perm16_ref,
                 qa_ref, ka_ref, va_ref, q4_ref, k4_ref, v4_ref, q16_ref, k16_ref, v16_ref,
                 qb_ref, kb_ref, vbt_ref, *, n_prompt_tiles):
    i = pl.program_id(0)
    tm = xp_ref.shape[0]
    x = jnp.where(i < n_prompt_tiles, xp_ref[...], xs_ref[...]).astype(BF16)
    p = jnp.dot(x, w1_ref[...], preferred_element_type=F32)
    qa = (p[:, 0:A_WIDTH] * (A_HEAD_DIM ** -0.5 * LOG2E)).astype(BF16)
    ka = p[:, A_WIDTH:2 * A_WIDTH].astype(BF16)
    va = p[:, 2 * A_WIDTH:3 * A_WIDTH].astype(BF16)
    qa_ref[...] = qa
    ka_ref[...] = ka
    va_ref[...] = va
    qkv = jnp.concatenate([qa, ka, va], axis=1)
    for dil, perm_ref, outs in ((4, perm4_ref, (q4_ref, k4_ref, v4_ref)),
                                (16, perm16_ref, (q16_ref, k16_ref, v16_ref))):
        cm = jnp.dot(perm_ref[...], qkv, preferred_element_type=F32).astype(BF16)
        rows = tm // dil
        for r in range(dil):
            for j, o_ref in enumerate(outs):
                o_ref[:, r * A_WIDTH:(r + 1) * A_WIDTH] = cm[r * rows:(r + 1) * rows,
                                                             j * A_WIDTH:(j + 1) * A_WIDTH]
    c0 = 3 * A_WIDTH
    cq = p[:, c0:c0 + Q_LORA]
    ckv = p[:, c0 + Q_LORA:c0 + Q_LORA + KV_LORA]
    kr = p[:, c0 + Q_LORA + KV_LORA:c0 + Q_LORA + KV_LORA + LANES]
    krr = p[:, c0 + Q_LORA + KV_LORA + LANES:c0 + Q_LORA + KV_LORA + 2 * LANES]
    cos = cos_ref[...]
    sin = sin_ref[...]
    cos8 = jnp.concatenate([cos] * B_HEADS, axis=1)
    sin8 = jnp.concatenate([sin] * B_HEADS, axis=1)
    cqn = _rms_norm(cq, qg_ref[...]).astype(BF16)
    q = jnp.dot(cqn, wq_ref[...], preferred_element_type=F32)
    qr = jnp.dot(cqn, wqr_ref[...], preferred_element_type=F32)
    qscale = (B_NOPE + B_ROPE) ** -0.5 * LOG2E
    qb_ref[...] = ((q * cos8 + qr * sin8) * qscale).astype(BF16)
    ckvn = _rms_norm(ckv, kvg_ref[...]).astype(BF16)
    kn = jnp.dot(ckvn, wuk_ref[...], preferred_element_type=F32)
    krope = kr * cos + krr * sin
    kb_ref[...] = (kn + jnp.concatenate([krope] * B_HEADS, axis=1)).astype(BF16)
    vbt_ref[...] = _nt_dot(wuvt_ref[...], ckvn).astype(BF16)


def _class_perm(tm, dil):
    rows = tm // dil
    c = np.arange(tm)
    src = (c % rows) * dil + c // rows
    perm = np.zeros((tm, tm), np.float32)
    perm[c, src] = 1.0
    return jnp.asarray(perm, BF16)


def _proj(xp, xs, w1, qg, kvg, wq, wqr, wuk, wuvt, cos_t, sin_t, s_prompt, s_sample):
    tm = TM_PROJ
    n1 = xp.shape[0] // tm
    n2 = xs.shape[0] // tm
    t_all = xp.shape[0] + xs.shape[0]
    pt, st = s_prompt // tm, s_sample // tm
    assert tm == PERM_GROUP
    perm4, perm16 = _class_perm(tm, 4), _class_perm(tm, 16)

    def tab_idx(i):
        return (jnp.where(i < n1, i % pt, (i - n1) % st), 0)

    full = lambda shape: pl.BlockSpec(shape, lambda i: (0, 0))
    row = lambda cols: pl.BlockSpec((tm, cols), lambda i: (i, 0))
    cls = lambda dil: pl.BlockSpec((tm // dil, dil * A_WIDTH), lambda i: (i, 0))
    cls_shape = lambda dil: jax.ShapeDtypeStruct((t_all // dil, dil * A_WIDTH), BF16)
    return pl.pallas_call(
        functools.partial(_proj_kernel, n_prompt_tiles=n1),
        grid=(n1 + n2,),
        in_specs=[
            pl.BlockSpec((tm, D_MODEL), lambda i: (jnp.minimum(i, n1 - 1), 0)),
            pl.BlockSpec((tm, D_MODEL), lambda i: (jnp.maximum(i - n1, 0), 0)),
            full(w1.shape), full(qg.shape), full(kvg.shape), full(wq.shape), full(wqr.shape),
            full(wuk.shape), full(wuvt.shape),
            pl.BlockSpec((tm, LANES), tab_idx), pl.BlockSpec((tm, LANES), tab_idx),
            full(perm4.shape), full(perm16.shape),
        ],
        out_specs=[row(A_WIDTH)] * 3 + [cls(4)] * 3 + [cls(16)] * 3
        + [row(B_HEADS * HEAD_PAD), row(B_HEADS * HEAD_PAD),
           pl.BlockSpec((B_HEADS * B_V, tm), lambda i: (0, i))],
        out_shape=[jax.ShapeDtypeStruct((t_all, A_WIDTH), BF16)] * 3
        + [cls_shape(4)] * 3 + [cls_shape(16)] * 3
        + [jax.ShapeDtypeStruct((t_all, B_HEADS * HEAD_PAD), BF16)] * 2
        + [jax.ShapeDtypeStruct((B_HEADS * B_V, t_all), BF16)],
        compiler_params=pltpu.CompilerParams(dimension_semantics=("arbitrary",),
                                             vmem_limit_bytes=VMEM_LIMIT),
        name="proj",
    )(xp, xs, w1, qg, kvg, wq, wqr, wuk, wuvt, cos_t, sin_t, perm4, perm16)


def _attn_a_kernel(qblk_ref, pblk_ref, nblk_ref, tstart_ref, llen_ref,
                   q_ref, kp_ref, kc_ref, kn_ref, vp_ref, vc_ref, vn_ref, bias_ref, perm_ref,
                   o_ref, lse_ref, cm_ref, *, n_pos, n_cls):
    n = pl.program_id(0)
    tq, wk = TQ_A, TQ_A + 2 * A_HALF
    lane = lax.broadcasted_iota(I32, (tq, LANES), 1)
    low = lane < A_HEAD_DIM
    colbase = lax.broadcasted_iota(I32, (1, wk), 1) + (tstart_ref[n] - A_HALF)
    cls_len = llen_ref[n]

    def one_class(r):
        cols = slice(0, A_WIDTH) if n_cls == 1 else pl.ds(pl.multiple_of(r * A_WIDTH, A_WIDTH), A_WIDTH)
        kcat = jnp.concatenate([kp_ref[:, cols], kc_ref[:, cols], kn_ref[:, cols]], axis=0)
        vcat = jnp.concatenate([vp_ref[:, cols], vc_ref[:, cols], vn_ref[:, cols]], axis=0)
        for j in range(n_pos):
            rows = slice(j * tq, (j + 1) * tq)
            q = q_ref[rows, cols]
            kw, vw = kcat[j * tq:j * tq + wk, :], vcat[j * tq:j * tq + wk, :]
            col = colbase + j * tq
            colpen = jnp.where((col >= 0) & (col < cls_len), 0.0, NEG_BIG).astype(F32)
            pairs = []
            lse_c = jnp.zeros((tq, LANES), F32)
            for jp in range(A_HEADS // 2):
                sl = slice(jp * LANES, (jp + 1) * LANES)
                qp, kpair, vpair = q[:, sl], kw[:, sl], vw[:, sl]
                outs = []
                for e in range(2):
                    qm = jnp.where(low if e == 0 else ~low, qp, jnp.zeros_like(qp))
                    s = _nt_dot(qm, kpair) + bias_ref[2 * jp + e] + colpen
                    m = jnp.max(s, axis=1, keepdims=True)
                    p = jnp.exp2(s - m)
                    l = jnp.sum(p, axis=1, keepdims=True)
                    outs.append(jnp.dot(p.astype(BF16), vpair, preferred_element_type=F32) / l)
                    lse_c = lse_c + jnp.where(lane == 2 * jp + e, m + jnp.log2(l), 0.0)
                pairs.append(jnp.where(low, outs[0], outs[1]))
            o_full = jnp.concatenate(pairs, axis=1).astype(BF16)
            if n_cls == 1:
                o_ref[rows, :] = o_full
                lse_ref[rows, :] = lse_c
            else:
                hi = lse_c.astype(BF16)
                rest = lse_c - hi.astype(F32)
                mid = rest.astype(BF16)
                lo = (rest - mid.astype(F32)).astype(BF16)
                cm_ref[r, rows, :] = jnp.concatenate([o_full, hi, mid, lo], axis=1)

    if n_cls == 1:
        one_class(0)
        return

    def body(r, carry):
        one_class(r)
        return carry

    lax.fori_loop(0, n_cls, body, 0)
    per = PERM_GROUP // n_cls
    for a in range(n_pos * tq * n_cls // PERM_GROUP):
        stack = jnp.concatenate([cm_ref[r, a * per:(a + 1) * per, :] for r in range(n_cls)], axis=0)
        nat = jnp.dot(perm_ref[...], stack, preferred_element_type=F32)
        rows = slice(a * PERM_GROUP, (a + 1) * PERM_GROUP)
        o_ref[rows, :] = nat[:, :A_WIDTH].astype(BF16)
        lse_ref[rows, :] = (nat[:, A_WIDTH:A_WIDTH + LANES] + nat[:, A_WIDTH + LANES:A_WIDTH + 2 * LANES]
                            + nat[:, A_WIDTH + 2 * LANES:])


def _attn_a_tables(seqs, dil, n_pos):
    rows = n_pos * TQ_A
    per_halo = rows // A_HALF
    qblk, pblk, nblk, tstart, llen = [], [], [], [], []
    for off, s_len in seqs:
        cls = s_len // dil
        steps = cls // rows
        base = (off // dil) // rows
        for i in range(steps):
            qblk.append(base + i)
            pblk.append((base + i) * per_halo - (1 if i > 0 else 0))
            nblk.append((base + i + 1) * per_halo - (0 if i < steps - 1 else 1))
            tstart.append(i * rows)
            llen.append(cls)
    return [jnp.asarray(np.asarray(a, np.int32)) for a in (qblk, pblk, nblk, tstart, llen)]


def _attn_a_bias(dil):
    tq, wk = TQ_A, TQ_A + 2 * A_HALF
    delta = np.abs(np.arange(wk)[None, :] - A_HALF - np.arange(tq)[:, None]).astype(np.float64)
    slopes = 2.0 ** (-8.0 * (np.arange(A_HEADS) + 1.0) / A_HEADS)
    bias = -slopes[:, None, None] * (delta * dil)[None] * LOG2E
    bias = np.where((delta <= A_HALF)[None], bias, NEG_BIG)
    return jnp.asarray(bias.astype(np.float32))


def _attn_a(qc, kc, vc, seqs, dil):
    t_all = qc.shape[0] * dil
    n_pos = A_POS_TILES[dil]
    rows, width = n_pos * TQ_A, dil * A_WIDTH
    tabs = _attn_a_tables(seqs, dil, n_pos)
    n_steps = int(tabs[0].shape[0])
    bias = _attn_a_bias(dil)
    perm = _class_perm(PERM_GROUP, dil).T
    cur = pl.BlockSpec((rows, width), lambda n, qb, pb, nb, ts, ll: (qb[n], 0))
    prev = pl.BlockSpec((A_HALF, width), lambda n, qb, pb, nb, ts, ll: (pb[n], 0))
    nxt = pl.BlockSpec((A_HALF, width), lambda n, qb, pb, nb, ts, ll: (nb[n], 0))
    tok = lambda cols: pl.BlockSpec((rows * dil, cols), lambda n, *_: (n, 0))
    return pl.pallas_call(
        functools.partial(_attn_a_kernel, n_pos=n_pos, n_cls=dil),
        grid_spec=pltpu.PrefetchScalarGridSpec(
            num_scalar_prefetch=5, grid=(n_steps,),
            in_specs=[cur, prev, cur, nxt, prev, cur, nxt,
                      pl.BlockSpec(bias.shape, lambda n, *_: (0, 0, 0)),
                      pl.BlockSpec(perm.shape, lambda n, *_: (0, 0))],
            out_specs=[tok(A_WIDTH), tok(LANES)],
            scratch_shapes=[pltpu.VMEM((dil, rows, A_WIDTH + 3 * LANES), BF16)]),
        out_shape=[jax.ShapeDtypeStruct((t_all, A_WIDTH), BF16),
                   jax.ShapeDtypeStruct((t_all, LANES), F32)],
        compiler_params=pltpu.CompilerParams(dimension_semantics=("arbitrary",),
                                             vmem_limit_bytes=VMEM_LIMIT),
        name=f"attn_a_d{dil}",
    )(*tabs, qc, kc, kc, kc, vc, vc, vc, bias, perm)


def _attn_b_kernel(q_ref, k_ref, vt_ref, o_ref, sta_ref, stb_ref, ma_ref, mb_ref):
    s = pl.program_id(0)
    tq, s_len = q_ref.shape[0], k_ref.shape[0]

    @pl.when(s == 0)
    def _():
        stb_ref[...] = jnp.zeros_like(stb_ref)
        mb_ref[...] = jnp.zeros_like(mb_ref)

    def step(st_new_ref, m_new_ref, st_old_ref, m_old_ref):
        m_old = m_old_ref[...]
        q = q_ref[...]
        m_new = jnp.full((1, tq), -jnp.inf, F32)
        l = jnp.zeros((1, tq), F32)
        acc = jnp.zeros((B_V, tq), F32)
        for c in range(s_len // KEY_CHUNK_B):
            ks = slice(c * KEY_CHUNK_B, (c + 1) * KEY_CHUNK_B)
            st_c = _nt_dot(k_ref[ks, :], q)
            st_new_ref[ks, :] = st_c
            m_new = jnp.maximum(m_new, jnp.max(st_c, axis=0, keepdims=True))
            p = jnp.exp2(st_old_ref[ks, :] - m_old)
            l = l + jnp.sum(p, axis=0, keepdims=True)
            acc = acc + jnp.dot(vt_ref[:, ks], p.astype(BF16), preferred_element_type=F32)
        m_new_ref[...] = m_new
        o_ref[...] = (acc / l).astype(BF16)

    @pl.when(s % 2 == 0)
    def _():
        step(sta_ref, ma_ref, stb_ref, mb_ref)

    @pl.when(s % 2 == 1)
    def _():
        step(stb_ref, mb_ref, sta_ref, ma_ref)


def _attn_b(qb, kb, vbt, off, n_batch, s_len):
    tq = min(s_len, LOGITS_ELEMS_B // s_len)
    nq = s_len // tq
    qbase, kbase = off // tq, off // s_len
    n_items = n_batch * B_HEADS * nq

    def split(item):
        return item // (B_HEADS * nq), (item // nq) % B_HEADS, item % nq

    def q_map(s):
        b, h, qi = split(jnp.minimum(s, n_items - 1))
        return (qbase + b * nq + qi, h)

    def k_map(s):
        b, h, _ = split(jnp.minimum(s, n_items - 1))
        return (kbase + b, h)

    def v_map(s):
        b, h, _ = split(jnp.maximum(s - 1, 0))
        return (h, kbase + b)

    def o_map(s):
        b, h, qi = split(jnp.maximum(s - 1, 0))
        return (h, b * nq + qi)

    return pl.pallas_call(
        _attn_b_kernel,
        grid=(n_items + 1,),
        in_specs=[pl.BlockSpec((tq, HEAD_PAD), q_map),
                  pl.BlockSpec((s_len, HEAD_PAD), k_map),
                  pl.BlockSpec((B_V, s_len), v_map)],
        out_specs=pl.BlockSpec((B_V, tq), o_map),
        out_shape=jax.ShapeDtypeStruct((B_HEADS * B_V, n_batch * s_len), BF16),
        scratch_shapes=[pltpu.VMEM((s_len, tq), F32), pltpu.VMEM((s_len, tq), F32),
                        pltpu.VMEM((1, tq), F32), pltpu.VMEM((1, tq), F32)],
        compiler_params=pltpu.CompilerParams(dimension_semantics=("arbitrary",),
                                             vmem_limit_bytes=VMEM_LIMIT),
        name=f"attn_b_s{s_len}",
    )(qb, kb, vbt)


def _mix_kernel(xp_ref, xs_ref, o0_ref, o1_ref, o2_ref, l0_ref, l1_ref, l2_ref, spread_ref,
                obp_ref, obs_ref, wout_ref,
                g_ref, b_ref, wrh_ref, wrl_ref, rb_ref,
                h_ref, hp_ref, tope_ref, pos_ref, gate_ref, cnt_out_ref, cnt_ref, stage_ref,
                *, n_prompt_tiles):
    i = pl.program_id(0)
    tm = xp_ref.shape[0]

    @pl.when(i == 0)
    def _():
        cnt_ref[...] = jnp.zeros_like(cnt_ref)

    l0, l1, l2 = l0_ref[...], l1_ref[...], l2_ref[...]
    lmax = jnp.maximum(jnp.maximum(l0, l1), l2)
    e0, e1, e2 = jnp.exp2(l0 - lmax), jnp.exp2(l1 - lmax), jnp.exp2(l2 - lmax)
    inv = 1.0 / (e0 + e1 + e2)
    spread = spread_ref[...]

    def per_lane(w):
        hi = w.astype(BF16)
        lo = (w - hi.astype(F32)).astype(BF16)
        return (jnp.dot(hi, spread, preferred_element_type=F32)
                + jnp.dot(lo, spread, preferred_element_type=F32))

    oa = (per_lane(e0 * inv) * o0_ref[...].astype(F32) + per_lane(e1 * inv) * o1_ref[...].astype(F32)
          + per_lane(e2 * inv) * o2_ref[...].astype(F32))
    is_prompt = i < n_prompt_tiles
    obt = jnp.where(is_prompt, obp_ref[...], obs_ref[...])
    mix = (jnp.dot(oa.astype(BF16), wout_ref[0:A_WIDTH, :], preferred_element_type=F32)
           + lax.dot_general(obt, wout_ref[A_WIDTH:, :], (((0,), (0,)), ((), ())),
                             preferred_element_type=F32))
    x = jnp.where(is_prompt, xp_ref[...], xs_ref[...])
    h = _layer_norm(ALPHA * x + mix, g_ref[...], b_ref[...])
    h_ref[...] = h
    _slab_store_bf16(hp_ref, stage_ref, h)

    h_hi = h.astype(BF16)
    h_lo = (h - h_hi.astype(F32)).astype(BF16)
    wrh = wrh_ref[...]
    logits = _nt_dot(wrh, h_hi) + _nt_dot(wrh, h_lo) + _nt_dot(wrl_ref[...], h_hi)
    scores = jax.nn.sigmoid(logits)
    sel = scores + rb_ref[...]

    sub = lax.broadcasted_iota(I32, (GROUP_SIZE, tm), 0).astype(F32)
    gscore = []
    for g in range(N_GROUPS):
        sg = sel[g * GROUP_SIZE:(g + 1) * GROUP_SIZE, :]
        m1 = jnp.max(sg, axis=0, keepdims=True)
        first = jnp.min(jnp.where(sg == m1, sub, float(GROUP_SIZE)), axis=0, keepdims=True)
        m2 = jnp.max(jnp.where(sub == first, -jnp.inf, sg), axis=0, keepdims=True)
        gscore.append(m1 + m2)
    cands = []
    for g in range(N_GROUPS):
        beaten = jnp.zeros((1, tm), F32)
        for g2 in range(N_GROUPS):
            if g2 == g:
                continue
            wins = (gscore[g2] > gscore[g]) | ((gscore[g2] == gscore[g]) & (g2 < g))
            beaten = beaten + wins.astype(F32)
        keep = beaten < float(TOPK_GROUPS)
        sg = sel[g * GROUP_SIZE:(g + 1) * GROUP_SIZE, :]
        cands.append(jnp.where(keep, sg, NEG_BIG))
    cand = jnp.concatenate(cands, axis=0)

    eidx = lax.broadcasted_iota(I32, (N_EXPERTS, tm), 0).astype(F32)
    picked_idx, picked_gate = [], []
    onehot = jnp.zeros((N_EXPERTS, tm), F32)
    for _ in range(TOP_K):
        mx = jnp.max(cand, axis=0, keepdims=True)
        fi = jnp.min(jnp.where(cand == mx, eidx, float(N_EXPERTS)), axis=0, keepdims=True)
        pick = eidx == fi
        picked_idx.append(fi)
        picked_gate.append(jnp.sum(jnp.where(pick, scores, 0.0), axis=0, keepdims=True))
        onehot = onehot + pick.astype(F32)
        cand = jnp.where(pick, -jnp.inf, cand)
    gsum = picked_gate[0]
    for k in range(1, TOP_K):
        gsum = gsum + picked_gate[k]

    tri = (lax.broadcasted_iota(I32, (tm, tm), 0) < lax.broadcasted_iota(I32, (tm, tm), 1))
    before = jnp.dot(onehot.astype(BF16), tri.astype(BF16), preferred_element_type=F32)
    rank = before + cnt_ref[:, 0:1]
    for k in range(TOP_K):
        pick = eidx == picked_idx[k]
        tope_ref[k:k + 1, :] = picked_idx[k].astype(I32)
        pos_ref[k:k + 1, :] = jnp.sum(jnp.where(pick, rank, 0.0), axis=0, keepdims=True).astype(I32)
        gate_ref[k:k + 1, :] = picked_gate[k] / gsum * ROUTED_SCALE
    cnt_ref[...] = cnt_ref[...] + jnp.sum(onehot, axis=1, keepdims=True)
    cnt_out_ref[...] = cnt_ref[...]


def _mix(xp, xs, outs, lses, obp, obs, wout, g, b, wrh, wrl, rb):
    tm = TM_MIX
    n1 = xp.shape[0] // tm
    n2 = xs.shape[0] // tm
    t_all = xp.shape[0] + xs.shape[0]
    full = lambda a: pl.BlockSpec(a.shape, lambda i: (0,) * a.ndim)
    row = lambda cols: pl.BlockSpec((tm, cols), lambda i: (i, 0))
    prow = lambda cols: pl.BlockSpec((tm, cols), lambda i: (jnp.minimum(i, n1 - 1), 0))
    srow = lambda cols: pl.BlockSpec((tm, cols), lambda i: (jnp.maximum(i - n1, 0), 0))
    col = pl.BlockSpec((TOP_K, tm), lambda i: (0, i))
    head_of_lane = np.arange(A_WIDTH) // A_HEAD_DIM
    spread = jnp.asarray(np.arange(LANES)[:, None] == head_of_lane[None, :], BF16)
    return pl.pallas_call(
        functools.partial(_mix_kernel, n_prompt_tiles=n1),
        grid=(n1 + n2,),
        in_specs=[
            prow(D_MODEL), srow(D_MODEL),
            row(A_WIDTH), row(A_WIDTH), row(A_WIDTH), row(LANES), row(LANES), row(LANES), full(spread),
            pl.BlockSpec((B_HEADS * B_V, tm), lambda i: (0, jnp.minimum(i, n1 - 1))),
            pl.BlockSpec((B_HEADS * B_V, tm), lambda i: (0, jnp.maximum(i - n1, 0))),
            full(wout), full(g), full(b), full(wrh), full(wrl), full(rb),
        ],
        out_specs=[row(D_MODEL), pl.BlockSpec((SLAB * tm, LANES), lambda i: (i, 0)), col, col, col,
                   pl.BlockSpec((N_EXPERTS, LANES), lambda i: (0, 0))],
        out_shape=[jax.ShapeDtypeStruct((t_all, D_MODEL), F32),
                   jax.ShapeDtypeStruct((SLAB * t_all, LANES), BF16),
                   jax.ShapeDtypeStruct((TOP_K, t_all), I32),
                   jax.ShapeDtypeStruct((TOP_K, t_all), I32),
                   jax.ShapeDtypeStruct((TOP_K, t_all), F32),
                   jax.ShapeDtypeStruct((N_EXPERTS, LANES), F32)],
        scratch_shapes=[pltpu.VMEM((N_EXPERTS, LANES), F32), pltpu.VMEM((SLAB * tm, LANES), F32)],
        compiler_params=pltpu.CompilerParams(dimension_semantics=("arbitrary",),
                                             vmem_limit_bytes=VMEM_LIMIT),
        name="mix_router",
    )(xp, xs, *outs, *lses, spread, obp, obs, wout, g, b, wrh, wrl, rb)


def _dispatch_kernel(zstart_ref, nused_ref, h_ref, tope_ref, pos_ref, pstart_ref, buf_ref, dest_ref,
                     dsm_ref, zero_ref, sem_ref, zsem_ref):
    i = pl.program_id(0)
    tm = tope_ref.shape[1]
    tile_rows = SLAB * ROW_BLOCK

    @pl.when(i == 0)
    def _():
        zero_ref[...] = jnp.zeros_like(zero_ref)

        def zfill(e, carry):
            start = pl.multiple_of(zstart_ref[e], SLAB)
            pltpu.make_async_copy(zero_ref, buf_ref.at[pl.ds(start, tile_rows), :], zsem_ref).start()
            return carry

        lax.fori_loop(0, N_EXPERTS, zfill, 0)
        span = buf_ref.at[pl.ds(0, N_EXPERTS * tile_rows), :]
        pltpu.make_async_copy(span, span, zsem_ref).wait()

        def ztail(j, carry):
            start = pl.multiple_of(j * tile_rows, tile_rows)
            pltpu.make_async_copy(zero_ref, buf_ref.at[pl.ds(start, tile_rows), :], zsem_ref).start()
            return carry

        def zwait(j, carry):
            pltpu.make_async_copy(zero_ref, buf_ref.at[pl.ds(0, tile_rows), :], zsem_ref).wait()
            return carry

        n_tiles = buf_ref.shape[0] // tile_rows
        lax.fori_loop(nused_ref[0], n_tiles, ztail, 0)
        lax.fori_loop(nused_ref[0], n_tiles, zwait, 0)

    eidx = lax.broadcasted_iota(I32, (N_EXPERTS, tm), 0)
    pstart = pstart_ref[...]
    for k in range(TOP_K):
        hit = eidx == tope_ref[k:k + 1, :]
        base = jnp.sum(jnp.where(hit, pstart, 0.0), axis=0, keepdims=True)
        dest_ref[k:k + 1, :] = (pos_ref[k:k + 1, :] + base.astype(I32)) * SLAB
    pltpu.sync_copy(dest_ref, dsm_ref)

    def scatter(t, carry):
        src = h_ref.at[pl.ds(pl.multiple_of(t * SLAB, SLAB), SLAB), :]
        for k in range(TOP_K):
            dst = buf_ref.at[pl.ds(pl.multiple_of(dsm_ref[k, t], SLAB), SLAB), :]
            pltpu.make_async_copy(src, dst, sem_ref).start(priority=k % 2)
        return carry

    lax.fori_loop(0, tm, scatter, 0, unroll=8)
    span = buf_ref.at[pl.ds(0, TOP_K * tm * SLAB), :]
    pltpu.make_async_copy(span, span, sem_ref).wait()


def _dispatch(h, tope, pos, pstart, zstart, nused, n_rows):
    tm = TM_DISP
    t_all = tope.shape[1]
    col = pl.BlockSpec((TOP_K, tm), lambda i, z, nu: (0, i))
    return pl.pallas_call(
        _dispatch_kernel,
        grid_spec=pltpu.PrefetchScalarGridSpec(
            num_scalar_prefetch=2, grid=(t_all // tm,),
            in_specs=[pl.BlockSpec((SLAB * tm, LANES), lambda i, z, nu: (i, 0)), col, col,
                      pl.BlockSpec((N_EXPERTS, 1), lambda i, z, nu: (0, 0))],
            out_specs=[pl.BlockSpec(memory_space=pl.ANY), col],
            scratch_shapes=[pltpu.SMEM((TOP_K, tm), I32),
                            pltpu.VMEM((SLAB * ROW_BLOCK, LANES), BF16),
                            pltpu.SemaphoreType.DMA, pltpu.SemaphoreType.DMA]),
        out_shape=[jax.ShapeDtypeStruct((SLAB * n_rows, LANES), BF16),
                   jax.ShapeDtypeStruct((TOP_K, t_all), I32)],
        compiler_params=pltpu.CompilerParams(dimension_semantics=("arbitrary",),
                                             vmem_limit_bytes=VMEM_LIMIT),
        name="dispatch",
    )(zstart, nused, h, tope, pos, pstart)


def _expert_kernel(tfirst_ref, ntile_ref, nused_ref, x_hbm, wg_ref, wu_ref, wd_ref, o_hbm,
                   xbuf, obuf, stage_ref, wgb_ref, wub_ref, wdb_ref, xsem, osem):
    e = pl.program_id(0)
    tile_rows = SLAB * ROW_BLOCK
    nused = nused_ref[0]

    def x_copy(g, slot):
        start = pl.multiple_of(g * tile_rows, tile_rows)
        return pltpu.make_async_copy(x_hbm.at[pl.ds(start, tile_rows), :], xbuf.at[slot], xsem.at[slot])

    def o_copy(g, slot):
        start = pl.multiple_of(g * tile_rows, tile_rows)
        return pltpu.make_async_copy(obuf.at[slot], o_hbm.at[pl.ds(start, tile_rows), :], osem.at[slot])

    @pl.when(e == 0)
    def _():
        for j in range(N_XBUF - 2):
            @pl.when(j < nused)
            def _():
                x_copy(j, j).start()

    n_e = ntile_ref[e]

    @pl.when(n_e > 0)
    def _():
        wgb_ref[...] = wg_ref[...].astype(BF16)
        wub_ref[...] = wu_ref[...].astype(BF16)
        wdb_ref[...] = wd_ref[...].astype(BF16)

    def run_tiles(g0, n):
        for q in range(n):
            ahead = g0 + q + (N_XBUF - 2)

            @pl.when(ahead < nused)
            def _():
                x_copy(ahead, ahead % N_XBUF).start()

        for q in range(n):
            g = g0 + q
            x_copy(g, g % N_XBUF).wait()

            @pl.when(g >= N_OBUF)
            def _():
                o_copy(g - N_OBUF, g % N_OBUF).wait()

        for q in range(n):
            g = g0 + q
            x = _slab_load_bf16(xbuf.at[g % N_XBUF], stage_ref.at[q], ROW_BLOCK).astype(BF16)
            gt = jnp.dot(x, wgb_ref[...], preferred_element_type=F32)
            up = jnp.dot(x, wub_ref[...], preferred_element_type=F32)
            hmid = (gt * jax.nn.sigmoid(gt) * up).astype(BF16)
            out = jnp.dot(hmid, wdb_ref[...], preferred_element_type=F32)
            _slab_store(obuf.at[g % N_OBUF], out)

        for q in range(n):
            g = g0 + q
            o_copy(g, g % N_OBUF).start()

    def pair(j, carry):
        run_tiles(tfirst_ref[e] + 2 * j, 2)
        return carry

    lax.fori_loop(0, n_e // 2, pair, 0)

    @pl.when(n_e % 2 == 1)
    def _():
        run_tiles(tfirst_ref[e] + n_e - 1, 1)

    @pl.when(e == N_EXPERTS - 1)
    def _():
        for j in range(N_OBUF):
            @pl.when(nused > j)
            def _():
                o_copy(nused - 1 - j, (nused - 1 - j) % N_OBUF).wait()

        zero_ref = stage_ref.at[0]
        zero_ref[...] = jnp.zeros_like(zero_ref)
        n_tiles = o_hbm.shape[0] // tile_rows

        def tail_copy(g):
            start = pl.multiple_of(g * tile_rows, tile_rows)
            return pltpu.make_async_copy(zero_ref, o_hbm.at[pl.ds(start, tile_rows), :], osem.at[0])

        def ztail(g, carry):
            tail_copy(g).start()
            return carry

        def zwait(g, carry):
            tail_copy(g).wait()
            return carry

        lax.fori_loop(nused, n_tiles, ztail, 0)
        lax.fori_loop(nused, n_tiles, zwait, 0)


def _experts(buf, tfirst, ntile, nused, w_gate, w_up, w_down):
    tile_rows = SLAB * ROW_BLOCK
    wmap = lambda e, tf, nt, nu: (e, 0, 0)
    return pl.pallas_call(
        _expert_kernel,
        grid_spec=pltpu.PrefetchScalarGridSpec(
            num_scalar_prefetch=3, grid=(N_EXPERTS,),
            in_specs=[pl.BlockSpec(memory_space=pl.ANY),
                      pl.BlockSpec((None, D_MODEL, EXPERT_HIDDEN), wmap),
                      pl.BlockSpec((None, D_MODEL, EXPERT_HIDDEN), wmap),
                      pl.BlockSpec((None, EXPERT_HIDDEN, D_MODEL), wmap)],
            out_specs=pl.BlockSpec(memory_space=pl.ANY),
            scratch_shapes=[pltpu.VMEM((N_XBUF, tile_rows, LANES), BF16),
                            pltpu.VMEM((N_OBUF, tile_rows, LANES), F32),
                            pltpu.VMEM((2, tile_rows, LANES), F32),
                            pltpu.VMEM((D_MODEL, EXPERT_HIDDEN), BF16),
                            pltpu.VMEM((D_MODEL, EXPERT_HIDDEN), BF16),
                            pltpu.VMEM((EXPERT_HIDDEN, D_MODEL), BF16),
                            pltpu.SemaphoreType.DMA((N_XBUF,)),
                            pltpu.SemaphoreType.DMA((N_OBUF,))]),
        out_shape=jax.ShapeDtypeStruct(buf.shape, F32),
        compiler_params=pltpu.CompilerParams(dimension_semantics=("arbitrary",),
                                             vmem_limit_bytes=VMEM_LIMIT),
        name="experts",
    )(tfirst, ntile, nused, buf, w_gate, w_up, w_down)


def _combine_kernel(h_ref, gate_ref, dest_ref, dest1_ref, dest2_ref, eo_ref, wsg_ref, wsu_ref, wsd_ref,
                    g_ref, b_ref, yp_ref, ys_ref, dsm_ref, rows_ref, sem_ref, *, n_prompt_tiles, n_tiles):
    i = pl.program_id(0)
    tm = gate_ref.shape[0]
    slot = i % 2
    rows = rows_ref.at[slot]
    nxt_rows = rows_ref.at[1 - slot]
    nxt_sem = sem_ref.at[1 - slot]

    @pl.when(i == 0)
    def _():
        pltpu.sync_copy(dest_ref, dsm_ref)

        def gather(t, carry):
            for k in range(TOP_K):
                src = eo_ref.at[pl.ds(pl.multiple_of(dsm_ref[k, t], SLAB), SLAB), :]
                dst = rows_ref.at[0, pl.ds(pl.multiple_of((k * tm + t) * SLAB, SLAB), SLAB), :]
                pltpu.make_async_copy(src, dst, sem_ref.at[0]).start(priority=k % 2)
            return carry

        lax.fori_loop(0, tm, gather, 0, unroll=8)
        pltpu.sync_copy(dest1_ref, dsm_ref)

    def issue_next(k):
        for t in range(tm):
            src = eo_ref.at[pl.ds(pl.multiple_of(dsm_ref[k, t], SLAB), SLAB), :]
            dst = nxt_rows.at[pl.ds((k * tm + t) * SLAB, SLAB), :]
            pltpu.make_async_copy(src, dst, nxt_sem).start(priority=t % 2)

    h = h_ref[...]
    hb = h.astype(BF16)
    sg = jnp.dot(hb, wsg_ref[...], preferred_element_type=F32)
    su = jnp.dot(hb, wsu_ref[...], preferred_element_type=F32)
    shared = jnp.dot((sg * jax.nn.sigmoid(sg) * su).astype(BF16), wsd_ref[...],
                     preferred_element_type=F32)

    issue_next(0)
    whole = eo_ref.at[pl.ds(0, TOP_K * tm * SLAB), :]
    pltpu.make_async_copy(whole, rows, sem_ref.at[slot]).wait()
    gate = gate_ref[...]
    routed = gate[:, 0:1] * _slab_load(rows, tm)
    for k in range(1, TOP_K):
        issue_next(k)
        routed = routed + gate[:, k:k + 1] * _slab_load(rows, tm, k * tm)
    y = _layer_norm(ALPHA * h + (routed + shared), g_ref[...], b_ref[...])

    @pl.when(i < n_prompt_tiles)
    def _():
        yp_ref[...] = y

    @pl.when(i >= n_prompt_tiles)
    def _():
        ys_ref[...] = y

    pltpu.sync_copy(dest2_ref, dsm_ref)

    @pl.when(i == n_tiles - 1)
    def _():
        pltpu.make_async_copy(whole, nxt_rows, nxt_sem).wait()


def _combine(h, gate_t, dest, eo, wsg, wsu, wsd, g, b, t_prompt):
    tm = TM_COMB
    t_all = gate_t.shape[0]
    n1 = t_prompt // tm
    n2 = (t_all - t_prompt) // tm
    full = lambda a: pl.BlockSpec(a.shape, lambda i: (0,) * a.ndim)
    return pl.pallas_call(
        functools.partial(_combine_kernel, n_prompt_tiles=n1, n_tiles=n1 + n2),
        grid=(n1 + n2,),
        in_specs=[pl.BlockSpec((tm, D_MODEL), lambda i: (i, 0)),
                  pl.BlockSpec((tm, TOP_K), lambda i: (i, 0)),
                  pl.BlockSpec((TOP_K, tm), lambda i: (0, i)),
                  pl.BlockSpec((TOP_K, tm), lambda i: (0, jnp.minimum(i + 1, n1 + n2 - 1))),
                  pl.BlockSpec((TOP_K, tm), lambda i: (0, jnp.minimum(i + 2, n1 + n2 - 1))),
                  pl.BlockSpec(memory_space=pl.ANY),
                  full(wsg), full(wsu), full(wsd), full(g), full(b)],
        out_specs=[pl.BlockSpec((tm, D_MODEL), lambda i: (jnp.minimum(i, n1 - 1), 0)),
                   pl.BlockSpec((tm, D_MODEL), lambda i: (jnp.maximum(i - n1, 0), 0))],
        out_shape=[jax.ShapeDtypeStruct((t_prompt, D_MODEL), F32),
                   jax.ShapeDtypeStruct((t_all - t_prompt, D_MODEL), F32)],
        scratch_shapes=[pltpu.SMEM((TOP_K, tm), I32),
                        pltpu.VMEM((2, TOP_K * tm * SLAB, LANES), F32),
                        pltpu.SemaphoreType.DMA((2,))],
        compiler_params=pltpu.CompilerParams(dimension_semantics=("arbitrary",),
                                             vmem_limit_bytes=VMEM_LIMIT),
        name="combine",
    )(h, gate_t, dest, dest, dest, eo, wsg, wsu, wsd, g, b)


def _rope_tables(s_max):
    inv_freq = ROPE_BASE ** (-jnp.arange(0, B_ROPE, 2, dtype=F32) / B_ROPE)
    ang = jnp.arange(s_max, dtype=F32)[:, None] * inv_freq[None, :]
    cos, sin = jnp.cos(ang), jnp.sin(ang)
    ones = jnp.ones((s_max, B_NOPE), F32)
    zeros_n = jnp.zeros((s_max, B_NOPE), F32)
    zeros_p = jnp.zeros((s_max, HEAD_PAD - B_NOPE - B_ROPE), F32)
    return (jnp.concatenate([ones, cos, cos, zeros_p], axis=1),
            jnp.concatenate([zeros_n, sin, sin, zeros_p], axis=1))


def _rot_cols(w):
    half = B_ROPE // 2
    return jnp.concatenate([-w[..., half:], w[..., :half]], axis=-1)


def _layout_weights(w_in, w_uq, w_uk):
    c_kr = 3 * A_WIDTH + Q_LORA + KV_LORA
    w_kr = w_in[:, c_kr:c_kr + B_ROPE]
    pad_l = jnp.zeros((D_MODEL, B_NOPE), F32)
    pad_r = jnp.zeros((D_MODEL, HEAD_PAD - B_NOPE - B_ROPE), F32)
    w1 = jnp.concatenate([w_in[:, :c_kr], pad_l, w_kr, pad_r, pad_l, _rot_cols(w_kr), pad_r], axis=1)
    wq3 = w_uq.reshape(Q_LORA, B_HEADS, B_NOPE + B_ROPE)
    nope, rope = wq3[..., :B_NOPE], wq3[..., B_NOPE:]
    zpad = jnp.zeros((Q_LORA, B_HEADS, HEAD_PAD - B_NOPE - B_ROPE), F32)
    wq = jnp.concatenate([nope, rope, zpad], axis=-1).reshape(Q_LORA, B_HEADS * HEAD_PAD)
    wqr = jnp.concatenate([jnp.zeros_like(nope), _rot_cols(rope), zpad], axis=-1)
    wqr = wqr.reshape(Q_LORA, B_HEADS * HEAD_PAD)
    wk3 = w_uk.reshape(KV_LORA, B_HEADS, B_NOPE)
    wuk = jnp.concatenate([wk3, jnp.zeros((KV_LORA, B_HEADS, HEAD_PAD - B_NOPE), F32)], axis=-1)
    wuk = wuk.reshape(KV_LORA, B_HEADS * HEAD_PAD)
    return w1.astype(BF16), wq.astype(BF16), wqr.astype(BF16), wuk.astype(BF16)


def _forward(x_prompt, x_sample, w_in, w_out, ln1_g, ln1_b, q_norm_g, w_uq, kv_norm_g, w_uk, w_uv,
             w_router, router_bias, w_gate, w_up, w_down, ws_gate, ws_up, ws_down, ln2_g, ln2_b):
    b1, s1, _ = x_prompt.shape
    b2, s2, _ = x_sample.shape
    t1, t2 = b1 * s1, b2 * s2
    t_all = t1 + t2
    xp = x_prompt.reshape(t1, D_MODEL)
    xs = x_sample.reshape(t2, D_MODEL)
    seqs = [(b * s1, s1) for b in range(b1)] + [(t1 + b * s2, s2) for b in range(b2)]
    assert t1 % s2 == 0 and s1 % (TQ_A * 16) == 0 and s2 % (TQ_A * 16) == 0

    w1, wq, wqr, wuk = _layout_weights(w_in, w_uq, w_uk)
    cos_t, sin_t = _rope_tables(max(s1, s2))
    (qa, ka, va, q4, k4, v4, q16, k16, v16, qb, kb, vbt) = _proj(
        xp, xs, w1, q_norm_g.reshape(1, -1), kv_norm_g.reshape(1, -1), wq, wqr, wuk,
        w_uv.T.astype(BF16), cos_t, sin_t, s1, s2)

    outs, lses = [], []
    for (_, dil), qkv in zip(A_PATTERNS, ((qa, ka, va), (q4, k4, v4), (q16, k16, v16))):
        o, lse = _attn_a(*qkv, seqs, dil)
        outs.append(o)
        lses.append(lse)
    obp = _attn_b(qb, kb, vbt, 0, b1, s1)
    obs = _attn_b(qb, kb, vbt, t1, b2, s2)

    wr_t = w_router.T
    wr_hi = wr_t.astype(BF16)
    wr_lo = (wr_t - wr_hi.astype(F32)).astype(BF16)
    h, h_slab, tope, pos, gate, counts = _mix(xp, xs, outs, lses, obp, obs, w_out.astype(BF16),
                                                ln1_g.reshape(1, -1), ln1_b.reshape(1, -1),
                                                wr_hi, wr_lo, router_bias.reshape(-1, 1))

    cnt = counts[:, 0].astype(I32)
    padded = ((cnt + ROW_BLOCK - 1) // ROW_BLOCK) * ROW_BLOCK
    pend = jnp.cumsum(padded)
    pstart = pend - padded
    n_tiles = (t_all * TOP_K) // ROW_BLOCK + N_EXPERTS + 1
    nused = (pend[-1:] // ROW_BLOCK).astype(I32)

    buf, dest = _dispatch(h_slab, tope, pos, pstart.astype(F32).reshape(-1, 1),
                          ((pstart + cnt) * SLAB).astype(I32), nused, n_tiles * ROW_BLOCK)
    eo = _experts(buf, (pstart // ROW_BLOCK).astype(I32), (padded // ROW_BLOCK).astype(I32), nused,
                  w_gate, w_up, w_down)
    yp, ys = _combine(h, gate.T, dest, eo, ws_gate.astype(BF16), ws_up.astype(BF16),
                      ws_down.astype(BF16), ln2_g.reshape(1, -1), ln2_b.reshape(1, -1), t1)
    return yp.reshape(b1, s1, D_MODEL), ys.reshape(b2, s2, D_MODEL)


def kernel(x_prompt, x_sample, w_in, w_out, ln1_g, ln1_b, q_norm_g, w_uq, kv_norm_g, w_uk, w_uv,
           w_router, router_bias, w_gate, w_up, w_down, ws_gate, ws_up, ws_down, ln2_g, ln2_b):
    params = (w_in, w_out, ln1_g, ln1_b, q_norm_g, w_uq, kv_norm_g, w_uk, w_uv, w_router, router_bias,
              w_gate, w_up, w_down, ws_gate, ws_up, ws_down, ln2_g, ln2_b)
    assert all(p.shape[0] == 1 for p in params), "one encoder layer"
    return _forward(x_prompt, x_sample, *[p.reshape(p.shape[1:]) for p in params])
```

```python
import functools
import math

import numpy as np
import jax
import jax.numpy as jnp
from jax import lax
from jax.experimental import pallas as pl
from jax.experimental.pallas import tpu as pltpu

F32 = jnp.float32
BF16 = jnp.bfloat16
I32 = jnp.int32

D_MODEL = 1024
A_HEADS = 8
A_HEAD_DIM = 64
A_WIDTH = A_HEADS * A_HEAD_DIM
A_PATTERNS = ((128, 1), (512, 4), (2048, 16))
A_HALF = 64
B_HEADS = 8
B_NOPE = 64
B_ROPE = 32
B_V = 64
Q_LORA = 256
KV_LORA = 128
ROPE_BASE = 10000.0
N_EXPERTS = 256
TOP_K = 8
N_GROUPS = 8
GROUP_SIZE = N_EXPERTS // N_GROUPS
TOPK_GROUPS = 4
EXPERT_HIDDEN = 256
ROUTED_SCALE = 2.5
LN_EPS = 1e-5
RMS_EPS = 1e-6
NEG_BIG = -1e30
ALPHA = 2.0 ** 0.25
LOG2E = 1.4426950408889634

LANES = 128
HEAD_PAD = 128
SLAB = D_MODEL // LANES
W1_COLS = 3 * A_WIDTH + Q_LORA + KV_LORA + 2 * LANES

TM_PROJ = 256
TQ_A = 128
A_POS_TILES = {1: 4, 4: 2, 16: 1}
PERM_GROUP = 256
LOGITS_ELEMS_B = 4096 * 512
KEY_CHUNK_B = 512
TM_MIX = 256
TM_DISP = 256
ROW_BLOCK = 256
TM_COMB = 128
EXPERT_TILES = 4
N_XBUF = 8
N_OBUF = 8
X_AHEAD = N_XBUF - EXPERT_TILES
VMEM_LIMIT = 48 * 1024 * 1024


def _slab_load(ref, n_tok, tok0=0):
    return jnp.concatenate([ref[pl.ds(tok0 * SLAB + c, n_tok, stride=SLAB), :] for c in range(SLAB)],
                           axis=1)


def _slab_store(ref, val):
    for c in range(SLAB):
        ref[pl.ds(c, val.shape[0], stride=SLAB), :] = val[:, c * LANES:(c + 1) * LANES]


def _slab_load_bf16(ref, stage_ref, n_tok, tok0=0):
    stage_ref[...] = ref[tok0 * SLAB:(tok0 + n_tok) * SLAB, :].astype(F32)
    return _slab_load(stage_ref, n_tok)


def _slab_store_bf16(ref, stage_ref, val):
    _slab_store(stage_ref, val)
    ref[...] = stage_ref[...].astype(BF16)


def _nt_dot(a, b):
    return lax.dot_general(a, b, (((1,), (1,)), ((), ())), preferred_element_type=F32)


def _layer_norm(x, g, b):
    mu = jnp.mean(x, axis=-1, keepdims=True)
    xc = x - mu
    var = jnp.mean(xc * xc, axis=-1, keepdims=True)
    return xc * lax.rsqrt(var + LN_EPS) * g + b


def _rms_norm(x, g):
    return x * lax.rsqrt(jnp.mean(x * x, axis=-1, keepdims=True) + RMS_EPS) * g


def _proj_kernel(xp_ref, xs_ref, w1_ref, qg_ref, kvg_ref, wq_ref, wqr_ref, wuk_ref, wuvt_ref,
                 cos_ref, sin_ref, perm4_ref, perm16_ref,
                 qa_ref, ka_ref, va_ref, q4_ref, k4_ref, v4_ref, q16_ref, k16_ref, v16_ref,
                 qb_ref, kb_ref, vbt_ref, *, n_prompt_tiles):
    i = pl.program_id(0)
    tm = xp_ref.shape[0]
    x = jnp.where(i < n_prompt_tiles, xp_ref[...], xs_ref[...]).astype(BF16)
    p = jnp.dot(x, w1_ref[...], preferred_element_type=F32)
    qa = (p[:, 0:A_WIDTH] * (A_HEAD_DIM ** -0.5 * LOG2E)).astype(BF16)
    ka = p[:, A_WIDTH:2 * A_WIDTH].astype(BF16)
    va = p[:, 2 * A_WIDTH:3 * A_WIDTH].astype(BF16)
    qa_ref[...] = qa
    ka_ref[...] = ka
    va_ref[...] = va
    qkv = jnp.concatenate([qa, ka, va], axis=1)
    for dil, perm_ref, outs in ((4, perm4_ref, (q4_ref, k4_ref, v4_ref)),
                                (16, perm16_ref, (q16_ref, k16_ref, v16_ref))):
        cm = jnp.dot(perm_ref[...], qkv, preferred_element_type=F32).astype(BF16)
        rows = tm // dil
        for r in range(dil):
            for j, o_ref in enumerate(outs):
                o_ref[:, r * A_WIDTH:(r + 1) * A_WIDTH] = cm[r * rows:(r + 1) * rows,
                                                             j * A_WIDTH:(j + 1) * A_WIDTH]
    c0 = 3 * A_WIDTH
    cq = p[:, c0:c0 + Q_LORA]
    ckv = p[:, c0 + Q_LORA:c0 + Q_LORA + KV_LORA]
    kr = p[:, c0 + Q_LORA + KV_LORA:c0 + Q_LORA + KV_LORA + LANES]
    krr = p[:, c0 + Q_LORA + KV_LORA + LANES:c0 + Q_LORA + KV_LORA + 2 * LANES]
    cos = cos_ref[...]
    sin = sin_ref[...]
    cos8 = jnp.concatenate([cos] * B_HEADS, axis=1)
    sin8 = jnp.concatenate([sin] * B_HEADS, axis=1)
    cqn = _rms_norm(cq, qg_ref[...]).astype(BF16)
    q = jnp.dot(cqn, wq_ref[...], preferred_element_type=F32)
    qr = jnp.dot(cqn, wqr_ref[...], preferred_element_type=F32)
    qscale = (B_NOPE + B_ROPE) ** -0.5 * LOG2E
    qb_ref[...] = ((q * cos8 + qr * sin8) * qscale).astype(BF16)
    ckvn = _rms_norm(ckv, kvg_ref[...]).astype(BF16)
    kn = jnp.dot(ckvn, wuk_ref[...], preferred_element_type=F32)
    krope = kr * cos + krr * sin
    kb_ref[...] = (kn + jnp.concatenate([krope] * B_HEADS, axis=1)).astype(BF16)
    vbt_ref[...] = _nt_dot(wuvt_ref[...], ckvn).astype(BF16)


def _class_perm(tm, dil):
    rows = tm // dil
    c = np.arange(tm)
    src = (c % rows) * dil + c // rows
    perm = np.zeros((tm, tm), np.float32)
    perm[c, src] = 1.0
    return jnp.asarray(perm, BF16)


def _proj(xp, xs, w1, qg, kvg, wq, wqr, wuk, wuvt, cos_t, sin_t, s_prompt, s_sample):
    tm = TM_PROJ
    n1 = xp.shape[0] // tm
    n2 = xs.shape[0] // tm
    t_all = xp.shape[0] + xs.shape[0]
    pt, st = s_prompt // tm, s_sample // tm
    assert tm == PERM_GROUP
    perm4, perm16 = _class_perm(tm, 4), _class_perm(tm, 16)

    def tab_idx(i):
        return (jnp.where(i < n1, i % pt, (i - n1) % st), 0)

    full = lambda shape: pl.BlockSpec(shape, lambda i: (0, 0))
    row = lambda cols: pl.BlockSpec((tm, cols), lambda i: (i, 0))
    cls = lambda dil: pl.BlockSpec((tm // dil, dil * A_WIDTH), lambda i: (i, 0))
    cls_shape = lambda dil: jax.ShapeDtypeStruct((t_all // dil, dil * A_WIDTH), BF16)
    return pl.pallas_call(
        functools.partial(_proj_kernel, n_prompt_tiles=n1),
        grid=(n1 + n2,),
        in_specs=[
            pl.BlockSpec((tm, D_MODEL), lambda i: (jnp.minimum(i, n1 - 1), 0)),
            pl.BlockSpec((tm, D_MODEL), lambda i: (jnp.maximum(i - n1, 0), 0)),
            full(w1.shape), full(qg.shape), full(kvg.shape), full(wq.shape), full(wqr.shape),
            full(wuk.shape), full(wuvt.shape),
            pl.BlockSpec((tm, LANES), tab_idx), pl.BlockSpec((tm, LANES), tab_idx),
            full(perm4.shape), full(perm16.shape),
        ],
        out_specs=[row(A_WIDTH)] * 3 + [cls(4)] * 3 + [cls(16)] * 3
        + [row(B_HEADS * HEAD_PAD), row(B_HEADS * HEAD_PAD),
           pl.BlockSpec((B_HEADS * B_V, tm), lambda i: (0, i))],
        out_shape=[jax.ShapeDtypeStruct((t_all, A_WIDTH), BF16)] * 3
        + [cls_shape(4)] * 3 + [cls_shape(16)] * 3
        + [jax.ShapeDtypeStruct((t_all, B_HEADS * HEAD_PAD), BF16)] * 2
        + [jax.ShapeDtypeStruct((B_HEADS * B_V, t_all), BF16)],
        compiler_params=pltpu.CompilerParams(dimension_semantics=("arbitrary",),
                                             vmem_limit_bytes=VMEM_LIMIT),
        name="proj",
    )(xp, xs, w1, qg, kvg, wq, wqr, wuk, wuvt, cos_t, sin_t, perm4, perm16)


def _attn_a_kernel(qblk_ref, pblk_ref, nblk_ref, tstart_ref, llen_ref,
                   q_ref, kp_ref, kc_ref, kn_ref, vp_ref, vc_ref, vn_ref, bias_ref, perm_ref,
                   o_ref, lse_ref, cm_ref, *, n_pos, n_cls):
    n = pl.program_id(0)
    tq, wk = TQ_A, TQ_A + 2 * A_HALF
    lane = lax.broadcasted_iota(I32, (tq, LANES), 1)
    low = lane < A_HEAD_DIM
    colbase = lax.broadcasted_iota(I32, (1, wk), 1) + (tstart_ref[n] - A_HALF)
    cls_len = llen_ref[n]

    def one_class(r):
        cols = slice(0, A_WIDTH) if n_cls == 1 else pl.ds(pl.multiple_of(r * A_WIDTH, A_WIDTH), A_WIDTH)
        kcat = jnp.concatenate([kp_ref[:, cols], kc_ref[:, cols], kn_ref[:, cols]], axis=0)
        vcat = jnp.concatenate([vp_ref[:, cols], vc_ref[:, cols], vn_ref[:, cols]], axis=0)
        for j in range(n_pos):
            rows = slice(j * tq, (j + 1) * tq)
            q = q_ref[rows, cols]
            kw, vw = kcat[j * tq:j * tq + wk, :], vcat[j * tq:j * tq + wk, :]
            col = colbase + j * tq
            colpen = jnp.where((col >= 0) & (col < cls_len), 0.0, NEG_BIG).astype(F32)
            pairs = []
            lse_c = jnp.zeros((tq, LANES), F32)
            for jp in range(A_HEADS // 2):
                sl = slice(jp * LANES, (jp + 1) * LANES)
                qp, kpair, vpair = q[:, sl], kw[:, sl], vw[:, sl]
                outs = []
                for e in range(2):
                    qm = jnp.where(low if e == 0 else ~low, qp, jnp.zeros_like(qp))
                    s = _nt_dot(qm, kpair) + bias_ref[2 * jp + e] + colpen
                    m = jnp.max(s, axis=1, keepdims=True)
                    p = jnp.exp2(s - m)
                    l = jnp.sum(p, axis=1, keepdims=True)
                    outs.append(jnp.dot(p.astype(BF16), vpair, preferred_element_type=F32) / l)
                    lse_c = lse_c + jnp.where(lane == 2 * jp + e, m + jnp.log2(l), 0.0)
                pairs.append(jnp.where(low, outs[0], outs[1]))
            o_full = jnp.concatenate(pairs, axis=1).astype(BF16)
            if n_cls == 1:
                o_ref[rows, :] = o_full
                lse_ref[rows, :] = lse_c
            else:
                hi = lse_c.astype(BF16)
                rest = lse_c - hi.astype(F32)
                mid = rest.astype(BF16)
                lo = (rest - mid.astype(F32)).astype(BF16)
                cm_ref[r, rows, :] = jnp.concatenate([o_full, hi, mid, lo], axis=1)

    if n_cls == 1:
        one_class(0)
        return

    def body(r2, carry):
        one_class(2 * r2)
        one_class(2 * r2 + 1)
        return carry

    lax.fori_loop(0, n_cls // 2, body, 0)
    per = PERM_GROUP // n_cls
    for a in range(n_pos * tq * n_cls // PERM_GROUP):
        stack = jnp.concatenate([cm_ref[r, a * per:(a + 1) * per, :] for r in range(n_cls)], axis=0)
        nat = jnp.dot(perm_ref[...], stack, preferred_element_type=F32)
        rows = slice(a * PERM_GROUP, (a + 1) * PERM_GROUP)
        o_ref[rows, :] = nat[:, :A_WIDTH].astype(BF16)
        lse_ref[rows, :] = (nat[:, A_WIDTH:A_WIDTH + LANES] + nat[:, A_WIDTH + LANES:A_WIDTH + 2 * LANES]
                            + nat[:, A_WIDTH + 2 * LANES:])


def _attn_a_tables(seqs, dil, n_pos):
    rows = n_pos * TQ_A
    per_halo = rows // A_HALF
    qblk, pblk, nblk, tstart, llen = [], [], [], [], []
    for off, s_len in seqs:
        cls = s_len // dil
        steps = cls // rows
        base = (off // dil) // rows
        for i in range(steps):
            qblk.append(base + i)
            pblk.append((base + i) * per_halo - (1 if i > 0 else 0))
            nblk.append((base + i + 1) * per_halo - (0 if i < steps - 1 else 1))
            tstart.append(i * rows)
            llen.append(cls)
    return [jnp.asarray(np.asarray(a, np.int32)) for a in (qblk, pblk, nblk, tstart, llen)]


def _attn_a_bias(dil):
    tq, wk = TQ_A, TQ_A + 2 * A_HALF
    delta = np.abs(np.arange(wk)[None, :] - A_HALF - np.arange(tq)[:, None]).astype(np.float64)
    slopes = 2.0 ** (-8.0 * (np.arange(A_HEADS) + 1.0) / A_HEADS)
    bias = -slopes[:, None, None] * (delta * dil)[None] * LOG2E
    bias = np.where((delta <= A_HALF)[None], bias, NEG_BIG)
    return jnp.asarray(bias.astype(np.float32))


def _attn_a(qc, kc, vc, seqs, dil):
    t_all = qc.shape[0] * dil
    n_pos = A_POS_TILES[dil]
    rows, width = n_pos * TQ_A, dil * A_WIDTH
    tabs = _attn_a_tables(seqs, dil, n_pos)
    n_steps = int(tabs[0].shape[0])
    bias = _attn_a_bias(dil)
    perm = _class_perm(PERM_GROUP, dil).T
    cur = pl.BlockSpec((rows, width), lambda n, qb, pb, nb, ts, ll: (qb[n], 0))
    prev = pl.BlockSpec((A_HALF, width), lambda n, qb, pb, nb, ts, ll: (pb[n], 0))
    nxt = pl.BlockSpec((A_HALF, width), lambda n, qb, pb, nb, ts, ll: (nb[n], 0))
    tok = lambda cols: pl.BlockSpec((rows * dil, cols), lambda n, *_: (n, 0))
    return pl.pallas_call(
        functools.partial(_attn_a_kernel, n_pos=n_pos, n_cls=dil),
        grid_spec=pltpu.PrefetchScalarGridSpec(
            num_scalar_prefetch=5, grid=(n_steps,),
            in_specs=[cur, prev, cur, nxt, prev, cur, nxt,
                      pl.BlockSpec(bias.shape, lambda n, *_: (0, 0, 0)),
                      pl.BlockSpec(perm.shape, lambda n, *_: (0, 0))],
            out_specs=[tok(A_WIDTH), tok(LANES)],
            scratch_shapes=[pltpu.VMEM((dil, rows, A_WIDTH + 3 * LANES), BF16)]),
        out_shape=[jax.ShapeDtypeStruct((t_all, A_WIDTH), BF16),
                   jax.ShapeDtypeStruct((t_all, LANES), F32)],
        compiler_params=pltpu.CompilerParams(dimension_semantics=("arbitrary",),
                                             vmem_limit_bytes=VMEM_LIMIT),
        name=f"attn_a_d{dil}",
    )(*tabs, qc, kc, kc, kc, vc, vc, vc, bias, perm)


def _attn_b_kernel(q_ref, k_ref, vt_ref, o_ref, sta_ref, stb_ref, ma_ref, mb_ref):
    s = pl.program_id(0)
    tq, s_len = q_ref.shape[0], k_ref.shape[0]

    @pl.when(s == 0)
    def _():
        stb_ref[...] = jnp.zeros_like(stb_ref)
        mb_ref[...] = jnp.zeros_like(mb_ref)

    def step(st_new_ref, m_new_ref, st_old_ref, m_old_ref):
        m_old = m_old_ref[...]
        q = q_ref[...]
        m_new = jnp.full((1, tq), -jnp.inf, F32)
        l = jnp.zeros((1, tq), F32)
        acc = jnp.zeros((B_V, tq), F32)
        for c in range(s_len // KEY_CHUNK_B):
            ks = slice(c * KEY_CHUNK_B, (c + 1) * KEY_CHUNK_B)
            st_c = _nt_dot(k_ref[ks, :], q)
            st_new_ref[ks, :] = st_c
            m_new = jnp.maximum(m_new, jnp.max(st_c, axis=0, keepdims=True))
            p = jnp.exp2(st_old_ref[ks, :] - m_old)
            l = l + jnp.sum(p, axis=0, keepdims=True)
            acc = acc + jnp.dot(vt_ref[:, ks], p.astype(BF16), preferred_element_type=F32)
        m_new_ref[...] = m_new
        o_ref[...] = (acc / l).astype(BF16)

    @pl.when(s % 2 == 0)
    def _():
        step(sta_ref, ma_ref, stb_ref, mb_ref)

    @pl.when(s % 2 == 1)
    def _():
        step(stb_ref, mb_ref, sta_ref, ma_ref)


def _attn_b(qb, kb, vbt, off, n_batch, s_len):
    tq = min(s_len, LOGITS_ELEMS_B // s_len)
    nq = s_len // tq
    qbase, kbase = off // tq, off // s_len
    n_items = n_batch * B_HEADS * nq

    def split(item):
        return item // (B_HEADS * nq), (item // nq) % B_HEADS, item % nq

    def q_map(s):
        b, h, qi = split(jnp.minimum(s, n_items - 1))
        return (qbase + b * nq + qi, h)

    def k_map(s):
        b, h, _ = split(jnp.minimum(s, n_items - 1))
        return (kbase + b, h)

    def v_map(s):
        b, h, _ = split(jnp.maximum(s - 1, 0))
        return (h, kbase + b)

    def o_map(s):
        b, h, qi = split(jnp.maximum(s - 1, 0))
        return (h, b * nq + qi)

    return pl.pallas_call(
        _attn_b_kernel,
        grid=(n_items + 1,),
        in_specs=[pl.BlockSpec((tq, HEAD_PAD), q_map),
                  pl.BlockSpec((s_len, HEAD_PAD), k_map),
                  pl.BlockSpec((B_V, s_len), v_map)],
        out_specs=pl.BlockSpec((B_V, tq), o_map),
        out_shape=jax.ShapeDtypeStruct((B_HEADS * B_V, n_batch * s_len), BF16),
        scratch_shapes=[pltpu.VMEM((s_len, tq), F32), pltpu.VMEM((s_len, tq), F32),
                        pltpu.VMEM((1, tq), F32), pltpu.VMEM((1, tq), F32)],
        compiler_params=pltpu.CompilerParams(dimension_semantics=("arbitrary",),
                                             vmem_limit_bytes=VMEM_LIMIT),
        name=f"attn_b_s{s_len}",
    )(qb, kb, vbt)


def _mix_kernel(xp_ref, xs_ref, o0_ref, o1_ref, o2_ref, l0_ref, l1_ref, l2_ref, spread_ref,
                obp_ref, obs_ref, wout_ref,
                g_ref, b_ref, wrh_ref, wrl_ref, rb_ref,
                h_ref, hp_ref, tope_ref, pos_ref, gate_ref, cnt_out_ref, cnt_ref, stage_ref,
                *, n_prompt_tiles):
    i = pl.program_id(0)
    tm = xp_ref.shape[0]

    @pl.when(i == 0)
    def _():
        cnt_ref[...] = jnp.zeros_like(cnt_ref)

    l0, l1, l2 = l0_ref[...], l1_ref[...], l2_ref[...]
    lmax = jnp.maximum(jnp.maximum(l0, l1), l2)
    e0, e1, e2 = jnp.exp2(l0 - lmax), jnp.exp2(l1 - lmax), jnp.exp2(l2 - lmax)
    inv = 1.0 / (e0 + e1 + e2)
    spread = spread_ref[...]

    def per_lane(w):
        hi = w.astype(BF16)
        lo = (w - hi.astype(F32)).astype(BF16)
        return (jnp.dot(hi, spread, preferred_element_type=F32)
                + jnp.dot(lo, spread, preferred_element_type=F32))

    oa = (per_lane(e0 * inv) * o0_ref[...].astype(F32) + per_lane(e1 * inv) * o1_ref[...].astype(F32)
          + per_lane(e2 * inv) * o2_ref[...].astype(F32))
    is_prompt = i < n_prompt_tiles
    obt = jnp.where(is_prompt, obp_ref[...], obs_ref[...])
    mix = (jnp.dot(oa.astype(BF16), wout_ref[0:A_WIDTH, :], preferred_element_type=F32)
           + lax.dot_general(obt, wout_ref[A_WIDTH:, :], (((0,), (0,)), ((), ())),
                             preferred_element_type=F32))
    x = jnp.where(is_prompt, xp_ref[...], xs_ref[...])
    h = _layer_norm(ALPHA * x + mix, g_ref[...], b_ref[...])
    h_ref[...] = h
    _slab_store_bf16(hp_ref, stage_ref, h)

    h_hi = h.astype(BF16)
    h_lo = (h - h_hi.astype(F32)).astype(BF16)
    wrh = wrh_ref[...]
    logits = _nt_dot(wrh, h_hi) + _nt_dot(wrh, h_lo) + _nt_dot(wrl_ref[...], h_hi)
    scores = jax.nn.sigmoid(logits)
    sel = scores + rb_ref[...]

    sub = lax.broadcasted_iota(I32, (GROUP_SIZE, tm), 0).astype(F32)
    gscore = []
    for g in range(N_GROUPS):
        sg = sel[g * GROUP_SIZE:(g + 1) * GROUP_SIZE, :]
        m1 = jnp.max(sg, axis=0, keepdims=True)
        first = jnp.min(jnp.where(sg == m1, sub, float(GROUP_SIZE)), axis=0, keepdims=True)
        m2 = jnp.max(jnp.where(sub == first, -jnp.inf, sg), axis=0, keepdims=True)
        gscore.append(m1 + m2)
    cands = []
    for g in range(N_GROUPS):
        beaten = jnp.zeros((1, tm), F32)
        for g2 in range(N_GROUPS):
            if g2 == g:
                continue
            wins = (gscore[g2] > gscore[g]) | ((gscore[g2] == gscore[g]) & (g2 < g))
            beaten = beaten + wins.astype(F32)
        keep = beaten < float(TOPK_GROUPS)
        sg = sel[g * GROUP_SIZE:(g + 1) * GROUP_SIZE, :]
        cands.append(jnp.where(keep, sg, NEG_BIG))
    cand = jnp.concatenate(cands, axis=0)

    eidx = lax.broadcasted_iota(I32, (N_EXPERTS, tm), 0).astype(F32)
    picked_idx, picked_gate = [], []
    onehot = jnp.zeros((N_EXPERTS, tm), F32)
    for _ in range(TOP_K):
        mx = jnp.max(cand, axis=0, keepdims=True)
        fi = jnp.min(jnp.where(cand == mx, eidx, float(N_EXPERTS)), axis=0, keepdims=True)
        pick = eidx == fi
        picked_idx.append(fi)
        picked_gate.append(jnp.sum(jnp.where(pick, scores, 0.0), axis=0, keepdims=True))
        onehot = onehot + pick.astype(F32)
        cand = jnp.where(pick, -jnp.inf, cand)
    gsum = picked_gate[0]
    for k in range(1, TOP_K):
        gsum = gsum + picked_gate[k]

    tri = (lax.broadcasted_iota(I32, (tm, tm), 0) < lax.broadcasted_iota(I32, (tm, tm), 1))
    before = jnp.dot(onehot.astype(BF16), tri.astype(BF16), preferred_element_type=F32)
    rank = before + cnt_ref[:, 0:1]
    for k in range(TOP_K):
        pick = eidx == picked_idx[k]
        tope_ref[k:k + 1, :] = picked_idx[k].astype(I32)
        pos_ref[k:k + 1, :] = jnp.sum(jnp.where(pick, rank, 0.0), axis=0, keepdims=True).astype(I32)
        gate_ref[k:k + 1, :] = picked_gate[k] / gsum * ROUTED_SCALE
    cnt_ref[...] = cnt_ref[...] + jnp.sum(onehot, axis=1, keepdims=True)
    cnt_out_ref[...] = cnt_ref[...]


def _mix(xp, xs, outs, lses, obp, obs, wout, g, b, wrh, wrl, rb):
    tm = TM_MIX
    n1 = xp.shape[0] // tm
    n2 = xs.shape[0] // tm
    t_all = xp.shape[0] + xs.shape[0]
    full = lambda a: pl.BlockSpec(a.shape, lambda i: (0,) * a.ndim)
    row = lambda cols: pl.BlockSpec((tm, cols), lambda i: (i, 0))
    prow = lambda cols: pl.BlockSpec((tm, cols), lambda i: (jnp.minimum(i, n1 - 1), 0))
    srow = lambda cols: pl.BlockSpec((tm, cols), lambda i: (jnp.maximum(i - n1, 0), 0))
    col = pl.BlockSpec((TOP_K, tm), lambda i: (0, i))
    head_of_lane = np.arange(A_WIDTH) // A_HEAD_DIM
    spread = jnp.asarray(np.arange(LANES)[:, None] == head_of_lane[None, :], BF16)
    return pl.pallas_call(
        functools.partial(_mix_kernel, n_prompt_tiles=n1),
        grid=(n1 + n2,),
        in_specs=[
            prow(D_MODEL), srow(D_MODEL),
            row(A_WIDTH), row(A_WIDTH), row(A_WIDTH), row(LANES), row(LANES), row(LANES), full(spread),
            pl.BlockSpec((B_HEADS * B_V, tm), lambda i: (0, jnp.minimum(i, n1 - 1))),
            pl.BlockSpec((B_HEADS * B_V, tm), lambda i: (0, jnp.maximum(i - n1, 0))),
            full(wout), full(g), full(b), full(wrh), full(wrl), full(rb),
        ],
        out_specs=[row(D_MODEL), pl.BlockSpec((SLAB * tm, LANES), lambda i: (i, 0)), col, col, col,
                   pl.BlockSpec((N_EXPERTS, LANES), lambda i: (0, 0))],
        out_shape=[jax.ShapeDtypeStruct((t_all, D_MODEL), F32),
                   jax.ShapeDtypeStruct((SLAB * t_all, LANES), BF16),
                   jax.ShapeDtypeStruct((TOP_K, t_all), I32),
                   jax.ShapeDtypeStruct((TOP_K, t_all), I32),
                   jax.ShapeDtypeStruct((TOP_K, t_all), F32),
                   jax.ShapeDtypeStruct((N_EXPERTS, LANES), F32)],
        scratch_shapes=[pltpu.VMEM((N_EXPERTS, LANES), F32), pltpu.VMEM((SLAB * tm, LANES), F32)],
        compiler_params=pltpu.CompilerParams(dimension_semantics=("arbitrary",),
                                             vmem_limit_bytes=VMEM_LIMIT),
        name="mix_router",
    )(xp, xs, *outs, *lses, spread, obp, obs, wout, g, b, wrh, wrl, rb)


def _dispatch_kernel(zstart_ref, nused_ref, h_ref, tope_ref, pos_ref, pstart_ref, buf_ref, dest_ref,
                     dsm_ref, zero_ref, sem_ref, zsem_ref):
    i = pl.program_id(0)
    tm = tope_ref.shape[1]
    tile_rows = SLAB * ROW_BLOCK

    @pl.when(i == 0)
    def _():
        zero_ref[...] = jnp.zeros_like(zero_ref)

        def zfill(e, carry):
            start = pl.multiple_of(zstart_ref[e], SLAB)
            pltpu.make_async_copy(zero_ref, buf_ref.at[pl.ds(start, tile_rows), :], zsem_ref).start()
            return carry

        lax.fori_loop(0, N_EXPERTS, zfill, 0)
        span = buf_ref.at[pl.ds(0, N_EXPERTS * tile_rows), :]
        pltpu.make_async_copy(span, span, zsem_ref).wait()

        def ztail(j, carry):
            start = pl.multiple_of(j * tile_rows, tile_rows)
            pltpu.make_async_copy(zero_ref, buf_ref.at[pl.ds(start, tile_rows), :], zsem_ref).start()
            return carry

        def zwait(j, carry):
            pltpu.make_async_copy(zero_ref, buf_ref.at[pl.ds(0, tile_rows), :], zsem_ref).wait()
            return carry

        n_tiles = buf_ref.shape[0] // tile_rows
        lax.fori_loop(nused_ref[0], n_tiles, ztail, 0)
        lax.fori_loop(nused_ref[0], n_tiles, zwait, 0)

    eidx = lax.broadcasted_iota(I32, (N_EXPERTS, tm), 0)
    pstart = pstart_ref[...]
    for k in range(TOP_K):
        hit = eidx == tope_ref[k:k + 1, :]
        base = jnp.sum(jnp.where(hit, pstart, 0.0), axis=0, keepdims=True)
        dest_ref[k:k + 1, :] = (pos_ref[k:k + 1, :] + base.astype(I32)) * SLAB
    pltpu.sync_copy(dest_ref, dsm_ref)

    def scatter(t, carry):
        src = h_ref.at[pl.ds(pl.multiple_of(t * SLAB, SLAB), SLAB), :]
        for k in range(TOP_K):
            dst = buf_ref.at[pl.ds(pl.multiple_of(dsm_ref[k, t], SLAB), SLAB), :]
            pltpu.make_async_copy(src, dst, sem_ref).start(priority=k % 2)
        return carry

    lax.fori_loop(0, tm, scatter, 0, unroll=8)
    span = buf_ref.at[pl.ds(0, TOP_K * tm * SLAB), :]
    pltpu.make_async_copy(span, span, sem_ref).wait()


def _dispatch(h, tope, pos, pstart, zstart, nused, n_rows):
    tm = TM_DISP
    t_all = tope.shape[1]
    col = pl.BlockSpec((TOP_K, tm), lambda i, z, nu: (0, i))
    return pl.pallas_call(
        _dispatch_kernel,
        grid_spec=pltpu.PrefetchScalarGridSpec(
            num_scalar_prefetch=2, grid=(t_all // tm,),
            in_specs=[pl.BlockSpec((SLAB * tm, LANES), lambda i, z, nu: (i, 0)), col, col,
                      pl.BlockSpec((N_EXPERTS, 1), lambda i, z, nu: (0, 0))],
            out_specs=[pl.BlockSpec(memory_space=pl.ANY), col],
            scratch_shapes=[pltpu.SMEM((TOP_K, tm), I32),
                            pltpu.VMEM((SLAB * ROW_BLOCK, LANES), BF16),
                            pltpu.SemaphoreType.DMA, pltpu.SemaphoreType.DMA]),
        out_shape=[jax.ShapeDtypeStruct((SLAB * n_rows, LANES), BF16),
                   jax.ShapeDtypeStruct((TOP_K, t_all), I32)],
        compiler_params=pltpu.CompilerParams(dimension_semantics=("arbitrary",),
                                             vmem_limit_bytes=VMEM_LIMIT),
        name="dispatch",
    )(zstart, nused, h, tope, pos, pstart)


def _expert_kernel(tfirst_ref, ntile_ref, nused_ref, x_hbm, wg_ref, wu_ref, wd_ref, o_hbm,
                   xbuf, obuf, stage_ref, wgb_ref, wub_ref, wdb_ref, xsem, osem):
    e = pl.program_id(0)
    tile_rows = SLAB * ROW_BLOCK
    nused = nused_ref[0]

    def x_copy(g, slot):
        start = pl.multiple_of(g * tile_rows, tile_rows)
        return pltpu.make_async_copy(x_hbm.at[pl.ds(start, tile_rows), :], xbuf.at[slot], xsem.at[slot])

    def o_copy(g, slot):
        start = pl.multiple_of(g * tile_rows, tile_rows)
        return pltpu.make_async_copy(obuf.at[slot], o_hbm.at[pl.ds(start, tile_rows), :], osem.at[slot])

    @pl.when(e == 0)
    def _():
        for j in range(X_AHEAD):
            @pl.when(j < nused)
            def _():
                x_copy(j, j).start()

    n_e = ntile_ref[e]

    @pl.when(n_e > 0)
    def _():
        wgb_ref[...] = wg_ref[...].astype(BF16)
        wub_ref[...] = wu_ref[...].astype(BF16)
        wdb_ref[...] = wd_ref[...].astype(BF16)

    def run_tiles(g0, n):
        for q in range(n):
            ahead = g0 + q + X_AHEAD

            @pl.when(ahead < nused)
            def _():
                x_copy(ahead, ahead % N_XBUF).start()

        for q in range(n):
            g = g0 + q
            x_copy(g, g % N_XBUF).wait()

            @pl.when(g >= N_OBUF)
            def _():
                o_copy(g - N_OBUF, g % N_OBUF).wait()

        for q in range(n):
            g = g0 + q
            x = _slab_load_bf16(xbuf.at[g % N_XBUF], stage_ref.at[q], ROW_BLOCK).astype(BF16)
            gt = jnp.dot(x, wgb_ref[...], preferred_element_type=F32)
            up = jnp.dot(x, wub_ref[...], preferred_element_type=F32)
            hmid = (gt * jax.nn.sigmoid(gt) * up).astype(BF16)
            out = jnp.dot(hmid, wdb_ref[...], preferred_element_type=F32)
            _slab_store(obuf.at[g % N_OBUF], out)

        for q in range(n):
            g = g0 + q
            o_copy(g, g % N_OBUF).start()

    def group(j, carry):
        run_tiles(tfirst_ref[e] + EXPERT_TILES * j, EXPERT_TILES)
        return carry

    n_groups = n_e // EXPERT_TILES
    lax.fori_loop(0, n_groups, group, 0)
    done = n_groups * EXPERT_TILES
    left = n_e - done
    size = EXPERT_TILES // 2
    while size >= 1:
        @pl.when((left & size) != 0)
        def _(size=size, done=done):
            run_tiles(tfirst_ref[e] + done, size)

        done = done + (left & size)
        size //= 2

    @pl.when(e == N_EXPERTS - 1)
    def _():
        for j in range(N_OBUF):
            @pl.when(nused > j)
            def _():
                o_copy(nused - 1 - j, (nused - 1 - j) % N_OBUF).wait()

        zero_ref = stage_ref.at[0]
        zero_ref[...] = jnp.zeros_like(zero_ref)
        n_tiles = o_hbm.shape[0] // tile_rows

        def tail_copy(g):
            start = pl.multiple_of(g * tile_rows, tile_rows)
            return pltpu.make_async_copy(zero_ref, o_hbm.at[pl.ds(start, tile_rows), :], osem.at[0])

        def ztail(g, carry):
            tail_copy(g).start()
            return carry

        def zwait(g, carry):
            tail_copy(g).wait()
            return carry

        lax.fori_loop(nused, n_tiles, ztail, 0)
        lax.fori_loop(nused, n_tiles, zwait, 0)


def _experts(buf, tfirst, ntile, nused, w_gate, w_up, w_down):
    tile_rows = SLAB * ROW_BLOCK
    wmap = lambda e, tf, nt, nu: (e, 0, 0)
    return pl.pallas_call(
        _expert_kernel,
        grid_spec=pltpu.PrefetchScalarGridSpec(
            num_scalar_prefetch=3, grid=(N_EXPERTS,),
            in_specs=[pl.BlockSpec(memory_space=pl.ANY),
                      pl.BlockSpec((None, D_MODEL, EXPERT_HIDDEN), wmap),
                      pl.BlockSpec((None, D_MODEL, EXPERT_HIDDEN), wmap),
                      pl.BlockSpec((None, EXPERT_HIDDEN, D_MODEL), wmap)],
            out_specs=pl.BlockSpec(memory_space=pl.ANY),
            scratch_shapes=[pltpu.VMEM((N_XBUF, tile_rows, LANES), BF16),
                            pltpu.VMEM((N_OBUF, tile_rows, LANES), F32),
                            pltpu.VMEM((EXPERT_TILES, tile_rows, LANES), F32),
                            pltpu.VMEM((D_MODEL, EXPERT_HIDDEN), BF16),
                            pltpu.VMEM((D_MODEL, EXPERT_HIDDEN), BF16),
                            pltpu.VMEM((EXPERT_HIDDEN, D_MODEL), BF16),
                            pltpu.SemaphoreType.DMA((N_XBUF,)),
                            pltpu.SemaphoreType.DMA((N_OBUF,))]),
        out_shape=jax.ShapeDtypeStruct(buf.shape, F32),
        compiler_params=pltpu.CompilerParams(dimension_semantics=("arbitrary",),
                                             vmem_limit_bytes=VMEM_LIMIT),
        name="experts",
    )(tfirst, ntile, nused, buf, w_gate, w_up, w_down)


def _combine_kernel(h_ref, gate_ref, dest_ref, dest1_ref, dest2_ref, eo_ref, wsg_ref, wsu_ref, wsd_ref,
                    g_ref, b_ref, yp_ref, ys_ref, dsm_ref, rows_ref, sem_ref, *, n_prompt_tiles, n_tiles):
    i = pl.program_id(0)
    tm = gate_ref.shape[0]
    slot = i % 2
    rows = rows_ref.at[slot]
    nxt_rows = rows_ref.at[1 - slot]
    nxt_sem = sem_ref.at[1 - slot]

    @pl.when(i == 0)
    def _():
        pltpu.sync_copy(dest_ref, dsm_ref)

        def gather(t, carry):
            for k in range(TOP_K):
                src = eo_ref.at[pl.ds(pl.multiple_of(dsm_ref[k, t], SLAB), SLAB), :]
                dst = rows_ref.at[0, pl.ds(pl.multiple_of((k * tm + t) * SLAB, SLAB), SLAB), :]
                pltpu.make_async_copy(src, dst, sem_ref.at[0]).start(priority=k % 2)
            return carry

        lax.fori_loop(0, tm, gather, 0, unroll=8)
        pltpu.sync_copy(dest1_ref, dsm_ref)

    def issue_next(k):
        for t in range(tm):
            src = eo_ref.at[pl.ds(pl.multiple_of(dsm_ref[k, t], SLAB), SLAB), :]
            dst = nxt_rows.at[pl.ds((k * tm + t) * SLAB, SLAB), :]
            pltpu.make_async_copy(src, dst, nxt_sem).start(priority=t % 2)

    h = h_ref[...]
    hb = h.astype(BF16)
    sg = jnp.dot(hb, wsg_ref[...], preferred_element_type=F32)
    su = jnp.dot(hb, wsu_ref[...], preferred_element_type=F32)
    shared = jnp.dot((sg * jax.nn.sigmoid(sg) * su).astype(BF16), wsd_ref[...],
                     preferred_element_type=F32)

    issue_next(0)
    whole = eo_ref.at[pl.ds(0, TOP_K * tm * SLAB), :]
    pltpu.make_async_copy(whole, rows, sem_ref.at[slot]).wait()
    gate = gate_ref[...]
    routed = gate[:, 0:1] * _slab_load(rows, tm)
    for k in range(1, TOP_K):
        issue_next(k)
        routed = routed + gate[:, k:k + 1] * _slab_load(rows, tm, k * tm)
    y = _layer_norm(ALPHA * h + (routed + shared), g_ref[...], b_ref[...])

    @pl.when(i < n_prompt_tiles)
    def _():
        yp_ref[...] = y

    @pl.when(i >= n_prompt_tiles)
    def _():
        ys_ref[...] = y

    pltpu.sync_copy(dest2_ref, dsm_ref)

    @pl.when(i == n_tiles - 1)
    def _():
        pltpu.make_async_copy(whole, nxt_rows, nxt_sem).wait()


def _combine(h, gate_t, dest, eo, wsg, wsu, wsd, g, b, t_prompt):
    tm = TM_COMB
    t_all = gate_t.shape[0]
    n1 = t_prompt // tm
    n2 = (t_all - t_prompt) // tm
    full = lambda a: pl.BlockSpec(a.shape, lambda i: (0,) * a.ndim)
    return pl.pallas_call(
        functools.partial(_combine_kernel, n_prompt_tiles=n1, n_tiles=n1 + n2),
        grid=(n1 + n2,),
        in_specs=[pl.BlockSpec((tm, D_MODEL), lambda i: (i, 0)),
                  pl.BlockSpec((tm, TOP_K), lambda i: (i, 0)),
                  pl.BlockSpec((TOP_K, tm), lambda i: (0, i)),
                  pl.BlockSpec((TOP_K, tm), lambda i: (0, jnp.minimum(i + 1, n1 + n2 - 1))),
                  pl.BlockSpec((TOP_K, tm), lambda i: (0, jnp.minimum(i + 2, n1 + n2 - 1))),
                  pl.BlockSpec(memory_space=pl.ANY),
                  full(wsg), full(wsu), full(wsd), full(g), full(b)],
        out_specs=[pl.BlockSpec((tm, D_MODEL), lambda i: (jnp.minimum(i, n1 - 1), 0)),
                   pl.BlockSpec((tm, D_MODEL), lambda i: (jnp.maximum(i - n1, 0), 0))],
        out_shape=[jax.ShapeDtypeStruct((t_prompt, D_MODEL), F32),
                   jax.ShapeDtypeStruct((t_all - t_prompt, D_MODEL), F32)],
        scratch_shapes=[pltpu.SMEM((TOP_K, tm), I32),
                        pltpu.VMEM((2, TOP_K * tm * SLAB, LANES), F32),
                        pltpu.SemaphoreType.DMA((2,))],
        compiler_params=pltpu.CompilerParams(dimension_semantics=("arbitrary",),
                                             vmem_limit_bytes=VMEM_LIMIT),
        name="combine",
    )(h, gate_t, dest, dest, dest, eo, wsg, wsu, wsd, g, b)


def _rope_tables(s_max):
    inv_freq = ROPE_BASE ** (-jnp.arange(0, B_ROPE, 2, dtype=F32) / B_ROPE)
    ang = jnp.arange(s_max, dtype=F32)[:, None] * inv_freq[None, :]
    cos, sin = jnp.cos(ang), jnp.sin(ang)
    ones = jnp.ones((s_max, B_NOPE), F32)
    zeros_n = jnp.zeros((s_max, B_NOPE), F32)
    zeros_p = jnp.zeros((s_max, HEAD_PAD - B_NOPE - B_ROPE), F32)
    return (jnp.concatenate([ones, cos, cos, zeros_p], axis=1),
            jnp.concatenate([zeros_n, sin, sin, zeros_p], axis=1))


def _rot_cols(w):
    half = B_ROPE // 2
    return jnp.concatenate([-w[..., half:], w[..., :half]], axis=-1)


def _layout_weights(w_in, w_uq, w_uk):
    c_kr = 3 * A_WIDTH + Q_LORA + KV_LORA
    w_kr = w_in[:, c_kr:c_kr + B_ROPE]
    pad_l = jnp.zeros((D_MODEL, B_NOPE), F32)
    pad_r = jnp.zeros((D_MODEL, HEAD_PAD - B_NOPE - B_ROPE), F32)
    w1 = jnp.concatenate([w_in[:, :c_kr], pad_l, w_kr, pad_r, pad_l, _rot_cols(w_kr), pad_r], axis=1)
    wq3 = w_uq.reshape(Q_LORA, B_HEADS, B_NOPE + B_ROPE)
    nope, rope = wq3[..., :B_NOPE], wq3[..., B_NOPE:]
    zpad = jnp.zeros((Q_LORA, B_HEADS, HEAD_PAD - B_NOPE - B_ROPE), F32)
    wq = jnp.concatenate([nope, rope, zpad], axis=-1).reshape(Q_LORA, B_HEADS * HEAD_PAD)
    wqr = jnp.concatenate([jnp.zeros_like(nope), _rot_cols(rope), zpad], axis=-1)
    wqr = wqr.reshape(Q_LORA, B_HEADS * HEAD_PAD)
    wk3 = w_uk.reshape(KV_LORA, B_HEADS, B_NOPE)
    wuk = jnp.concatenate([wk3, jnp.zeros((KV_LORA, B_HEADS, HEAD_PAD - B_NOPE), F32)], axis=-1)
    wuk = wuk.reshape(KV_LORA, B_HEADS * HEAD_PAD)
    return w1.astype(BF16), wq.astype(BF16), wqr.astype(BF16), wuk.astype(BF16)


def _forward(x_prompt, x_sample, w_in, w_out, ln1_g, ln1_b, q_norm_g, w_uq, kv_norm_g, w_uk, w_uv,
             w_router, router_bias, w_gate, w_up, w_down, ws_gate, ws_up, ws_down, ln2_g, ln2_b):
    b1, s1, _ = x_prompt.shape
    b2, s2, _ = x_sample.shape
    t1, t2 = b1 * s1, b2 * s2
    t_all = t1 + t2
    xp = x_prompt.reshape(t1, D_MODEL)
    xs = x_sample.reshape(t2, D_MODEL)
    seqs = [(b * s1, s1) for b in range(b1)] + [(t1 + b * s2, s2) for b in range(b2)]
    assert t1 % s2 == 0 and s1 % (TQ_A * 16) == 0 and s2 % (TQ_A * 16) == 0

    w1, wq, wqr, wuk = _layout_weights(w_in, w_uq, w_uk)
    cos_t, sin_t = _rope_tables(max(s1, s2))
    (qa, ka, va, q4, k4, v4, q16, k16, v16, qb, kb, vbt) = _proj(
        xp, xs, w1, q_norm_g.reshape(1, -1), kv_norm_g.reshape(1, -1), wq, wqr, wuk,
        w_uv.T.astype(BF16), cos_t, sin_t, s1, s2)

    outs, lses = [], []
    for (_, dil), qkv in zip(A_PATTERNS, ((qa, ka, va), (q4, k4, v4), (q16, k16, v16))):
        o, lse = _attn_a(*qkv, seqs, dil)
        outs.append(o)
        lses.append(lse)
    obp = _attn_b(qb, kb, vbt, 0, b1, s1)
    obs = _attn_b(qb, kb, vbt, t1, b2, s2)

    wr_t = w_router.T
    wr_hi = wr_t.astype(BF16)
    wr_lo = (wr_t - wr_hi.astype(F32)).astype(BF16)
    h, h_slab, tope, pos, gate, counts = _mix(xp, xs, outs, lses, obp, obs, w_out.astype(BF16),
                                                ln1_g.reshape(1, -1), ln1_b.reshape(1, -1),
                                                wr_hi, wr_lo, router_bias.reshape(-1, 1))

    cnt = counts[:, 0].astype(I32)
    padded = ((cnt + ROW_BLOCK - 1) // ROW_BLOCK) * ROW_BLOCK
    pend = jnp.cumsum(padded)
    pstart = pend - padded
    n_tiles = (t_all * TOP_K) // ROW_BLOCK + N_EXPERTS + 1
    nused = (pend[-1:] // ROW_BLOCK).astype(I32)

    buf, dest = _dispatch(h_slab, tope, pos, pstart.astype(F32).reshape(-1, 1),
                          ((pstart + cnt) * SLAB).astype(I32), nused, n_tiles * ROW_BLOCK)
    eo = _experts(buf, (pstart // ROW_BLOCK).astype(I32), (padded // ROW_BLOCK).astype(I32), nused,
                  w_gate, w_up, w_down)
    yp, ys = _combine(h, gate.T, dest, eo, ws_gate.astype(BF16), ws_up.astype(BF16),
                      ws_down.astype(BF16), ln2_g.reshape(1, -1), ln2_b.reshape(1, -1), t1)
    return yp.reshape(b1, s1, D_MODEL), ys.reshape(b2, s2, D_MODEL)


def kernel(x_prompt, x_sample, w_in, w_out, ln1_g, ln1_b, q_norm_g, w_uq, kv_norm_g, w_uk, w_uv,
           w_router, router_bias, w_gate, w_up, w_down, ws_gate, ws_up, ws_down, ln2_g, ln2_b):
    params = (w_in, w_out, ln1_g, ln1_b, q_norm_g, w_uq, kv_norm_g, w_uk, w_uv, w_router, router_bias,
              w_gate, w_up, w_down, ws_gate, ws_up, ws_down, ln2_g, ln2_b)
    assert all(p.shape[0] == 1 for p in params), "one encoder layer"
    return _forward(x_prompt, x_sample, *[p.reshape(p.shape[1:]) for p in params])
```

```python
import functools
import math

import numpy as np
import jax
import jax.numpy as jnp
from jax import lax
from jax.experimental import pallas as pl
from jax.experimental.pallas import tpu as pltpu

F32 = jnp.float32
BF16 = jnp.bfloat16
I32 = jnp.int32

D_MODEL = 1024
A_HEADS = 8
A_HEAD_DIM = 64
A_WIDTH = A_HEADS * A_HEAD_DIM
A_PATTERNS = ((128, 1), (512, 4), (2048, 16))
A_HALF = 64
B_HEADS = 8
B_NOPE = 64
B_ROPE = 32
B_V = 64
Q_LORA = 256
KV_LORA = 128
ROPE_BASE = 10000.0
N_EXPERTS = 256
TOP_K = 8
N_GROUPS = 8
GROUP_SIZE = N_EXPERTS // N_GROUPS
TOPK_GROUPS = 4
EXPERT_HIDDEN = 256
ROUTED_SCALE = 2.5
LN_EPS = 1e-5
RMS_EPS = 1e-6
NEG_BIG = -1e30
ALPHA = 2.0 ** 0.25
LOG2E = 1.4426950408889634

LANES = 128
HEAD_PAD = 128
SLAB = D_MODEL // LANES
W1_COLS = 3 * A_WIDTH + Q_LORA + KV_LORA + 2 * LANES

TM_PROJ = 256
TQ_A = 128
A_POS_TILES = {1: 4, 4: 2, 16: 1}
PERM_GROUP = 256
LOGITS_ELEMS_B = 4096 * 512
KEY_CHUNK_B = 512
TM_MIX = 256
TM_DISP = 256
ROW_BLOCK = 256
TM_COMB = 128
EXPERT_TILES = 2
N_XBUF = 8
N_OBUF = 4
X_AHEAD = N_XBUF - EXPERT_TILES
VMEM_LIMIT = 48 * 1024 * 1024


def _slab_load(ref, n_tok, tok0=0):
    return jnp.concatenate([ref[pl.ds(tok0 * SLAB + c, n_tok, stride=SLAB), :] for c in range(SLAB)],
                           axis=1)


def _slab_store(ref, val):
    for c in range(SLAB):
        ref[pl.ds(c, val.shape[0], stride=SLAB), :] = val[:, c * LANES:(c + 1) * LANES]


def _slab_load_bf16(ref, stage_ref, n_tok, tok0=0):
    stage_ref[...] = ref[tok0 * SLAB:(tok0 + n_tok) * SLAB, :].astype(F32)
    return _slab_load(stage_ref, n_tok)


def _slab_store_bf16(ref, stage_ref, val):
    _slab_store(stage_ref, val)
    ref[...] = stage_ref[...].astype(BF16)


def _nt_dot(a, b):
    return lax.dot_general(a, b, (((1,), (1,)), ((), ())), preferred_element_type=F32)


def _layer_norm(x, g, b):
    mu = jnp.mean(x, axis=-1, keepdims=True)
    xc = x - mu
    var = jnp.mean(xc * xc, axis=-1, keepdims=True)
    return xc * lax.rsqrt(var + LN_EPS) * g + b


def _rms_norm(x, g):
    return x * lax.rsqrt(jnp.mean(x * x, axis=-1, keepdims=True) + RMS_EPS) * g


def _proj_kernel(xp_ref, xs_ref, w1_ref, qg_ref, kvg_ref, wq_ref, wqr_ref, wuk_ref, wuvt_ref,
                 cos_ref, sin_ref, perm4_ref, perm16_ref,
                 qa_ref, ka_ref, va_ref, q4_ref, k4_ref, v4_ref, q16_ref, k16_ref, v16_ref,
                 qb_ref, kb_ref, vbt_ref, *, n_prompt_tiles):
    i = pl.program_id(0)
    tm = xp_ref.shape[0]
    x = jnp.where(i < n_prompt_tiles, xp_ref[...], xs_ref[...]).astype(BF16)
    p = jnp.dot(x, w1_ref[...], preferred_element_type=F32)
    qa = (p[:, 0:A_WIDTH] * (A_HEAD_DIM ** -0.5 * LOG2E)).astype(BF16)
    ka = p[:, A_WIDTH:2 * A_WIDTH].astype(BF16)
    va = p[:, 2 * A_WIDTH:3 * A_WIDTH].astype(BF16)
    qa_ref[...] = qa
    ka_ref[...] = ka
    va_ref[...] = va
    qkv = jnp.concatenate([qa, ka, va], axis=1)
    for dil, perm_ref, outs in ((4, perm4_ref, (q4_ref, k4_ref, v4_ref)),
                                (16, perm16_ref, (q16_ref, k16_ref, v16_ref))):
        cm = jnp.dot(perm_ref[...], qkv, preferred_element_type=F32).astype(BF16)
        rows = tm // dil
        for r in range(dil):
            for j, o_ref in enumerate(outs):
                o_ref[:, r * A_WIDTH:(r + 1) * A_WIDTH] = cm[r * rows:(r + 1) * rows,
                                                             j * A_WIDTH:(j + 1) * A_WIDTH]
    c0 = 3 * A_WIDTH
    cq = p[:, c0:c0 + Q_LORA]
    ckv = p[:, c0 + Q_LORA:c0 + Q_LORA + KV_LORA]
    kr = p[:, c0 + Q_LORA + KV_LORA:c0 + Q_LORA + KV_LORA + LANES]
    krr = p[:, c0 + Q_LORA + KV_LORA + LANES:c0 + Q_LORA + KV_LORA + 2 * LANES]
    cos = cos_ref[...]
    sin = sin_ref[...]
    cos8 = jnp.concatenate([cos] * B_HEADS, axis=1)
    sin8 = jnp.concatenate([sin] * B_HEADS, axis=1)
    cqn = _rms_norm(cq, qg_ref[...]).astype(BF16)
    q = jnp.dot(cqn, wq_ref[...], preferred_element_type=F32)
    qr = jnp.dot(cqn, wqr_ref[...], preferred_element_type=F32)
    qscale = (B_NOPE + B_ROPE) ** -0.5 * LOG2E
    qb_ref[...] = ((q * cos8 + qr * sin8) * qscale).astype(BF16)
    ckvn = _rms_norm(ckv, kvg_ref[...]).astype(BF16)
    kn = jnp.dot(ckvn, wuk_ref[...], preferred_element_type=F32)
    krope = kr * cos + krr * sin
    kb_ref[...] = (kn + jnp.concatenate([krope] * B_HEADS, axis=1)).astype(BF16)
    vbt_ref[...] = _nt_dot(wuvt_ref[...], ckvn).astype(BF16)


def _class_perm(tm, dil):
    rows = tm // dil
    c = np.arange(tm)
    src = (c % rows) * dil + c // rows
    perm = np.zeros((tm, tm), np.float32)
    perm[c, src] = 1.0
    return jnp.asarray(perm, BF16)


def _proj(xp, xs, w1, qg, kvg, wq, wqr, wuk, wuvt, cos_t, sin_t, s_prompt, s_sample):
    tm = TM_PROJ
    n1 = xp.shape[0] // tm
    n2 = xs.shape[0] // tm
    t_all = xp.shape[0] + xs.shape[0]
    pt, st = s_prompt // tm, s_sample // tm
    assert tm == PERM_GROUP
    perm4, perm16 = _class_perm(tm, 4), _class_perm(tm, 16)

    def tab_idx(i):
        return (jnp.where(i < n1, i % pt, (i - n1) % st), 0)

    full = lambda shape: pl.BlockSpec(shape, lambda i: (0, 0))
    row = lambda cols: pl.BlockSpec((tm, cols), lambda i: (i, 0))
    cls = lambda dil: pl.BlockSpec((tm // dil, dil * A_WIDTH), lambda i: (i, 0))
    cls_shape = lambda dil: jax.ShapeDtypeStruct((t_all // dil, dil * A_WIDTH), BF16)
    return pl.pallas_call(
        functools.partial(_proj_kernel, n_prompt_tiles=n1),
        grid=(n1 + n2,),
        in_specs=[
            pl.BlockSpec((tm, D_MODEL), lambda i: (jnp.minimum(i, n1 - 1), 0)),
            pl.BlockSpec((tm, D_MODEL), lambda i: (jnp.maximum(i - n1, 0), 0)),
            full(w1.shape), full(qg.shape), full(kvg.shape), full(wq.shape), full(wqr.shape),
            full(wuk.shape), full(wuvt.shape),
            pl.BlockSpec((tm, LANES), tab_idx), pl.BlockSpec((tm, LANES), tab_idx),
            full(perm4.shape), full(perm16.shape),
        ],
        out_specs=[row(A_WIDTH)] * 3 + [cls(4)] * 3 + [cls(16)] * 3
        + [row(B_HEADS * HEAD_PAD), row(B_HEADS * HEAD_PAD),
           pl.BlockSpec((B_HEADS * B_V, tm), lambda i: (0, i))],
        out_shape=[jax.ShapeDtypeStruct((t_all, A_WIDTH), BF16)] * 3
        + [cls_shape(4)] * 3 + [cls_shape(16)] * 3
        + [jax.ShapeDtypeStruct((t_all, B_HEADS * HEAD_PAD), BF16)] * 2
        + [jax.ShapeDtypeStruct((B_HEADS * B_V, t_all), BF16)],
        compiler_params=pltpu.CompilerParams(dimension_semantics=("arbitrary",),
                                             vmem_limit_bytes=VMEM_LIMIT),
        name="proj",
    )(xp, xs, w1, qg, kvg, wq, wqr, wuk, wuvt, cos_t, sin_t, perm4, perm16)


def _attn_a_kernel(qblk_ref, pblk_ref, nblk_ref, tstart_ref, llen_ref,
                   q_ref, kp_ref, kc_ref, kn_ref, vp_ref, vc_ref, vn_ref, bias_ref, perm_ref,
                   o_ref, lse_ref, cm_ref, *, n_pos, n_cls):
    n = pl.program_id(0)
    tq, wk = TQ_A, TQ_A + 2 * A_HALF
    lane = lax.broadcasted_iota(I32, (tq, LANES), 1)
    low = lane < A_HEAD_DIM
    colbase = lax.broadcasted_iota(I32, (1, wk), 1) + (tstart_ref[n] - A_HALF)
    cls_len = llen_ref[n]

    def one_class(r):
        cols = slice(0, A_WIDTH) if n_cls == 1 else pl.ds(pl.multiple_of(r * A_WIDTH, A_WIDTH), A_WIDTH)
        kcat = jnp.concatenate([kp_ref[:, cols], kc_ref[:, cols], kn_ref[:, cols]], axis=0)
        vcat = jnp.concatenate([vp_ref[:, cols], vc_ref[:, cols], vn_ref[:, cols]], axis=0)
        for j in range(n_pos):
            rows = slice(j * tq, (j + 1) * tq)
            q = q_ref[rows, cols]
            kw, vw = kcat[j * tq:j * tq + wk, :], vcat[j * tq:j * tq + wk, :]
            col = colbase + j * tq
            colpen = jnp.where((col >= 0) & (col < cls_len), 0.0, NEG_BIG).astype(F32)
            pairs = []
            lse_c = jnp.zeros((tq, LANES), F32)
            for jp in range(A_HEADS // 2):
                sl = slice(jp * LANES, (jp + 1) * LANES)
                qp, kpair, vpair = q[:, sl], kw[:, sl], vw[:, sl]
                outs = []
                for e in range(2):
                    qm = jnp.where(low if e == 0 else ~low, qp, jnp.zeros_like(qp))
                    s = _nt_dot(qm, kpair) + bias_ref[2 * jp + e] + colpen
                    m = jnp.max(s, axis=1, keepdims=True)
                    p = jnp.exp2(s - m)
                    l = jnp.sum(p, axis=1, keepdims=True)
                    outs.append(jnp.dot(p.astype(BF16), vpair, preferred_element_type=F32) / l)
                    lse_c = lse_c + jnp.where(lane == 2 * jp + e, m + jnp.log2(l), 0.0)
                pairs.append(jnp.where(low, outs[0], outs[1]))
            o_full = jnp.concatenate(pairs, axis=1).astype(BF16)
            if n_cls == 1:
                o_ref[rows, :] = o_full
                lse_ref[rows, :] = lse_c
            else:
                hi = lse_c.astype(BF16)
                rest = lse_c - hi.astype(F32)
                mid = rest.astype(BF16)
                lo = (rest - mid.astype(F32)).astype(BF16)
                cm_ref[r, rows, :] = jnp.concatenate([o_full, hi, mid, lo], axis=1)

    if n_cls == 1:
        one_class(0)
        return

    def body(r2, carry):
        one_class(2 * r2)
        one_class(2 * r2 + 1)
        return carry

    lax.fori_loop(0, n_cls // 2, body, 0)
    per = PERM_GROUP // n_cls
    for a in range(n_pos * tq * n_cls // PERM_GROUP):
        stack = jnp.concatenate([cm_ref[r, a * per:(a + 1) * per, :] for r in range(n_cls)], axis=0)
        nat = jnp.dot(perm_ref[...], stack, preferred_element_type=F32)
        rows = slice(a * PERM_GROUP, (a + 1) * PERM_GROUP)
        o_ref[rows, :] = nat[:, :A_WIDTH].astype(BF16)
        lse_ref[rows, :] = (nat[:, A_WIDTH:A_WIDTH + LANES] + nat[:, A_WIDTH + LANES:A_WIDTH + 2 * LANES]
                            + nat[:, A_WIDTH + 2 * LANES:])


def _attn_a_tables(seqs, dil, n_pos):
    rows = n_pos * TQ_A
    per_halo = rows // A_HALF
    qblk, pblk, nblk, tstart, llen = [], [], [], [], []
    for off, s_len in seqs:
        cls = s_len // dil
        steps = cls // rows
        base = (off // dil) // rows
        for i in range(steps):
            qblk.append(base + i)
            pblk.append((base + i) * per_halo - (1 if i > 0 else 0))
            nblk.append((base + i + 1) * per_halo - (0 if i < steps - 1 else 1))
            tstart.append(i * rows)
            llen.append(cls)
    return [jnp.asarray(np.asarray(a, np.int32)) for a in (qblk, pblk, nblk, tstart, llen)]


def _attn_a_bias(dil):
    tq, wk = TQ_A, TQ_A + 2 * A_HALF
    delta = np.abs(np.arange(wk)[None, :] - A_HALF - np.arange(tq)[:, None]).astype(np.float64)
    slopes = 2.0 ** (-8.0 * (np.arange(A_HEADS) + 1.0) / A_HEADS)
    bias = -slopes[:, None, None] * (delta * dil)[None] * LOG2E
    bias = np.where((delta <= A_HALF)[None], bias, NEG_BIG)
    return jnp.asarray(bias.astype(np.float32))


def _attn_a(qc, kc, vc, seqs, dil):
    t_all = qc.shape[0] * dil
    n_pos = A_POS_TILES[dil]
    rows, width = n_pos * TQ_A, dil * A_WIDTH
    tabs = _attn_a_tables(seqs, dil, n_pos)
    n_steps = int(tabs[0].shape[0])
    bias = _attn_a_bias(dil)
    perm = _class_perm(PERM_GROUP, dil).T
    cur = pl.BlockSpec((rows, width), lambda n, qb, pb, nb, ts, ll: (qb[n], 0))
    prev = pl.BlockSpec((A_HALF, width), lambda n, qb, pb, nb, ts, ll: (pb[n], 0))
    nxt = pl.BlockSpec((A_HALF, width), lambda n, qb, pb, nb, ts, ll: (nb[n], 0))
    tok = lambda cols: pl.BlockSpec((rows * dil, cols), lambda n, *_: (n, 0))
    return pl.pallas_call(
        functools.partial(_attn_a_kernel, n_pos=n_pos, n_cls=dil),
        grid_spec=pltpu.PrefetchScalarGridSpec(
            num_scalar_prefetch=5, grid=(n_steps,),
            in_specs=[cur, prev, cur, nxt, prev, cur, nxt,
                      pl.BlockSpec(bias.shape, lambda n, *_: (0, 0, 0)),
                      pl.BlockSpec(perm.shape, lambda n, *_: (0, 0))],
            out_specs=[tok(A_WIDTH), tok(LANES)],
            scratch_shapes=[pltpu.VMEM((dil, rows, A_WIDTH + 3 * LANES), BF16)]),
        out_shape=[jax.ShapeDtypeStruct((t_all, A_WIDTH), BF16),
                   jax.ShapeDtypeStruct((t_all, LANES), F32)],
        compiler_params=pltpu.CompilerParams(dimension_semantics=("arbitrary",),
                                             vmem_limit_bytes=VMEM_LIMIT),
        name=f"attn_a_d{dil}",
    )(*tabs, qc, kc, kc, kc, vc, vc, vc, bias, perm)


def _attn_b_kernel(q_ref, k_ref, vt_ref, o_ref, sta_ref, stb_ref, ma_ref, mb_ref):
    s = pl.program_id(0)
    tq, s_len = q_ref.shape[0], k_ref.shape[0]

    @pl.when(s == 0)
    def _():
        stb_ref[...] = jnp.zeros_like(stb_ref)
        mb_ref[...] = jnp.zeros_like(mb_ref)

    def step(st_new_ref, m_new_ref, st_old_ref, m_old_ref):
        m_old = m_old_ref[...]
        q = q_ref[...]
        m_new = jnp.full((1, tq), -jnp.inf, F32)
        l = jnp.zeros((1, tq), F32)
        acc = jnp.zeros((B_V, tq), F32)
        for c in range(s_len // KEY_CHUNK_B):
            ks = slice(c * KEY_CHUNK_B, (c + 1) * KEY_CHUNK_B)
            st_c = _nt_dot(k_ref[ks, :], q)
            st_new_ref[ks, :] = st_c
            m_new = jnp.maximum(m_new, jnp.max(st_c, axis=0, keepdims=True))
            p = jnp.exp2(st_old_ref[ks, :] - m_old)
            l = l + jnp.sum(p, axis=0, keepdims=True)
            acc = acc + jnp.dot(vt_ref[:, ks], p.astype(BF16), preferred_element_type=F32)
        m_new_ref[...] = m_new
        o_ref[...] = (acc / l).astype(BF16)

    @pl.when(s % 2 == 0)
    def _():
        step(sta_ref, ma_ref, stb_ref, mb_ref)

    @pl.when(s % 2 == 1)
    def _():
        step(stb_ref, mb_ref, sta_ref, ma_ref)


def _attn_b(qb, kb, vbt, off, n_batch, s_len):
    tq = min(s_len, LOGITS_ELEMS_B // s_len)
    nq = s_len // tq
    qbase, kbase = off // tq, off // s_len
    n_items = n_batch * B_HEADS * nq

    def split(item):
        return item // (B_HEADS * nq), (item // nq) % B_HEADS, item % nq

    def q_map(s):
        b, h, qi = split(jnp.minimum(s, n_items - 1))
        return (qbase + b * nq + qi, h)

    def k_map(s):
        b, h, _ = split(jnp.minimum(s, n_items - 1))
        return (kbase + b, h)

    def v_map(s):
        b, h, _ = split(jnp.maximum(s - 1, 0))
        return (h, kbase + b)

    def o_map(s):
        b, h, qi = split(jnp.maximum(s - 1, 0))
        return (h, b * nq + qi)

    return pl.pallas_call(
        _attn_b_kernel,
        grid=(n_items + 1,),
        in_specs=[pl.BlockSpec((tq, HEAD_PAD), q_map),
                  pl.BlockSpec((s_len, HEAD_PAD), k_map),
                  pl.BlockSpec((B_V, s_len), v_map)],
        out_specs=pl.BlockSpec((B_V, tq), o_map),
        out_shape=jax.ShapeDtypeStruct((B_HEADS * B_V, n_batch * s_len), BF16),
        scratch_shapes=[pltpu.VMEM((s_len, tq), F32), pltpu.VMEM((s_len, tq), F32),
                        pltpu.VMEM((1, tq), F32), pltpu.VMEM((1, tq), F32)],
        compiler_params=pltpu.CompilerParams(dimension_semantics=("arbitrary",),
                                             vmem_limit_bytes=VMEM_LIMIT),
        name=f"attn_b_s{s_len}",
    )(qb, kb, vbt)


def _mix_kernel(xp_ref, xs_ref, o0_ref, o1_ref, o2_ref, l0_ref, l1_ref, l2_ref, spread_ref,
                obp_ref, obs_ref, wout_ref,
                g_ref, b_ref, wrh_ref, wrl_ref, rb_ref,
                h_ref, hp_ref, tope_ref, pos_ref, gate_ref, cnt_out_ref, cnt_ref, stage_ref,
                *, n_prompt_tiles):
    i = pl.program_id(0)
    tm = xp_ref.shape[0]

    @pl.when(i == 0)
    def _():
        cnt_ref[...] = jnp.zeros_like(cnt_ref)

    l0, l1, l2 = l0_ref[...], l1_ref[...], l2_ref[...]
    lmax = jnp.maximum(jnp.maximum(l0, l1), l2)
    e0, e1, e2 = jnp.exp2(l0 - lmax), jnp.exp2(l1 - lmax), jnp.exp2(l2 - lmax)
    inv = 1.0 / (e0 + e1 + e2)
    spread = spread_ref[...]

    def per_lane(w):
        hi = w.astype(BF16)
        lo = (w - hi.astype(F32)).astype(BF16)
        return (jnp.dot(hi, spread, preferred_element_type=F32)
                + jnp.dot(lo, spread, preferred_element_type=F32))

    oa = (per_lane(e0 * inv) * o0_ref[...].astype(F32) + per_lane(e1 * inv) * o1_ref[...].astype(F32)
          + per_lane(e2 * inv) * o2_ref[...].astype(F32))
    is_prompt = i < n_prompt_tiles
    obt = jnp.where(is_prompt, obp_ref[...], obs_ref[...])
    mix = (jnp.dot(oa.astype(BF16), wout_ref[0:A_WIDTH, :], preferred_element_type=F32)
           + lax.dot_general(obt, wout_ref[A_WIDTH:, :], (((0,), (0,)), ((), ())),
                             preferred_element_type=F32))
    x = jnp.where(is_prompt, xp_ref[...], xs_ref[...])
    h = _layer_norm(ALPHA * x + mix, g_ref[...], b_ref[...])
    h_ref[...] = h
    _slab_store_bf16(hp_ref, stage_ref, h)

    h_hi = h.astype(BF16)
    h_lo = (h - h_hi.astype(F32)).astype(BF16)
    wrh = wrh_ref[...]
    logits = _nt_dot(wrh, h_hi) + _nt_dot(wrh, h_lo) + _nt_dot(wrl_ref[...], h_hi)
    scores = jax.nn.sigmoid(logits)
    sel = scores + rb_ref[...]

    sub = lax.broadcasted_iota(I32, (GROUP_SIZE, tm), 0).astype(F32)
    gscore = []
    for g in range(N_GROUPS):
        sg = sel[g * GROUP_SIZE:(g + 1) * GROUP_SIZE, :]
        m1 = jnp.max(sg, axis=0, keepdims=True)
        first = jnp.min(jnp.where(sg == m1, sub, float(GROUP_SIZE)), axis=0, keepdims=True)
        m2 = jnp.max(jnp.where(sub == first, -jnp.inf, sg), axis=0, keepdims=True)
        gscore.append(m1 + m2)
    cands = []
    for g in range(N_GROUPS):
        beaten = jnp.zeros((1, tm), F32)
        for g2 in range(N_GROUPS):
            if g2 == g:
                continue
            wins = (gscore[g2] > gscore[g]) | ((gscore[g2] == gscore[g]) & (g2 < g))
            beaten = beaten + wins.astype(F32)
        keep = beaten < float(TOPK_GROUPS)
        sg = sel[g * GROUP_SIZE:(g + 1) * GROUP_SIZE, :]
        cands.append(jnp.where(keep, sg, NEG_BIG))
    cand = jnp.concatenate(cands, axis=0)

    eidx = lax.broadcasted_iota(I32, (N_EXPERTS, tm), 0).astype(F32)
    picked_idx, picked_gate = [], []
    onehot = jnp.zeros((N_EXPERTS, tm), F32)
    for _ in range(TOP_K):
        mx = jnp.max(cand, axis=0, keepdims=True)
        fi = jnp.min(jnp.where(cand == mx, eidx, float(N_EXPERTS)), axis=0, keepdims=True)
        pick = eidx == fi
        picked_idx.append(fi)
        picked_gate.append(jnp.sum(jnp.where(pick, scores, 0.0), axis=0, keepdims=True))
        onehot = onehot + pick.astype(F32)
        cand = jnp.where(pick, -jnp.inf, cand)
    gsum = picked_gate[0]
    for k in range(1, TOP_K):
        gsum = gsum + picked_gate[k]

    tri = (lax.broadcasted_iota(I32, (tm, tm), 0) < lax.broadcasted_iota(I32, (tm, tm), 1))
    before = jnp.dot(onehot.astype(BF16), tri.astype(BF16), preferred_element_type=F32)
    rank = before + cnt_ref[:, 0:1]
    for k in range(TOP_K):
        pick = eidx == picked_idx[k]
        tope_ref[k:k + 1, :] = picked_idx[k].astype(I32)
        pos_ref[k:k + 1, :] = jnp.sum(jnp.where(pick, rank, 0.0), axis=0, keepdims=True).astype(I32)
        gate_ref[k:k + 1, :] = picked_gate[k] / gsum * ROUTED_SCALE
    cnt_ref[...] = cnt_ref[...] + jnp.sum(onehot, axis=1, keepdims=True)
    cnt_out_ref[...] = cnt_ref[...]


def _mix(xp, xs, outs, lses, obp, obs, wout, g, b, wrh, wrl, rb):
    tm = TM_MIX
    n1 = xp.shape[0] // tm
    n2 = xs.shape[0] // tm
    t_all = xp.shape[0] + xs.shape[0]
    full = lambda a: pl.BlockSpec(a.shape, lambda i: (0,) * a.ndim)
    row = lambda cols: pl.BlockSpec((tm, cols), lambda i: (i, 0))
    prow = lambda cols: pl.BlockSpec((tm, cols), lambda i: (jnp.minimum(i, n1 - 1), 0))
    srow = lambda cols: pl.BlockSpec((tm, cols), lambda i: (jnp.maximum(i - n1, 0), 0))
    col = pl.BlockSpec((TOP_K, tm), lambda i: (0, i))
    head_of_lane = np.arange(A_WIDTH) // A_HEAD_DIM
    spread = jnp.asarray(np.arange(LANES)[:, None] == head_of_lane[None, :], BF16)
    return pl.pallas_call(
        functools.partial(_mix_kernel, n_prompt_tiles=n1),
        grid=(n1 + n2,),
        in_specs=[
            prow(D_MODEL), srow(D_MODEL),
            row(A_WIDTH), row(A_WIDTH), row(A_WIDTH), row(LANES), row(LANES), row(LANES), full(spread),
            pl.BlockSpec((B_HEADS * B_V, tm), lambda i: (0, jnp.minimum(i, n1 - 1))),
            pl.BlockSpec((B_HEADS * B_V, tm), lambda i: (0, jnp.maximum(i - n1, 0))),
            full(wout), full(g), full(b), full(wrh), full(wrl), full(rb),
        ],
        out_specs=[row(D_MODEL), pl.BlockSpec((SLAB * tm, LANES), lambda i: (i, 0)), col, col, col,
                   pl.BlockSpec((N_EXPERTS, LANES), lambda i: (0, 0))],
        out_shape=[jax.ShapeDtypeStruct((t_all, D_MODEL), F32),
                   jax.ShapeDtypeStruct((SLAB * t_all, LANES), BF16),
                   jax.ShapeDtypeStruct((TOP_K, t_all), I32),
                   jax.ShapeDtypeStruct((TOP_K, t_all), I32),
                   jax.ShapeDtypeStruct((TOP_K, t_all), F32),
                   jax.ShapeDtypeStruct((N_EXPERTS, LANES), F32)],
        scratch_shapes=[pltpu.VMEM((N_EXPERTS, LANES), F32), pltpu.VMEM((SLAB * tm, LANES), F32)],
        compiler_params=pltpu.CompilerParams(dimension_semantics=("arbitrary",),
                                             vmem_limit_bytes=VMEM_LIMIT),
        name="mix_router",
    )(xp, xs, *outs, *lses, spread, obp, obs, wout, g, b, wrh, wrl, rb)


def _dispatch_kernel(zstart_ref, nused_ref, h_ref, tope_ref, pos_ref, pstart_ref, buf_ref, dest_ref,
                     dsm_ref, zero_ref, sem_ref, zsem_ref):
    i = pl.program_id(0)
    tm = tope_ref.shape[1]
    tile_rows = SLAB * ROW_BLOCK

    @pl.when(i == 0)
    def _():
        zero_ref[...] = jnp.zeros_like(zero_ref)

        def zfill(e, carry):
            start = pl.multiple_of(zstart_ref[e], SLAB)
            pltpu.make_async_copy(zero_ref, buf_ref.at[pl.ds(start, tile_rows), :], zsem_ref).start()
            return carry

        lax.fori_loop(0, N_EXPERTS, zfill, 0)
        span = buf_ref.at[pl.ds(0, N_EXPERTS * tile_rows), :]
        pltpu.make_async_copy(span, span, zsem_ref).wait()

        def ztail(j, carry):
            start = pl.multiple_of(j * tile_rows, tile_rows)
            pltpu.make_async_copy(zero_ref, buf_ref.at[pl.ds(start, tile_rows), :], zsem_ref).start()
            return carry

        def zwait(j, carry):
            pltpu.make_async_copy(zero_ref, buf_ref.at[pl.ds(0, tile_rows), :], zsem_ref).wait()
            return carry

        n_tiles = buf_ref.shape[0] // tile_rows
        lax.fori_loop(nused_ref[0], n_tiles, ztail, 0)
        lax.fori_loop(nused_ref[0], n_tiles, zwait, 0)

    eidx = lax.broadcasted_iota(I32, (N_EXPERTS, tm), 0)
    pstart = pstart_ref[...]
    for k in range(TOP_K):
        hit = eidx == tope_ref[k:k + 1, :]
        base = jnp.sum(jnp.where(hit, pstart, 0.0), axis=0, keepdims=True)
        dest_ref[k:k + 1, :] = (pos_ref[k:k + 1, :] + base.astype(I32)) * SLAB
    pltpu.sync_copy(dest_ref, dsm_ref)

    def scatter(t, carry):
        src = h_ref.at[pl.ds(pl.multiple_of(t * SLAB, SLAB), SLAB), :]
        for k in range(TOP_K):
            dst = buf_ref.at[pl.ds(pl.multiple_of(dsm_ref[k, t], SLAB), SLAB), :]
            pltpu.make_async_copy(src, dst, sem_ref).start(priority=k % 2)
        return carry

    lax.fori_loop(0, tm, scatter, 0, unroll=8)
    span = buf_ref.at[pl.ds(0, TOP_K * tm * SLAB), :]
    pltpu.make_async_copy(span, span, sem_ref).wait()


def _dispatch(h, tope, pos, pstart, zstart, nused, n_rows):
    tm = TM_DISP
    t_all = tope.shape[1]
    col = pl.BlockSpec((TOP_K, tm), lambda i, z, nu: (0, i))
    return pl.pallas_call(
        _dispatch_kernel,
        grid_spec=pltpu.PrefetchScalarGridSpec(
            num_scalar_prefetch=2, grid=(t_all // tm,),
            in_specs=[pl.BlockSpec((SLAB * tm, LANES), lambda i, z, nu: (i, 0)), col, col,
                      pl.BlockSpec((N_EXPERTS, 1), lambda i, z, nu: (0, 0))],
            out_specs=[pl.BlockSpec(memory_space=pl.ANY), col],
            scratch_shapes=[pltpu.SMEM((TOP_K, tm), I32),
                            pltpu.VMEM((SLAB * ROW_BLOCK, LANES), BF16),
                            pltpu.SemaphoreType.DMA, pltpu.SemaphoreType.DMA]),
        out_shape=[jax.ShapeDtypeStruct((SLAB * n_rows, LANES), BF16),
                   jax.ShapeDtypeStruct((TOP_K, t_all), I32)],
        compiler_params=pltpu.CompilerParams(dimension_semantics=("arbitrary",),
                                             vmem_limit_bytes=VMEM_LIMIT),
        name="dispatch",
    )(zstart, nused, h, tope, pos, pstart)


def _expert_kernel(tfirst_ref, ntile_ref, nused_ref, x_hbm, wg_ref, wu_ref, wd_ref, o_hbm,
                   xbuf, obuf, stage_ref, wgb_ref, wub_ref, wdb_ref, xsem, osem):
    e = pl.program_id(0)
    tile_rows = SLAB * ROW_BLOCK
    nused = nused_ref[0]

    def x_copy(g, slot):
        start = pl.multiple_of(g * tile_rows, tile_rows)
        return pltpu.make_async_copy(x_hbm.at[pl.ds(start, tile_rows), :], xbuf.at[slot], xsem.at[slot])

    def o_copy(g, slot):
        start = pl.multiple_of(g * tile_rows, tile_rows)
        return pltpu.make_async_copy(obuf.at[slot], o_hbm.at[pl.ds(start, tile_rows), :], osem.at[slot])

    @pl.when(e == 0)
    def _():
        for j in range(X_AHEAD):
            @pl.when(j < nused)
            def _():
                x_copy(j, j).start()

    n_e = ntile_ref[e]

    @pl.when(n_e > 0)
    def _():
        wgb_ref[...] = wg_ref[...].astype(BF16)
        wub_ref[...] = wu_ref[...].astype(BF16)
        wdb_ref[...] = wd_ref[...].astype(BF16)

    def run_tiles(g0, n):
        for q in range(n):
            ahead = g0 + q + X_AHEAD

            @pl.when(ahead < nused)
            def _():
                x_copy(ahead, ahead % N_XBUF).start()

        for q in range(n):
            g = g0 + q
            x_copy(g, g % N_XBUF).wait()

            @pl.when(g >= N_OBUF)
            def _():
                o_copy(g - N_OBUF, g % N_OBUF).wait()

        for q in range(n):
            g = g0 + q
            x = _slab_load_bf16(xbuf.at[g % N_XBUF], stage_ref.at[q], ROW_BLOCK).astype(BF16)
            gt = jnp.dot(x, wgb_ref[...], preferred_element_type=F32)
            up = jnp.dot(x, wub_ref[...], preferred_element_type=F32)
            hmid = (gt * jax.nn.sigmoid(gt) * up).astype(BF16)
            out = jnp.dot(hmid, wdb_ref[...], preferred_element_type=F32)
            _slab_store(obuf.at[g % N_OBUF], out)

        for q in range(n):
            g = g0 + q
            o_copy(g, g % N_OBUF).start()

    def group(j, carry):
        run_tiles(tfirst_ref[e] + EXPERT_TILES * j, EXPERT_TILES)
        return carry

    n_groups = n_e // EXPERT_TILES
    lax.fori_loop(0, n_groups, group, 0)
    done = n_groups * EXPERT_TILES
    left = n_e - done
    size = EXPERT_TILES // 2
    while size >= 1:
        @pl.when((left & size) != 0)
        def _(size=size, done=done):
            run_tiles(tfirst_ref[e] + done, size)

        done = done + (left & size)
        size //= 2

    @pl.when(e == N_EXPERTS - 1)
    def _():
        for j in range(N_OBUF):
            @pl.when(nused > j)
            def _():
                o_copy(nused - 1 - j, (nused - 1 - j) % N_OBUF).wait()

        zero_ref = stage_ref.at[0]
        zero_ref[...] = jnp.zeros_like(zero_ref)
        n_tiles = o_hbm.shape[0] // tile_rows

        def tail_copy(g):
            start = pl.multiple_of(g * tile_rows, tile_rows)
            return pltpu.make_async_copy(zero_ref, o_hbm.at[pl.ds(start, tile_rows), :], osem.at[0])

        def ztail(g, carry):
            tail_copy(g).start()
            return carry

        def zwait(g, carry):
            tail_copy(g).wait()
            return carry

        lax.fori_loop(nused, n_tiles, ztail, 0)
        lax.fori_loop(nused, n_tiles, zwait, 0)


def _experts(buf, tfirst, ntile, nused, w_gate, w_up, w_down):
    tile_rows = SLAB * ROW_BLOCK
    wmap = lambda e, tf, nt, nu: (e, 0, 0)
    return pl.pallas_call(
        _expert_kernel,
        grid_spec=pltpu.PrefetchScalarGridSpec(
            num_scalar_prefetch=3, grid=(N_EXPERTS,),
            in_specs=[pl.BlockSpec(memory_space=pl.ANY),
                      pl.BlockSpec((None, D_MODEL, EXPERT_HIDDEN), wmap),
                      pl.BlockSpec((None, D_MODEL, EXPERT_HIDDEN), wmap),
                      pl.BlockSpec((None, EXPERT_HIDDEN, D_MODEL), wmap)],
            out_specs=pl.BlockSpec(memory_space=pl.ANY),
            scratch_shapes=[pltpu.VMEM((N_XBUF, tile_rows, LANES), BF16),
                            pltpu.VMEM((N_OBUF, tile_rows, LANES), F32),
                            pltpu.VMEM((EXPERT_TILES, tile_rows, LANES), F32),
                            pltpu.VMEM((D_MODEL, EXPERT_HIDDEN), BF16),
                            pltpu.VMEM((D_MODEL, EXPERT_HIDDEN), BF16),
                            pltpu.VMEM((EXPERT_HIDDEN, D_MODEL), BF16),
                            pltpu.SemaphoreType.DMA((N_XBUF,)),
                            pltpu.SemaphoreType.DMA((N_OBUF,))]),
        out_shape=jax.ShapeDtypeStruct(buf.shape, F32),
        compiler_params=pltpu.CompilerParams(dimension_semantics=("arbitrary",),
                                             vmem_limit_bytes=VMEM_LIMIT),
        name="experts",
    )(tfirst, ntile, nused, buf, w_gate, w_up, w_down)


def _combine_kernel(h_ref, gate_ref, dest_ref, dest1_ref, dest2_ref, eo_ref, wsg_ref, wsu_ref, wsd_ref,
                    g_ref, b_ref, yp_ref, ys_ref, dsm_ref, rows_ref, sem_ref, *, n_prompt_tiles, n_tiles):
    i = pl.program_id(0)
    tm = gate_ref.shape[0]
    slot = i % 2
    rows = rows_ref.at[slot]
    nxt_rows = rows_ref.at[1 - slot]
    nxt_sem = sem_ref.at[1 - slot]

    @pl.when(i == 0)
    def _():
        pltpu.sync_copy(dest_ref, dsm_ref)

        def gather(t, carry):
            for k in range(TOP_K):
                src = eo_ref.at[pl.ds(pl.multiple_of(dsm_ref[k, t], SLAB), SLAB), :]
                dst = rows_ref.at[0, pl.ds(pl.multiple_of((k * tm + t) * SLAB, SLAB), SLAB), :]
                pltpu.make_async_copy(src, dst, sem_ref.at[0]).start(priority=k % 2)
            return carry

        lax.fori_loop(0, tm, gather, 0, unroll=8)
        pltpu.sync_copy(dest1_ref, dsm_ref)

    def issue_next(k):
        for t in range(tm):
            src = eo_ref.at[pl.ds(pl.multiple_of(dsm_ref[k, t], SLAB), SLAB), :]
            dst = nxt_rows.at[pl.ds((k * tm + t) * SLAB, SLAB), :]
            pltpu.make_async_copy(src, dst, nxt_sem).start(priority=t % 2)

    h = h_ref[...]
    hb = h.astype(BF16)
    sg = jnp.dot(hb, wsg_ref[...], preferred_element_type=F32)
    su = jnp.dot(hb, wsu_ref[...], preferred_element_type=F32)
    shared = jnp.dot((sg * jax.nn.sigmoid(sg) * su).astype(BF16), wsd_ref[...],
                     preferred_element_type=F32)

    issue_next(0)
    whole = eo_ref.at[pl.ds(0, TOP_K * tm * SLAB), :]
    pltpu.make_async_copy(whole, rows, sem_ref.at[slot]).wait()
    gate = gate_ref[...]
    routed = gate[:, 0:1] * _slab_load(rows, tm)
    for k in range(1, TOP_K):
        issue_next(k)
        routed = routed + gate[:, k:k + 1] * _slab_load(rows, tm, k * tm)
    y = _layer_norm(ALPHA * h + (routed + shared), g_ref[...], b_ref[...])

    @pl.when(i < n_prompt_tiles)
    def _():
        yp_ref[...] = y

    @pl.when(i >= n_prompt_tiles)
    def _():
        ys_ref[...] = y

    pltpu.sync_copy(dest2_ref, dsm_ref)

    @pl.when(i == n_tiles - 1)
    def _():
        pltpu.make_async_copy(whole, nxt_rows, nxt_sem).wait()


def _combine(h, gate_t, dest, eo, wsg, wsu, wsd, g, b, t_prompt):
    tm = TM_COMB
    t_all = gate_t.shape[0]
    n1 = t_prompt // tm
    n2 = (t_all - t_prompt) // tm
    full = lambda a: pl.BlockSpec(a.shape, lambda i: (0,) * a.ndim)
    return pl.pallas_call(
        functools.partial(_combine_kernel, n_prompt_tiles=n1, n_tiles=n1 + n2),
        grid=(n1 + n2,),
        in_specs=[pl.BlockSpec((tm, D_MODEL), lambda i: (i, 0)),
                  pl.BlockSpec((tm, TOP_K), lambda i: (i, 0)),
                  pl.BlockSpec((TOP_K, tm), lambda i: (0, i)),
                  pl.BlockSpec((TOP_K, tm), lambda i: (0, jnp.minimum(i + 1, n1 + n2 - 1))),
                  pl.BlockSpec((TOP_K, tm), lambda i: (0, jnp.minimum(i + 2, n1 + n2 - 1))),
                  pl.BlockSpec(memory_space=pl.ANY),
                  full(wsg), full(wsu), full(wsd), full(g), full(b)],
        out_specs=[pl.BlockSpec((tm, D_MODEL), lambda i: (jnp.minimum(i, n1 - 1), 0)),
                   pl.BlockSpec((tm, D_MODEL), lambda i: (jnp.maximum(i - n1, 0), 0))],
        out_shape=[jax.ShapeDtypeStruct((t_prompt, D_MODEL), F32),
                   jax.ShapeDtypeStruct((t_all - t_prompt, D_MODEL), F32)],
        scratch_shapes=[pltpu.SMEM((TOP_K, tm), I32),
                        pltpu.VMEM((2, TOP_K * tm * SLAB, LANES), F32),
                        pltpu.SemaphoreType.DMA((2,))],
        compiler_params=pltpu.CompilerParams(dimension_semantics=("arbitrary",),
                                             vmem_limit_bytes=VMEM_LIMIT),
        name="combine",
    )(h, gate_t, dest, dest, dest, eo, wsg, wsu, wsd, g, b)


def _rope_tables(s_max):
    inv_freq = ROPE_BASE ** (-jnp.arange(0, B_ROPE, 2, dtype=F32) / B_ROPE)
    ang = jnp.arange(s_max, dtype=F32)[:, None] * inv_freq[None, :]
    cos, sin = jnp.cos(ang), jnp.sin(ang)
    ones = jnp.ones((s_max, B_NOPE), F32)
    zeros_n = jnp.zeros((s_max, B_NOPE), F32)
    zeros_p = jnp.zeros((s_max, HEAD_PAD - B_NOPE - B_ROPE), F32)
    return (jnp.concatenate([ones, cos, cos, zeros_p], axis=1),
            jnp.concatenate([zeros_n, sin, sin, zeros_p], axis=1))


def _rot_cols(w):
    half = B_ROPE // 2
    return jnp.concatenate([-w[..., half:], w[..., :half]], axis=-1)


def _layout_weights(w_in, w_uq, w_uk):
    c_kr = 3 * A_WIDTH + Q_LORA + KV_LORA
    w_kr = w_in[:, c_kr:c_kr + B_ROPE]
    pad_l = jnp.zeros((D_MODEL, B_NOPE), F32)
    pad_r = jnp.zeros((D_MODEL, HEAD_PAD - B_NOPE - B_ROPE), F32)
    w1 = jnp.concatenate([w_in[:, :c_kr], pad_l, w_kr, pad_r, pad_l, _rot_cols(w_kr), pad_r], axis=1)
    wq3 = w_uq.reshape(Q_LORA, B_HEADS, B_NOPE + B_ROPE)
    nope, rope = wq3[..., :B_NOPE], wq3[..., B_NOPE:]
    zpad = jnp.zeros((Q_LORA, B_HEADS, HEAD_PAD - B_NOPE - B_ROPE), F32)
    wq = jnp.concatenate([nope, rope, zpad], axis=-1).reshape(Q_LORA, B_HEADS * HEAD_PAD)
    wqr = jnp.concatenate([jnp.zeros_like(nope), _rot_cols(rope), zpad], axis=-1)
    wqr = wqr.reshape(Q_LORA, B_HEADS * HEAD_PAD)
    wk3 = w_uk.reshape(KV_LORA, B_HEADS, B_NOPE)
    wuk = jnp.concatenate([wk3, jnp.zeros((KV_LORA, B_HEADS, HEAD_PAD - B_NOPE), F32)], axis=-1)
    wuk = wuk.reshape(KV_LORA, B_HEADS * HEAD_PAD)
    return w1.astype(BF16), wq.astype(BF16), wqr.astype(BF16), wuk.astype(BF16)


def _forward(x_prompt, x_sample, w_in, w_out, ln1_g, ln1_b, q_norm_g, w_uq, kv_norm_g, w_uk, w_uv,
             w_router, router_bias, w_gate, w_up, w_down, ws_gate, ws_up, ws_down, ln2_g, ln2_b):
    b1, s1, _ = x_prompt.shape
    b2, s2, _ = x_sample.shape
    t1, t2 = b1 * s1, b2 * s2
    t_all = t1 + t2
    xp = x_prompt.reshape(t1, D_MODEL)
    xs = x_sample.reshape(t2, D_MODEL)
    seqs = [(b * s1, s1) for b in range(b1)] + [(t1 + b * s2, s2) for b in range(b2)]
    assert t1 % s2 == 0 and s1 % (TQ_A * 16) == 0 and s2 % (TQ_A * 16) == 0

    w1, wq, wqr, wuk = _layout_weights(w_in, w_uq, w_uk)
    cos_t, sin_t = _rope_tables(max(s1, s2))
    (qa, ka, va, q4, k4, v4, q16, k16, v16, qb, kb, vbt) = _proj(
        xp, xs, w1, q_norm_g.reshape(1, -1), kv_norm_g.reshape(1, -1), wq, wqr, wuk,
        w_uv.T.astype(BF16), cos_t, sin_t, s1, s2)

    outs, lses = [], []
    for (_, dil), qkv in zip(A_PATTERNS, ((qa, ka, va), (q4, k4, v4), (q16, k16, v16))):
        o, lse = _attn_a(*qkv, seqs, dil)
        outs.append(o)
        lses.append(lse)
    obp = _attn_b(qb, kb, vbt, 0, b1, s1)
    obs = _attn_b(qb, kb, vbt, t1, b2, s2)

    wr_t = w_router.T
    wr_hi = wr_t.astype(BF16)
    wr_lo = (wr_t - wr_hi.astype(F32)).astype(BF16)
    h, h_slab, tope, pos, gate, counts = _mix(xp, xs, outs, lses, obp, obs, w_out.astype(BF16),
                                                ln1_g.reshape(1, -1), ln1_b.reshape(1, -1),
                                                wr_hi, wr_lo, router_bias.reshape(-1, 1))

    cnt = counts[:, 0].astype(I32)
    padded = ((cnt + ROW_BLOCK - 1) // ROW_BLOCK) * ROW_BLOCK
    pend = jnp.cumsum(padded)
    pstart = pend - padded
    n_tiles = (t_all * TOP_K) // ROW_BLOCK + N_EXPERTS + 1
    nused = (pend[-1:] // ROW_BLOCK).astype(I32)

    buf, dest = _dispatch(h_slab, tope, pos, pstart.astype(F32).reshape(-1, 1),
                          ((pstart + cnt) * SLAB).astype(I32), nused, n_tiles * ROW_BLOCK)
    eo = _experts(buf, (pstart // ROW_BLOCK).astype(I32), (padded // ROW_BLOCK).astype(I32), nused,
                  w_gate, w_up, w_down)
    yp, ys = _combine(h, gate.T, dest, eo, ws_gate.astype(BF16), ws_up.astype(BF16),
                      ws_down.astype(BF16), ln2_g.reshape(1, -1), ln2_b.reshape(1, -1), t1)
    return yp.reshape(b1, s1, D_MODEL), ys.reshape(b2, s2, D_MODEL)


def kernel(x_prompt, x_sample, w_in, w_out, ln1_g, ln1_b, q_norm_g, w_uq, kv_norm_g, w_uk, w_uv,
           w_router, router_bias, w_gate, w_up, w_down, ws_gate, ws_up, ws_down, ln2_g, ln2_b):
    params = (w_in, w_out, ln1_g, ln1_b, q_norm_g, w_uq, kv_norm_g, w_uk, w_uv, w_router, router_bias,
              w_gate, w_up, w_down, ws_gate, ws_up, ws_down, ln2_g, ln2_b)
    assert all(p.shape[0] == 1 for p in params), "one encoder layer"
    return _forward(x_prompt, x_sample, *[p.reshape(p.shape[1:]) for p in params])
```

```python
import functools
import math

import numpy as np
import jax
import jax.numpy as jnp
from jax import lax
from jax.experimental import pallas as pl
from jax.experimental.pallas import tpu as pltpu

F32 = jnp.float32
BF16 = jnp.bfloat16
I32 = jnp.int32

D_MODEL = 1024
A_HEADS = 8
A_HEAD_DIM = 64
A_WIDTH = A_HEADS * A_HEAD_DIM
A_PATTERNS = ((128, 1), (512, 4), (2048, 16))
A_HALF = 64
B_HEADS = 8
B_NOPE = 64
B_ROPE = 32
B_V = 64
Q_LORA = 256
KV_LORA = 128
ROPE_BASE = 10000.0
N_EXPERTS = 256
TOP_K = 8
N_GROUPS = 8
GROUP_SIZE = N_EXPERTS // N_GROUPS
TOPK_GROUPS = 4
EXPERT_HIDDEN = 256
ROUTED_SCALE = 2.5
LN_EPS = 1e-5
RMS_EPS = 1e-6
NEG_BIG = -1e30
ALPHA = 2.0 ** 0.25
LOG2E = 1.4426950408889634

LANES = 128
HEAD_PAD = 128
SLAB = D_MODEL // LANES
W1_COLS = 3 * A_WIDTH + Q_LORA + KV_LORA + 2 * LANES

TM_PROJ = 256
TQ_A = 128
A_POS_TILES = {1: 4, 4: 2, 16: 1}
PERM_GROUP = 256
LOGITS_ELEMS_B = 4096 * 512
KEY_CHUNK_B = 512
TM_MIX = 256
TM_DISP = 256
ROW_BLOCK = 256
TM_COMB = 128
EXPERT_TILES = 2
N_XBUF = 8
N_OBUF = 4
X_AHEAD = N_XBUF - EXPERT_TILES
VMEM_LIMIT = 48 * 1024 * 1024


def _slab_load(ref, n_tok, tok0=0):
    return jnp.concatenate([ref[pl.ds(tok0 * SLAB + c, n_tok, stride=SLAB), :] for c in range(SLAB)],
                           axis=1)


def _slab_store(ref, val):
    for c in range(SLAB):
        ref[pl.ds(c, val.shape[0], stride=SLAB), :] = val[:, c * LANES:(c + 1) * LANES]


def _slab_load_bf16(ref, stage_ref, n_tok, tok0=0):
    stage_ref[...] = ref[tok0 * SLAB:(tok0 + n_tok) * SLAB, :].astype(F32)
    return _slab_load(stage_ref, n_tok)


def _slab_store_bf16(ref, stage_ref, val):
    _slab_store(stage_ref, val)
    ref[...] = stage_ref[...].astype(BF16)


def _nt_dot(a, b):
    return lax.dot_general(a, b, (((1,), (1,)), ((), ())), preferred_element_type=F32)


def _layer_norm(x, g, b):
    mu = jnp.mean(x, axis=-1, keepdims=True)
    xc = x - mu
    var = jnp.mean(xc * xc, axis=-1, keepdims=True)
    return xc * lax.rsqrt(var + LN_EPS) * g + b


def _rms_norm(x, g):
    return x * lax.rsqrt(jnp.mean(x * x, axis=-1, keepdims=True) + RMS_EPS) * g


def _proj_kernel(xp_ref, xs_ref, w1_ref, qg_ref, kvg_ref, wq_ref, wqr_ref, wuk_ref, wuvt_ref,
                 cos_ref, sin_ref, perm4_ref, perm16_ref,
                 qa_ref, ka_ref, va_ref, q4_ref, k4_ref, v4_ref, q16_ref, k16_ref, v16_ref,
                 qb_ref, kb_ref, vbt_ref, *, n_prompt_tiles):
    i = pl.program_id(0)
    tm = xp_ref.shape[0]
    x = jnp.where(i < n_prompt_tiles, xp_ref[...], xs_ref[...]).astype(BF16)
    p = jnp.dot(x, w1_ref[...], preferred_element_type=F32)
    qa = (p[:, 0:A_WIDTH] * (A_HEAD_DIM ** -0.5 * LOG2E)).astype(BF16)
    ka = p[:, A_WIDTH:2 * A_WIDTH].astype(BF16)
    va = p[:, 2 * A_WIDTH:3 * A_WIDTH].astype(BF16)
    qa_ref[...] = qa
    ka_ref[...] = ka
    va_ref[...] = va
    qkv = jnp.concatenate([qa, ka, va], axis=1)
    for dil, perm_ref, outs in ((4, perm4_ref, (q4_ref, k4_ref, v4_ref)),
                                (16, perm16_ref, (q16_ref, k16_ref, v16_ref))):
        cm = jnp.dot(perm_ref[...], qkv, preferred_element_type=F32).astype(BF16)
        rows = tm // dil
        for r in range(dil):
            for j, o_ref in enumerate(outs):
                o_ref[:, r * A_WIDTH:(r + 1) * A_WIDTH] = cm[r * rows:(r + 1) * rows,
                                                             j * A_WIDTH:(j + 1) * A_WIDTH]
    c0 = 3 * A_WIDTH
    cq = p[:, c0:c0 + Q_LORA]
    ckv = p[:, c0 + Q_LORA:c0 + Q_LORA + KV_LORA]
    kr = p[:, c0 + Q_LORA + KV_LORA:c0 + Q_LORA + KV_LORA + LANES]
    krr = p[:, c0 + Q_LORA + KV_LORA + LANES:c0 + Q_LORA + KV_LORA + 2 * LANES]
    cos = cos_ref[...]
    sin = sin_ref[...]
    cos8 = jnp.concatenate([cos] * B_HEADS, axis=1)
    sin8 = jnp.concatenate([sin] * B_HEADS, axis=1)
    cqn = _rms_norm(cq, qg_ref[...]).astype(BF16)
    q = jnp.dot(cqn, wq_ref[...], preferred_element_type=F32)
    qr = jnp.dot(cqn, wqr_ref[...], preferred_element_type=F32)
    qscale = (B_NOPE + B_ROPE) ** -0.5 * LOG2E
    qb_ref[...] = ((q * cos8 + qr * sin8) * qscale).astype(BF16)
    ckvn = _rms_norm(ckv, kvg_ref[...]).astype(BF16)
    kn = jnp.dot(ckvn, wuk_ref[...], preferred_element_type=F32)
    krope = kr * cos + krr * sin
    kb_ref[...] = (kn + jnp.concatenate([krope] * B_HEADS, axis=1)).astype(BF16)
    vbt_ref[...] = _nt_dot(wuvt_ref[...], ckvn).astype(BF16)


def _class_perm(tm, dil):
    rows = tm // dil
    c = np.arange(tm)
    src = (c % rows) * dil + c // rows
    perm = np.zeros((tm, tm), np.float32)
    perm[c, src] = 1.0
    return jnp.asarray(perm, BF16)


def _proj(xp, xs, w1, qg, kvg, wq, wqr, wuk, wuvt, cos_t, sin_t, s_prompt, s_sample):
    tm = TM_PROJ
    n1 = xp.shape[0] // tm
    n2 = xs.shape[0] // tm
    t_all = xp.shape[0] + xs.shape[0]
    pt, st = s_prompt // tm, s_sample // tm
    assert tm == PERM_GROUP
    perm4, perm16 = _class_perm(tm, 4), _class_perm(tm, 16)

    def tab_idx(i):
        return (jnp.where(i < n1, i % pt, (i - n1) % st), 0)

    full = lambda shape: pl.BlockSpec(shape, lambda i: (0, 0))
    row = lambda cols: pl.BlockSpec((tm, cols), lambda i: (i, 0))
    cls = lambda dil: pl.BlockSpec((tm // dil, dil * A_WIDTH), lambda i: (i, 0))
    cls_shape = lambda dil: jax.ShapeDtypeStruct((t_all // dil, dil * A_WIDTH), BF16)
    return pl.pallas_call(
        functools.partial(_proj_kernel, n_prompt_tiles=n1),
        grid=(n1 + n2,),
        in_specs=[
            pl.BlockSpec((tm, D_MODEL), lambda i: (jnp.minimum(i, n1 - 1), 0)),
            pl.BlockSpec((tm, D_MODEL), lambda i: (jnp.maximum(i - n1, 0), 0)),
            full(w1.shape), full(qg.shape), full(kvg.shape), full(wq.shape), full(wqr.shape),
            full(wuk.shape), full(wuvt.shape),
            pl.BlockSpec((tm, LANES), tab_idx), pl.BlockSpec((tm, LANES), tab_idx),
            full(perm4.shape), full(perm16.shape),
        ],
        out_specs=[row(A_WIDTH)] * 3 + [cls(4)] * 3 + [cls(16)] * 3
        + [row(B_HEADS * HEAD_PAD), row(B_HEADS * HEAD_PAD),
           pl.BlockSpec((B_HEADS * B_V, tm), lambda i: (0, i))],
        out_shape=[jax.ShapeDtypeStruct((t_all, A_WIDTH), BF16)] * 3
        + [cls_shape(4)] * 3 + [cls_shape(16)] * 3
        + [jax.ShapeDtypeStruct((t_all, B_HEADS * HEAD_PAD), BF16)] * 2
        + [jax.ShapeDtypeStruct((B_HEADS * B_V, t_all), BF16)],
        compiler_params=pltpu.CompilerParams(dimension_semantics=("arbitrary",),
                                             vmem_limit_bytes=VMEM_LIMIT),
        name="proj",
    )(xp, xs, w1, qg, kvg, wq, wqr, wuk, wuvt, cos_t, sin_t, perm4, perm16)


def _attn_a_kernel(qblk_ref, pblk_ref, nblk_ref, tstart_ref, llen_ref,
                   q_ref, kp_ref, kc_ref, kn_ref, vp_ref, vc_ref, vn_ref, bias_ref, perm_ref,
                   o_ref, lse_ref, cm_ref, *, n_pos, n_cls):
    n = pl.program_id(0)
    tq, wk = TQ_A, TQ_A + 2 * A_HALF
    lane = lax.broadcasted_iota(I32, (tq, LANES), 1)
    low = lane < A_HEAD_DIM
    colbase = lax.broadcasted_iota(I32, (1, wk), 1) + (tstart_ref[n] - A_HALF)
    cls_len = llen_ref[n]

    def one_class(r):
        cols = slice(0, A_WIDTH) if n_cls == 1 else pl.ds(pl.multiple_of(r * A_WIDTH, A_WIDTH), A_WIDTH)
        kcat = jnp.concatenate([kp_ref[:, cols], kc_ref[:, cols], kn_ref[:, cols]], axis=0)
        vcat = jnp.concatenate([vp_ref[:, cols], vc_ref[:, cols], vn_ref[:, cols]], axis=0)
        for j in range(n_pos):
            rows = slice(j * tq, (j + 1) * tq)
            q = q_ref[rows, cols]
            kw, vw = kcat[j * tq:j * tq + wk, :], vcat[j * tq:j * tq + wk, :]
            col = colbase + j * tq
            colpen = jnp.where((col >= 0) & (col < cls_len), 0.0, NEG_BIG).astype(F32)
            pairs = []
            lse_c = jnp.zeros((tq, LANES), F32)
            for jp in range(A_HEADS // 2):
                sl = slice(jp * LANES, (jp + 1) * LANES)
                qp, kpair, vpair = q[:, sl], kw[:, sl], vw[:, sl]
                outs = []
                for e in range(2):
                    qm = jnp.where(low if e == 0 else ~low, qp, jnp.zeros_like(qp))
                    s = _nt_dot(qm, kpair) + bias_ref[2 * jp + e] + colpen
                    m = jnp.max(s, axis=1, keepdims=True)
                    p = jnp.exp2(s - m)
                    l = jnp.sum(p, axis=1, keepdims=True)
                    outs.append(jnp.dot(p.astype(BF16), vpair, preferred_element_type=F32) / l)
                    lse_c = lse_c + jnp.where(lane == 2 * jp + e, m + jnp.log2(l), 0.0)
                pairs.append(jnp.where(low, outs[0], outs[1]))
            o_full = jnp.concatenate(pairs, axis=1).astype(BF16)
            if n_cls == 1:
                o_ref[rows, :] = o_full
                lse_ref[rows, :] = lse_c
            else:
                hi = lse_c.astype(BF16)
                rest = lse_c - hi.astype(F32)
                mid = rest.astype(BF16)
                lo = (rest - mid.astype(F32)).astype(BF16)
                cm_ref[r, rows, :] = jnp.concatenate([o_full, hi, mid, lo], axis=1)

    if n_cls == 1:
        one_class(0)
        return

    def body(r2, carry):
        one_class(2 * r2)
        one_class(2 * r2 + 1)
        return carry

    lax.fori_loop(0, n_cls // 2, body, 0)
    per = PERM_GROUP // n_cls
    for a in range(n_pos * tq * n_cls // PERM_GROUP):
        stack = jnp.concatenate([cm_ref[r, a * per:(a + 1) * per, :] for r in range(n_cls)], axis=0)
        nat = jnp.dot(perm_ref[...], stack, preferred_element_type=F32)
        rows = slice(a * PERM_GROUP, (a + 1) * PERM_GROUP)
        o_ref[rows, :] = nat[:, :A_WIDTH].astype(BF16)
        lse_ref[rows, :] = (nat[:, A_WIDTH:A_WIDTH + LANES] + nat[:, A_WIDTH + LANES:A_WIDTH + 2 * LANES]
                            + nat[:, A_WIDTH + 2 * LANES:])


def _attn_a_tables(seqs, dil, n_pos):
    rows = n_pos * TQ_A
    per_halo = rows // A_HALF
    qblk, pblk, nblk, tstart, llen = [], [], [], [], []
    for off, s_len in seqs:
        cls = s_len // dil
        steps = cls // rows
        base = (off // dil) // rows
        for i in range(steps):
            qblk.append(base + i)
            pblk.append((base + i) * per_halo - (1 if i > 0 else 0))
            nblk.append((base + i + 1) * per_halo - (0 if i < steps - 1 else 1))
            tstart.append(i * rows)
            llen.append(cls)
    return [jnp.asarray(np.asarray(a, np.int32)) for a in (qblk, pblk, nblk, tstart, llen)]


def _attn_a_bias(dil):
    tq, wk = TQ_A, TQ_A + 2 * A_HALF
    delta = np.abs(np.arange(wk)[None, :] - A_HALF - np.arange(tq)[:, None]).astype(np.float64)
    slopes = 2.0 ** (-8.0 * (np.arange(A_HEADS) + 1.0) / A_HEADS)
    bias = -slopes[:, None, None] * (delta * dil)[None] * LOG2E
    bias = np.where((delta <= A_HALF)[None], bias, NEG_BIG)
    return jnp.asarray(bias.astype(np.float32))


def _attn_a(qc, kc, vc, seqs, dil):
    t_all = qc.shape[0] * dil
    n_pos = A_POS_TILES[dil]
    rows, width = n_pos * TQ_A, dil * A_WIDTH
    tabs = _attn_a_tables(seqs, dil, n_pos)
    n_steps = int(tabs[0].shape[0])
    bias = _attn_a_bias(dil)
    perm = _class_perm(PERM_GROUP, dil).T
    cur = pl.BlockSpec((rows, width), lambda n, qb, pb, nb, ts, ll: (qb[n], 0))
    prev = pl.BlockSpec((A_HALF, width), lambda n, qb, pb, nb, ts, ll: (pb[n], 0))
    nxt = pl.BlockSpec((A_HALF, width), lambda n, qb, pb, nb, ts, ll: (nb[n], 0))
    tok = lambda cols: pl.BlockSpec((rows * dil, cols), lambda n, *_: (n, 0))
    return pl.pallas_call(
        functools.partial(_attn_a_kernel, n_pos=n_pos, n_cls=dil),
        grid_spec=pltpu.PrefetchScalarGridSpec(
            num_scalar_prefetch=5, grid=(n_steps,),
            in_specs=[cur, prev, cur, nxt, prev, cur, nxt,
                      pl.BlockSpec(bias.shape, lambda n, *_: (0, 0, 0)),
                      pl.BlockSpec(perm.shape, lambda n, *_: (0, 0))],
            out_specs=[tok(A_WIDTH), tok(LANES)],
            scratch_shapes=[pltpu.VMEM((dil, rows, A_WIDTH + 3 * LANES), BF16)]),
        out_shape=[jax.ShapeDtypeStruct((t_all, A_WIDTH), BF16),
                   jax.ShapeDtypeStruct((t_all, LANES), F32)],
        compiler_params=pltpu.CompilerParams(dimension_semantics=("arbitrary",),
                                             vmem_limit_bytes=VMEM_LIMIT),
        name=f"attn_a_d{dil}",
    )(*tabs, qc, kc, kc, kc, vc, vc, vc, bias, perm)


def _attn_b_kernel(q_ref, k_ref, vt_ref, o_ref, sta_ref, stb_ref, ma_ref, mb_ref):
    s = pl.program_id(0)
    tq, s_len = q_ref.shape[0], k_ref.shape[0]

    @pl.when(s == 0)
    def _():
        stb_ref[...] = jnp.zeros_like(stb_ref)
        mb_ref[...] = jnp.zeros_like(mb_ref)

    def step(st_new_ref, m_new_ref, st_old_ref, m_old_ref):
        m_old = m_old_ref[...]
        q = q_ref[...]
        m_new = jnp.full((1, tq), -jnp.inf, F32)
        l = jnp.zeros((1, tq), F32)
        acc = jnp.zeros((B_V, tq), F32)
        for c in range(s_len // KEY_CHUNK_B):
            ks = slice(c * KEY_CHUNK_B, (c + 1) * KEY_CHUNK_B)
            st_c = _nt_dot(k_ref[ks, :], q)
            st_new_ref[ks, :] = st_c
            m_new = jnp.maximum(m_new, jnp.max(st_c, axis=0, keepdims=True))
            p = jnp.exp2(st_old_ref[ks, :] - m_old)
            l = l + jnp.sum(p, axis=0, keepdims=True)
            acc = acc + jnp.dot(vt_ref[:, ks], p.astype(BF16), preferred_element_type=F32)
        m_new_ref[...] = m_new
        o_ref[...] = (acc / l).astype(BF16)

    @pl.when(s % 2 == 0)
    def _():
        step(sta_ref, ma_ref, stb_ref, mb_ref)

    @pl.when(s % 2 == 1)
    def _():
        step(stb_ref, mb_ref, sta_ref, ma_ref)


def _attn_b(qb, kb, vbt, off, n_batch, s_len):
    tq = min(s_len, LOGITS_ELEMS_B // s_len)
    nq = s_len // tq
    qbase, kbase = off // tq, off // s_len
    n_items = n_batch * B_HEADS * nq

    def split(item):
        return item // (B_HEADS * nq), (item // nq) % B_HEADS, item % nq

    def q_map(s):
        b, h, qi = split(jnp.minimum(s, n_items - 1))
        return (qbase + b * nq + qi, h)

    def k_map(s):
        b, h, _ = split(jnp.minimum(s, n_items - 1))
        return (kbase + b, h)

    def v_map(s):
        b, h, _ = split(jnp.maximum(s - 1, 0))
        return (h, kbase + b)

    def o_map(s):
        b, h, qi = split(jnp.maximum(s - 1, 0))
        return (h, b * nq + qi)

    return pl.pallas_call(
        _attn_b_kernel,
        grid=(n_items + 1,),
        in_specs=[pl.BlockSpec((tq, HEAD_PAD), q_map),
                  pl.BlockSpec((s_len, HEAD_PAD), k_map),
                  pl.BlockSpec((B_V, s_len), v_map)],
        out_specs=pl.BlockSpec((B_V, tq), o_map),
        out_shape=jax.ShapeDtypeStruct((B_HEADS * B_V, n_batch * s_len), BF16),
        scratch_shapes=[pltpu.VMEM((s_len, tq), F32), pltpu.VMEM((s_len, tq), F32),
                        pltpu.VMEM((1, tq), F32), pltpu.VMEM((1, tq), F32)],
        compiler_params=pltpu.CompilerParams(dimension_semantics=("arbitrary",),
                                             vmem_limit_bytes=VMEM_LIMIT),
        name=f"attn_b_s{s_len}",
    )(qb, kb, vbt)


def _mix_kernel(xp_ref, xs_ref, o0_ref, o1_ref, o2_ref, l0_ref, l1_ref, l2_ref, spread_ref,
                obp_ref, obs_ref, wout_ref,
                g_ref, b_ref, wrh_ref, wrl_ref, rb_ref,
                h_ref, hp_ref, tope_ref, pos_ref, gate_ref, cnt_out_ref, cnt_ref, stage_ref,
                *, n_prompt_tiles):
    i = pl.program_id(0)
    tm = xp_ref.shape[0]

    @pl.when(i == 0)
    def _():
        cnt_ref[...] = jnp.zeros_like(cnt_ref)

    l0, l1, l2 = l0_ref[...], l1_ref[...], l2_ref[...]
    lmax = jnp.maximum(jnp.maximum(l0, l1), l2)
    e0, e1, e2 = jnp.exp2(l0 - lmax), jnp.exp2(l1 - lmax), jnp.exp2(l2 - lmax)
    inv = 1.0 / (e0 + e1 + e2)
    spread = spread_ref[...]

    def per_lane(w):
        hi = w.astype(BF16)
        lo = (w - hi.astype(F32)).astype(BF16)
        return (jnp.dot(hi, spread, preferred_element_type=F32)
                + jnp.dot(lo, spread, preferred_element_type=F32))

    oa = (per_lane(e0 * inv) * o0_ref[...].astype(F32) + per_lane(e1 * inv) * o1_ref[...].astype(F32)
          + per_lane(e2 * inv) * o2_ref[...].astype(F32))
    is_prompt = i < n_prompt_tiles
    obt = jnp.where(is_prompt, obp_ref[...], obs_ref[...])
    mix = (jnp.dot(oa.astype(BF16), wout_ref[0:A_WIDTH, :], preferred_element_type=F32)
           + lax.dot_general(obt, wout_ref[A_WIDTH:, :], (((0,), (0,)), ((), ())),
                             preferred_element_type=F32))
    x = jnp.where(is_prompt, xp_ref[...], xs_ref[...])
    h = _layer_norm(ALPHA * x + mix, g_ref[...], b_ref[...])
    h_ref[...] = h
    _slab_store_bf16(hp_ref, stage_ref, h)

    h_hi = h.astype(BF16)
    h_lo = (h - h_hi.astype(F32)).astype(BF16)
    wrh = wrh_ref[...]
    logits = _nt_dot(wrh, h_hi) + _nt_dot(wrh, h_lo) + _nt_dot(wrl_ref[...], h_hi)
    scores = jax.nn.sigmoid(logits)
    sel = scores + rb_ref[...]

    sub = lax.broadcasted_iota(I32, (GROUP_SIZE, tm), 0).astype(F32)
    gscore = []
    for g in range(N_GROUPS):
        sg = sel[g * GROUP_SIZE:(g + 1) * GROUP_SIZE, :]
        m1 = jnp.max(sg, axis=0, keepdims=True)
        first = jnp.min(jnp.where(sg == m1, sub, float(GROUP_SIZE)), axis=0, keepdims=True)
        m2 = jnp.max(jnp.where(sub == first, -jnp.inf, sg), axis=0, keepdims=True)
        gscore.append(m1 + m2)
    cands = []
    for g in range(N_GROUPS):
        beaten = jnp.zeros((1, tm), F32)
        for g2 in range(N_GROUPS):
            if g2 == g:
                continue
            wins = (gscore[g2] > gscore[g]) | ((gscore[g2] == gscore[g]) & (g2 < g))
            beaten = beaten + wins.astype(F32)
        keep = beaten < float(TOPK_GROUPS)
        sg = sel[g * GROUP_SIZE:(g + 1) * GROUP_SIZE, :]
        cands.append(jnp.where(keep, sg, NEG_BIG))
    cand = jnp.concatenate(cands, axis=0)

    eidx = lax.broadcasted_iota(I32, (N_EXPERTS, tm), 0).astype(F32)
    picked_idx, picked_gate = [], []
    onehot = jnp.zeros((N_EXPERTS, tm), F32)
    for _ in range(TOP_K):
        mx = jnp.max(cand, axis=0, keepdims=True)
        fi = jnp.min(jnp.where(cand == mx, eidx, float(N_EXPERTS)), axis=0, keepdims=True)
        pick = eidx == fi
        picked_idx.append(fi)
        picked_gate.append(jnp.sum(jnp.where(pick, scores, 0.0), axis=0, keepdims=True))
        onehot = onehot + pick.astype(F32)
        cand = jnp.where(pick, -jnp.inf, cand)
    gsum = picked_gate[0]
    for k in range(1, TOP_K):
        gsum = gsum + picked_gate[k]

    tri = (lax.broadcasted_iota(I32, (tm, tm), 0) < lax.broadcasted_iota(I32, (tm, tm), 1))
    before = jnp.dot(onehot.astype(BF16), tri.astype(BF16), preferred_element_type=F32)
    rank = before + cnt_ref[:, 0:1]
    for k in range(TOP_K):
        pick = eidx == picked_idx[k]
        tope_ref[k:k + 1, :] = picked_idx[k].astype(I32)
        pos_ref[k:k + 1, :] = jnp.sum(jnp.where(pick, rank, 0.0), axis=0, keepdims=True).astype(I32)
        gate_ref[k:k + 1, :] = picked_gate[k] / gsum * ROUTED_SCALE
    cnt_ref[...] = cnt_ref[...] + jnp.sum(onehot, axis=1, keepdims=True)
    cnt_out_ref[...] = cnt_ref[...]


def _mix(xp, xs, outs, lses, obp, obs, wout, g, b, wrh, wrl, rb):
    tm = TM_MIX
    n1 = xp.shape[0] // tm
    n2 = xs.shape[0] // tm
    t_all = xp.shape[0] + xs.shape[0]
    full = lambda a: pl.BlockSpec(a.shape, lambda i: (0,) * a.ndim)
    row = lambda cols: pl.BlockSpec((tm, cols), lambda i: (i, 0))
    prow = lambda cols: pl.BlockSpec((tm, cols), lambda i: (jnp.minimum(i, n1 - 1), 0))
    srow = lambda cols: pl.BlockSpec((tm, cols), lambda i: (jnp.maximum(i - n1, 0), 0))
    col = pl.BlockSpec((TOP_K, tm), lambda i: (0, i))
    head_of_lane = np.arange(A_WIDTH) // A_HEAD_DIM
    spread = jnp.asarray(np.arange(LANES)[:, None] == head_of_lane[None, :], BF16)
    return pl.pallas_call(
        functools.partial(_mix_kernel, n_prompt_tiles=n1),
        grid=(n1 + n2,),
        in_specs=[
            prow(D_MODEL), srow(D_MODEL),
            row(A_WIDTH), row(A_WIDTH), row(A_WIDTH), row(LANES), row(LANES), row(LANES), full(spread),
            pl.BlockSpec((B_HEADS * B_V, tm), lambda i: (0, jnp.minimum(i, n1 - 1))),
            pl.BlockSpec((B_HEADS * B_V, tm), lambda i: (0, jnp.maximum(i - n1, 0))),
            full(wout), full(g), full(b), full(wrh), full(wrl), full(rb),
        ],
        out_specs=[row(D_MODEL), pl.BlockSpec((SLAB * tm, LANES), lambda i: (i, 0)), col, col, col,
                   pl.BlockSpec((N_EXPERTS, LANES), lambda i: (0, 0))],
        out_shape=[jax.ShapeDtypeStruct((t_all, D_MODEL), F32),
                   jax.ShapeDtypeStruct((SLAB * t_all, LANES), BF16),
                   jax.ShapeDtypeStruct((TOP_K, t_all), I32),
                   jax.ShapeDtypeStruct((TOP_K, t_all), I32),
                   jax.ShapeDtypeStruct((TOP_K, t_all), F32),
                   jax.ShapeDtypeStruct((N_EXPERTS, LANES), F32)],
        scratch_shapes=[pltpu.VMEM((N_EXPERTS, LANES), F32), pltpu.VMEM((SLAB * tm, LANES), F32)],
        compiler_params=pltpu.CompilerParams(dimension_semantics=("arbitrary",),
                                             vmem_limit_bytes=VMEM_LIMIT),
        name="mix_router",
    )(xp, xs, *outs, *lses, spread, obp, obs, wout, g, b, wrh, wrl, rb)


def _dispatch_kernel(zstart_ref, nused_ref, h_ref, tope_ref, pos_ref, pstart_ref, buf_ref, dest_ref,
                     dsm_ref, zero_ref, sem_ref, zsem_ref):
    i = pl.program_id(0)
    tm = tope_ref.shape[1]
    tile_rows = SLAB * ROW_BLOCK

    @pl.when(i == 0)
    def _():
        zero_ref[...] = jnp.zeros_like(zero_ref)

        def zfill(e, carry):
            start = pl.multiple_of(zstart_ref[e], SLAB)
            pltpu.make_async_copy(zero_ref, buf_ref.at[pl.ds(start, tile_rows), :], zsem_ref).start()
            return carry

        lax.fori_loop(0, N_EXPERTS, zfill, 0)
        span = buf_ref.at[pl.ds(0, N_EXPERTS * tile_rows), :]
        pltpu.make_async_copy(span, span, zsem_ref).wait()

        def ztail(j, carry):
            start = pl.multiple_of(j * tile_rows, tile_rows)
            pltpu.make_async_copy(zero_ref, buf_ref.at[pl.ds(start, tile_rows), :], zsem_ref).start()
            return carry

        def zwait(j, carry):
            pltpu.make_async_copy(zero_ref, buf_ref.at[pl.ds(0, tile_rows), :], zsem_ref).wait()
            return carry

        n_tiles = buf_ref.shape[0] // tile_rows
        lax.fori_loop(nused_ref[0], n_tiles, ztail, 0)
        lax.fori_loop(nused_ref[0], n_tiles, zwait, 0)

    eidx = lax.broadcasted_iota(I32, (N_EXPERTS, tm), 0)
    pstart = pstart_ref[...]
    for k in range(TOP_K):
        hit = eidx == tope_ref[k:k + 1, :]
        base = jnp.sum(jnp.where(hit, pstart, 0.0), axis=0, keepdims=True)
        dest_ref[k:k + 1, :] = (pos_ref[k:k + 1, :] + base.astype(I32)) * SLAB
    pltpu.sync_copy(dest_ref, dsm_ref)

    def scatter(t, carry):
        src = h_ref.at[pl.ds(pl.multiple_of(t * SLAB, SLAB), SLAB), :]
        for k in range(TOP_K):
            dst = buf_ref.at[pl.ds(pl.multiple_of(dsm_ref[k, t], SLAB), SLAB), :]
            pltpu.make_async_copy(src, dst, sem_ref).start(priority=k % 2)
        return carry

    lax.fori_loop(0, tm, scatter, 0, unroll=8)
    span = buf_ref.at[pl.ds(0, TOP_K * tm * SLAB), :]
    pltpu.make_async_copy(span, span, sem_ref).wait()


def _dispatch(h, tope, pos, pstart, zstart, nused, n_rows):
    tm = TM_DISP
    t_all = tope.shape[1]
    col = pl.BlockSpec((TOP_K, tm), lambda i, z, nu: (0, i))
    return pl.pallas_call(
        _dispatch_kernel,
        grid_spec=pltpu.PrefetchScalarGridSpec(
            num_scalar_prefetch=2, grid=(t_all // tm,),
            in_specs=[pl.BlockSpec((SLAB * tm, LANES), lambda i, z, nu: (i, 0)), col, col,
                      pl.BlockSpec((N_EXPERTS, 1), lambda i, z, nu: (0, 0))],
            out_specs=[pl.BlockSpec(memory_space=pl.ANY), col],
            scratch_shapes=[pltpu.SMEM((TOP_K, tm), I32),
                            pltpu.VMEM((SLAB * ROW_BLOCK, LANES), BF16),
                            pltpu.SemaphoreType.DMA, pltpu.SemaphoreType.DMA]),
        out_shape=[jax.ShapeDtypeStruct((SLAB * n_rows, LANES), BF16),
                   jax.ShapeDtypeStruct((TOP_K, t_all), I32)],
        compiler_params=pltpu.CompilerParams(dimension_semantics=("arbitrary",),
                                             vmem_limit_bytes=VMEM_LIMIT),
        name="dispatch",
    )(zstart, nused, h, tope, pos, pstart)


def _expert_kernel(tfirst_ref, ntile_ref, nused_ref, x_hbm, wg_ref, wu_ref, wd_ref, o_hbm,
                   xbuf, obuf, stage_ref, wgb_ref, wub_ref, wdb_ref, xsem, osem):
    e = pl.program_id(0)
    tile_rows = SLAB * ROW_BLOCK
    nused = nused_ref[0]

    def x_copy(g, slot):
        start = pl.multiple_of(g * tile_rows, tile_rows)
        return pltpu.make_async_copy(x_hbm.at[pl.ds(start, tile_rows), :], xbuf.at[slot], xsem.at[slot])

    def o_copy(g, slot):
        start = pl.multiple_of(g * tile_rows, tile_rows)
        return pltpu.make_async_copy(obuf.at[slot], o_hbm.at[pl.ds(start, tile_rows), :], osem.at[slot])

    @pl.when(e == 0)
    def _():
        for j in range(X_AHEAD):
            @pl.when(j < nused)
            def _():
                x_copy(j, j).start()

    n_e = ntile_ref[e]

    @pl.when(n_e > 0)
    def _():
        wgb_ref[...] = wg_ref[...].astype(BF16)
        wub_ref[...] = wu_ref[...].astype(BF16)
        wdb_ref[...] = wd_ref[...].astype(BF16)

    def run_tiles(g0, n):
        for q in range(n):
            ahead = g0 + q + X_AHEAD

            @pl.when(ahead < nused)
            def _():
                x_copy(ahead, ahead % N_XBUF).start()

        for q in range(n):
            g = g0 + q
            x_copy(g, g % N_XBUF).wait()

            @pl.when(g >= N_OBUF)
            def _():
                o_copy(g - N_OBUF, g % N_OBUF).wait()

        for q in range(n):
            g = g0 + q
            x = _slab_load_bf16(xbuf.at[g % N_XBUF], stage_ref.at[q], ROW_BLOCK).astype(BF16)
            gt = jnp.dot(x, wgb_ref[...], preferred_element_type=F32)
            up = jnp.dot(x, wub_ref[...], preferred_element_type=F32)
            hmid = (gt * jax.nn.sigmoid(gt) * up).astype(BF16)
            out = jnp.dot(hmid, wdb_ref[...], preferred_element_type=F32)
            _slab_store(obuf.at[g % N_OBUF], out)

        for q in range(n):
            g = g0 + q
            o_copy(g, g % N_OBUF).start()

    def group(j, carry):
        run_tiles(tfirst_ref[e] + EXPERT_TILES * j, EXPERT_TILES)
        return carry

    n_groups = n_e // EXPERT_TILES
    lax.fori_loop(0, n_groups, group, 0)
    done = n_groups * EXPERT_TILES
    left = n_e - done
    size = EXPERT_TILES // 2
    while size >= 1:
        @pl.when((left & size) != 0)
        def _(size=size, done=done):
            run_tiles(tfirst_ref[e] + done, size)

        done = done + (left & size)
        size //= 2

    @pl.when(e == N_EXPERTS - 1)
    def _():
        for j in range(N_OBUF):
            @pl.when(nused > j)
            def _():
                o_copy(nused - 1 - j, (nused - 1 - j) % N_OBUF).wait()

        zero_ref = stage_ref.at[0]
        zero_ref[...] = jnp.zeros_like(zero_ref)
        n_tiles = o_hbm.shape[0] // tile_rows

        def tail_copy(g):
            start = pl.multiple_of(g * tile_rows, tile_rows)
            return pltpu.make_async_copy(zero_ref, o_hbm.at[pl.ds(start, tile_rows), :], osem.at[0])

        def ztail(g, carry):
            tail_copy(g).start()
            return carry

        def zwait(g, carry):
            tail_copy(g).wait()
            return carry

        lax.fori_loop(nused, n_tiles, ztail, 0)
        lax.fori_loop(nused, n_tiles, zwait, 0)


def _experts(buf, tfirst, ntile, nused, w_gate, w_up, w_down):
    tile_rows = SLAB * ROW_BLOCK
    wmap = lambda e, tf, nt, nu: (e, 0, 0)
    return pl.pallas_call(
        _expert_kernel,
        grid_spec=pltpu.PrefetchScalarGridSpec(
            num_scalar_prefetch=3, grid=(N_EXPERTS,),
            in_specs=[pl.BlockSpec(memory_space=pl.ANY),
                      pl.BlockSpec((None, D_MODEL, EXPERT_HIDDEN), wmap),
                      pl.BlockSpec((None, D_MODEL, EXPERT_HIDDEN), wmap),
                      pl.BlockSpec((None, EXPERT_HIDDEN, D_MODEL), wmap)],
            out_specs=pl.BlockSpec(memory_space=pl.ANY),
            scratch_shapes=[pltpu.VMEM((N_XBUF, tile_rows, LANES), BF16),
                            pltpu.VMEM((N_OBUF, tile_rows, LANES), F32),
                            pltpu.VMEM((EXPERT_TILES, tile_rows, LANES), F32),
                            pltpu.VMEM((D_MODEL, EXPERT_HIDDEN), BF16),
                            pltpu.VMEM((D_MODEL, EXPERT_HIDDEN), BF16),
                            pltpu.VMEM((EXPERT_HIDDEN, D_MODEL), BF16),
                            pltpu.SemaphoreType.DMA((N_XBUF,)),
                            pltpu.SemaphoreType.DMA((N_OBUF,))]),
        out_shape=jax.ShapeDtypeStruct(buf.shape, F32),
        compiler_params=pltpu.CompilerParams(dimension_semantics=("arbitrary",),
                                             vmem_limit_bytes=VMEM_LIMIT),
        name="experts",
    )(tfirst, ntile, nused, buf, w_gate, w_up, w_down)


def _combine_kernel(h_ref, gate_ref, dest_ref, dest1_ref, dest2_ref, eo_ref, wsg_ref, wsu_ref, wsd_ref,
                    g_ref, b_ref, yp_ref, ys_ref, dsm_ref, rows_ref, sem_ref, tsem_ref,
                    *, n_prompt_tiles, n_tiles):
    i = pl.program_id(0)
    tm = gate_ref.shape[0]
    slot = i % 2
    rows = rows_ref.at[slot]
    nxt_rows = rows_ref.at[1 - slot]
    nxt_sem = sem_ref.at[1 - slot]

    @pl.when(i == 0)
    def _():
        pltpu.sync_copy(dest_ref, dsm_ref)

        def gather(t, carry):
            for k in range(TOP_K):
                src = eo_ref.at[pl.ds(pl.multiple_of(dsm_ref[k, t], SLAB), SLAB), :]
                dst = rows_ref.at[0, pl.ds(pl.multiple_of((k * tm + t) * SLAB, SLAB), SLAB), :]
                pltpu.make_async_copy(src, dst, sem_ref.at[0]).start(priority=k % 2)
            return carry

        lax.fori_loop(0, tm, gather, 0, unroll=8)
        pltpu.sync_copy(dest1_ref, dsm_ref)

    def issue_next(k):
        for t in range(tm):
            src = eo_ref.at[pl.ds(pl.multiple_of(dsm_ref[k, t], SLAB), SLAB), :]
            dst = nxt_rows.at[pl.ds((k * tm + t) * SLAB, SLAB), :]
            pltpu.make_async_copy(src, dst, nxt_sem).start(priority=t % 2)

    h = h_ref[...]
    hb = h.astype(BF16)
    sg = jnp.dot(hb, wsg_ref[...], preferred_element_type=F32)
    su = jnp.dot(hb, wsu_ref[...], preferred_element_type=F32)
    shared = jnp.dot((sg * jax.nn.sigmoid(sg) * su).astype(BF16), wsd_ref[...],
                     preferred_element_type=F32)

    issue_next(0)
    whole = eo_ref.at[pl.ds(0, TOP_K * tm * SLAB), :]
    pltpu.make_async_copy(whole, rows, sem_ref.at[slot]).wait()
    gate = gate_ref[...]
    routed = gate[:, 0:1] * _slab_load(rows, tm)
    for k in range(1, TOP_K):
        issue_next(k)
        if k == TOP_K - 1:
            table_copy = pltpu.make_async_copy(dest2_ref, dsm_ref, tsem_ref)
            table_copy.start()
        routed = routed + gate[:, k:k + 1] * _slab_load(rows, tm, k * tm)
    y = _layer_norm(ALPHA * h + (routed + shared), g_ref[...], b_ref[...])

    @pl.when(i < n_prompt_tiles)
    def _():
        yp_ref[...] = y

    @pl.when(i >= n_prompt_tiles)
    def _():
        ys_ref[...] = y

    table_copy.wait()

    @pl.when(i == n_tiles - 1)
    def _():
        pltpu.make_async_copy(whole, nxt_rows, nxt_sem).wait()


def _combine(h, gate_t, dest, eo, wsg, wsu, wsd, g, b, t_prompt):
    tm = TM_COMB
    t_all = gate_t.shape[0]
    n1 = t_prompt // tm
    n2 = (t_all - t_prompt) // tm
    full = lambda a: pl.BlockSpec(a.shape, lambda i: (0,) * a.ndim)
    return pl.pallas_call(
        functools.partial(_combine_kernel, n_prompt_tiles=n1, n_tiles=n1 + n2),
        grid=(n1 + n2,),
        in_specs=[pl.BlockSpec((tm, D_MODEL), lambda i: (i, 0)),
                  pl.BlockSpec((tm, TOP_K), lambda i: (i, 0)),
                  pl.BlockSpec((TOP_K, tm), lambda i: (0, i)),
                  pl.BlockSpec((TOP_K, tm), lambda i: (0, jnp.minimum(i + 1, n1 + n2 - 1))),
                  pl.BlockSpec((TOP_K, tm), lambda i: (0, jnp.minimum(i + 2, n1 + n2 - 1))),
                  pl.BlockSpec(memory_space=pl.ANY),
                  full(wsg), full(wsu), full(wsd), full(g), full(b)],
        out_specs=[pl.BlockSpec((tm, D_MODEL), lambda i: (jnp.minimum(i, n1 - 1), 0)),
                   pl.BlockSpec((tm, D_MODEL), lambda i: (jnp.maximum(i - n1, 0), 0))],
        out_shape=[jax.ShapeDtypeStruct((t_prompt, D_MODEL), F32),
                   jax.ShapeDtypeStruct((t_all - t_prompt, D_MODEL), F32)],
        scratch_shapes=[pltpu.SMEM((TOP_K, tm), I32),
                        pltpu.VMEM((2, TOP_K * tm * SLAB, LANES), F32),
                        pltpu.SemaphoreType.DMA((2,)), pltpu.SemaphoreType.DMA],
        compiler_params=pltpu.CompilerParams(dimension_semantics=("arbitrary",),
                                             vmem_limit_bytes=VMEM_LIMIT),
        name="combine",
    )(h, gate_t, dest, dest, dest, eo, wsg, wsu, wsd, g, b)


def _rope_tables(s_max):
    inv_freq = ROPE_BASE ** (-jnp.arange(0, B_ROPE, 2, dtype=F32) / B_ROPE)
    ang = jnp.arange(s_max, dtype=F32)[:, None] * inv_freq[None, :]
    cos, sin = jnp.cos(ang), jnp.sin(ang)
    ones = jnp.ones((s_max, B_NOPE), F32)
    zeros_n = jnp.zeros((s_max, B_NOPE), F32)
    zeros_p = jnp.zeros((s_max, HEAD_PAD - B_NOPE - B_ROPE), F32)
    return (jnp.concatenate([ones, cos, cos, zeros_p], axis=1),
            jnp.concatenate([zeros_n, sin, sin, zeros_p], axis=1))


def _rot_cols(w):
    half = B_ROPE // 2
    return jnp.concatenate([-w[..., half:], w[..., :half]], axis=-1)


def _layout_weights(w_in, w_uq, w_uk):
    c_kr = 3 * A_WIDTH + Q_LORA + KV_LORA
    w_kr = w_in[:, c_kr:c_kr + B_ROPE]
    pad_l = jnp.zeros((D_MODEL, B_NOPE), F32)
    pad_r = jnp.zeros((D_MODEL, HEAD_PAD - B_NOPE - B_ROPE), F32)
    w1 = jnp.concatenate([w_in[:, :c_kr], pad_l, w_kr, pad_r, pad_l, _rot_cols(w_kr), pad_r], axis=1)
    wq3 = w_uq.reshape(Q_LORA, B_HEADS, B_NOPE + B_ROPE)
    nope, rope = wq3[..., :B_NOPE], wq3[..., B_NOPE:]
    zpad = jnp.zeros((Q_LORA, B_HEADS, HEAD_PAD - B_NOPE - B_ROPE), F32)
    wq = jnp.concatenate([nope, rope, zpad], axis=-1).reshape(Q_LORA, B_HEADS * HEAD_PAD)
    wqr = jnp.concatenate([jnp.zeros_like(nope), _rot_cols(rope), zpad], axis=-1)
    wqr = wqr.reshape(Q_LORA, B_HEADS * HEAD_PAD)
    wk3 = w_uk.reshape(KV_LORA, B_HEADS, B_NOPE)
    wuk = jnp.concatenate([wk3, jnp.zeros((KV_LORA, B_HEADS, HEAD_PAD - B_NOPE), F32)], axis=-1)
    wuk = wuk.reshape(KV_LORA, B_HEADS * HEAD_PAD)
    return w1.astype(BF16), wq.astype(BF16), wqr.astype(BF16), wuk.astype(BF16)


def _forward(x_prompt, x_sample, w_in, w_out, ln1_g, ln1_b, q_norm_g, w_uq, kv_norm_g, w_uk, w_uv,
             w_router, router_bias, w_gate, w_up, w_down, ws_gate, ws_up, ws_down, ln2_g, ln2_b):
    b1, s1, _ = x_prompt.shape
    b2, s2, _ = x_sample.shape
    t1, t2 = b1 * s1, b2 * s2
    t_all = t1 + t2
    xp = x_prompt.reshape(t1, D_MODEL)
    xs = x_sample.reshape(t2, D_MODEL)
    seqs = [(b * s1, s1) for b in range(b1)] + [(t1 + b * s2, s2) for b in range(b2)]
    assert t1 % s2 == 0 and s1 % (TQ_A * 16) == 0 and s2 % (TQ_A * 16) == 0

    w1, wq, wqr, wuk = _layout_weights(w_in, w_uq, w_uk)
    cos_t, sin_t = _rope_tables(max(s1, s2))
    (qa, ka, va, q4, k4, v4, q16, k16, v16, qb, kb, vbt) = _proj(
        xp, xs, w1, q_norm_g.reshape(1, -1), kv_norm_g.reshape(1, -1), wq, wqr, wuk,
        w_uv.T.astype(BF16), cos_t, sin_t, s1, s2)

    outs, lses = [], []
    for (_, dil), qkv in zip(A_PATTERNS, ((qa, ka, va), (q4, k4, v4), (q16, k16, v16))):
        o, lse = _attn_a(*qkv, seqs, dil)
        outs.append(o)
        lses.append(lse)
    obp = _attn_b(qb, kb, vbt, 0, b1, s1)
    obs = _attn_b(qb, kb, vbt, t1, b2, s2)

    wr_t = w_router.T
    wr_hi = wr_t.astype(BF16)
    wr_lo = (wr_t - wr_hi.astype(F32)).astype(BF16)
    h, h_slab, tope, pos, gate, counts = _mix(xp, xs, outs, lses, obp, obs, w_out.astype(BF16),
                                                ln1_g.reshape(1, -1), ln1_b.reshape(1, -1),
                                                wr_hi, wr_lo, router_bias.reshape(-1, 1))

    cnt = counts[:, 0].astype(I32)
    padded = ((cnt + ROW_BLOCK - 1) // ROW_BLOCK) * ROW_BLOCK
    pend = jnp.cumsum(padded)
    pstart = pend - padded
    n_tiles = (t_all * TOP_K) // ROW_BLOCK + N_EXPERTS + 1
    nused = (pend[-1:] // ROW_BLOCK).astype(I32)

    buf, dest = _dispatch(h_slab, tope, pos, pstart.astype(F32).reshape(-1, 1),
                          ((pstart + cnt) * SLAB).astype(I32), nused, n_tiles * ROW_BLOCK)
    eo = _experts(buf, (pstart // ROW_BLOCK).astype(I32), (padded // ROW_BLOCK).astype(I32), nused,
                  w_gate, w_up, w_down)
    yp, ys = _combine(h, gate.T, dest, eo, ws_gate.astype(BF16), ws_up.astype(BF16),
                      ws_down.astype(BF16), ln2_g.reshape(1, -1), ln2_b.reshape(1, -1), t1)
    return yp.reshape(b1, s1, D_MODEL), ys.reshape(b2, s2, D_MODEL)


def kernel(x_prompt, x_sample, w_in, w_out, ln1_g, ln1_b, q_norm_g, w_uq, kv_norm_g, w_uk, w_uv,
           w_router, router_bias, w_gate, w_up, w_down, ws_gate, ws_up, ws_down, ln2_g, ln2_b):
    params = (w_in, w_out, ln1_g, ln1_b, q_norm_g, w_uq, kv_norm_g, w_uk, w_uv, w_router, router_bias,
              w_gate, w_up, w_down, ws_gate, ws_up, ws_down, ln2_g, ln2_b)
    assert all(p.shape[0] == 1 for p in params), "one encoder layer"
    return _forward(x_prompt, x_sample, *[p.reshape(p.shape[1:]) for p in params])
```

```python
import functools

import numpy as np
import jax
import jax.numpy as jnp
from jax import lax
from jax.experimental import pallas as pl
from jax.experimental.pallas import tpu as pltpu

F32 = jnp.float32
BF16 = jnp.bfloat16
I32 = jnp.int32

D_MODEL = 1024
A_HEADS = 8
A_HEAD_DIM = 64
A_WIDTH = A_HEADS * A_HEAD_DIM
A_PATTERNS = ((128, 1), (512, 4), (2048, 16))
A_HALF = 64
B_HEADS = 8
B_NOPE = 64
B_ROPE = 32
B_V = 64
Q_LORA = 256
KV_LORA = 128
ROPE_BASE = 10000.0
N_EXPERTS = 256
TOP_K = 8
N_GROUPS = 8
GROUP_SIZE = N_EXPERTS // N_GROUPS
TOPK_GROUPS = 4
EXPERT_HIDDEN = 256
ROUTED_SCALE = 2.5
LN_EPS = 1e-5
RMS_EPS = 1e-6
NEG_BIG = -1e30
ALPHA = 2.0 ** 0.25
LOG2E = 1.4426950408889634

LANES = 128
HEAD_PAD = 128
SLAB = D_MODEL // LANES

TM_PROJ = 256
TQ_A = 128
A_POS_TILES = {1: 4, 4: 2, 16: 1}
PERM_GROUP = 256
LOGITS_ELEMS_B = 4096 * 512
KEY_CHUNK_B = 512
TM_MIX = 256
TM_DISP = 256
ROW_BLOCK = 256
TM_COMB = 128
EXPERT_TILES = 2
N_XBUF = 8
N_OBUF = 4
X_AHEAD = N_XBUF - EXPERT_TILES
VMEM_LIMIT = 48 * 1024 * 1024


def _slab_load(ref, n_tok, tok0=0):
    return jnp.concatenate([ref[pl.ds(tok0 * SLAB + c, n_tok, stride=SLAB), :] for c in range(SLAB)],
                           axis=1)


def _slab_store(ref, val):
    for c in range(SLAB):
        ref[pl.ds(c, val.shape[0], stride=SLAB), :] = val[:, c * LANES:(c + 1) * LANES]


def _slab_load_bf16(ref, stage_ref, n_tok, tok0=0):
    stage_ref[...] = ref[tok0 * SLAB:(tok0 + n_tok) * SLAB, :].astype(F32)
    return _slab_load(stage_ref, n_tok)


def _slab_store_bf16(ref, stage_ref, val):
    _slab_store(stage_ref, val)
    ref[...] = stage_ref[...].astype(BF16)


def _nt_dot(a, b):
    return lax.dot_general(a, b, (((1,), (1,)), ((), ())), preferred_element_type=F32)


def _layer_norm(x, g, b):
    mu = jnp.mean(x, axis=-1, keepdims=True)
    xc = x - mu
    var = jnp.mean(xc * xc, axis=-1, keepdims=True)
    return xc * lax.rsqrt(var + LN_EPS) * g + b


def _rms_norm(x, g):
    return x * lax.rsqrt(jnp.mean(x * x, axis=-1, keepdims=True) + RMS_EPS) * g


def _proj_kernel(xp_ref, xs_ref, w1_ref, qg_ref, kvg_ref, wq_ref, wqr_ref, wuk_ref, wuvt_ref,
                 cos_ref, sin_ref, perm4_ref, perm16_ref,
                 qa_ref, ka_ref, va_ref, q4_ref, k4_ref, v4_ref, q16_ref, k16_ref, v16_ref,
                 qb_ref, kb_ref, vbt_ref, *, n_prompt_tiles):
    i = pl.program_id(0)
    tm = xp_ref.shape[0]
    x = jnp.where(i < n_prompt_tiles, xp_ref[...], xs_ref[...]).astype(BF16)
    p = jnp.dot(x, w1_ref[...], preferred_element_type=F32)
    qa = (p[:, 0:A_WIDTH] * (A_HEAD_DIM ** -0.5 * LOG2E)).astype(BF16)
    ka = p[:, A_WIDTH:2 * A_WIDTH].astype(BF16)
    va = p[:, 2 * A_WIDTH:3 * A_WIDTH].astype(BF16)
    qa_ref[...] = qa
    ka_ref[...] = ka
    va_ref[...] = va
    qkv = jnp.concatenate([qa, ka, va], axis=1)
    for dil, perm_ref, outs in ((4, perm4_ref, (q4_ref, k4_ref, v4_ref)),
                                (16, perm16_ref, (q16_ref, k16_ref, v16_ref))):
        cm = jnp.dot(perm_ref[...], qkv, preferred_element_type=F32).astype(BF16)
        rows = tm // dil
        for r in range(dil):
            for j, o_ref in enumerate(outs):
                o_ref[:, r * A_WIDTH:(r + 1) * A_WIDTH] = cm[r * rows:(r + 1) * rows,
                                                             j * A_WIDTH:(j + 1) * A_WIDTH]
    c0 = 3 * A_WIDTH
    cq = p[:, c0:c0 + Q_LORA]
    ckv = p[:, c0 + Q_LORA:c0 + Q_LORA + KV_LORA]
    kr = p[:, c0 + Q_LORA + KV_LORA:c0 + Q_LORA + KV_LORA + LANES]
    krr = p[:, c0 + Q_LORA + KV_LORA + LANES:c0 + Q_LORA + KV_LORA + 2 * LANES]
    cos = cos_ref[...]
    sin = sin_ref[...]
    cos8 = jnp.concatenate([cos] * B_HEADS, axis=1)
    sin8 = jnp.concatenate([sin] * B_HEADS, axis=1)
    cqn = _rms_norm(cq, qg_ref[...]).astype(BF16)
    q = jnp.dot(cqn, wq_ref[...], preferred_element_type=F32)
    qr = jnp.dot(cqn, wqr_ref[...], preferred_element_type=F32)
    qscale = (B_NOPE + B_ROPE) ** -0.5 * LOG2E
    qb_ref[...] = ((q * cos8 + qr * sin8) * qscale).astype(BF16)
    ckvn = _rms_norm(ckv, kvg_ref[...]).astype(BF16)
    kn = jnp.dot(ckvn, wuk_ref[...], preferred_element_type=F32)
    krope = kr * cos + krr * sin
    kb_ref[...] = (kn + jnp.concatenate([krope] * B_HEADS, axis=1)).astype(BF16)
    vbt_ref[...] = _nt_dot(wuvt_ref[...], ckvn).astype(BF16)


def _class_perm(tm, dil):
    rows = tm // dil
    c = np.arange(tm)
    src = (c % rows) * dil + c // rows
    perm = np.zeros((tm, tm), np.float32)
    perm[c, src] = 1.0
    return jnp.asarray(perm, BF16)


def _proj(xp, xs, w1, qg, kvg, wq, wqr, wuk, wuvt, cos_t, sin_t, s_prompt, s_sample):
    tm = TM_PROJ
    n1 = xp.shape[0] // tm
    n2 = xs.shape[0] // tm
    t_all = xp.shape[0] + xs.shape[0]
    pt, st = s_prompt // tm, s_sample // tm
    assert tm == PERM_GROUP
    perm4, perm16 = _class_perm(tm, 4), _class_perm(tm, 16)

    def tab_idx(i):
        return (jnp.where(i < n1, i % pt, (i - n1) % st), 0)

    full = lambda shape: pl.BlockSpec(shape, lambda i: (0, 0))
    row = lambda cols: pl.BlockSpec((tm, cols), lambda i: (i, 0))
    cls = lambda dil: pl.BlockSpec((tm // dil, dil * A_WIDTH), lambda i: (i, 0))
    cls_shape = lambda dil: jax.ShapeDtypeStruct((t_all // dil, dil * A_WIDTH), BF16)
    return pl.pallas_call(
        functools.partial(_proj_kernel, n_prompt_tiles=n1),
        grid=(n1 + n2,),
        in_specs=[
            pl.BlockSpec((tm, D_MODEL), lambda i: (jnp.minimum(i, n1 - 1), 0)),
            pl.BlockSpec((tm, D_MODEL), lambda i: (jnp.maximum(i - n1, 0), 0)),
            full(w1.shape), full(qg.shape), full(kvg.shape), full(wq.shape), full(wqr.shape),
            full(wuk.shape), full(wuvt.shape),
            pl.BlockSpec((tm, LANES), tab_idx), pl.BlockSpec((tm, LANES), tab_idx),
            full(perm4.shape), full(perm16.shape),
        ],
        out_specs=[row(A_WIDTH)] * 3 + [cls(4)] * 3 + [cls(16)] * 3
        + [row(B_HEADS * HEAD_PAD), row(B_HEADS * HEAD_PAD),
           pl.BlockSpec((B_HEADS * B_V, tm), lambda i: (0, i))],
        out_shape=[jax.ShapeDtypeStruct((t_all, A_WIDTH), BF16)] * 3
        + [cls_shape(4)] * 3 + [cls_shape(16)] * 3
        + [jax.ShapeDtypeStruct((t_all, B_HEADS * HEAD_PAD), BF16)] * 2
        + [jax.ShapeDtypeStruct((B_HEADS * B_V, t_all), BF16)],
        compiler_params=pltpu.CompilerParams(dimension_semantics=("arbitrary",),
                                             vmem_limit_bytes=VMEM_LIMIT),
        name="proj",
    )(xp, xs, w1, qg, kvg, wq, wqr, wuk, wuvt, cos_t, sin_t, perm4, perm16)


def _attn_a_kernel(qblk_ref, pblk_ref, nblk_ref, tstart_ref, llen_ref,
                   q_ref, kp_ref, kc_ref, kn_ref, vp_ref, vc_ref, vn_ref, bias_ref, perm_ref,
                   o_ref, lse_ref, cm_ref, *, n_pos, n_cls):
    n = pl.program_id(0)
    tq, wk = TQ_A, TQ_A + 2 * A_HALF
    lane = lax.broadcasted_iota(I32, (tq, LANES), 1)
    low = lane < A_HEAD_DIM
    colbase = lax.broadcasted_iota(I32, (1, wk), 1) + (tstart_ref[n] - A_HALF)
    cls_len = llen_ref[n]

    def one_class(r):
        cols = slice(0, A_WIDTH) if n_cls == 1 else pl.ds(pl.multiple_of(r * A_WIDTH, A_WIDTH), A_WIDTH)
        kcat = jnp.concatenate([kp_ref[:, cols], kc_ref[:, cols], kn_ref[:, cols]], axis=0)
        vcat = jnp.concatenate([vp_ref[:, cols], vc_ref[:, cols], vn_ref[:, cols]], axis=0)
        for j in range(n_pos):
            rows = slice(j * tq, (j + 1) * tq)
            q = q_ref[rows, cols]
            kw, vw = kcat[j * tq:j * tq + wk, :], vcat[j * tq:j * tq + wk, :]
            col = colbase + j * tq
            colpen = jnp.where((col >= 0) & (col < cls_len), 0.0, NEG_BIG).astype(F32)
            pairs = []
            lse_c = jnp.zeros((tq, LANES), F32)
            for jp in range(A_HEADS // 2):
                sl = slice(jp * LANES, (jp + 1) * LANES)
                qp, kpair, vpair = q[:, sl], kw[:, sl], vw[:, sl]
                outs = []
                for e in range(2):
                    qm = jnp.where(low if e == 0 else ~low, qp, jnp.zeros_like(qp))
                    s = _nt_dot(qm, kpair) + bias_ref[2 * jp + e] + colpen
                    m = jnp.max(s, axis=1, keepdims=True)
                    p = jnp.exp2(s - m)
                    l = jnp.sum(p, axis=1, keepdims=True)
                    outs.append(jnp.dot(p.astype(BF16), vpair, preferred_element_type=F32) / l)
                    lse_c = lse_c + jnp.where(lane == 2 * jp + e, m + jnp.log2(l), 0.0)
                pairs.append(jnp.where(low, outs[0], outs[1]))
            o_full = jnp.concatenate(pairs, axis=1).astype(BF16)
            if n_cls == 1:
                o_ref[rows, :] = o_full
                lse_ref[rows, :] = lse_c
            else:
                hi = lse_c.astype(BF16)
                rest = lse_c - hi.astype(F32)
                mid = rest.astype(BF16)
                lo = (rest - mid.astype(F32)).astype(BF16)
                cm_ref[r, rows, :] = jnp.concatenate([o_full, hi, mid, lo], axis=1)

    if n_cls == 1:
        one_class(0)
        return

    def body(r2, carry):
        one_class(2 * r2)
        one_class(2 * r2 + 1)
        return carry

    lax.fori_loop(0, n_cls // 2, body, 0)
    per = PERM_GROUP // n_cls
    for a in range(n_pos * tq * n_cls // PERM_GROUP):
        stack = jnp.concatenate([cm_ref[r, a * per:(a + 1) * per, :] for r in range(n_cls)], axis=0)
        nat = jnp.dot(perm_ref[...], stack, preferred_element_type=F32)
        rows = slice(a * PERM_GROUP, (a + 1) * PERM_GROUP)
        o_ref[rows, :] = nat[:, :A_WIDTH].astype(BF16)
        lse_ref[rows, :] = (nat[:, A_WIDTH:A_WIDTH + LANES] + nat[:, A_WIDTH + LANES:A_WIDTH + 2 * LANES]
                            + nat[:, A_WIDTH + 2 * LANES:])


def _attn_a_tables(seqs, dil, n_pos):
    rows = n_pos * TQ_A
    per_halo = rows // A_HALF
    qblk, pblk, nblk, tstart, llen = [], [], [], [], []
    for off, s_len in seqs:
        cls = s_len // dil
        steps = cls // rows
        base = (off // dil) // rows
        for i in range(steps):
            qblk.append(base + i)
            pblk.append((base + i) * per_halo - (1 if i > 0 else 0))
            nblk.append((base + i + 1) * per_halo - (0 if i < steps - 1 else 1))
            tstart.append(i * rows)
            llen.append(cls)
    return [jnp.asarray(np.asarray(a, np.int32)) for a in (qblk, pblk, nblk, tstart, llen)]


def _attn_a_bias(dil):
    tq, wk = TQ_A, TQ_A + 2 * A_HALF
    delta = np.abs(np.arange(wk)[None, :] - A_HALF - np.arange(tq)[:, None]).astype(np.float64)
    slopes = 2.0 ** (-8.0 * (np.arange(A_HEADS) + 1.0) / A_HEADS)
    bias = -slopes[:, None, None] * (delta * dil)[None] * LOG2E
    bias = np.where((delta <= A_HALF)[None], bias, NEG_BIG)
    return jnp.asarray(bias.astype(np.float32))


def _attn_a(qc, kc, vc, seqs, dil):
    t_all = qc.shape[0] * dil
    n_pos = A_POS_TILES[dil]
    rows, width = n_pos * TQ_A, dil * A_WIDTH
    tabs = _attn_a_tables(seqs, dil, n_pos)
    n_steps = int(tabs[0].shape[0])
    bias = _attn_a_bias(dil)
    perm = _class_perm(PERM_GROUP, dil).T
    cur = pl.BlockSpec((rows, width), lambda n, qb, pb, nb, ts, ll: (qb[n], 0))
    prev = pl.BlockSpec((A_HALF, width), lambda n, qb, pb, nb, ts, ll: (pb[n], 0))
    nxt = pl.BlockSpec((A_HALF, width), lambda n, qb, pb, nb, ts, ll: (nb[n], 0))
    tok = lambda cols: pl.BlockSpec((rows * dil, cols), lambda n, *_: (n, 0))
    return pl.pallas_call(
        functools.partial(_attn_a_kernel, n_pos=n_pos, n_cls=dil),
        grid_spec=pltpu.PrefetchScalarGridSpec(
            num_scalar_prefetch=5, grid=(n_steps,),
            in_specs=[cur, prev, cur, nxt, prev, cur, nxt,
                      pl.BlockSpec(bias.shape, lambda n, *_: (0, 0, 0)),
                      pl.BlockSpec(perm.shape, lambda n, *_: (0, 0))],
            out_specs=[tok(A_WIDTH), tok(LANES)],
            scratch_shapes=[pltpu.VMEM((dil, rows, A_WIDTH + 3 * LANES), BF16)]),
        out_shape=[jax.ShapeDtypeStruct((t_all, A_WIDTH), BF16),
                   jax.ShapeDtypeStruct((t_all, LANES), F32)],
        compiler_params=pltpu.CompilerParams(dimension_semantics=("arbitrary",),
                                             vmem_limit_bytes=VMEM_LIMIT),
        name=f"attn_a_d{dil}",
    )(*tabs, qc, kc, kc, kc, vc, vc, vc, bias, perm)


def _attn_b_kernel(q_ref, k_ref, vt_ref, o_ref, sta_ref, stb_ref, ma_ref, mb_ref):
    s = pl.program_id(0)
    tq, s_len = q_ref.shape[0], k_ref.shape[0]

    @pl.when(s == 0)
    def _():
        stb_ref[...] = jnp.zeros_like(stb_ref)
        mb_ref[...] = jnp.zeros_like(mb_ref)

    def step(st_new_ref, m_new_ref, st_old_ref, m_old_ref):
        m_old = m_old_ref[...]
        q = q_ref[...]
        m_new = jnp.full((1, tq), -jnp.inf, F32)
        l = jnp.zeros((1, tq), F32)
        acc = jnp.zeros((B_V, tq), F32)
        for c in range(s_len // KEY_CHUNK_B):
            ks = slice(c * KEY_CHUNK_B, (c + 1) * KEY_CHUNK_B)
            st_c = _nt_dot(k_ref[ks, :], q)
            st_new_ref[ks, :] = st_c
            m_new = jnp.maximum(m_new, jnp.max(st_c, axis=0, keepdims=True))
            p = jnp.exp2(st_old_ref[ks, :] - m_old)
            l = l + jnp.sum(p, axis=0, keepdims=True)
            acc = acc + jnp.dot(vt_ref[:, ks], p.astype(BF16), preferred_element_type=F32)
        m_new_ref[...] = m_new
        o_ref[...] = (acc / l).astype(BF16)

    @pl.when(s % 2 == 0)
    def _():
        step(sta_ref, ma_ref, stb_ref, mb_ref)

    @pl.when(s % 2 == 1)
    def _():
        step(stb_ref, mb_ref, sta_ref, ma_ref)


def _attn_b(qb, kb, vbt, off, n_batch, s_len):
    tq = min(s_len, LOGITS_ELEMS_B // s_len)
    nq = s_len // tq
    qbase, kbase = off // tq, off // s_len
    n_items = n_batch * B_HEADS * nq

    def split(item):
        return item // (B_HEADS * nq), (item // nq) % B_HEADS, item % nq

    def q_map(s):
        b, h, qi = split(jnp.minimum(s, n_items - 1))
        return (qbase + b * nq + qi, h)

    def k_map(s):
        b, h, _ = split(jnp.minimum(s, n_items - 1))
        return (kbase + b, h)

    def v_map(s):
        b, h, _ = split(jnp.maximum(s - 1, 0))
        return (h, kbase + b)

    def o_map(s):
        b, h, qi = split(jnp.maximum(s - 1, 0))
        return (h, b * nq + qi)

    return pl.pallas_call(
        _attn_b_kernel,
        grid=(n_items + 1,),
        in_specs=[pl.BlockSpec((tq, HEAD_PAD), q_map),
                  pl.BlockSpec((s_len, HEAD_PAD), k_map),
                  pl.BlockSpec((B_V, s_len), v_map)],
        out_specs=pl.BlockSpec((B_V, tq), o_map),
        out_shape=jax.ShapeDtypeStruct((B_HEADS * B_V, n_batch * s_len), BF16),
        scratch_shapes=[pltpu.VMEM((s_len, tq), F32), pltpu.VMEM((s_len, tq), F32),
                        pltpu.VMEM((1, tq), F32), pltpu.VMEM((1, tq), F32)],
        compiler_params=pltpu.CompilerParams(dimension_semantics=("arbitrary",),
                                             vmem_limit_bytes=VMEM_LIMIT),
        name=f"attn_b_s{s_len}",
    )(qb, kb, vbt)


def _mix_kernel(xp_ref, xs_ref, o0_ref, o1_ref, o2_ref, l0_ref, l1_ref, l2_ref, spread_ref,
                obp_ref, obs_ref, wout_ref,
                g_ref, b_ref, wrh_ref, wrl_ref, rb_ref,
                h_ref, hp_ref, tope_ref, pos_ref, gate_ref, cnt_out_ref, cnt_ref, stage_ref,
                *, n_prompt_tiles):
    i = pl.program_id(0)
    tm = xp_ref.shape[0]

    @pl.when(i == 0)
    def _():
        cnt_ref[...] = jnp.zeros_like(cnt_ref)

    l0, l1, l2 = l0_ref[...], l1_ref[...], l2_ref[...]
    lmax = jnp.maximum(jnp.maximum(l0, l1), l2)
    e0, e1, e2 = jnp.exp2(l0 - lmax), jnp.exp2(l1 - lmax), jnp.exp2(l2 - lmax)
    inv = 1.0 / (e0 + e1 + e2)
    spread = spread_ref[...]

    def per_lane(w):
        hi = w.astype(BF16)
        lo = (w - hi.astype(F32)).astype(BF16)
        return (jnp.dot(hi, spread, preferred_element_type=F32)
                + jnp.dot(lo, spread, preferred_element_type=F32))

    oa = (per_lane(e0 * inv) * o0_ref[...].astype(F32) + per_lane(e1 * inv) * o1_ref[...].astype(F32)
          + per_lane(e2 * inv) * o2_ref[...].astype(F32))
    is_prompt = i < n_prompt_tiles
    obt = jnp.where(is_prompt, obp_ref[...], obs_ref[...])
    mix = (jnp.dot(oa.astype(BF16), wout_ref[0:A_WIDTH, :], preferred_element_type=F32)
           + lax.dot_general(obt, wout_ref[A_WIDTH:, :], (((0,), (0,)), ((), ())),
                             preferred_element_type=F32))
    x = jnp.where(is_prompt, xp_ref[...], xs_ref[...])
    h = _layer_norm(ALPHA * x + mix, g_ref[...], b_ref[...])
    h_ref[...] = h
    _slab_store_bf16(hp_ref, stage_ref, h)

    h_hi = h.astype(BF16)
    h_lo = (h - h_hi.astype(F32)).astype(BF16)
    wrh = wrh_ref[...]
    logits = _nt_dot(wrh, h_hi) + _nt_dot(wrh, h_lo) + _nt_dot(wrl_ref[...], h_hi)
    scores = jax.nn.sigmoid(logits)
    sel = scores + rb_ref[...]

    sub = lax.broadcasted_iota(I32, (GROUP_SIZE, tm), 0).astype(F32)
    gscore = []
    for g in range(N_GROUPS):
        sg = sel[g * GROUP_SIZE:(g + 1) * GROUP_SIZE, :]
        m1 = jnp.max(sg, axis=0, keepdims=True)
        first = jnp.min(jnp.where(sg == m1, sub, float(GROUP_SIZE)), axis=0, keepdims=True)
        m2 = jnp.max(jnp.where(sub == first, -jnp.inf, sg), axis=0, keepdims=True)
        gscore.append(m1 + m2)
    cands = []
    for g in range(N_GROUPS):
        beaten = jnp.zeros((1, tm), F32)
        for g2 in range(N_GROUPS):
            if g2 == g:
                continue
            wins = (gscore[g2] > gscore[g]) | ((gscore[g2] == gscore[g]) & (g2 < g))
            beaten = beaten + wins.astype(F32)
        keep = beaten < float(TOPK_GROUPS)
        sg = sel[g * GROUP_SIZE:(g + 1) * GROUP_SIZE, :]
        cands.append(jnp.where(keep, sg, NEG_BIG))
    cand = jnp.concatenate(cands, axis=0)

    eidx = lax.broadcasted_iota(I32, (N_EXPERTS, tm), 0).astype(F32)
    picked_idx, picked_gate = [], []
    onehot = jnp.zeros((N_EXPERTS, tm), F32)
    for _ in range(TOP_K):
        mx = jnp.max(cand, axis=0, keepdims=True)
        fi = jnp.min(jnp.where(cand == mx, eidx, float(N_EXPERTS)), axis=0, keepdims=True)
        pick = eidx == fi
        picked_idx.append(fi)
        picked_gate.append(jnp.sum(jnp.where(pick, scores, 0.0), axis=0, keepdims=True))
        onehot = jnp.where(pick, 1.0, onehot)
        cand = jnp.where(pick, -jnp.inf, cand)
    gsum = picked_gate[0]
    for k in range(1, TOP_K):
        gsum = gsum + picked_gate[k]

    tri = (lax.broadcasted_iota(I32, (tm, tm), 0) < lax.broadcasted_iota(I32, (tm, tm), 1))
    before = jnp.dot(onehot.astype(BF16), tri.astype(BF16), preferred_element_type=F32)
    rank = before + cnt_ref[:, 0:1]
    for k in range(TOP_K):
        pick = eidx == picked_idx[k]
        tope_ref[k:k + 1, :] = picked_idx[k].astype(I32)
        pos_ref[k:k + 1, :] = jnp.sum(jnp.where(pick, rank, 0.0), axis=0, keepdims=True).astype(I32)
        gate_ref[k:k + 1, :] = picked_gate[k] / gsum * ROUTED_SCALE
    cnt_ref[...] = cnt_ref[...] + jnp.sum(onehot, axis=1, keepdims=True)
    cnt_out_ref[...] = cnt_ref[...]


def _mix(xp, xs, outs, lses, obp, obs, wout, g, b, wrh, wrl, rb):
    tm = TM_MIX
    n1 = xp.shape[0] // tm
    n2 = xs.shape[0] // tm
    t_all = xp.shape[0] + xs.shape[0]
    full = lambda a: pl.BlockSpec(a.shape, lambda i: (0,) * a.ndim)
    row = lambda cols: pl.BlockSpec((tm, cols), lambda i: (i, 0))
    prow = lambda cols: pl.BlockSpec((tm, cols), lambda i: (jnp.minimum(i, n1 - 1), 0))
    srow = lambda cols: pl.BlockSpec((tm, cols), lambda i: (jnp.maximum(i - n1, 0), 0))
    col = pl.BlockSpec((TOP_K, tm), lambda i: (0, i))
    head_of_lane = np.arange(A_WIDTH) // A_HEAD_DIM
    spread = jnp.asarray(np.arange(LANES)[:, None] == head_of_lane[None, :], BF16)
    return pl.pallas_call(
        functools.partial(_mix_kernel, n_prompt_tiles=n1),
        grid=(n1 + n2,),
        in_specs=[
            prow(D_MODEL), srow(D_MODEL),
            row(A_WIDTH), row(A_WIDTH), row(A_WIDTH), row(LANES), row(LANES), row(LANES), full(spread),
            pl.BlockSpec((B_HEADS * B_V, tm), lambda i: (0, jnp.minimum(i, n1 - 1))),
            pl.BlockSpec((B_HEADS * B_V, tm), lambda i: (0, jnp.maximum(i - n1, 0))),
            full(wout), full(g), full(b), full(wrh), full(wrl), full(rb),
        ],
        out_specs=[row(D_MODEL), pl.BlockSpec((SLAB * tm, LANES), lambda i: (i, 0)), col, col, col,
                   pl.BlockSpec((N_EXPERTS, LANES), lambda i: (0, 0))],
        out_shape=[jax.ShapeDtypeStruct((t_all, D_MODEL), F32),
                   jax.ShapeDtypeStruct((SLAB * t_all, LANES), BF16),
                   jax.ShapeDtypeStruct((TOP_K, t_all), I32),
                   jax.ShapeDtypeStruct((TOP_K, t_all), I32),
                   jax.ShapeDtypeStruct((TOP_K, t_all), F32),
                   jax.ShapeDtypeStruct((N_EXPERTS, LANES), F32)],
        scratch_shapes=[pltpu.VMEM((N_EXPERTS, LANES), F32), pltpu.VMEM((SLAB * tm, LANES), F32)],
        compiler_params=pltpu.CompilerParams(dimension_semantics=("arbitrary",),
                                             vmem_limit_bytes=VMEM_LIMIT),
        name="mix_router",
    )(xp, xs, *outs, *lses, spread, obp, obs, wout, g, b, wrh, wrl, rb)


def _dispatch_kernel(zstart_ref, nused_ref, h_ref, tope_ref, pos_ref, pstart_ref, buf_ref, dest_ref,
                     dsm_ref, zero_ref, sem_ref, zsem_ref):
    i = pl.program_id(0)
    tm = tope_ref.shape[1]
    tile_rows = SLAB * ROW_BLOCK

    @pl.when(i == 0)
    def _():
        zero_ref[...] = jnp.zeros_like(zero_ref)

        def zfill(e, carry):
            start = pl.multiple_of(zstart_ref[e], SLAB)
            pltpu.make_async_copy(zero_ref, buf_ref.at[pl.ds(start, tile_rows), :], zsem_ref).start()
            return carry

        lax.fori_loop(0, N_EXPERTS, zfill, 0)
        span = buf_ref.at[pl.ds(0, N_EXPERTS * tile_rows), :]
        pltpu.make_async_copy(span, span, zsem_ref).wait()

        def ztail(j, carry):
            start = pl.multiple_of(j * tile_rows, tile_rows)
            pltpu.make_async_copy(zero_ref, buf_ref.at[pl.ds(start, tile_rows), :], zsem_ref).start()
            return carry

        def zwait(j, carry):
            pltpu.make_async_copy(zero_ref, buf_ref.at[pl.ds(0, tile_rows), :], zsem_ref).wait()
            return carry

        n_tiles = buf_ref.shape[0] // tile_rows
        lax.fori_loop(nused_ref[0], n_tiles, ztail, 0)
        lax.fori_loop(nused_ref[0], n_tiles, zwait, 0)

    eidx = lax.broadcasted_iota(I32, (N_EXPERTS, tm), 0)
    pstart = pstart_ref[...]
    for k in range(TOP_K):
        hit = eidx == tope_ref[k:k + 1, :]
        base = jnp.sum(jnp.where(hit, pstart, 0.0), axis=0, keepdims=True)
        dest_ref[k:k + 1, :] = (pos_ref[k:k + 1, :] + base.astype(I32)) * SLAB
    pltpu.sync_copy(dest_ref, dsm_ref)

    def scatter(t, carry):
        src = h_ref.at[pl.ds(pl.multiple_of(t * SLAB, SLAB), SLAB), :]
        for k in range(TOP_K):
            dst = buf_ref.at[pl.ds(pl.multiple_of(dsm_ref[k, t], SLAB), SLAB), :]
            pltpu.make_async_copy(src, dst, sem_ref).start(priority=k % 2)
        return carry

    lax.fori_loop(0, tm, scatter, 0, unroll=8)
    span = buf_ref.at[pl.ds(0, TOP_K * tm * SLAB), :]
    pltpu.make_async_copy(span, span, sem_ref).wait()


def _dispatch(h, tope, pos, pstart, zstart, nused, n_rows):
    tm = TM_DISP
    t_all = tope.shape[1]
    col = pl.BlockSpec((TOP_K, tm), lambda i, z, nu: (0, i))
    return pl.pallas_call(
        _dispatch_kernel,
        grid_spec=pltpu.PrefetchScalarGridSpec(
            num_scalar_prefetch=2, grid=(t_all // tm,),
            in_specs=[pl.BlockSpec((SLAB * tm, LANES), lambda i, z, nu: (i, 0)), col, col,
                      pl.BlockSpec((N_EXPERTS, 1), lambda i, z, nu: (0, 0))],
            out_specs=[pl.BlockSpec(memory_space=pl.ANY), col],
            scratch_shapes=[pltpu.SMEM((TOP_K, tm), I32),
                            pltpu.VMEM((SLAB * ROW_BLOCK, LANES), BF16),
                            pltpu.SemaphoreType.DMA, pltpu.SemaphoreType.DMA]),
        out_shape=[jax.ShapeDtypeStruct((SLAB * n_rows, LANES), BF16),
                   jax.ShapeDtypeStruct((TOP_K, t_all), I32)],
        compiler_params=pltpu.CompilerParams(dimension_semantics=("arbitrary",),
                                             vmem_limit_bytes=VMEM_LIMIT),
        name="dispatch",
    )(zstart, nused, h, tope, pos, pstart)


def _expert_kernel(tfirst_ref, ntile_ref, nused_ref, x_hbm, wg_ref, wu_ref, wd_ref, o_hbm,
                   xbuf, obuf, stage_ref, wgb_ref, wub_ref, wdb_ref, xsem, osem):
    e = pl.program_id(0)
    tile_rows = SLAB * ROW_BLOCK
    nused = nused_ref[0]

    def x_copy(g, slot):
        start = pl.multiple_of(g * tile_rows, tile_rows)
        return pltpu.make_async_copy(x_hbm.at[pl.ds(start, tile_rows), :], xbuf.at[slot], xsem.at[slot])

    def o_copy(g, slot):
        start = pl.multiple_of(g * tile_rows, tile_rows)
        return pltpu.make_async_copy(obuf.at[slot], o_hbm.at[pl.ds(start, tile_rows), :], osem.at[slot])

    @pl.when(e == 0)
    def _():
        for j in range(X_AHEAD):
            @pl.when(j < nused)
            def _():
                x_copy(j, j).start()

    n_e = ntile_ref[e]

    @pl.when(n_e > 0)
    def _():
        wgb_ref[...] = wg_ref[...].astype(BF16)
        wub_ref[...] = wu_ref[...].astype(BF16)
        wdb_ref[...] = wd_ref[...].astype(BF16)

    def run_tiles(g0, n):
        for q in range(n):
            ahead = g0 + q + X_AHEAD

            @pl.when(ahead < nused)
            def _():
                x_copy(ahead, ahead % N_XBUF).start()

        for q in range(n):
            g = g0 + q
            x_copy(g, g % N_XBUF).wait()

            @pl.when(g >= N_OBUF)
            def _():
                o_copy(g - N_OBUF, g % N_OBUF).wait()

        for q in range(n):
            g = g0 + q
            x = _slab_load_bf16(xbuf.at[g % N_XBUF], stage_ref.at[q], ROW_BLOCK).astype(BF16)
            gt = jnp.dot(x, wgb_ref[...], preferred_element_type=F32)
            up = jnp.dot(x, wub_ref[...], preferred_element_type=F32)
            hmid = (gt * jax.nn.sigmoid(gt) * up).astype(BF16)
            out = jnp.dot(hmid, wdb_ref[...], preferred_element_type=F32)
            _slab_store(obuf.at[g % N_OBUF], out)

        for q in range(n):
            g = g0 + q
            o_copy(g, g % N_OBUF).start()

    def group(j, carry):
        run_tiles(tfirst_ref[e] + EXPERT_TILES * j, EXPERT_TILES)
        return carry

    n_groups = n_e // EXPERT_TILES
    lax.fori_loop(0, n_groups, group, 0)
    done = n_groups * EXPERT_TILES
    left = n_e - done
    size = EXPERT_TILES // 2
    while size >= 1:
        @pl.when((left & size) != 0)
        def _(size=size, done=done):
            run_tiles(tfirst_ref[e] + done, size)

        done = done + (left & size)
        size //= 2

    @pl.when(e == N_EXPERTS - 1)
    def _():
        for j in range(N_OBUF):
            @pl.when(nused > j)
            def _():
                o_copy(nused - 1 - j, (nused - 1 - j) % N_OBUF).wait()

        zero_ref = stage_ref.at[0]
        zero_ref[...] = jnp.zeros_like(zero_ref)
        n_tiles = o_hbm.shape[0] // tile_rows

        def tail_copy(g):
            start = pl.multiple_of(g * tile_rows, tile_rows)
            return pltpu.make_async_copy(zero_ref, o_hbm.at[pl.ds(start, tile_rows), :], osem.at[0])

        def ztail(g, carry):
            tail_copy(g).start()
            return carry

        def zwait(g, carry):
            tail_copy(g).wait()
            return carry

        lax.fori_loop(nused, n_tiles, ztail, 0)
        lax.fori_loop(nused, n_tiles, zwait, 0)


def _experts(buf, tfirst, ntile, nused, w_gate, w_up, w_down):
    tile_rows = SLAB * ROW_BLOCK
    wmap = lambda e, tf, nt, nu: (e, 0, 0)
    return pl.pallas_call(
        _expert_kernel,
        grid_spec=pltpu.PrefetchScalarGridSpec(
            num_scalar_prefetch=3, grid=(N_EXPERTS,),
            in_specs=[pl.BlockSpec(memory_space=pl.ANY),
                      pl.BlockSpec((None, D_MODEL, EXPERT_HIDDEN), wmap),
                      pl.BlockSpec((None, D_MODEL, EXPERT_HIDDEN), wmap),
                      pl.BlockSpec((None, EXPERT_HIDDEN, D_MODEL), wmap)],
            out_specs=pl.BlockSpec(memory_space=pl.ANY),
            scratch_shapes=[pltpu.VMEM((N_XBUF, tile_rows, LANES), BF16),
                            pltpu.VMEM((N_OBUF, tile_rows, LANES), F32),
                            pltpu.VMEM((EXPERT_TILES, tile_rows, LANES), F32),
                            pltpu.VMEM((D_MODEL, EXPERT_HIDDEN), BF16),
                            pltpu.VMEM((D_MODEL, EXPERT_HIDDEN), BF16),
                            pltpu.VMEM((EXPERT_HIDDEN, D_MODEL), BF16),
                            pltpu.SemaphoreType.DMA((N_XBUF,)),
                            pltpu.SemaphoreType.DMA((N_OBUF,))]),
        out_shape=jax.ShapeDtypeStruct(buf.shape, F32),
        compiler_params=pltpu.CompilerParams(dimension_semantics=("arbitrary",),
                                             vmem_limit_bytes=VMEM_LIMIT),
        name="experts",
    )(tfirst, ntile, nused, buf, w_gate, w_up, w_down)


def _combine_kernel(h_ref, gate_ref, dest_ref, dest1_ref, dest2_ref, eo_ref, wsg_ref, wsu_ref, wsd_ref,
                    g_ref, b_ref, yp_ref, ys_ref, dsm_ref, rows_ref, sem_ref, tsem_ref,
                    *, n_prompt_tiles, n_tiles):
    i = pl.program_id(0)
    tm = gate_ref.shape[0]
    slot = i % 2
    rows = rows_ref.at[slot]
    nxt_rows = rows_ref.at[1 - slot]
    nxt_sem = sem_ref.at[1 - slot]

    @pl.when(i == 0)
    def _():
        pltpu.sync_copy(dest_ref, dsm_ref)

        def gather(t, carry):
            for k in range(TOP_K):
                src = eo_ref.at[pl.ds(pl.multiple_of(dsm_ref[k, t], SLAB), SLAB), :]
                dst = rows_ref.at[0, pl.ds(pl.multiple_of((k * tm + t) * SLAB, SLAB), SLAB), :]
                pltpu.make_async_copy(src, dst, sem_ref.at[0]).start(priority=k % 2)
            return carry

        lax.fori_loop(0, tm, gather, 0, unroll=8)
        pltpu.sync_copy(dest1_ref, dsm_ref)

    def issue_next(k):
        for t in range(tm):
            src = eo_ref.at[pl.ds(pl.multiple_of(dsm_ref[k, t], SLAB), SLAB), :]
            dst = nxt_rows.at[pl.ds((k * tm + t) * SLAB, SLAB), :]
            pltpu.make_async_copy(src, dst, nxt_sem).start(priority=t % 2)

    h = h_ref[...]
    hb = h.astype(BF16)
    sg = jnp.dot(hb, wsg_ref[...], preferred_element_type=F32)
    su = jnp.dot(hb, wsu_ref[...], preferred_element_type=F32)
    shared = jnp.dot((sg * jax.nn.sigmoid(sg) * su).astype(BF16), wsd_ref[...],
                     preferred_element_type=F32)

    issue_next(0)
    whole = eo_ref.at[pl.ds(0, TOP_K * tm * SLAB), :]
    pltpu.make_async_copy(whole, rows, sem_ref.at[slot]).wait()
    gate = gate_ref[...]
    routed = gate[:, 0:1] * _slab_load(rows, tm)
    for k in range(1, TOP_K):
        issue_next(k)
        if k == TOP_K - 1:
            table_copy = pltpu.make_async_copy(dest2_ref, dsm_ref, tsem_ref)
            table_copy.start()
        routed = routed + gate[:, k:k + 1] * _slab_load(rows, tm, k * tm)
    y = _layer_norm(ALPHA * h + (routed + shared), g_ref[...], b_ref[...])

    @pl.when(i < n_prompt_tiles)
    def _():
        yp_ref[...] = y

    @pl.when(i >= n_prompt_tiles)
    def _():
        ys_ref[...] = y

    table_copy.wait()

    @pl.when(i == n_tiles - 1)
    def _():
        pltpu.make_async_copy(whole, nxt_rows, nxt_sem).wait()


def _combine(h, gate_t, dest, eo, wsg, wsu, wsd, g, b, t_prompt):
    tm = TM_COMB
    t_all = gate_t.shape[0]
    n1 = t_prompt // tm
    n2 = (t_all - t_prompt) // tm
    full = lambda a: pl.BlockSpec(a.shape, lambda i: (0,) * a.ndim)
    return pl.pallas_call(
        functools.partial(_combine_kernel, n_prompt_tiles=n1, n_tiles=n1 + n2),
        grid=(n1 + n2,),
        in_specs=[pl.BlockSpec((tm, D_MODEL), lambda i: (i, 0)),
                  pl.BlockSpec((tm, TOP_K), lambda i: (i, 0)),
                  pl.BlockSpec((TOP_K, tm), lambda i: (0, i)),
                  pl.BlockSpec((TOP_K, tm), lambda i: (0, jnp.minimum(i + 1, n1 + n2 - 1))),
                  pl.BlockSpec((TOP_K, tm), lambda i: (0, jnp.minimum(i + 2, n1 + n2 - 1))),
                  pl.BlockSpec(memory_space=pl.ANY),
                  full(wsg), full(wsu), full(wsd), full(g), full(b)],
        out_specs=[pl.BlockSpec((tm, D_MODEL), lambda i: (jnp.minimum(i, n1 - 1), 0)),
                   pl.BlockSpec((tm, D_MODEL), lambda i: (jnp.maximum(i - n1, 0), 0))],
        out_shape=[jax.ShapeDtypeStruct((t_prompt, D_MODEL), F32),
                   jax.ShapeDtypeStruct((t_all - t_prompt, D_MODEL), F32)],
        scratch_shapes=[pltpu.SMEM((TOP_K, tm), I32),
                        pltpu.VMEM((2, TOP_K * tm * SLAB, LANES), F32),
                        pltpu.SemaphoreType.DMA((2,)), pltpu.SemaphoreType.DMA],
        compiler_params=pltpu.CompilerParams(dimension_semantics=("arbitrary",),
                                             vmem_limit_bytes=VMEM_LIMIT),
        name="combine",
    )(h, gate_t, dest, dest, dest, eo, wsg, wsu, wsd, g, b)


def _rope_tables(s_max):
    inv_freq = ROPE_BASE ** (-jnp.arange(0, B_ROPE, 2, dtype=F32) / B_ROPE)
    ang = jnp.arange(s_max, dtype=F32)[:, None] * inv_freq[None, :]
    cos, sin = jnp.cos(ang), jnp.sin(ang)
    ones = jnp.ones((s_max, B_NOPE), F32)
    zeros_n = jnp.zeros((s_max, B_NOPE), F32)
    zeros_p = jnp.zeros((s_max, HEAD_PAD - B_NOPE - B_ROPE), F32)
    return (jnp.concatenate([ones, cos, cos, zeros_p], axis=1),
            jnp.concatenate([zeros_n, sin, sin, zeros_p], axis=1))


def _rot_cols(w):
    half = B_ROPE // 2
    return jnp.concatenate([-w[..., half:], w[..., :half]], axis=-1)


def _layout_weights(w_in, w_uq, w_uk):
    c_kr = 3 * A_WIDTH + Q_LORA + KV_LORA
    w_kr = w_in[:, c_kr:c_kr + B_ROPE]
    pad_l = jnp.zeros((D_MODEL, B_NOPE), F32)
    pad_r = jnp.zeros((D_MODEL, HEAD_PAD - B_NOPE - B_ROPE), F32)
    w1 = jnp.concatenate([w_in[:, :c_kr], pad_l, w_kr, pad_r, pad_l, _rot_cols(w_kr), pad_r], axis=1)
    wq3 = w_uq.reshape(Q_LORA, B_HEADS, B_NOPE + B_ROPE)
    nope, rope = wq3[..., :B_NOPE], wq3[..., B_NOPE:]
    zpad = jnp.zeros((Q_LORA, B_HEADS, HEAD_PAD - B_NOPE - B_ROPE), F32)
    wq = jnp.concatenate([nope, rope, zpad], axis=-1).reshape(Q_LORA, B_HEADS * HEAD_PAD)
    wqr = jnp.concatenate([jnp.zeros_like(nope), _rot_cols(rope), zpad], axis=-1)
    wqr = wqr.reshape(Q_LORA, B_HEADS * HEAD_PAD)
    wk3 = w_uk.reshape(KV_LORA, B_HEADS, B_NOPE)
    wuk = jnp.concatenate([wk3, jnp.zeros((KV_LORA, B_HEADS, HEAD_PAD - B_NOPE), F32)], axis=-1)
    wuk = wuk.reshape(KV_LORA, B_HEADS * HEAD_PAD)
    return w1.astype(BF16), wq.astype(BF16), wqr.astype(BF16), wuk.astype(BF16)


def _forward(x_prompt, x_sample, w_in, w_out, ln1_g, ln1_b, q_norm_g, w_uq, kv_norm_g, w_uk, w_uv,
             w_router, router_bias, w_gate, w_up, w_down, ws_gate, ws_up, ws_down, ln2_g, ln2_b):
    b1, s1, _ = x_prompt.shape
    b2, s2, _ = x_sample.shape
    t1, t2 = b1 * s1, b2 * s2
    t_all = t1 + t2
    xp = x_prompt.reshape(t1, D_MODEL)
    xs = x_sample.reshape(t2, D_MODEL)
    seqs = [(b * s1, s1) for b in range(b1)] + [(t1 + b * s2, s2) for b in range(b2)]
    assert t1 % s2 == 0 and s1 % (TQ_A * 16) == 0 and s2 % (TQ_A * 16) == 0

    w1, wq, wqr, wuk = _layout_weights(w_in, w_uq, w_uk)
    cos_t, sin_t = _rope_tables(max(s1, s2))
    (qa, ka, va, q4, k4, v4, q16, k16, v16, qb, kb, vbt) = _proj(
        xp, xs, w1, q_norm_g.reshape(1, -1), kv_norm_g.reshape(1, -1), wq, wqr, wuk,
        w_uv.T.astype(BF16), cos_t, sin_t, s1, s2)

    outs, lses = [], []
    for (_, dil), qkv in zip(A_PATTERNS, ((qa, ka, va), (q4, k4, v4), (q16, k16, v16))):
        o, lse = _attn_a(*qkv, seqs, dil)
        outs.append(o)
        lses.append(lse)
    obp = _attn_b(qb, kb, vbt, 0, b1, s1)
    obs = _attn_b(qb, kb, vbt, t1, b2, s2)

    wr_t = w_router.T
    wr_hi = wr_t.astype(BF16)
    wr_lo = (wr_t - wr_hi.astype(F32)).astype(BF16)
    h, h_slab, tope, pos, gate, counts = _mix(xp, xs, outs, lses, obp, obs, w_out.astype(BF16),
                                                ln1_g.reshape(1, -1), ln1_b.reshape(1, -1),
                                                wr_hi, wr_lo, router_bias.reshape(-1, 1))

    cnt = counts[:, 0].astype(I32)
    padded = ((cnt + ROW_BLOCK - 1) // ROW_BLOCK) * ROW_BLOCK
    pend = jnp.cumsum(padded)
    pstart = pend - padded
    n_tiles = (t_all * TOP_K) // ROW_BLOCK + N_EXPERTS + 1
    nused = (pend[-1:] // ROW_BLOCK).astype(I32)

    buf, dest = _dispatch(h_slab, tope, pos, pstart.astype(F32).reshape(-1, 1),
                          ((pstart + cnt) * SLAB).astype(I32), nused, n_tiles * ROW_BLOCK)
    eo = _experts(buf, (pstart // ROW_BLOCK).astype(I32), (padded // ROW_BLOCK).astype(I32), nused,
                  w_gate, w_up, w_down)
    yp, ys = _combine(h, gate.T, dest, eo, ws_gate.astype(BF16), ws_up.astype(BF16),
                      ws_down.astype(BF16), ln2_g.reshape(1, -1), ln2_b.reshape(1, -1), t1)
    return yp.reshape(b1, s1, D_MODEL), ys.reshape(b2, s2, D_MODEL)


def kernel(x_prompt, x_sample, w_in, w_out, ln1_g, ln1_b, q_norm_g, w_uq, kv_norm_g, w_uk, w_uv,
           w_router, router_bias, w_gate, w_up, w_down, ws_gate, ws_up, ws_down, ln2_g, ln2_b):
    params = (w_in, w_out, ln1_g, ln1_b, q_norm_g, w_uq, kv_norm_g, w_uk, w_uv, w_router, router_bias,
              w_gate, w_up, w_down, ws_gate, ws_up, ws_down, ln2_g, ln2_b)
    assert all(p.shape[0] == 1 for p in params), "one encoder layer"
    return _forward(x_prompt, x_sample, *[p.reshape(p.shape[1:]) for p in params])
```

```python
import functools

import numpy as np
import jax
import jax.numpy as jnp
from jax import lax
from jax.experimental import pallas as pl
from jax.experimental.pallas import tpu as pltpu

F32 = jnp.float32
BF16 = jnp.bfloat16
I32 = jnp.int32

D_MODEL = 1024
A_HEADS = 8
A_HEAD_DIM = 64
A_WIDTH = A_HEADS * A_HEAD_DIM
A_PATTERNS = ((128, 1), (512, 4), (2048, 16))
A_HALF = 64
B_HEADS = 8
B_NOPE = 64
B_ROPE = 32
B_V = 64
Q_LORA = 256
KV_LORA = 128
ROPE_BASE = 10000.0
N_EXPERTS = 256
TOP_K = 8
N_GROUPS = 8
GROUP_SIZE = N_EXPERTS // N_GROUPS
TOPK_GROUPS = 4
EXPERT_HIDDEN = 256
ROUTED_SCALE = 2.5
LN_EPS = 1e-5
RMS_EPS = 1e-6
NEG_BIG = -1e30
ALPHA = 2.0 ** 0.25
LOG2E = 1.4426950408889634

LANES = 128
HEAD_PAD = 128
SLAB = D_MODEL // LANES

TM_PROJ = 256
TQ_A = 128
A_POS_TILES = {1: 4, 4: 2, 16: 1}
PERM_GROUP = 256
LOGITS_ELEMS_B = 4096 * 512
KEY_CHUNK_B = 256
TM_MIX = 256
TM_DISP = 256
ROW_BLOCK = 256
TM_COMB = 128
EXPERT_TILES = 2
N_XBUF = 8
N_OBUF = 4
X_AHEAD = N_XBUF - EXPERT_TILES
VMEM_LIMIT = 48 * 1024 * 1024


def _slab_load(ref, n_tok, tok0=0):
    return jnp.concatenate([ref[pl.ds(tok0 * SLAB + c, n_tok, stride=SLAB), :] for c in range(SLAB)],
                           axis=1)


def _slab_store(ref, val):
    for c in range(SLAB):
        ref[pl.ds(c, val.shape[0], stride=SLAB), :] = val[:, c * LANES:(c + 1) * LANES]


def _slab_load_bf16(ref, stage_ref, n_tok, tok0=0):
    stage_ref[...] = ref[tok0 * SLAB:(tok0 + n_tok) * SLAB, :].astype(F32)
    return _slab_load(stage_ref, n_tok)


def _slab_store_bf16(ref, stage_ref, val):
    _slab_store(stage_ref, val)
    ref[...] = stage_ref[...].astype(BF16)


def _nt_dot(a, b):
    return lax.dot_general(a, b, (((1,), (1,)), ((), ())), preferred_element_type=F32)


def _layer_norm(x, g, b):
    mu = jnp.mean(x, axis=-1, keepdims=True)
    xc = x - mu
    var = jnp.mean(xc * xc, axis=-1, keepdims=True)
    return xc * lax.rsqrt(var + LN_EPS) * g + b


def _rms_norm(x, g):
    return x * lax.rsqrt(jnp.mean(x * x, axis=-1, keepdims=True) + RMS_EPS) * g


def _proj_kernel(xp_ref, xs_ref, w1_ref, qg_ref, kvg_ref, wq_ref, wqr_ref, wuk_ref, wuvt_ref,
                 cos_ref, sin_ref, perm4_ref, perm16_ref,
                 qa_ref, ka_ref, va_ref, q4_ref, k4_ref, v4_ref, q16_ref, k16_ref, v16_ref,
                 qb_ref, kb_ref, vbt_ref, *, n_prompt_tiles):
    i = pl.program_id(0)
    tm = xp_ref.shape[0]
    x = jnp.where(i < n_prompt_tiles, xp_ref[...], xs_ref[...]).astype(BF16)
    p = jnp.dot(x, w1_ref[...], preferred_element_type=F32)
    qa = (p[:, 0:A_WIDTH] * (A_HEAD_DIM ** -0.5 * LOG2E)).astype(BF16)
    ka = p[:, A_WIDTH:2 * A_WIDTH].astype(BF16)
    va = p[:, 2 * A_WIDTH:3 * A_WIDTH].astype(BF16)
    qa_ref[...] = qa
    ka_ref[...] = ka
    va_ref[...] = va
    qkv = jnp.concatenate([qa, ka, va], axis=1)
    for dil, perm_ref, outs in ((4, perm4_ref, (q4_ref, k4_ref, v4_ref)),
                                (16, perm16_ref, (q16_ref, k16_ref, v16_ref))):
        cm = jnp.dot(perm_ref[...], qkv, preferred_element_type=F32).astype(BF16)
        rows = tm // dil
        for r in range(dil):
            for j, o_ref in enumerate(outs):
                o_ref[:, r * A_WIDTH:(r + 1) * A_WIDTH] = cm[r * rows:(r + 1) * rows,
                                                             j * A_WIDTH:(j + 1) * A_WIDTH]
    c0 = 3 * A_WIDTH
    cq = p[:, c0:c0 + Q_LORA]
    ckv = p[:, c0 + Q_LORA:c0 + Q_LORA + KV_LORA]
    kr = p[:, c0 + Q_LORA + KV_LORA:c0 + Q_LORA + KV_LORA + LANES]
    krr = p[:, c0 + Q_LORA + KV_LORA + LANES:c0 + Q_LORA + KV_LORA + 2 * LANES]
    cos = cos_ref[...]
    sin = sin_ref[...]
    cos8 = jnp.concatenate([cos] * B_HEADS, axis=1)
    sin8 = jnp.concatenate([sin] * B_HEADS, axis=1)
    cqn = _rms_norm(cq, qg_ref[...]).astype(BF16)
    q = jnp.dot(cqn, wq_ref[...], preferred_element_type=F32)
    qr = jnp.dot(cqn, wqr_ref[...], preferred_element_type=F32)
    qscale = (B_NOPE + B_ROPE) ** -0.5 * LOG2E
    qb_ref[...] = ((q * cos8 + qr * sin8) * qscale).astype(BF16)
    ckvn = _rms_norm(ckv, kvg_ref[...]).astype(BF16)
    kn = jnp.dot(ckvn, wuk_ref[...], preferred_element_type=F32)
    krope = kr * cos + krr * sin
    kb_ref[...] = (kn + jnp.concatenate([krope] * B_HEADS, axis=1)).astype(BF16)
    vbt_ref[...] = _nt_dot(wuvt_ref[...], ckvn).astype(BF16)


def _class_perm(tm, dil):
    rows = tm // dil
    c = np.arange(tm)
    src = (c % rows) * dil + c // rows
    perm = np.zeros((tm, tm), np.float32)
    perm[c, src] = 1.0
    return jnp.asarray(perm, BF16)


def _proj(xp, xs, w1, qg, kvg, wq, wqr, wuk, wuvt, cos_t, sin_t, s_prompt, s_sample):
    tm = TM_PROJ
    n1 = xp.shape[0] // tm
    n2 = xs.shape[0] // tm
    t_all = xp.shape[0] + xs.shape[0]
    pt, st = s_prompt // tm, s_sample // tm
    assert tm == PERM_GROUP
    perm4, perm16 = _class_perm(tm, 4), _class_perm(tm, 16)

    def tab_idx(i):
        return (jnp.where(i < n1, i % pt, (i - n1) % st), 0)

    full = lambda shape: pl.BlockSpec(shape, lambda i: (0, 0))
    row = lambda cols: pl.BlockSpec((tm, cols), lambda i: (i, 0))
    cls = lambda dil: pl.BlockSpec((tm // dil, dil * A_WIDTH), lambda i: (i, 0))
    cls_shape = lambda dil: jax.ShapeDtypeStruct((t_all // dil, dil * A_WIDTH), BF16)
    return pl.pallas_call(
        functools.partial(_proj_kernel, n_prompt_tiles=n1),
        grid=(n1 + n2,),
        in_specs=[
            pl.BlockSpec((tm, D_MODEL), lambda i: (jnp.minimum(i, n1 - 1), 0)),
            pl.BlockSpec((tm, D_MODEL), lambda i: (jnp.maximum(i - n1, 0), 0)),
            full(w1.shape), full(qg.shape), full(kvg.shape), full(wq.shape), full(wqr.shape),
            full(wuk.shape), full(wuvt.shape),
            pl.BlockSpec((tm, LANES), tab_idx), pl.BlockSpec((tm, LANES), tab_idx),
            full(perm4.shape), full(perm16.shape),
        ],
        out_specs=[row(A_WIDTH)] * 3 + [cls(4)] * 3 + [cls(16)] * 3
        + [row(B_HEADS * HEAD_PAD), row(B_HEADS * HEAD_PAD),
           pl.BlockSpec((B_HEADS * B_V, tm), lambda i: (0, i))],
        out_shape=[jax.ShapeDtypeStruct((t_all, A_WIDTH), BF16)] * 3
        + [cls_shape(4)] * 3 + [cls_shape(16)] * 3
        + [jax.ShapeDtypeStruct((t_all, B_HEADS * HEAD_PAD), BF16)] * 2
        + [jax.ShapeDtypeStruct((B_HEADS * B_V, t_all), BF16)],
        compiler_params=pltpu.CompilerParams(dimension_semantics=("arbitrary",),
                                             vmem_limit_bytes=VMEM_LIMIT),
        name="proj",
    )(xp, xs, w1, qg, kvg, wq, wqr, wuk, wuvt, cos_t, sin_t, perm4, perm16)


def _attn_a_kernel(qblk_ref, pblk_ref, nblk_ref, tstart_ref, llen_ref,
                   q_ref, kp_ref, kc_ref, kn_ref, vp_ref, vc_ref, vn_ref, bias_ref, perm_ref,
                   o_ref, lse_ref, cm_ref, *, n_pos, n_cls):
    n = pl.program_id(0)
    tq, wk = TQ_A, TQ_A + 2 * A_HALF
    lane = lax.broadcasted_iota(I32, (tq, LANES), 1)
    low = lane < A_HEAD_DIM
    colbase = lax.broadcasted_iota(I32, (1, wk), 1) + (tstart_ref[n] - A_HALF)
    cls_len = llen_ref[n]

    def one_class(r):
        cols = slice(0, A_WIDTH) if n_cls == 1 else pl.ds(pl.multiple_of(r * A_WIDTH, A_WIDTH), A_WIDTH)
        kcat = jnp.concatenate([kp_ref[:, cols], kc_ref[:, cols], kn_ref[:, cols]], axis=0)
        vcat = jnp.concatenate([vp_ref[:, cols], vc_ref[:, cols], vn_ref[:, cols]], axis=0)
        for j in range(n_pos):
            rows = slice(j * tq, (j + 1) * tq)
            q = q_ref[rows, cols]
            kw, vw = kcat[j * tq:j * tq + wk, :], vcat[j * tq:j * tq + wk, :]
            col = colbase + j * tq
            colpen = jnp.where((col >= 0) & (col < cls_len), 0.0, NEG_BIG).astype(F32)
            pairs = []
            lse_c = jnp.zeros((tq, LANES), F32)
            for jp in range(A_HEADS // 2):
                sl = slice(jp * LANES, (jp + 1) * LANES)
                qp, kpair, vpair = q[:, sl], kw[:, sl], vw[:, sl]
                outs = []
                for e in range(2):
                    qm = jnp.where(low if e == 0 else ~low, qp, jnp.zeros_like(qp))
                    s = _nt_dot(qm, kpair) + bias_ref[2 * jp + e] + colpen
                    m = jnp.max(s, axis=1, keepdims=True)
                    p = jnp.exp2(s - m)
                    l = jnp.sum(p, axis=1, keepdims=True)
                    outs.append(jnp.dot(p.astype(BF16), vpair, preferred_element_type=F32) / l)
                    lse_c = lse_c + jnp.where(lane == 2 * jp + e, m + jnp.log2(l), 0.0)
                pairs.append(jnp.where(low, outs[0], outs[1]))
            o_full = jnp.concatenate(pairs, axis=1).astype(BF16)
            if n_cls == 1:
                o_ref[rows, :] = o_full
                lse_ref[rows, :] = lse_c
            else:
                hi = lse_c.astype(BF16)
                rest = lse_c - hi.astype(F32)
                mid = rest.astype(BF16)
                lo = (rest - mid.astype(F32)).astype(BF16)
                cm_ref[r, rows, :] = jnp.concatenate([o_full, hi, mid, lo], axis=1)

    if n_cls == 1:
        one_class(0)
        return

    def body(r2, carry):
        one_class(2 * r2)
        one_class(2 * r2 + 1)
        return carry

    lax.fori_loop(0, n_cls // 2, body, 0)
    per = PERM_GROUP // n_cls
    for a in range(n_pos * tq * n_cls // PERM_GROUP):
        stack = jnp.concatenate([cm_ref[r, a * per:(a + 1) * per, :] for r in range(n_cls)], axis=0)
        nat = jnp.dot(perm_ref[...], stack, preferred_element_type=F32)
        rows = slice(a * PERM_GROUP, (a + 1) * PERM_GROUP)
        o_ref[rows, :] = nat[:, :A_WIDTH].astype(BF16)
        lse_ref[rows, :] = (nat[:, A_WIDTH:A_WIDTH + LANES] + nat[:, A_WIDTH + LANES:A_WIDTH + 2 * LANES]
                            + nat[:, A_WIDTH + 2 * LANES:])


def _attn_a_tables(seqs, dil, n_pos):
    rows = n_pos * TQ_A
    per_halo = rows // A_HALF
    qblk, pblk, nblk, tstart, llen = [], [], [], [], []
    for off, s_len in seqs:
        cls = s_len // dil
        steps = cls // rows
        base = (off // dil) // rows
        for i in range(steps):
            qblk.append(base + i)
            pblk.append((base + i) * per_halo - (1 if i > 0 else 0))
            nblk.append((base + i + 1) * per_halo - (0 if i < steps - 1 else 1))
            tstart.append(i * rows)
            llen.append(cls)
    return [jnp.asarray(np.asarray(a, np.int32)) for a in (qblk, pblk, nblk, tstart, llen)]


def _attn_a_bias(dil):
    tq, wk = TQ_A, TQ_A + 2 * A_HALF
    delta = np.abs(np.arange(wk)[None, :] - A_HALF - np.arange(tq)[:, None]).astype(np.float64)
    slopes = 2.0 ** (-8.0 * (np.arange(A_HEADS) + 1.0) / A_HEADS)
    bias = -slopes[:, None, None] * (delta * dil)[None] * LOG2E
    bias = np.where((delta <= A_HALF)[None], bias, NEG_BIG)
    return jnp.asarray(bias.astype(np.float32))


def _attn_a(qc, kc, vc, seqs, dil):
    t_all = qc.shape[0] * dil
    n_pos = A_POS_TILES[dil]
    rows, width = n_pos * TQ_A, dil * A_WIDTH
    tabs = _attn_a_tables(seqs, dil, n_pos)
    n_steps = int(tabs[0].shape[0])
    bias = _attn_a_bias(dil)
    perm = _class_perm(PERM_GROUP, dil).T
    cur = pl.BlockSpec((rows, width), lambda n, qb, pb, nb, ts, ll: (qb[n], 0))
    prev = pl.BlockSpec((A_HALF, width), lambda n, qb, pb, nb, ts, ll: (pb[n], 0))
    nxt = pl.BlockSpec((A_HALF, width), lambda n, qb, pb, nb, ts, ll: (nb[n], 0))
    tok = lambda cols: pl.BlockSpec((rows * dil, cols), lambda n, *_: (n, 0))
    return pl.pallas_call(
        functools.partial(_attn_a_kernel, n_pos=n_pos, n_cls=dil),
        grid_spec=pltpu.PrefetchScalarGridSpec(
            num_scalar_prefetch=5, grid=(n_steps,),
            in_specs=[cur, prev, cur, nxt, prev, cur, nxt,
                      pl.BlockSpec(bias.shape, lambda n, *_: (0, 0, 0)),
                      pl.BlockSpec(perm.shape, lambda n, *_: (0, 0))],
            out_specs=[tok(A_WIDTH), tok(LANES)],
            scratch_shapes=[pltpu.VMEM((dil, rows, A_WIDTH + 3 * LANES), BF16)]),
        out_shape=[jax.ShapeDtypeStruct((t_all, A_WIDTH), BF16),
                   jax.ShapeDtypeStruct((t_all, LANES), F32)],
        compiler_params=pltpu.CompilerParams(dimension_semantics=("arbitrary",),
                                             vmem_limit_bytes=VMEM_LIMIT),
        name=f"attn_a_d{dil}",
    )(*tabs, qc, kc, kc, kc, vc, vc, vc, bias, perm)


def _attn_b_kernel(q_ref, k_ref, vt_ref, o_ref, sta_ref, stb_ref, ma_ref, mb_ref):
    s = pl.program_id(0)
    tq, s_len = q_ref.shape[0], k_ref.shape[0]

    @pl.when(s == 0)
    def _():
        stb_ref[...] = jnp.zeros_like(stb_ref)
        mb_ref[...] = jnp.zeros_like(mb_ref)

    def step(st_new_ref, m_new_ref, st_old_ref, m_old_ref):
        m_old = m_old_ref[...]
        q = q_ref[...]
        m_new = jnp.full((1, tq), -jnp.inf, F32)
        l = jnp.zeros((1, tq), F32)
        acc = jnp.zeros((B_V, tq), F32)
        for c in range(s_len // KEY_CHUNK_B):
            ks = slice(c * KEY_CHUNK_B, (c + 1) * KEY_CHUNK_B)
            st_c = _nt_dot(k_ref[ks, :], q)
            st_new_ref[ks, :] = st_c
            m_new = jnp.maximum(m_new, jnp.max(st_c, axis=0, keepdims=True))
            p = jnp.exp2(st_old_ref[ks, :] - m_old)
            l = l + jnp.sum(p, axis=0, keepdims=True)
            acc = acc + jnp.dot(vt_ref[:, ks], p.astype(BF16), preferred_element_type=F32)
        m_new_ref[...] = m_new
        o_ref[...] = (acc / l).astype(BF16)

    @pl.when(s % 2 == 0)
    def _():
        step(sta_ref, ma_ref, stb_ref, mb_ref)

    @pl.when(s % 2 == 1)
    def _():
        step(stb_ref, mb_ref, sta_ref, ma_ref)


def _attn_b(qb, kb, vbt, off, n_batch, s_len):
    tq = min(s_len, LOGITS_ELEMS_B // s_len)
    nq = s_len // tq
    qbase, kbase = off // tq, off // s_len
    n_items = n_batch * B_HEADS * nq

    def split(item):
        return item // (B_HEADS * nq), (item // nq) % B_HEADS, item % nq

    def q_map(s):
        b, h, qi = split(jnp.minimum(s, n_items - 1))
        return (qbase + b * nq + qi, h)

    def k_map(s):
        b, h, _ = split(jnp.minimum(s, n_items - 1))
        return (kbase + b, h)

    def v_map(s):
        b, h, _ = split(jnp.maximum(s - 1, 0))
        return (h, kbase + b)

    def o_map(s):
        b, h, qi = split(jnp.maximum(s - 1, 0))
        return (h, b * nq + qi)

    return pl.pallas_call(
        _attn_b_kernel,
        grid=(n_items + 1,),
        in_specs=[pl.BlockSpec((tq, HEAD_PAD), q_map),
                  pl.BlockSpec((s_len, HEAD_PAD), k_map),
                  pl.BlockSpec((B_V, s_len), v_map)],
        out_specs=pl.BlockSpec((B_V, tq), o_map),
        out_shape=jax.ShapeDtypeStruct((B_HEADS * B_V, n_batch * s_len), BF16),
        scratch_shapes=[pltpu.VMEM((s_len, tq), F32), pltpu.VMEM((s_len, tq), F32),
                        pltpu.VMEM((1, tq), F32), pltpu.VMEM((1, tq), F32)],
        compiler_params=pltpu.CompilerParams(dimension_semantics=("arbitrary",),
                                             vmem_limit_bytes=VMEM_LIMIT),
        name=f"attn_b_s{s_len}",
    )(qb, kb, vbt)


def _mix_kernel(xp_ref, xs_ref, o0_ref, o1_ref, o2_ref, l0_ref, l1_ref, l2_ref, spread_ref,
                obp_ref, obs_ref, wout_ref,
                g_ref, b_ref, wrh_ref, wrl_ref, rb_ref,
                h_ref, hp_ref, tope_ref, pos_ref, gate_ref, cnt_out_ref, cnt_ref, stage_ref,
                *, n_prompt_tiles):
    i = pl.program_id(0)
    tm = xp_ref.shape[0]

    @pl.when(i == 0)
    def _():
        cnt_ref[...] = jnp.zeros_like(cnt_ref)

    l0, l1, l2 = l0_ref[...], l1_ref[...], l2_ref[...]
    lmax = jnp.maximum(jnp.maximum(l0, l1), l2)
    e0, e1, e2 = jnp.exp2(l0 - lmax), jnp.exp2(l1 - lmax), jnp.exp2(l2 - lmax)
    inv = 1.0 / (e0 + e1 + e2)
    spread = spread_ref[...]

    def per_lane(w):
        hi = w.astype(BF16)
        lo = (w - hi.astype(F32)).astype(BF16)
        return (jnp.dot(hi, spread, preferred_element_type=F32)
                + jnp.dot(lo, spread, preferred_element_type=F32))

    oa = (per_lane(e0 * inv) * o0_ref[...].astype(F32) + per_lane(e1 * inv) * o1_ref[...].astype(F32)
          + per_lane(e2 * inv) * o2_ref[...].astype(F32))
    is_prompt = i < n_prompt_tiles
    obt = jnp.where(is_prompt, obp_ref[...], obs_ref[...])
    mix = (jnp.dot(oa.astype(BF16), wout_ref[0:A_WIDTH, :], preferred_element_type=F32)
           + lax.dot_general(obt, wout_ref[A_WIDTH:, :], (((0,), (0,)), ((), ())),
                             preferred_element_type=F32))
    x = jnp.where(is_prompt, xp_ref[...], xs_ref[...])
    h = _layer_norm(ALPHA * x + mix, g_ref[...], b_ref[...])
    h_ref[...] = h
    _slab_store_bf16(hp_ref, stage_ref, h)

    h_hi = h.astype(BF16)
    h_lo = (h - h_hi.astype(F32)).astype(BF16)
    wrh = wrh_ref[...]
    logits = _nt_dot(wrh, h_hi) + _nt_dot(wrh, h_lo) + _nt_dot(wrl_ref[...], h_hi)
    scores = jax.nn.sigmoid(logits)
    sel = scores + rb_ref[...]

    sub = lax.broadcasted_iota(I32, (GROUP_SIZE, tm), 0).astype(F32)
    gscore = []
    for g in range(N_GROUPS):
        sg = sel[g * GROUP_SIZE:(g + 1) * GROUP_SIZE, :]
        m1 = jnp.max(sg, axis=0, keepdims=True)
        first = jnp.min(jnp.where(sg == m1, sub, float(GROUP_SIZE)), axis=0, keepdims=True)
        m2 = jnp.max(jnp.where(sub == first, -jnp.inf, sg), axis=0, keepdims=True)
        gscore.append(m1 + m2)
    cands = []
    for g in range(N_GROUPS):
        beaten = jnp.zeros((1, tm), F32)
        for g2 in range(N_GROUPS):
            if g2 == g:
                continue
            wins = (gscore[g2] > gscore[g]) | ((gscore[g2] == gscore[g]) & (g2 < g))
            beaten = beaten + wins.astype(F32)
        keep = beaten < float(TOPK_GROUPS)
        sg = sel[g * GROUP_SIZE:(g + 1) * GROUP_SIZE, :]
        cands.append(jnp.where(keep, sg, NEG_BIG))
    cand = jnp.concatenate(cands, axis=0)

    eidx = lax.broadcasted_iota(I32, (N_EXPERTS, tm), 0).astype(F32)
    picked_idx, picked_gate = [], []
    onehot = jnp.zeros((N_EXPERTS, tm), F32)
    for _ in range(TOP_K):
        mx = jnp.max(cand, axis=0, keepdims=True)
        fi = jnp.min(jnp.where(cand == mx, eidx, float(N_EXPERTS)), axis=0, keepdims=True)
        pick = eidx == fi
        picked_idx.append(fi)
        picked_gate.append(jnp.sum(jnp.where(pick, scores, 0.0), axis=0, keepdims=True))
        onehot = jnp.where(pick, 1.0, onehot)
        cand = jnp.where(pick, -jnp.inf, cand)
    gsum = picked_gate[0]
    for k in range(1, TOP_K):
        gsum = gsum + picked_gate[k]

    tri = (lax.broadcasted_iota(I32, (tm, tm), 0) < lax.broadcasted_iota(I32, (tm, tm), 1))
    before = jnp.dot(onehot.astype(BF16), tri.astype(BF16), preferred_element_type=F32)
    rank = before + cnt_ref[:, 0:1]
    for k in range(TOP_K):
        pick = eidx == picked_idx[k]
        tope_ref[k:k + 1, :] = picked_idx[k].astype(I32)
        pos_ref[k:k + 1, :] = jnp.sum(jnp.where(pick, rank, 0.0), axis=0, keepdims=True).astype(I32)
        gate_ref[k:k + 1, :] = picked_gate[k] / gsum * ROUTED_SCALE
    cnt_ref[...] = cnt_ref[...] + jnp.sum(onehot, axis=1, keepdims=True)
    cnt_out_ref[...] = cnt_ref[...]


def _mix(xp, xs, outs, lses, obp, obs, wout, g, b, wrh, wrl, rb):
    tm = TM_MIX
    n1 = xp.shape[0] // tm
    n2 = xs.shape[0] // tm
    t_all = xp.shape[0] + xs.shape[0]
    full = lambda a: pl.BlockSpec(a.shape, lambda i: (0,) * a.ndim)
    row = lambda cols: pl.BlockSpec((tm, cols), lambda i: (i, 0))
    prow = lambda cols: pl.BlockSpec((tm, cols), lambda i: (jnp.minimum(i, n1 - 1), 0))
    srow = lambda cols: pl.BlockSpec((tm, cols), lambda i: (jnp.maximum(i - n1, 0), 0))
    col = pl.BlockSpec((TOP_K, tm), lambda i: (0, i))
    head_of_lane = np.arange(A_WIDTH) // A_HEAD_DIM
    spread = jnp.asarray(np.arange(LANES)[:, None] == head_of_lane[None, :], BF16)
    return pl.pallas_call(
        functools.partial(_mix_kernel, n_prompt_tiles=n1),
        grid=(n1 + n2,),
        in_specs=[
            prow(D_MODEL), srow(D_MODEL),
            row(A_WIDTH), row(A_WIDTH), row(A_WIDTH), row(LANES), row(LANES), row(LANES), full(spread),
            pl.BlockSpec((B_HEADS * B_V, tm), lambda i: (0, jnp.minimum(i, n1 - 1))),
            pl.BlockSpec((B_HEADS * B_V, tm), lambda i: (0, jnp.maximum(i - n1, 0))),
            full(wout), full(g), full(b), full(wrh), full(wrl), full(rb),
        ],
        out_specs=[row(D_MODEL), pl.BlockSpec((SLAB * tm, LANES), lambda i: (i, 0)), col, col, col,
                   pl.BlockSpec((N_EXPERTS, LANES), lambda i: (0, 0))],
        out_shape=[jax.ShapeDtypeStruct((t_all, D_MODEL), F32),
                   jax.ShapeDtypeStruct((SLAB * t_all, LANES), BF16),
                   jax.ShapeDtypeStruct((TOP_K, t_all), I32),
                   jax.ShapeDtypeStruct((TOP_K, t_all), I32),
                   jax.ShapeDtypeStruct((TOP_K, t_all), F32),
                   jax.ShapeDtypeStruct((N_EXPERTS, LANES), F32)],
        scratch_shapes=[pltpu.VMEM((N_EXPERTS, LANES), F32), pltpu.VMEM((SLAB * tm, LANES), F32)],
        compiler_params=pltpu.CompilerParams(dimension_semantics=("arbitrary",),
                                             vmem_limit_bytes=VMEM_LIMIT),
        name="mix_router",
    )(xp, xs, *outs, *lses, spread, obp, obs, wout, g, b, wrh, wrl, rb)


def _dispatch_kernel(zstart_ref, nused_ref, h_ref, tope_ref, pos_ref, pstart_ref, buf_ref, dest_ref,
                     dsm_ref, zero_ref, sem_ref, zsem_ref):
    i = pl.program_id(0)
    tm = tope_ref.shape[1]
    tile_rows = SLAB * ROW_BLOCK

    @pl.when(i == 0)
    def _():
        zero_ref[...] = jnp.zeros_like(zero_ref)

        def zfill(e, carry):
            start = pl.multiple_of(zstart_ref[e], SLAB)
            pltpu.make_async_copy(zero_ref, buf_ref.at[pl.ds(start, tile_rows), :], zsem_ref).start()
            return carry

        lax.fori_loop(0, N_EXPERTS, zfill, 0)
        span = buf_ref.at[pl.ds(0, N_EXPERTS * tile_rows), :]
        pltpu.make_async_copy(span, span, zsem_ref).wait()

        def ztail(j, carry):
            start = pl.multiple_of(j * tile_rows, tile_rows)
            pltpu.make_async_copy(zero_ref, buf_ref.at[pl.ds(start, tile_rows), :], zsem_ref).start()
            return carry

        def zwait(j, carry):
            pltpu.make_async_copy(zero_ref, buf_ref.at[pl.ds(0, tile_rows), :], zsem_ref).wait()
            return carry

        n_tiles = buf_ref.shape[0] // tile_rows
        lax.fori_loop(nused_ref[0], n_tiles, ztail, 0)
        lax.fori_loop(nused_ref[0], n_tiles, zwait, 0)

    eidx = lax.broadcasted_iota(I32, (N_EXPERTS, tm), 0)
    pstart = pstart_ref[...]
    for k in range(TOP_K):
        hit = eidx == tope_ref[k:k + 1, :]
        base = jnp.sum(jnp.where(hit, pstart, 0.0), axis=0, keepdims=True)
        dest_ref[k:k + 1, :] = (pos_ref[k:k + 1, :] + base.astype(I32)) * SLAB
    pltpu.sync_copy(dest_ref, dsm_ref)

    def scatter(t, carry):
        src = h_ref.at[pl.ds(pl.multiple_of(t * SLAB, SLAB), SLAB), :]
        for k in range(TOP_K):
            dst = buf_ref.at[pl.ds(pl.multiple_of(dsm_ref[k, t], SLAB), SLAB), :]
            pltpu.make_async_copy(src, dst, sem_ref).start(priority=k % 2)
        return carry

    lax.fori_loop(0, tm, scatter, 0, unroll=8)
    span = buf_ref.at[pl.ds(0, TOP_K * tm * SLAB), :]
    pltpu.make_async_copy(span, span, sem_ref).wait()


def _dispatch(h, tope, pos, pstart, zstart, nused, n_rows):
    tm = TM_DISP
    t_all = tope.shape[1]
    col = pl.BlockSpec((TOP_K, tm), lambda i, z, nu: (0, i))
    return pl.pallas_call(
        _dispatch_kernel,
        grid_spec=pltpu.PrefetchScalarGridSpec(
            num_scalar_prefetch=2, grid=(t_all // tm,),
            in_specs=[pl.BlockSpec((SLAB * tm, LANES), lambda i, z, nu: (i, 0)), col, col,
                      pl.BlockSpec((N_EXPERTS, 1), lambda i, z, nu: (0, 0))],
            out_specs=[pl.BlockSpec(memory_space=pl.ANY), col],
            scratch_shapes=[pltpu.SMEM((TOP_K, tm), I32),
                            pltpu.VMEM((SLAB * ROW_BLOCK, LANES), BF16),
                            pltpu.SemaphoreType.DMA, pltpu.SemaphoreType.DMA]),
        out_shape=[jax.ShapeDtypeStruct((SLAB * n_rows, LANES), BF16),
                   jax.ShapeDtypeStruct((TOP_K, t_all), I32)],
        compiler_params=pltpu.CompilerParams(dimension_semantics=("arbitrary",),
                                             vmem_limit_bytes=VMEM_LIMIT),
        name="dispatch",
    )(zstart, nused, h, tope, pos, pstart)


def _expert_kernel(tfirst_ref, ntile_ref, nused_ref, x_hbm, wg_ref, wu_ref, wd_ref, o_hbm,
                   xbuf, obuf, stage_ref, wgb_ref, wub_ref, wdb_ref, xsem, osem):
    e = pl.program_id(0)
    tile_rows = SLAB * ROW_BLOCK
    nused = nused_ref[0]

    def x_copy(g, slot):
        start = pl.multiple_of(g * tile_rows, tile_rows)
        return pltpu.make_async_copy(x_hbm.at[pl.ds(start, tile_rows), :], xbuf.at[slot], xsem.at[slot])

    def o_copy(g, slot):
        start = pl.multiple_of(g * tile_rows, tile_rows)
        return pltpu.make_async_copy(obuf.at[slot], o_hbm.at[pl.ds(start, tile_rows), :], osem.at[slot])

    @pl.when(e == 0)
    def _():
        for j in range(X_AHEAD):
            @pl.when(j < nused)
            def _():
                x_copy(j, j).start()

    n_e = ntile_ref[e]

    @pl.when(n_e > 0)
    def _():
        wgb_ref[...] = wg_ref[...].astype(BF16)
        wub_ref[...] = wu_ref[...].astype(BF16)
        wdb_ref[...] = wd_ref[...].astype(BF16)

    def run_tiles(g0, n):
        for q in range(n):
            ahead = g0 + q + X_AHEAD

            @pl.when(ahead < nused)
            def _():
                x_copy(ahead, ahead % N_XBUF).start()

        for q in range(n):
            g = g0 + q
            x_copy(g, g % N_XBUF).wait()

            @pl.when(g >= N_OBUF)
            def _():
                o_copy(g - N_OBUF, g % N_OBUF).wait()

        for q in range(n):
            g = g0 + q
            x = _slab_load_bf16(xbuf.at[g % N_XBUF], stage_ref.at[q], ROW_BLOCK).astype(BF16)
            gt = jnp.dot(x, wgb_ref[...], preferred_element_type=F32)
            up = jnp.dot(x, wub_ref[...], preferred_element_type=F32)
            hmid = (gt * jax.nn.sigmoid(gt) * up).astype(BF16)
            out = jnp.dot(hmid, wdb_ref[...], preferred_element_type=F32)
            _slab_store(obuf.at[g % N_OBUF], out)

        for q in range(n):
            g = g0 + q
            o_copy(g, g % N_OBUF).start()

    def group(j, carry):
        run_tiles(tfirst_ref[e] + EXPERT_TILES * j, EXPERT_TILES)
        return carry

    n_groups = n_e // EXPERT_TILES
    lax.fori_loop(0, n_groups, group, 0)
    done = n_groups * EXPERT_TILES
    left = n_e - done
    size = EXPERT_TILES // 2
    while size >= 1:
        @pl.when((left & size) != 0)
        def _(size=size, done=done):
            run_tiles(tfirst_ref[e] + done, size)

        done = done + (left & size)
        size //= 2

    @pl.when(e == N_EXPERTS - 1)
    def _():
        for j in range(N_OBUF):
            @pl.when(nused > j)
            def _():
                o_copy(nused - 1 - j, (nused - 1 - j) % N_OBUF).wait()

        zero_ref = stage_ref.at[0]
        zero_ref[...] = jnp.zeros_like(zero_ref)
        n_tiles = o_hbm.shape[0] // tile_rows

        def tail_copy(g):
            start = pl.multiple_of(g * tile_rows, tile_rows)
            return pltpu.make_async_copy(zero_ref, o_hbm.at[pl.ds(start, tile_rows), :], osem.at[0])

        def ztail(g, carry):
            tail_copy(g).start()
            return carry

        def zwait(g, carry):
            tail_copy(g).wait()
            return carry

        lax.fori_loop(nused, n_tiles, ztail, 0)
        lax.fori_loop(nused, n_tiles, zwait, 0)


def _experts(buf, tfirst, ntile, nused, w_gate, w_up, w_down):
    tile_rows = SLAB * ROW_BLOCK
    wmap = lambda e, tf, nt, nu: (e, 0, 0)
    return pl.pallas_call(
        _expert_kernel,
        grid_spec=pltpu.PrefetchScalarGridSpec(
            num_scalar_prefetch=3, grid=(N_EXPERTS,),
            in_specs=[pl.BlockSpec(memory_space=pl.ANY),
                      pl.BlockSpec((None, D_MODEL, EXPERT_HIDDEN), wmap),
                      pl.BlockSpec((None, D_MODEL, EXPERT_HIDDEN), wmap),
                      pl.BlockSpec((None, EXPERT_HIDDEN, D_MODEL), wmap)],
            out_specs=pl.BlockSpec(memory_space=pl.ANY),
            scratch_shapes=[pltpu.VMEM((N_XBUF, tile_rows, LANES), BF16),
                            pltpu.VMEM((N_OBUF, tile_rows, LANES), F32),
                            pltpu.VMEM((EXPERT_TILES, tile_rows, LANES), F32),
                            pltpu.VMEM((D_MODEL, EXPERT_HIDDEN), BF16),
                            pltpu.VMEM((D_MODEL, EXPERT_HIDDEN), BF16),
                            pltpu.VMEM((EXPERT_HIDDEN, D_MODEL), BF16),
                            pltpu.SemaphoreType.DMA((N_XBUF,)),
                            pltpu.SemaphoreType.DMA((N_OBUF,))]),
        out_shape=jax.ShapeDtypeStruct(buf.shape, F32),
        compiler_params=pltpu.CompilerParams(dimension_semantics=("arbitrary",),
                                             vmem_limit_bytes=VMEM_LIMIT),
        name="experts",
    )(tfirst, ntile, nused, buf, w_gate, w_up, w_down)


def _combine_kernel(h_ref, gate_ref, dest_ref, dest1_ref, dest2_ref, eo_ref, wsg_ref, wsu_ref, wsd_ref,
                    g_ref, b_ref, yp_ref, ys_ref, dsm_ref, rows_ref, sem_ref, tsem_ref,
                    *, n_prompt_tiles, n_tiles):
    i = pl.program_id(0)
    tm = gate_ref.shape[0]
    slot = i % 2
    rows = rows_ref.at[slot]
    nxt_rows = rows_ref.at[1 - slot]
    nxt_sem = sem_ref.at[1 - slot]

    @pl.when(i == 0)
    def _():
        pltpu.sync_copy(dest_ref, dsm_ref)

        def gather(t, carry):
            for k in range(TOP_K):
                src = eo_ref.at[pl.ds(pl.multiple_of(dsm_ref[k, t], SLAB), SLAB), :]
                dst = rows_ref.at[0, pl.ds(pl.multiple_of((k * tm + t) * SLAB, SLAB), SLAB), :]
                pltpu.make_async_copy(src, dst, sem_ref.at[0]).start(priority=k % 2)
            return carry

        lax.fori_loop(0, tm, gather, 0, unroll=8)
        pltpu.sync_copy(dest1_ref, dsm_ref)

    def issue_next(k):
        for t in range(tm):
            src = eo_ref.at[pl.ds(pl.multiple_of(dsm_ref[k, t], SLAB), SLAB), :]
            dst = nxt_rows.at[pl.ds((k * tm + t) * SLAB, SLAB), :]
            pltpu.make_async_copy(src, dst, nxt_sem).start(priority=t % 2)

    h = h_ref[...]
    hb = h.astype(BF16)
    sg = jnp.dot(hb, wsg_ref[...], preferred_element_type=F32)
    su = jnp.dot(hb, wsu_ref[...], preferred_element_type=F32)
    shared = jnp.dot((sg * jax.nn.sigmoid(sg) * su).astype(BF16), wsd_ref[...],
                     preferred_element_type=F32)

    issue_next(0)
    whole = eo_ref.at[pl.ds(0, TOP_K * tm * SLAB), :]
    pltpu.make_async_copy(whole, rows, sem_ref.at[slot]).wait()
    gate = gate_ref[...]
    routed = gate[:, 0:1] * _slab_load(rows, tm)
    for k in range(1, TOP_K):
        issue_next(k)
        if k == TOP_K - 1:
            table_copy = pltpu.make_async_copy(dest2_ref, dsm_ref, tsem_ref)
            table_copy.start()
        routed = routed + gate[:, k:k + 1] * _slab_load(rows, tm, k * tm)
    y = _layer_norm(ALPHA * h + (routed + shared), g_ref[...], b_ref[...])

    @pl.when(i < n_prompt_tiles)
    def _():
        yp_ref[...] = y

    @pl.when(i >= n_prompt_tiles)
    def _():
        ys_ref[...] = y

    table_copy.wait()

    @pl.when(i == n_tiles - 1)
    def _():
        pltpu.make_async_copy(whole, nxt_rows, nxt_sem).wait()


def _combine(h, gate_t, dest, eo, wsg, wsu, wsd, g, b, t_prompt):
    tm = TM_COMB
    t_all = gate_t.shape[0]
    n1 = t_prompt // tm
    n2 = (t_all - t_prompt) // tm
    full = lambda a: pl.BlockSpec(a.shape, lambda i: (0,) * a.ndim)
    return pl.pallas_call(
        functools.partial(_combine_kernel, n_prompt_tiles=n1, n_tiles=n1 + n2),
        grid=(n1 + n2,),
        in_specs=[pl.BlockSpec((tm, D_MODEL), lambda i: (i, 0)),
                  pl.BlockSpec((tm, TOP_K), lambda i: (i, 0)),
                  pl.BlockSpec((TOP_K, tm), lambda i: (0, i)),
                  pl.BlockSpec((TOP_K, tm), lambda i: (0, jnp.minimum(i + 1, n1 + n2 - 1))),
                  pl.BlockSpec((TOP_K, tm), lambda i: (0, jnp.minimum(i + 2, n1 + n2 - 1))),
                  pl.BlockSpec(memory_space=pl.ANY),
                  full(wsg), full(wsu), full(wsd), full(g), full(b)],
        out_specs=[pl.BlockSpec((tm, D_MODEL), lambda i: (jnp.minimum(i, n1 - 1), 0)),
                   pl.BlockSpec((tm, D_MODEL), lambda i: (jnp.maximum(i - n1, 0), 0))],
        out_shape=[jax.ShapeDtypeStruct((t_prompt, D_MODEL), F32),
                   jax.ShapeDtypeStruct((t_all - t_prompt, D_MODEL), F32)],
        scratch_shapes=[pltpu.SMEM((TOP_K, tm), I32),
                        pltpu.VMEM((2, TOP_K * tm * SLAB, LANES), F32),
                        pltpu.SemaphoreType.DMA((2,)), pltpu.SemaphoreType.DMA],
        compiler_params=pltpu.CompilerParams(dimension_semantics=("arbitrary",),
                                             vmem_limit_bytes=VMEM_LIMIT),
        name="combine",
    )(h, gate_t, dest, dest, dest, eo, wsg, wsu, wsd, g, b)


def _rope_tables(s_max):
    inv_freq = ROPE_BASE ** (-jnp.arange(0, B_ROPE, 2, dtype=F32) / B_ROPE)
    ang = jnp.arange(s_max, dtype=F32)[:, None] * inv_freq[None, :]
    cos, sin = jnp.cos(ang), jnp.sin(ang)
    ones = jnp.ones((s_max, B_NOPE), F32)
    zeros_n = jnp.zeros((s_max, B_NOPE), F32)
    zeros_p = jnp.zeros((s_max, HEAD_PAD - B_NOPE - B_ROPE), F32)
    return (jnp.concatenate([ones, cos, cos, zeros_p], axis=1),
            jnp.concatenate([zeros_n, sin, sin, zeros_p], axis=1))


def _rot_cols(w):
    half = B_ROPE // 2
    return jnp.concatenate([-w[..., half:], w[..., :half]], axis=-1)


def _layout_weights(w_in, w_uq, w_uk):
    c_kr = 3 * A_WIDTH + Q_LORA + KV_LORA
    w_kr = w_in[:, c_kr:c_kr + B_ROPE]
    pad_l = jnp.zeros((D_MODEL, B_NOPE), F32)
    pad_r = jnp.zeros((D_MODEL, HEAD_PAD - B_NOPE - B_ROPE), F32)
    w1 = jnp.concatenate([w_in[:, :c_kr], pad_l, w_kr, pad_r, pad_l, _rot_cols(w_kr), pad_r], axis=1)
    wq3 = w_uq.reshape(Q_LORA, B_HEADS, B_NOPE + B_ROPE)
    nope, rope = wq3[..., :B_NOPE], wq3[..., B_NOPE:]
    zpad = jnp.zeros((Q_LORA, B_HEADS, HEAD_PAD - B_NOPE - B_ROPE), F32)
    wq = jnp.concatenate([nope, rope, zpad], axis=-1).reshape(Q_LORA, B_HEADS * HEAD_PAD)
    wqr = jnp.concatenate([jnp.zeros_like(nope), _rot_cols(rope), zpad], axis=-1)
    wqr = wqr.reshape(Q_LORA, B_HEADS * HEAD_PAD)
    wk3 = w_uk.reshape(KV_LORA, B_HEADS, B_NOPE)
    wuk = jnp.concatenate([wk3, jnp.zeros((KV_LORA, B_HEADS, HEAD_PAD - B_NOPE), F32)], axis=-1)
    wuk = wuk.reshape(KV_LORA, B_HEADS * HEAD_PAD)
    return w1.astype(BF16), wq.astype(BF16), wqr.astype(BF16), wuk.astype(BF16)


def _forward(x_prompt, x_sample, w_in, w_out, ln1_g, ln1_b, q_norm_g, w_uq, kv_norm_g, w_uk, w_uv,
             w_router, router_bias, w_gate, w_up, w_down, ws_gate, ws_up, ws_down, ln2_g, ln2_b):
    b1, s1, _ = x_prompt.shape
    b2, s2, _ = x_sample.shape
    t1, t2 = b1 * s1, b2 * s2
    t_all = t1 + t2
    xp = x_prompt.reshape(t1, D_MODEL)
    xs = x_sample.reshape(t2, D_MODEL)
    seqs = [(b * s1, s1) for b in range(b1)] + [(t1 + b * s2, s2) for b in range(b2)]
    assert t1 % s2 == 0 and s1 % (TQ_A * 16) == 0 and s2 % (TQ_A * 16) == 0

    w1, wq, wqr, wuk = _layout_weights(w_in, w_uq, w_uk)
    cos_t, sin_t = _rope_tables(max(s1, s2))
    (qa, ka, va, q4, k4, v4, q16, k16, v16, qb, kb, vbt) = _proj(
        xp, xs, w1, q_norm_g.reshape(1, -1), kv_norm_g.reshape(1, -1), wq, wqr, wuk,
        w_uv.T.astype(BF16), cos_t, sin_t, s1, s2)

    outs, lses = [], []
    for (_, dil), qkv in zip(A_PATTERNS, ((qa, ka, va), (q4, k4, v4), (q16, k16, v16))):
        o, lse = _attn_a(*qkv, seqs, dil)
        outs.append(o)
        lses.append(lse)
    obp = _attn_b(qb, kb, vbt, 0, b1, s1)
    obs = _attn_b(qb, kb, vbt, t1, b2, s2)

    wr_t = w_router.T
    wr_hi = wr_t.astype(BF16)
    wr_lo = (wr_t - wr_hi.astype(F32)).astype(BF16)
    h, h_slab, tope, pos, gate, counts = _mix(xp, xs, outs, lses, obp, obs, w_out.astype(BF16),
                                                ln1_g.reshape(1, -1), ln1_b.reshape(1, -1),
                                                wr_hi, wr_lo, router_bias.reshape(-1, 1))

    cnt = counts[:, 0].astype(I32)
    padded = ((cnt + ROW_BLOCK - 1) // ROW_BLOCK) * ROW_BLOCK
    pend = jnp.cumsum(padded)
    pstart = pend - padded
    n_tiles = (t_all * TOP_K) // ROW_BLOCK + N_EXPERTS + 1
    nused = (pend[-1:] // ROW_BLOCK).astype(I32)

    buf, dest = _dispatch(h_slab, tope, pos, pstart.astype(F32).reshape(-1, 1),
                          ((pstart + cnt) * SLAB).astype(I32), nused, n_tiles * ROW_BLOCK)
    eo = _experts(buf, (pstart // ROW_BLOCK).astype(I32), (padded // ROW_BLOCK).astype(I32), nused,
                  w_gate, w_up, w_down)
    yp, ys = _combine(h, gate.T, dest, eo, ws_gate.astype(BF16), ws_up.astype(BF16),
                      ws_down.astype(BF16), ln2_g.reshape(1, -1), ln2_b.reshape(1, -1), t1)
    return yp.reshape(b1, s1, D_MODEL), ys.reshape(b2, s2, D_MODEL)


def kernel(x_prompt, x_sample, w_in, w_out, ln1_g, ln1_b, q_norm_g, w_uq, kv_norm_g, w_uk, w_uv,
           w_router, router_bias, w_gate, w_up, w_down, ws_gate, ws_up, ws_down, ln2_g, ln2_b):
    params = (w_in, w_out, ln1_g, ln1_b, q_norm_g, w_uq, kv_norm_g, w_uk, w_uv, w_router, router_bias,
              w_gate, w_up, w_down, ws_gate, ws_up, ws_down, ln2_g, ln2_b)
    assert all(p.shape[0] == 1 for p in params), "one encoder layer"
    return _forward(x_prompt, x_sample, *[p.reshape(p.shape[1:]) for p in params])
```
